```python
import math
import jax
import jax.numpy as jnp
from jax import lax
import numpy as np

D_MODEL = 2048
BATCH = 8
SEQ = 2048
DEPTH = 2

CHUNK = 64
N_MEM = 256
N_A_LAYERS = DEPTH // 2
N_B_LAYERS = DEPTH - N_A_LAYERS
MIX_WIDTH = D_MODEL
MAIN_WIDTH = 3 * MIX_WIDTH // 4
MEM_WIDTH = MIX_WIDTH - MAIN_WIDTH
HEAD_DIM = 128
SSM_GROUP = 16
SSM_GROUPS = MAIN_WIDTH // SSM_GROUP
SSM_STATE = 64
FOX_HEADS = MAIN_WIDTH // HEAD_DIM
MEM_HEADS = 4
MEM_HEAD_DIM = MEM_WIDTH // MEM_HEADS
Q_BLOCK = 128
IN_WIDTH = 2 * MAIN_WIDTH + 2 * MEM_WIDTH
EPS = 1e-6
DT_MIN = 1e-3
DT_MAX = 1e-1

kernel_name = "yoco_s5_fox_memory_hybrid"


def rmsnorm(x, g):
    xf = x.astype(jnp.float32)
    y = xf * lax.rsqrt(jnp.mean(xf * xf, axis=-1, keepdims=True) + EPS) * g.astype(jnp.float32)
    return y.astype(x.dtype)


def _scan_binop(e1, e2):
    a1r, a1i, b1r, b1i = e1
    a2r, a2i, b2r, b2i = e2
    ar = a2r * a1r - a2i * a1i
    ai = a2r * a1i + a2i * a1r
    br = a2r * b1r - a2i * b1i + b2r
    bi = a2r * b1i + a2i * b1r + b2i
    return (ar, ai, br, bi)


def s5_ssm(u, lam_re, lam_im, log_step, b_re, b_im, c_re, c_im, d_skip):
    bsz, seqlen, _ = u.shape
    uf = u.astype(jnp.float32)
    ug = uf.reshape(bsz, seqlen, SSM_GROUPS, SSM_GROUP)
    lr = lam_re.astype(jnp.float32)
    li = lam_im.astype(jnp.float32)
    dt = jnp.exp(log_step.astype(jnp.float32))[:, None]
    mag = jnp.exp(lr * dt)
    ar = mag * jnp.cos(li * dt)
    ai = mag * jnp.sin(li * dt)
    den = lr * lr + li * li
    cr = ((ar - 1.0) * lr + ai * li) / den
    ci = (ai * lr - (ar - 1.0) * li) / den
    br = b_re.astype(jnp.float32)
    bi = b_im.astype(jnp.float32)
    bbar_re = cr[..., None] * br - ci[..., None] * bi
    bbar_im = cr[..., None] * bi + ci[..., None] * br
    bu_re = jnp.einsum('blgh,gph->blgp', ug, bbar_re)
    bu_im = jnp.einsum('blgh,gph->blgp', ug, bbar_im)
    a_re = jnp.broadcast_to(ar[None, None], (1, seqlen, SSM_GROUPS, SSM_STATE))
    a_im = jnp.broadcast_to(ai[None, None], (1, seqlen, SSM_GROUPS, SSM_STATE))
    _, _, x_re, x_im = lax.associative_scan(_scan_binop, (a_re, a_im, bu_re, bu_im), axis=1)
    y = (jnp.einsum('blgp,ghp->blgh', x_re, c_re.astype(jnp.float32))
         - jnp.einsum('blgp,ghp->blgh', x_im, c_im.astype(jnp.float32)))
    y = y.reshape(bsz, seqlen, MAIN_WIDTH) + d_skip.astype(jnp.float32) * uf
    return y.astype(u.dtype)


def memory_attention(qm, mem, mem_g, w_mem_kv):
    bsz, seqlen, _ = qm.shape
    memn = rmsnorm(mem, mem_g)
    kv = memn @ w_mem_kv
    km, vm = jnp.split(kv, 2, axis=-1)
    km = km.reshape(bsz, -1, MEM_HEADS, MEM_HEAD_DIM).astype(jnp.float32)
    vm = vm.reshape(bsz, -1, MEM_HEADS, MEM_HEAD_DIM).astype(jnp.float32)
    q = qm.reshape(bsz, seqlen, MEM_HEADS, MEM_HEAD_DIM).astype(jnp.float32) * (MEM_HEAD_DIM ** -0.5)
    s = jnp.einsum('blhd,bmhd->bhlm', q, km)
    p = jax.nn.softmax(s, axis=-1)
    o = jnp.einsum('bhlm,bmhd->blhd', p, vm)
    return o.reshape(bsz, seqlen, MEM_WIDTH).astype(qm.dtype)


def forgetting_attention(q, k, v, fcum):
    _, seqlen, _, dh = q.shape
    qf = q.astype(jnp.float32) * (dh ** -0.5)
    kf = k.astype(jnp.float32)
    vf = v.astype(jnp.float32)
    outs = []
    for blk in range(seqlen // Q_BLOCK):
        q0 = blk * Q_BLOCK
        q1 = q0 + Q_BLOCK
        s = jnp.einsum('bqhd,bkhd->bhqk', qf[:, q0:q1], kf[:, :q1])
        s = s + fcum[:, :, q0:q1, None] - fcum[:, :, None, :q1]
        causal = jnp.arange(q0, q1)[:, None] >= jnp.arange(q1)[None, :]
        s = jnp.where(causal, s, -jnp.inf)
        p = jax.nn.softmax(s, axis=-1)
        outs.append(jnp.einsum('bhqk,bkhd->bqhd', p, vf[:, :q1]))
    return jnp.concatenate(outs, axis=1).astype(q.dtype)


def setup_inputs(seed: int = 0) -> dict:
    key = jax.random.key(seed)
    ks = jax.random.split(key, 32)
    f32 = jnp.float32
    D = D_MODEL
    nrm = lambda k, shape, scale: jax.random.normal(k, shape, f32) * scale
    x = jax.random.normal(ks[0], (BATCH, SEQ, D), f32)
    mem = jax.random.normal(ks[1], (BATCH, N_MEM, D), f32)
    pre_norm_g = 1.0 + nrm(ks[2], (DEPTH, D), 0.02)
    post_norm_g = 1.0 + nrm(ks[3], (DEPTH, D), 0.02)
    w_in_a = nrm(ks[4], (N_A_LAYERS, D, IN_WIDTH), D ** -0.5)
    lam_re = -0.5 + nrm(ks[5], (N_A_LAYERS, SSM_GROUPS, SSM_STATE), 0.01)
    lam_im = (math.pi * jnp.arange(SSM_STATE, dtype=f32))[None, None, :] + nrm(ks[6], (N_A_LAYERS, SSM_GROUPS, SSM_STATE), 0.01)
    log_step = jax.random.uniform(ks[7], (N_A_LAYERS, SSM_GROUPS), f32, math.log(DT_MIN), math.log(DT_MAX))
    b_re = nrm(ks[8], (N_A_LAYERS, SSM_GROUPS, SSM_STATE, SSM_GROUP), (2.0 * SSM_GROUP) ** -0.5)
    b_im = nrm(ks[9], (N_A_LAYERS, SSM_GROUPS, SSM_STATE, SSM_GROUP), (2.0 * SSM_GROUP) ** -0.5)
    c_re = nrm(ks[10], (N_A_LAYERS, SSM_GROUPS, SSM_GROUP, SSM_STATE), (2.0 * SSM_STATE) ** -0.5)
    c_im = nrm(ks[11], (N_A_LAYERS, SSM_GROUPS, SSM_GROUP, SSM_STATE), (2.0 * SSM_STATE) ** -0.5)
    d_skip = nrm(ks[12], (N_A_LAYERS, MAIN_WIDTH), 1.0)
    w_glu = nrm(ks[13], (N_A_LAYERS, MAIN_WIDTH, MAIN_WIDTH), MAIN_WIDTH ** -0.5)
    b_glu = nrm(ks[14], (N_A_LAYERS, MAIN_WIDTH), 0.01)
    kv_norm_g = 1.0 + nrm(ks[15], (D,), 0.02)
    w_kv = nrm(ks[16], (D, 2 * MAIN_WIDTH), D ** -0.5)
    w_fgate = nrm(ks[17], (D, FOX_HEADS), D ** -0.5)
    b_fgate = nrm(ks[18], (FOX_HEADS,), 0.1)
    w_in_b = nrm(ks[19], (N_B_LAYERS, D, IN_WIDTH), D ** -0.5)
    mem_norm_g = 1.0 + nrm(ks[20], (DEPTH, D), 0.02)
    w_mem_kv = nrm(ks[21], (DEPTH, D, 2 * MEM_WIDTH), D ** -0.5)
    w_out = nrm(ks[22], (DEPTH, MIX_WIDTH, D), MIX_WIDTH ** -0.5)
    return {"x": x, "mem": mem, "pre_norm_g": pre_norm_g, "post_norm_g": post_norm_g,
            "w_in_a": w_in_a, "lam_re": lam_re, "lam_im": lam_im, "log_step": log_step,
            "b_re": b_re, "b_im": b_im, "c_re": c_re, "c_im": c_im, "d_skip": d_skip,
            "w_glu": w_glu, "b_glu": b_glu, "kv_norm_g": kv_norm_g, "w_kv": w_kv,
            "w_fgate": w_fgate, "b_fgate": b_fgate, "w_in_b": w_in_b,
            "mem_norm_g": mem_norm_g, "w_mem_kv": w_mem_kv, "w_out": w_out}


def reference(x, mem, pre_norm_g, post_norm_g, w_in_a, lam_re, lam_im, log_step,
              b_re, b_im, c_re, c_im, d_skip, w_glu, b_glu, kv_norm_g, w_kv,
              w_fgate, b_fgate, w_in_b, mem_norm_g, w_mem_kv, w_out):
    bsz, seqlen, _ = x.shape
    h = x
    k_sh = v_sh = fcum = None
    split_pts = [MAIN_WIDTH, 2 * MAIN_WIDTH, 2 * MAIN_WIDTH + MEM_WIDTH]
    for i in range(DEPTH):
        hn = rmsnorm(h, pre_norm_g[i])
        if i < N_A_LAYERS:
            la = i
            proj = hn @ w_in_a[la]
            u, z, qm, zm = jnp.split(proj, split_pts, axis=-1)
            y = s5_ssm(u, lam_re[la], lam_im[la], log_step[la], b_re[la], b_im[la],
                       c_re[la], c_im[la], d_skip[la])
            yg = jax.nn.gelu(y)
            y = yg * jax.nn.sigmoid(yg @ w_glu[la] + b_glu[la])
            main = y * jax.nn.silu(z)
        else:
            lb = i - N_A_LAYERS
            proj = hn @ w_in_b[lb]
            q, z, qm, zm = jnp.split(proj, split_pts, axis=-1)
            q = q.reshape(bsz, seqlen, FOX_HEADS, HEAD_DIM)
            att = forgetting_attention(q, k_sh, v_sh, fcum).reshape(bsz, seqlen, MAIN_WIDTH)
            main = att * jax.nn.silu(z)
        memo = memory_attention(qm, mem, mem_norm_g[i], w_mem_kv[i]) * jax.nn.silu(zm)
        o = jnp.concatenate([main, memo], axis=-1) @ w_out[i]
        h = h + rmsnorm(o, post_norm_g[i])
        if i == N_A_LAYERS - 1:
            kv_in = rmsnorm(h, kv_norm_g)
            kv = kv_in @ w_kv
            k_sh, v_sh = jnp.split(kv, 2, axis=-1)
            k_sh = k_sh.reshape(bsz, seqlen, FOX_HEADS, HEAD_DIM)
            v_sh = v_sh.reshape(bsz, seqlen, FOX_HEADS, HEAD_DIM)
            logf = jax.nn.log_sigmoid((kv_in @ w_fgate).astype(jnp.float32) + b_fgate.astype(jnp.float32))
            fcum = jnp.transpose(jnp.cumsum(logf, axis=1), (0, 2, 1))
    return h
```

```python
import functools
import math

import jax
import jax.numpy as jnp
from jax import lax
from jax.experimental import pallas as pl
from jax.experimental.pallas import tpu as pltpu

F32 = jnp.float32
BF16 = jnp.bfloat16

EPS = 1e-6
HEAD_DIM = 128
SSM_GROUP = 16
SSM_STATE = 64
MEM_HEADS = 4
LANES = 128
SUBLANES = 8
GROUPS_PER_BLOCK = LANES // SSM_GROUP
STATE_COLS = GROUPS_PER_BLOCK * SSM_STATE
VMEM_LIMIT = 56 * 1024 * 1024


def _params(sem, vmem=VMEM_LIMIT):
    return pltpu.CompilerParams(dimension_semantics=sem, vmem_limit_bytes=vmem)


def _norm_matmul_kernel(x_ref, g_ref, w_ref, o_ref, xn_ref):
    @pl.when(pl.program_id(1) == 0)
    def _():
        x = x_ref[...]
        ms = jnp.mean(x * x, axis=-1, keepdims=True)
        xn_ref[...] = (x * lax.rsqrt(ms + EPS) * g_ref[...]).astype(BF16)

    o_ref[...] = jnp.dot(xn_ref[...], w_ref[...],
                         preferred_element_type=F32).astype(o_ref.dtype)


def _norm_matmul(x2d, g, w, *, out_dtype, tm, tn, time_major_batches=None, name):
    m, d = x2d.shape
    n = w.shape[1]
    assert m % tm == 0 and n % tn == 0
    nj = n // tn
    if time_major_batches is None:
        out_shape = jax.ShapeDtypeStruct((m, n), out_dtype)
        out_spec = pl.BlockSpec((tm, tn), lambda i, j: (i, j))
    else:
        bsz = time_major_batches
        seqlen = m // bsz
        assert seqlen % tm == 0
        per_b = seqlen // tm
        out_shape = jax.ShapeDtypeStruct((seqlen, bsz * n), out_dtype)
        out_spec = pl.BlockSpec((tm, tn), lambda i, j: (i % per_b, (i // per_b) * nj + j))
    return pl.pallas_call(
        _norm_matmul_kernel,
        grid=(m // tm, nj),
        in_specs=[pl.BlockSpec((tm, d), lambda i, j: (i, 0)),
                  pl.BlockSpec((1, d), lambda i, j: (0, 0)),
                  pl.BlockSpec((d, tn), lambda i, j: (0, j))],
        out_specs=out_spec,
        out_shape=out_shape,
        scratch_shapes=[pltpu.VMEM((tm, d), BF16)],
        compiler_params=_params(("parallel", "arbitrary")),
        name=name,
    )(x2d, g.reshape(1, d), w)


def _fgate_kernel(x_ref, g_ref, wt_ref, b_ref, o_ref, carry_ref):
    @pl.when(pl.program_id(1) == 0)
    def _():
        carry_ref[...] = jnp.zeros_like(carry_ref)

    x = x_ref[...]
    tl = x.shape[0]
    ms = jnp.mean(x * x, axis=-1, keepdims=True)
    xn = x * lax.rsqrt(ms + EPS) * g_ref[...]
    logit = lax.dot_general(wt_ref[...], xn, (((1,), (1,)), ((), ())),
                            precision=lax.Precision.HIGHEST,
                            preferred_element_type=F32) + b_ref[...]
    logf = jnp.minimum(logit, 0.0) - jnp.log(1.0 + jnp.exp(-jnp.abs(logit)))
    s_idx = lax.broadcasted_iota(jnp.int32, (tl, tl), 0)
    t_idx = lax.broadcasted_iota(jnp.int32, (tl, tl), 1)
    upper = (s_idx <= t_idx).astype(F32)
    csum = jnp.dot(logf, upper, precision=lax.Precision.HIGHEST,
                   preferred_element_type=F32) + carry_ref[...]
    o_ref[0] = csum
    carry_ref[...] = csum[:, tl - 1:tl]


def _fgate(x2d, g, w_fgate, b_fgate, *, bsz, tl):
    m, d = x2d.shape
    seqlen = m // bsz
    heads = w_fgate.shape[1]
    hp = 2 * SUBLANES
    assert heads <= hp and seqlen % tl == 0
    wt = jnp.zeros((hp, d), F32).at[:heads].set(w_fgate.T.astype(F32))
    bb = jnp.zeros((hp, 1), F32).at[:heads, 0].set(b_fgate.astype(F32))
    per_b = seqlen // tl
    return pl.pallas_call(
        _fgate_kernel,
        grid=(bsz, per_b),
        in_specs=[pl.BlockSpec((tl, d), lambda b, i: (b * per_b + i, 0)),
                  pl.BlockSpec((1, d), lambda b, i: (0, 0)),
                  pl.BlockSpec((hp, d), lambda b, i: (0, 0)),
                  pl.BlockSpec((hp, 1), lambda b, i: (0, 0))],
        out_specs=pl.BlockSpec((1, hp, tl), lambda b, i: (b, 0, i)),
        out_shape=jax.ShapeDtypeStruct((bsz, hp, seqlen), F32),
        scratch_shapes=[pltpu.VMEM((hp, 1), F32)],
        compiler_params=_params(("parallel", "arbitrary")),
        name="fgate_cumsum",
    )(x2d, g.reshape(1, d), wt, bb)


def _mem_attn_kernel(q_ref, zm_ref, kv_ref, o_ref):
    width = MEM_HEADS * HEAD_DIM
    for h in range(MEM_HEADS):
        lo, hi = h * HEAD_DIM, (h + 1) * HEAD_DIM
        q = q_ref[:, lo:hi]
        k = kv_ref[:, lo:hi]
        v = kv_ref[:, width + lo:width + hi]
        s = lax.dot_general(q, k, (((1,), (1,)), ((), ())), preferred_element_type=F32)
        m = jnp.max(s, axis=-1, keepdims=True)
        p = jnp.exp(s - m)
        l = jnp.sum(p, axis=-1, keepdims=True)
        o = jnp.dot(p.astype(BF16), v, preferred_element_type=F32) / l
        zm = zm_ref[:, lo:hi].astype(F32)
        o_ref[:, lo:hi] = (o * (zm * jax.nn.sigmoid(zm))).astype(o_ref.dtype)


def _mem_attn(proj, q_blk, zm_blk, kvm, *, bsz, tq, name):
    m = proj.shape[0]
    seqlen = m // bsz
    n_mem = kvm.shape[0] // bsz
    width = MEM_HEADS * HEAD_DIM
    per_b = seqlen // tq
    return pl.pallas_call(
        _mem_attn_kernel,
        grid=(bsz, per_b),
        in_specs=[pl.BlockSpec((tq, width), lambda b, i: (b * per_b + i, q_blk)),
                  pl.BlockSpec((tq, width), lambda b, i: (b * per_b + i, zm_blk)),
                  pl.BlockSpec((n_mem, 2 * width), lambda b, i: (b, 0))],
        out_specs=pl.BlockSpec((tq, width), lambda b, i: (b * per_b + i, 0)),
        out_shape=jax.ShapeDtypeStruct((m, width), BF16),
        compiler_params=_params(("parallel", "arbitrary")),
        name=name,
    )(proj, proj, kvm)


def _s5_kernel(u_ref, bcat_ref, ccat_ref, ar_ref, ai_ref, d_ref, y_ref,
               bu_ref, xs_ref, st_ref, *, t_chunk):
    @pl.when(pl.program_id(1) == 0)
    def _():
        st_ref[...] = jnp.zeros_like(st_ref)

    rows = t_chunk * SUBLANES
    u = u_ref[...].reshape(rows, LANES)
    bu_ref[...] = jnp.dot(u.astype(BF16), bcat_ref[0], preferred_element_type=F32)
    ar = jnp.broadcast_to(ar_ref[0], (SUBLANES, STATE_COLS))
    ai = jnp.broadcast_to(ai_ref[0], (SUBLANES, STATE_COLS))

    def body(t, carry):
        xr, xi = carry
        r0 = pl.multiple_of(t * SUBLANES, SUBLANES)
        bur = bu_ref[pl.ds(r0, SUBLANES), 0:STATE_COLS]
        bui = bu_ref[pl.ds(r0, SUBLANES), STATE_COLS:2 * STATE_COLS]
        nxr = ar * xr - ai * xi + bur
        nxi = ar * xi + ai * xr + bui
        xs_ref[pl.ds(r0, SUBLANES), 0:STATE_COLS] = nxr
        xs_ref[pl.ds(r0, SUBLANES), STATE_COLS:2 * STATE_COLS] = nxi
        return nxr, nxi

    xr, xi = lax.fori_loop(0, t_chunk, body, (st_ref[0], st_ref[1]), unroll=8)
    st_ref[0] = xr
    st_ref[1] = xi
    y = jnp.dot(xs_ref[...].astype(BF16), ccat_ref[0], preferred_element_type=F32)
    y = y + d_ref[...] * u
    y_ref[...] = jax.nn.gelu(y).reshape(t_chunk, SUBLANES, LANES)


def _s5_discretise(lam_re, lam_im, log_step, b_re, b_im, c_re, c_im):
    groups = lam_re.shape[0]
    nblk = groups // GROUPS_PER_BLOCK
    lr = lam_re.astype(F32)
    li = lam_im.astype(F32)
    dt = jnp.exp(log_step.astype(F32))[:, None]
    mag = jnp.exp(lr * dt)
    ar = mag * jnp.cos(li * dt)
    ai = mag * jnp.sin(li * dt)
    den = lr * lr + li * li
    cr = ((ar - 1.0) * lr + ai * li) / den
    ci = (ai * lr - (ar - 1.0) * li) / den
    br = b_re.astype(F32)
    bi = b_im.astype(F32)
    bbar_re = cr[..., None] * br - ci[..., None] * bi
    bbar_im = cr[..., None] * bi + ci[..., None] * br
    eye = jnp.eye(GROUPS_PER_BLOCK, dtype=F32)

    def blockdiag_in(bbar):
        b4 = bbar.reshape(nblk, GROUPS_PER_BLOCK, SSM_STATE, SSM_GROUP)
        out = jnp.einsum('jgph,gk->jghkp', b4, eye)
        return out.reshape(nblk, LANES, STATE_COLS)

    def blockdiag_out(c):
        c4 = c.reshape(nblk, GROUPS_PER_BLOCK, SSM_GROUP, SSM_STATE)
        out = jnp.einsum('jghp,gk->jgpkh', c4, eye)
        return out.reshape(nblk, STATE_COLS, LANES)

    bcat = jnp.concatenate([blockdiag_in(bbar_re), blockdiag_in(bbar_im)], axis=2)
    ccat = jnp.concatenate([blockdiag_out(c_re.astype(F32)),
                            -blockdiag_out(c_im.astype(F32))], axis=1)
    return (bcat.astype(BF16), ccat.astype(BF16),
            ar.reshape(nblk, 1, STATE_COLS), ai.reshape(nblk, 1, STATE_COLS))


def _s5(u_tm, bcat, ccat, ar, ai, d_skip, *, bsz, t_chunk):
    seqlen = u_tm.shape[0]
    width = u_tm.shape[1] // bsz
    assert bsz == SUBLANES and seqlen % t_chunk == 0 and width % LANES == 0
    nblk = width // LANES
    rows = t_chunk * SUBLANES
    u3 = u_tm.reshape(seqlen, bsz, width)
    out = pl.pallas_call(
        functools.partial(_s5_kernel, t_chunk=t_chunk),
        grid=(nblk, seqlen // t_chunk),
        in_specs=[pl.BlockSpec((t_chunk, bsz, LANES), lambda j, c: (c, 0, j)),
                  pl.BlockSpec((1, LANES, 2 * STATE_COLS), lambda j, c: (j, 0, 0)),
                  pl.BlockSpec((1, 2 * STATE_COLS, LANES), lambda j, c: (j, 0, 0)),
                  pl.BlockSpec((1, 1, STATE_COLS), lambda j, c: (j, 0, 0)),
                  pl.BlockSpec((1, 1, STATE_COLS), lambda j, c: (j, 0, 0)),
                  pl.BlockSpec((1, LANES), lambda j, c: (0, j))],
        out_specs=pl.BlockSpec((t_chunk, bsz, LANES), lambda j, c: (c, 0, j)),
        out_shape=jax.ShapeDtypeStruct((seqlen, bsz, width), F32),
        scratch_shapes=[pltpu.VMEM((rows, 2 * STATE_COLS), F32),
                        pltpu.VMEM((rows, 2 * STATE_COLS), F32),
                        pltpu.VMEM((2, SUBLANES, STATE_COLS), F32)],
        compiler_params=_params(("parallel", "arbitrary")),
        name="s5_scan",
    )(u3, bcat, ccat, ar, ai, d_skip.reshape(1, width).astype(F32))
    return out.reshape(seqlen, bsz * width)


def _post(o, g_ref, h_ref, out_ref):
    ms = jnp.mean(o * o, axis=-1, keepdims=True)
    out_ref[...] = h_ref[...] + o * lax.rsqrt(ms + EPS) * g_ref[...]


def _glu_out_kernel(yg_ref, z_ref, memo_ref, h_ref, wglu_ref, bglu_ref,
                    wmain_ref, wmem_ref, g_ref, out_ref):
    yg = yg_ref[...]
    t = jnp.dot(yg.astype(BF16), wglu_ref[...], preferred_element_type=F32) + bglu_ref[...]
    z = z_ref[...].astype(F32)
    main = yg * jax.nn.sigmoid(t) * (z * jax.nn.sigmoid(z))
    o = jnp.dot(main.astype(BF16), wmain_ref[...], preferred_element_type=F32)
    o = o + jnp.dot(memo_ref[...], wmem_ref[...], preferred_element_type=F32)
    _post(o, g_ref, h_ref, out_ref)


def _out_kernel(main_ref, memo_ref, h_ref, wmain_ref, wmem_ref, g_ref, out_ref):
    o = jnp.dot(main_ref[...], wmain_ref[...], preferred_element_type=F32)
    o = o + jnp.dot(memo_ref[...], wmem_ref[...], preferred_element_type=F32)
    _post(o, g_ref, h_ref, out_ref)


def _resident(shape):
    return pl.BlockSpec(shape, lambda *_: (0,) * len(shape), pipeline_mode=pl.Buffered(1))


def _glu_out(yg_tm, rest, memo, h2d, w_glu, b_glu, w_out, g, *, bsz, tl):
    m, d = h2d.shape
    seqlen = m // bsz
    main_w = w_glu.shape[0]
    mem_w = w_out.shape[0] - main_w
    per_b = seqlen // tl
    row = lambda b, i: (b * per_b + i, 0)
    return pl.pallas_call(
        _glu_out_kernel,
        grid=(bsz, per_b),
        in_specs=[pl.BlockSpec((tl, main_w), lambda b, i: (i, b)),
                  pl.BlockSpec((tl, main_w), row),
                  pl.BlockSpec((tl, mem_w), row),
                  pl.BlockSpec((tl, d), row),
                  _resident((main_w, main_w)),
                  _resident((1, main_w)),
                  _resident((main_w, d)),
                  _resident((mem_w, d)),
                  _resident((1, d))],
        out_specs=pl.BlockSpec((tl, d), row),
        out_shape=jax.ShapeDtypeStruct((m, d), F32),
        compiler_params=_params(("parallel", "arbitrary")),
        name="glu_out_proj",
    )(yg_tm, rest, memo, h2d, w_glu.astype(BF16), b_glu.reshape(1, main_w).astype(F32),
      w_out[:main_w].astype(BF16), w_out[main_w:].astype(BF16), g.reshape(1, d).astype(F32))


def _out_proj(main, memo, h2d, w_out, g, *, tl):
    m, d = h2d.shape
    main_w = main.shape[1]
    mem_w = memo.shape[1]
    row = lambda i: (i, 0)
    return pl.pallas_call(
        _out_kernel,
        grid=(m // tl,),
        in_specs=[pl.BlockSpec((tl, main_w), row),
                  pl.BlockSpec((tl, mem_w), row),
                  pl.BlockSpec((tl, d), row),
                  _resident((main_w, d)),
                  _resident((mem_w, d)),
                  _resident((1, d))],
        out_specs=pl.BlockSpec((tl, d), row),
        out_shape=jax.ShapeDtypeStruct((m, d), F32),
        compiler_params=_params(("parallel",)),
        name="out_proj",
    )(main, memo, h2d, w_out[:main_w].astype(BF16), w_out[main_w:].astype(BF16),
      g.reshape(1, d).astype(F32))


def _fox_kernel(q_ref, k_ref, v_ref, f_ref, z_ref, o_ref, m_ref, l_ref, acc_ref, *, tq):
    qi = pl.program_id(2)
    q = q_ref[...]
    row = lax.broadcasted_iota(jnp.int32, (tq, tq), 0)
    col = lax.broadcasted_iota(jnp.int32, (tq, tq), 1)
    q0 = pl.multiple_of(qi * tq, tq)
    f_q_row = f_ref[0, :, pl.ds(q0, tq)]
    fq = jnp.sum(jnp.where(row == col, jnp.broadcast_to(f_q_row, (tq, tq)), 0.0),
                 axis=1, keepdims=True)
    m_ref[...] = jnp.full_like(m_ref, -jnp.inf)
    l_ref[...] = jnp.zeros_like(l_ref)
    acc_ref[...] = jnp.zeros_like(acc_ref)

    def step(ki, masked):
        k0 = pl.multiple_of(ki * tq, tq)
        k = k_ref[pl.ds(k0, tq), :]
        v = v_ref[pl.ds(k0, tq), :]
        s = lax.dot_general(q, k, (((1,), (1,)), ((), ())), preferred_element_type=F32)
        sb = s - f_ref[0, :, pl.ds(k0, tq)]
        if masked:
            sb = jnp.where(col <= row, sb, -jnp.inf)
        m_old = m_ref[...]
        m_new = jnp.maximum(m_old, jnp.max(sb, axis=1, keepdims=True) + fq)
        alpha = jnp.exp(m_old - m_new)
        p = jnp.exp(sb + (fq - m_new))
        l_ref[...] = alpha * l_ref[...] + jnp.sum(p, axis=1, keepdims=True)
        acc_ref[...] = alpha * acc_ref[...] + jnp.dot(p.astype(BF16), v,
                                                      preferred_element_type=F32)
        m_ref[...] = m_new

    def body(ki, c):
        step(ki, False)
        return c

    lax.fori_loop(0, qi, body, 0)
    step(qi, True)
    z = z_ref[...].astype(F32)
    o_ref[...] = (acc_ref[...] / l_ref[...] * (z * jax.nn.sigmoid(z))).astype(o_ref.dtype)


def _fox_attn(proj, kv, fcum, *, bsz, heads, tq):
    m = proj.shape[0]
    seqlen = m // bsz
    hp = fcum.shape[1]
    per_b = seqlen // tq
    f3 = fcum.reshape(bsz * hp, 1, seqlen)
    return pl.pallas_call(
        functools.partial(_fox_kernel, tq=tq),
        grid=(bsz, heads, per_b),
        in_specs=[pl.BlockSpec((tq, HEAD_DIM), lambda b, h, i: (b * per_b + i, h)),
                  pl.BlockSpec((seqlen, HEAD_DIM), lambda b, h, i: (b, h)),
                  pl.BlockSpec((seqlen, HEAD_DIM), lambda b, h, i: (b, heads + h)),
                  pl.BlockSpec((1, 1, seqlen), lambda b, h, i: (b * hp + h, 0, 0)),
                  pl.BlockSpec((tq, HEAD_DIM), lambda b, h, i: (b * per_b + i, heads + h))],
        out_specs=pl.BlockSpec((tq, HEAD_DIM), lambda b, h, i: (b * per_b + i, h)),
        out_shape=jax.ShapeDtypeStruct((m, heads * HEAD_DIM), BF16),
        scratch_shapes=[pltpu.VMEM((tq, 1), F32),
                        pltpu.VMEM((tq, 1), F32),
                        pltpu.VMEM((tq, HEAD_DIM), F32)],
        compiler_params=_params(("parallel", "parallel", "arbitrary")),
        name="fox_attn",
    )(proj, kv, kv, f3, proj)


def kernel(x, mem, pre_norm_g, post_norm_g, w_in_a, lam_re, lam_im, log_step, b_re, b_im,
           c_re, c_im, d_skip, w_glu, b_glu, kv_norm_g, w_kv, w_fgate, b_fgate, w_in_b,
           mem_norm_g, w_mem_kv, w_out):
    bsz, seqlen, d = x.shape
    n_mem = mem.shape[1]
    main_w = w_glu.shape[1]
    mem_w = w_out.shape[1] - main_w
    heads = main_w // HEAD_DIM
    scale = HEAD_DIM ** -0.5
    x2d = x.reshape(bsz * seqlen, d)
    mem2d = mem.reshape(bsz * n_mem, d)

    wa = w_in_a[0]
    w_u = wa[:, :main_w].astype(BF16)
    w_rest = jnp.concatenate(
        [wa[:, main_w:2 * main_w], wa[:, 2 * main_w:2 * main_w + mem_w] * scale,
         wa[:, 2 * main_w + mem_w:]], axis=1).astype(BF16)
    u_tm = _norm_matmul(x2d, pre_norm_g[0], w_u, out_dtype=F32, tm=1024, tn=512,
                        time_major_batches=bsz, name="in_proj_a_u")
    rest_a = _norm_matmul(x2d, pre_norm_g[0], w_rest, out_dtype=BF16, tm=1024, tn=512,
                          name="in_proj_a_rest")
    kvm0 = _norm_matmul(mem2d, mem_norm_g[0], w_mem_kv[0].astype(BF16), out_dtype=BF16,
                        tm=512, tn=512, name="mem_kv0")
    memo0 = _mem_attn(rest_a, main_w // mem_w, main_w // mem_w + 1, kvm0,
                      bsz=bsz, tq=512, name="mem_attn0")
    bcat, ccat, ar, ai = _s5_discretise(lam_re[0], lam_im[0], log_step[0], b_re[0], b_im[0],
                                        c_re[0], c_im[0])
    yg_tm = _s5(u_tm, bcat, ccat, ar, ai, d_skip[0], bsz=bsz, t_chunk=128)
    h1 = _glu_out(yg_tm, rest_a, memo0, x2d, w_glu[0], b_glu[0], w_out[0], post_norm_g[0],
                  bsz=bsz, tl=256)

    kv = _norm_matmul(h1, kv_norm_g, w_kv.astype(BF16), out_dtype=BF16, tm=1024, tn=512,
                      name="kv_proj")
    fcum = _fgate(h1, kv_norm_g, w_fgate, b_fgate, bsz=bsz, tl=512)

    wb = w_in_b[0]
    w_b = jnp.concatenate(
        [wb[:, :main_w] * scale, wb[:, main_w:2 * main_w],
         wb[:, 2 * main_w:2 * main_w + mem_w] * scale, wb[:, 2 * main_w + mem_w:]],
        axis=1).astype(BF16)
    proj_b = _norm_matmul(h1, pre_norm_g[1], w_b, out_dtype=BF16, tm=1024, tn=512,
                          name="in_proj_b")
    kvm1 = _norm_matmul(mem2d, mem_norm_g[1], w_mem_kv[1].astype(BF16), out_dtype=BF16,
                        tm=512, tn=512, name="mem_kv1")
    memo1 = _mem_attn(proj_b, 2 * main_w // mem_w, 2 * main_w // mem_w + 1, kvm1,
                      bsz=bsz, tq=512, name="mem_attn1")
    att = _fox_attn(proj_b, kv, fcum, bsz=bsz, heads=heads, tq=512)
    out = _out_proj(att, memo1, h1, w_out[1], post_norm_g[1], tl=512)
    return out.reshape(bsz, seqlen, d)
```

```python
import functools
import math

import jax
import jax.numpy as jnp
from jax import lax
from jax.experimental import pallas as pl
from jax.experimental.pallas import tpu as pltpu

F32 = jnp.float32
BF16 = jnp.bfloat16

EPS = 1e-6
LOG2E = 1.4426950408889634
HEAD_DIM = 128
SSM_GROUP = 16
SSM_STATE = 64
MEM_HEADS = 4
LANES = 128
SUBLANES = 8
GROUPS_PER_BLOCK = LANES // SSM_GROUP
STATE_COLS = GROUPS_PER_BLOCK * SSM_STATE
VMEM_LIMIT = 56 * 1024 * 1024


def _params(sem, vmem=VMEM_LIMIT):
    return pltpu.CompilerParams(dimension_semantics=sem, vmem_limit_bytes=vmem)


def _norm_matmul_kernel(x_ref, g_ref, w_ref, o_ref, xn_ref):
    @pl.when(pl.program_id(1) == 0)
    def _():
        x = x_ref[...]
        ms = jnp.mean(x * x, axis=-1, keepdims=True)
        xn_ref[...] = (x * lax.rsqrt(ms + EPS) * g_ref[...]).astype(BF16)

    o_ref[...] = jnp.dot(xn_ref[...], w_ref[...],
                         preferred_element_type=F32).astype(o_ref.dtype)


def _norm_matmul(x2d, g, w, *, out_dtype, tm, tn, time_major_batches=None, name):
    m, d = x2d.shape
    n = w.shape[1]
    assert m % tm == 0 and n % tn == 0
    nj = n // tn
    if time_major_batches is None:
        out_shape = jax.ShapeDtypeStruct((m, n), out_dtype)
        out_spec = pl.BlockSpec((tm, tn), lambda i, j: (i, j))
    else:
        bsz = time_major_batches
        seqlen = m // bsz
        assert seqlen % tm == 0
        per_b = seqlen // tm
        out_shape = jax.ShapeDtypeStruct((seqlen, bsz * n), out_dtype)
        out_spec = pl.BlockSpec((tm, tn), lambda i, j: (i % per_b, (i // per_b) * nj + j))
    return pl.pallas_call(
        _norm_matmul_kernel,
        grid=(m // tm, nj),
        in_specs=[pl.BlockSpec((tm, d), lambda i, j: (i, 0)),
                  pl.BlockSpec((1, d), lambda i, j: (0, 0)),
                  pl.BlockSpec((d, tn), lambda i, j: (0, j))],
        out_specs=out_spec,
        out_shape=out_shape,
        scratch_shapes=[pltpu.VMEM((tm, d), BF16)],
        compiler_params=_params(("parallel", "arbitrary")),
        name=name,
    )(x2d, g.reshape(1, d), w)


def _split2(x):
    hi = x.astype(BF16)
    return hi, (x - hi.astype(F32)).astype(BF16)


def _split3(x):
    hi = x.astype(BF16)
    r = x - hi.astype(F32)
    mid = r.astype(BF16)
    return hi, mid, (r - mid.astype(F32)).astype(BF16)


def _fgate_kernel(x_ref, g_ref, whi_ref, wlo_ref, b_ref, o_ref, carry_ref):
    @pl.when(pl.program_id(1) == 0)
    def _():
        carry_ref[...] = jnp.zeros_like(carry_ref)

    x = x_ref[...]
    tl = x.shape[0]
    ms = jnp.mean(x * x, axis=-1, keepdims=True)
    xn = x * lax.rsqrt(ms + EPS) * g_ref[...]
    xh, xl = _split2(xn)
    whi = whi_ref[...]
    logit = (jnp.dot(xh, whi, preferred_element_type=F32)
             + jnp.dot(xl, whi, preferred_element_type=F32)
             + jnp.dot(xh, wlo_ref[...], preferred_element_type=F32)) + b_ref[...]
    logf = jnp.minimum(logit, 0.0) - jnp.log(1.0 + jnp.exp(-jnp.abs(logit)))
    t_idx = lax.broadcasted_iota(jnp.int32, (tl, tl), 0)
    s_idx = lax.broadcasted_iota(jnp.int32, (tl, tl), 1)
    tril = (s_idx <= t_idx).astype(BF16)
    csum = carry_ref[...]
    for part in _split3(logf):
        csum = csum + jnp.dot(tril, part, preferred_element_type=F32)
    carry_ref[...] = csum[tl - 1:tl, :]
    for n, part in enumerate(_split3(csum * LOG2E)):
        o_ref[:, n * LANES:(n + 1) * LANES] = part


def _fgate(x2d, g, w_fgate, b_fgate, *, bsz, tl):
    m, d = x2d.shape
    seqlen = m // bsz
    heads = w_fgate.shape[1]
    assert heads <= LANES and seqlen % tl == 0
    wpad = jnp.zeros((d, LANES), F32).at[:, :heads].set(w_fgate.astype(F32))
    whi, wlo = _split2(wpad)
    bb = jnp.zeros((1, LANES), F32).at[0, :heads].set(b_fgate.astype(F32))
    per_b = seqlen // tl
    return pl.pallas_call(
        _fgate_kernel,
        grid=(bsz, per_b),
        in_specs=[pl.BlockSpec((tl, d), lambda b, i: (b * per_b + i, 0)),
                  pl.BlockSpec((1, d), lambda b, i: (0, 0)),
                  pl.BlockSpec((d, LANES), lambda b, i: (0, 0)),
                  pl.BlockSpec((d, LANES), lambda b, i: (0, 0)),
                  pl.BlockSpec((1, LANES), lambda b, i: (0, 0))],
        out_specs=pl.BlockSpec((tl, 3 * LANES), lambda b, i: (b * per_b + i, 0)),
        out_shape=jax.ShapeDtypeStruct((m, 3 * LANES), BF16),
        scratch_shapes=[pltpu.VMEM((1, LANES), F32)],
        compiler_params=_params(("parallel", "arbitrary")),
        name="fgate_cumsum",
    )(x2d, g.reshape(1, d), whi, wlo, bb)


def _mem_attn_kernel(q_ref, zm_ref, kv_ref, o_ref):
    width = MEM_HEADS * HEAD_DIM
    for h in range(MEM_HEADS):
        lo, hi = h * HEAD_DIM, (h + 1) * HEAD_DIM
        q = q_ref[:, lo:hi]
        k = kv_ref[:, lo:hi]
        v = kv_ref[:, width + lo:width + hi]
        s = lax.dot_general(q, k, (((1,), (1,)), ((), ())), preferred_element_type=F32)
        m = jnp.max(s, axis=-1, keepdims=True)
        p = jnp.exp(s - m)
        l = jnp.sum(p, axis=-1, keepdims=True)
        o = jnp.dot(p.astype(BF16), v, preferred_element_type=F32) / l
        zm = zm_ref[:, lo:hi].astype(F32)
        o_ref[:, lo:hi] = (o * (zm * jax.nn.sigmoid(zm))).astype(o_ref.dtype)


def _mem_attn(proj, q_blk, zm_blk, kvm, *, bsz, tq, name):
    m = proj.shape[0]
    seqlen = m // bsz
    n_mem = kvm.shape[0] // bsz
    width = MEM_HEADS * HEAD_DIM
    per_b = seqlen // tq
    return pl.pallas_call(
        _mem_attn_kernel,
        grid=(bsz, per_b),
        in_specs=[pl.BlockSpec((tq, width), lambda b, i: (b * per_b + i, q_blk)),
                  pl.BlockSpec((tq, width), lambda b, i: (b * per_b + i, zm_blk)),
                  pl.BlockSpec((n_mem, 2 * width), lambda b, i: (b, 0))],
        out_specs=pl.BlockSpec((tq, width), lambda b, i: (b * per_b + i, 0)),
        out_shape=jax.ShapeDtypeStruct((m, width), BF16),
        compiler_params=_params(("parallel", "arbitrary")),
        name=name,
    )(proj, proj, kvm)


def _s5_kernel(u_ref, bcat_ref, ccat_ref, ar_ref, ai_ref, d_ref, y_ref,
               bu_ref, xs_ref, st_ref, *, t_chunk):
    @pl.when(pl.program_id(1) == 0)
    def _():
        st_ref[...] = jnp.zeros_like(st_ref)

    rows = t_chunk * SUBLANES
    u = u_ref[...].reshape(rows, LANES)
    bu_ref[...] = jnp.dot(u.astype(BF16), bcat_ref[0], preferred_element_type=F32)
    ar = jnp.broadcast_to(ar_ref[0], (SUBLANES, STATE_COLS))
    ai = jnp.broadcast_to(ai_ref[0], (SUBLANES, STATE_COLS))

    def body(t, carry):
        xr, xi = carry
        r0 = pl.multiple_of(t * SUBLANES, SUBLANES)
        bur = bu_ref[pl.ds(r0, SUBLANES), 0:STATE_COLS]
        bui = bu_ref[pl.ds(r0, SUBLANES), STATE_COLS:2 * STATE_COLS]
        nxr = ar * xr - ai * xi + bur
        nxi = ar * xi + ai * xr + bui
        xs_ref[pl.ds(r0, SUBLANES), 0:STATE_COLS] = nxr
        xs_ref[pl.ds(r0, SUBLANES), STATE_COLS:2 * STATE_COLS] = nxi
        return nxr, nxi

    xr, xi = lax.fori_loop(0, t_chunk, body, (st_ref[0], st_ref[1]), unroll=8)
    st_ref[0] = xr
    st_ref[1] = xi
    y = jnp.dot(xs_ref[...].astype(BF16), ccat_ref[0], preferred_element_type=F32)
    y = y + d_ref[...] * u
    y_ref[...] = jax.nn.gelu(y).reshape(t_chunk, SUBLANES, LANES)


def _s5_discretise(lam_re, lam_im, log_step, b_re, b_im, c_re, c_im):
    groups = lam_re.shape[0]
    nblk = groups // GROUPS_PER_BLOCK
    lr = lam_re.astype(F32)
    li = lam_im.astype(F32)
    dt = jnp.exp(log_step.astype(F32))[:, None]
    mag = jnp.exp(lr * dt)
    ar = mag * jnp.cos(li * dt)
    ai = mag * jnp.sin(li * dt)
    den = lr * lr + li * li
    cr = ((ar - 1.0) * lr + ai * li) / den
    ci = (ai * lr - (ar - 1.0) * li) / den
    br = b_re.astype(F32)
    bi = b_im.astype(F32)
    bbar_re = cr[..., None] * br - ci[..., None] * bi
    bbar_im = cr[..., None] * bi + ci[..., None] * br
    eye = jnp.eye(GROUPS_PER_BLOCK, dtype=F32)

    def blockdiag_in(bbar):
        b4 = bbar.reshape(nblk, GROUPS_PER_BLOCK, SSM_STATE, SSM_GROUP)
        out = jnp.einsum('jgph,gk->jghkp', b4, eye)
        return out.reshape(nblk, LANES, STATE_COLS)

    def blockdiag_out(c):
        c4 = c.reshape(nblk, GROUPS_PER_BLOCK, SSM_GROUP, SSM_STATE)
        out = jnp.einsum('jghp,gk->jgpkh', c4, eye)
        return out.reshape(nblk, STATE_COLS, LANES)

    bcat = jnp.concatenate([blockdiag_in(bbar_re), blockdiag_in(bbar_im)], axis=2)
    ccat = jnp.concatenate([blockdiag_out(c_re.astype(F32)),
                            -blockdiag_out(c_im.astype(F32))], axis=1)
    return (bcat.astype(BF16), ccat.astype(BF16),
            ar.reshape(nblk, 1, STATE_COLS), ai.reshape(nblk, 1, STATE_COLS))


def _s5(u_tm, bcat, ccat, ar, ai, d_skip, *, bsz, t_chunk):
    seqlen = u_tm.shape[0]
    width = u_tm.shape[1] // bsz
    assert bsz == SUBLANES and seqlen % t_chunk == 0 and width % LANES == 0
    nblk = width // LANES
    rows = t_chunk * SUBLANES
    u3 = u_tm.reshape(seqlen, bsz, width)
    out = pl.pallas_call(
        functools.partial(_s5_kernel, t_chunk=t_chunk),
        grid=(nblk, seqlen // t_chunk),
        in_specs=[pl.BlockSpec((t_chunk, bsz, LANES), lambda j, c: (c, 0, j)),
                  pl.BlockSpec((1, LANES, 2 * STATE_COLS), lambda j, c: (j, 0, 0)),
                  pl.BlockSpec((1, 2 * STATE_COLS, LANES), lambda j, c: (j, 0, 0)),
                  pl.BlockSpec((1, 1, STATE_COLS), lambda j, c: (j, 0, 0)),
                  pl.BlockSpec((1, 1, STATE_COLS), lambda j, c: (j, 0, 0)),
                  pl.BlockSpec((1, LANES), lambda j, c: (0, j))],
        out_specs=pl.BlockSpec((t_chunk, bsz, LANES), lambda j, c: (c, 0, j)),
        out_shape=jax.ShapeDtypeStruct((seqlen, bsz, width), F32),
        scratch_shapes=[pltpu.VMEM((rows, 2 * STATE_COLS), F32),
                        pltpu.VMEM((rows, 2 * STATE_COLS), F32),
                        pltpu.VMEM((2, SUBLANES, STATE_COLS), F32)],
        compiler_params=_params(("parallel", "arbitrary")),
        name="s5_scan",
    )(u3, bcat, ccat, ar, ai, d_skip.reshape(1, width).astype(F32))
    return out.reshape(seqlen, bsz * width)


def _post(o, g_ref, h_ref, out_ref):
    ms = jnp.mean(o * o, axis=-1, keepdims=True)
    out_ref[...] = h_ref[...] + o * lax.rsqrt(ms + EPS) * g_ref[...]


def _glu_out_kernel(yg_ref, z_ref, memo_ref, h_ref, wglu_ref, bglu_ref,
                    wmain_ref, wmem_ref, g_ref, out_ref):
    yg = yg_ref[...]
    t = jnp.dot(yg.astype(BF16), wglu_ref[...], preferred_element_type=F32) + bglu_ref[...]
    z = z_ref[...].astype(F32)
    main = yg * jax.nn.sigmoid(t) * (z * jax.nn.sigmoid(z))
    o = jnp.dot(main.astype(BF16), wmain_ref[...], preferred_element_type=F32)
    o = o + jnp.dot(memo_ref[...], wmem_ref[...], preferred_element_type=F32)
    _post(o, g_ref, h_ref, out_ref)


def _out_kernel(main_ref, memo_ref, h_ref, wmain_ref, wmem_ref, g_ref, out_ref):
    o = jnp.dot(main_ref[...], wmain_ref[...], preferred_element_type=F32)
    o = o + jnp.dot(memo_ref[...], wmem_ref[...], preferred_element_type=F32)
    _post(o, g_ref, h_ref, out_ref)


def _resident(shape):
    return pl.BlockSpec(shape, lambda *_: (0,) * len(shape), pipeline_mode=pl.Buffered(1))


def _glu_out(yg_tm, rest, memo, h2d, w_glu, b_glu, w_out, g, *, bsz, tl):
    m, d = h2d.shape
    seqlen = m // bsz
    main_w = w_glu.shape[0]
    mem_w = w_out.shape[0] - main_w
    per_b = seqlen // tl
    row = lambda b, i: (b * per_b + i, 0)
    return pl.pallas_call(
        _glu_out_kernel,
        grid=(bsz, per_b),
        in_specs=[pl.BlockSpec((tl, main_w), lambda b, i: (i, b)),
                  pl.BlockSpec((tl, main_w), row),
                  pl.BlockSpec((tl, mem_w), row),
                  pl.BlockSpec((tl, d), row),
                  _resident((main_w, main_w)),
                  _resident((1, main_w)),
                  _resident((main_w, d)),
                  _resident((mem_w, d)),
                  _resident((1, d))],
        out_specs=pl.BlockSpec((tl, d), row),
        out_shape=jax.ShapeDtypeStruct((m, d), F32),
        compiler_params=_params(("parallel", "arbitrary")),
        name="glu_out_proj",
    )(yg_tm, rest, memo, h2d, w_glu.astype(BF16), b_glu.reshape(1, main_w).astype(F32),
      w_out[:main_w].astype(BF16), w_out[main_w:].astype(BF16), g.reshape(1, d).astype(F32))


def _out_proj(main, memo, h2d, w_out, g, *, tl):
    m, d = h2d.shape
    main_w = main.shape[1]
    mem_w = memo.shape[1]
    row = lambda i: (i, 0)
    return pl.pallas_call(
        _out_kernel,
        grid=(m // tl,),
        in_specs=[pl.BlockSpec((tl, main_w), row),
                  pl.BlockSpec((tl, mem_w), row),
                  pl.BlockSpec((tl, d), row),
                  _resident((main_w, d)),
                  _resident((mem_w, d)),
                  _resident((1, d))],
        out_specs=pl.BlockSpec((tl, d), row),
        out_shape=jax.ShapeDtypeStruct((m, d), F32),
        compiler_params=_params(("parallel",)),
        name="out_proj",
    )(main, memo, h2d, w_out[:main_w].astype(BF16), w_out[main_w:].astype(BF16),
      g.reshape(1, d).astype(F32))


def _fox_kernel(q_ref, k_ref, v_ref, fp_ref, rep_ref, z_ref, o_ref,
                fkb_ref, vt_ref, m_ref, l_ref, acc_ref, *, tq):
    qi = pl.program_id(2)

    @pl.when(qi == 0)
    def _():
        fkb_ref[...] = jnp.dot(fp_ref[...], rep_ref[0], preferred_element_type=F32)
        vt_ref[...] = v_ref[...].astype(F32).T.astype(BF16)

    q = q_ref[...]
    q0 = pl.multiple_of(qi * tq, tq)
    fq = fkb_ref[pl.ds(q0, tq), :].T[0:1, :]
    key = lax.broadcasted_iota(jnp.int32, (tq, tq), 0)
    qry = lax.broadcasted_iota(jnp.int32, (tq, tq), 1)
    m_ref[...] = jnp.full_like(m_ref, -jnp.inf)
    l_ref[...] = jnp.zeros_like(l_ref)
    acc_ref[...] = jnp.zeros_like(acc_ref)

    def step(ki, masked):
        k0 = pl.multiple_of(ki * tq, tq)
        st = lax.dot_general(k_ref[pl.ds(k0, tq), :], q, (((1,), (1,)), ((), ())),
                             preferred_element_type=F32)
        fkb = fkb_ref[pl.ds(k0, tq), :]
        st = st - jnp.concatenate([fkb] * (tq // LANES), axis=1)
        if masked:
            st = jnp.where(key <= qry, st, -jnp.inf)
        m_old = m_ref[...]
        m_new = jnp.maximum(m_old, jnp.max(st, axis=0, keepdims=True) + fq)
        alpha = jnp.exp2(m_old - m_new)
        pt = jnp.exp2(st + (fq - m_new))
        l_ref[...] = alpha * l_ref[...] + jnp.sum(pt, axis=0, keepdims=True)
        acc_ref[...] = alpha * acc_ref[...] + jnp.dot(
            vt_ref[:, pl.ds(k0, tq)], pt.astype(BF16), preferred_element_type=F32)
        m_ref[...] = m_new

    def body(ki, c):
        step(ki, False)
        return c

    lax.fori_loop(0, qi, body, 0)
    step(qi, True)
    z = z_ref[...].astype(F32)
    o = (acc_ref[...] / l_ref[...]).T
    o_ref[...] = (o * (z * jax.nn.sigmoid(z))).astype(o_ref.dtype)


def _fox_attn(proj, kv, fparts, *, bsz, heads, tq):
    m = proj.shape[0]
    seqlen = m // bsz
    per_b = seqlen // tq
    nparts = fparts.shape[1] // LANES
    sel = (jnp.arange(nparts * LANES)[None, :] % LANES) == jnp.arange(heads)[:, None]
    rep = jnp.broadcast_to(sel[:, :, None], (heads, nparts * LANES, LANES)).astype(BF16)
    return pl.pallas_call(
        functools.partial(_fox_kernel, tq=tq),
        grid=(bsz, heads, per_b),
        in_specs=[pl.BlockSpec((tq, HEAD_DIM), lambda b, h, i: (b * per_b + i, h)),
                  pl.BlockSpec((seqlen, HEAD_DIM), lambda b, h, i: (b, h)),
                  pl.BlockSpec((seqlen, HEAD_DIM), lambda b, h, i: (b, heads + h)),
                  pl.BlockSpec((seqlen, nparts * LANES), lambda b, h, i: (b, 0)),
                  pl.BlockSpec((1, nparts * LANES, LANES), lambda b, h, i: (h, 0, 0)),
                  pl.BlockSpec((tq, HEAD_DIM), lambda b, h, i: (b * per_b + i, heads + h))],
        out_specs=pl.BlockSpec((tq, HEAD_DIM), lambda b, h, i: (b * per_b + i, h)),
        out_shape=jax.ShapeDtypeStruct((m, heads * HEAD_DIM), BF16),
        scratch_shapes=[pltpu.VMEM((seqlen, LANES), F32),
                        pltpu.VMEM((HEAD_DIM, seqlen), BF16),
                        pltpu.VMEM((1, tq), F32),
                        pltpu.VMEM((1, tq), F32),
                        pltpu.VMEM((HEAD_DIM, tq), F32)],
        compiler_params=_params(("arbitrary", "arbitrary", "arbitrary")),
        name="fox_attn",
    )(proj, kv, kv, fparts, rep, proj)


def kernel(x, mem, pre_norm_g, post_norm_g, w_in_a, lam_re, lam_im, log_step, b_re, b_im,
           c_re, c_im, d_skip, w_glu, b_glu, kv_norm_g, w_kv, w_fgate, b_fgate, w_in_b,
           mem_norm_g, w_mem_kv, w_out):
    bsz, seqlen, d = x.shape
    n_mem = mem.shape[1]
    main_w = w_glu.shape[1]
    mem_w = w_out.shape[1] - main_w
    heads = main_w // HEAD_DIM
    scale = HEAD_DIM ** -0.5
    x2d = x.reshape(bsz * seqlen, d)
    mem2d = mem.reshape(bsz * n_mem, d)

    wa = w_in_a[0]
    w_u = wa[:, :main_w].astype(BF16)
    w_rest = jnp.concatenate(
        [wa[:, main_w:2 * main_w], wa[:, 2 * main_w:2 * main_w + mem_w] * scale,
         wa[:, 2 * main_w + mem_w:]], axis=1).astype(BF16)
    u_tm = _norm_matmul(x2d, pre_norm_g[0], w_u, out_dtype=F32, tm=1024, tn=512,
                        time_major_batches=bsz, name="in_proj_a_u")
    rest_a = _norm_matmul(x2d, pre_norm_g[0], w_rest, out_dtype=BF16, tm=1024, tn=512,
                          name="in_proj_a_rest")
    kvm0 = _norm_matmul(mem2d, mem_norm_g[0], w_mem_kv[0].astype(BF16), out_dtype=BF16,
                        tm=512, tn=512, name="mem_kv0")
    memo0 = _mem_attn(rest_a, main_w // mem_w, main_w // mem_w + 1, kvm0,
                      bsz=bsz, tq=512, name="mem_attn0")
    bcat, ccat, ar, ai = _s5_discretise(lam_re[0], lam_im[0], log_step[0], b_re[0], b_im[0],
                                        c_re[0], c_im[0])
    yg_tm = _s5(u_tm, bcat, ccat, ar, ai, d_skip[0], bsz=bsz, t_chunk=128)
    h1 = _glu_out(yg_tm, rest_a, memo0, x2d, w_glu[0], b_glu[0], w_out[0], post_norm_g[0],
                  bsz=bsz, tl=256)

    kv = _norm_matmul(h1, kv_norm_g, w_kv.astype(BF16), out_dtype=BF16, tm=1024, tn=512,
                      name="kv_proj")
    fparts = _fgate(h1, kv_norm_g, w_fgate, b_fgate, bsz=bsz, tl=512)

    wb = w_in_b[0]
    w_b = jnp.concatenate(
        [wb[:, :main_w] * (scale * LOG2E), wb[:, main_w:2 * main_w],
         wb[:, 2 * main_w:2 * main_w + mem_w] * scale, wb[:, 2 * main_w + mem_w:]],
        axis=1).astype(BF16)
    proj_b = _norm_matmul(h1, pre_norm_g[1], w_b, out_dtype=BF16, tm=1024, tn=512,
                          name="in_proj_b")
    kvm1 = _norm_matmul(mem2d, mem_norm_g[1], w_mem_kv[1].astype(BF16), out_dtype=BF16,
                        tm=512, tn=512, name="mem_kv1")
    memo1 = _mem_attn(proj_b, 2 * main_w // mem_w, 2 * main_w // mem_w + 1, kvm1,
                      bsz=bsz, tq=512, name="mem_attn1")
    att = _fox_attn(proj_b, kv, fparts, bsz=bsz, heads=heads, tq=512)
    out = _out_proj(att, memo1, h1, w_out[1], post_norm_g[1], tl=512)
    return out.reshape(bsz, seqlen, d)
```

```python
import functools
import math

import jax
import jax.numpy as jnp
from jax import lax
from jax.experimental import pallas as pl
from jax.experimental.pallas import tpu as pltpu

F32 = jnp.float32
BF16 = jnp.bfloat16

EPS = 1e-6
LOG2E = 1.4426950408889634
HEAD_DIM = 128
SSM_GROUP = 16
SSM_STATE = 64
MEM_HEADS = 4
LANES = 128
SUBLANES = 8
GROUPS_PER_BLOCK = LANES // SSM_GROUP
STATE_COLS = GROUPS_PER_BLOCK * SSM_STATE
VMEM_LIMIT = 56 * 1024 * 1024


def _params(sem, vmem=VMEM_LIMIT):
    return pltpu.CompilerParams(dimension_semantics=sem, vmem_limit_bytes=vmem)


def _norm_matmul_kernel(x_ref, g_ref, w_ref, o_ref, xn_ref):
    @pl.when(pl.program_id(1) == 0)
    def _():
        x = x_ref[...]
        ms = jnp.mean(x * x, axis=-1, keepdims=True)
        xn_ref[...] = (x * lax.rsqrt(ms + EPS) * g_ref[...]).astype(BF16)

    o_ref[...] = jnp.dot(xn_ref[...], w_ref[...],
                         preferred_element_type=F32).astype(o_ref.dtype)


def _norm_matmul(x2d, g, w, *, out_dtype, tm, tn, time_major_batches=None, name):
    m, d = x2d.shape
    n = w.shape[1]
    assert m % tm == 0 and n % tn == 0
    nj = n // tn
    if time_major_batches is None:
        out_shape = jax.ShapeDtypeStruct((m, n), out_dtype)
        out_spec = pl.BlockSpec((tm, tn), lambda i, j: (i, j))
    else:
        bsz = time_major_batches
        seqlen = m // bsz
        assert seqlen % tm == 0
        per_b = seqlen // tm
        out_shape = jax.ShapeDtypeStruct((seqlen, bsz * n), out_dtype)
        out_spec = pl.BlockSpec((tm, tn), lambda i, j: (i % per_b, (i // per_b) * nj + j))
    return pl.pallas_call(
        _norm_matmul_kernel,
        grid=(m // tm, nj),
        in_specs=[pl.BlockSpec((tm, d), lambda i, j: (i, 0)),
                  pl.BlockSpec((1, d), lambda i, j: (0, 0)),
                  pl.BlockSpec((d, tn), lambda i, j: (0, j))],
        out_specs=out_spec,
        out_shape=out_shape,
        scratch_shapes=[pltpu.VMEM((tm, d), BF16)],
        compiler_params=_params(("parallel", "arbitrary")),
        name=name,
    )(x2d, g.reshape(1, d), w)


def _prenorm_kernel(x_ref, g_ref, o_ref):
    x = x_ref[...]
    ms = jnp.mean(x * x, axis=-1, keepdims=True)
    o_ref[...] = (x * lax.rsqrt(ms + EPS) * g_ref[...]).astype(o_ref.dtype)


def _prenorm(x2d, g, *, tm):
    m, d = x2d.shape
    return pl.pallas_call(
        _prenorm_kernel,
        grid=(m // tm,),
        in_specs=[pl.BlockSpec((tm, d), lambda i: (i, 0)),
                  pl.BlockSpec((1, d), lambda i: (0, 0))],
        out_specs=pl.BlockSpec((tm, d), lambda i: (i, 0)),
        out_shape=jax.ShapeDtypeStruct((m, d), BF16),
        compiler_params=_params(("parallel",)),
        name="prenorm",
    )(x2d, g.reshape(1, d).astype(F32))


def _matmul_kernel(x_ref, w_ref, cs_ref, o_ref, wb_ref):
    @pl.when(pl.program_id(1) == 0)
    def _():
        wb_ref[...] = (w_ref[...] * cs_ref[...]).astype(BF16)

    o_ref[...] = jnp.dot(x_ref[...], wb_ref[...],
                         preferred_element_type=F32).astype(o_ref.dtype)


def _matmul(xn, w, colscale, *, col0, n, out_dtype, tm, tn, time_major_batches=None, name):
    m, d = xn.shape
    assert m % tm == 0 and n % tn == 0 and col0 % tn == 0
    nj = n // tn
    j0 = col0 // tn
    if time_major_batches is None:
        out_shape = jax.ShapeDtypeStruct((m, n), out_dtype)
        out_spec = pl.BlockSpec((tm, tn), lambda j, i: (i, j))
    else:
        bsz = time_major_batches
        seqlen = m // bsz
        assert seqlen % tm == 0
        per_b = seqlen // tm
        out_shape = jax.ShapeDtypeStruct((seqlen, bsz * n), out_dtype)
        out_spec = pl.BlockSpec((tm, tn), lambda j, i: (i % per_b, (i // per_b) * nj + j))
    return pl.pallas_call(
        _matmul_kernel,
        grid=(nj, m // tm),
        in_specs=[pl.BlockSpec((tm, d), lambda j, i: (i, 0)),
                  pl.BlockSpec((d, tn), lambda j, i: (0, j0 + j), pipeline_mode=pl.Buffered(1)),
                  pl.BlockSpec((1, tn), lambda j, i: (0, j0 + j))],
        out_specs=out_spec,
        out_shape=out_shape,
        scratch_shapes=[pltpu.VMEM((d, tn), BF16)],
        compiler_params=_params(("arbitrary", "arbitrary")),
        name=name,
    )(xn, w, colscale.reshape(1, -1).astype(F32))


def _split2(x):
    hi = x.astype(BF16)
    return hi, (x - hi.astype(F32)).astype(BF16)


def _split3(x):
    hi = x.astype(BF16)
    r = x - hi.astype(F32)
    mid = r.astype(BF16)
    return hi, mid, (r - mid.astype(F32)).astype(BF16)


def _fgate_block(xn, wcat_ref, b_ref, carry_ref, o_ref):
    tl = xn.shape[0]
    xh, xl = _split2(xn)
    both = jnp.dot(xh, wcat_ref[...], preferred_element_type=F32)
    logit = (both[:, :LANES] + both[:, LANES:]
             + jnp.dot(xl, wcat_ref[:, :LANES], preferred_element_type=F32)) + b_ref[...]
    logf = jnp.minimum(logit, 0.0) - jnp.log(1.0 + jnp.exp(-jnp.abs(logit)))
    t_idx = lax.broadcasted_iota(jnp.int32, (tl, tl), 0)
    s_idx = lax.broadcasted_iota(jnp.int32, (tl, tl), 1)
    tril = (s_idx <= t_idx).astype(BF16)
    csum = carry_ref[...]
    for part in _split3(logf):
        csum = csum + jnp.dot(tril, part, preferred_element_type=F32)
    carry_ref[...] = csum[tl - 1:tl, :]
    for n, part in enumerate(_split3(csum * LOG2E)):
        o_ref[:, n * LANES:(n + 1) * LANES] = part


def _mem_attn_kernel(q_ref, zm_ref, kv_ref, o_ref):
    width = MEM_HEADS * HEAD_DIM
    for h in range(MEM_HEADS):
        lo, hi = h * HEAD_DIM, (h + 1) * HEAD_DIM
        q = q_ref[:, lo:hi]
        k = kv_ref[:, lo:hi]
        v = kv_ref[:, width + lo:width + hi]
        s = lax.dot_general(q, k, (((1,), (1,)), ((), ())), preferred_element_type=F32)
        m = jnp.max(s, axis=-1, keepdims=True)
        p = jnp.exp(s - m)
        l = jnp.sum(p, axis=-1, keepdims=True)
        o = jnp.dot(p.astype(BF16), v, preferred_element_type=F32) / l
        zm = zm_ref[:, lo:hi].astype(F32)
        o_ref[:, lo:hi] = (o * (zm * jax.nn.sigmoid(zm))).astype(o_ref.dtype)


def _mem_attn(proj, q_blk, zm_blk, kvm, *, bsz, tq, name):
    m = proj.shape[0]
    seqlen = m // bsz
    n_mem = kvm.shape[0] // bsz
    width = MEM_HEADS * HEAD_DIM
    per_b = seqlen // tq
    return pl.pallas_call(
        _mem_attn_kernel,
        grid=(bsz, per_b),
        in_specs=[pl.BlockSpec((tq, width), lambda b, i: (b * per_b + i, q_blk)),
                  pl.BlockSpec((tq, width), lambda b, i: (b * per_b + i, zm_blk)),
                  pl.BlockSpec((n_mem, 2 * width), lambda b, i: (b, 0))],
        out_specs=pl.BlockSpec((tq, width), lambda b, i: (b * per_b + i, 0)),
        out_shape=jax.ShapeDtypeStruct((m, width), BF16),
        compiler_params=_params(("parallel", "arbitrary")),
        name=name,
    )(proj, proj, kvm)


def _s5_kernel(u_ref, bcat_ref, ccat_ref, ar_ref, ai_ref, d_ref, y_ref,
               bu_ref, xs_ref, st_ref, *, t_chunk):
    @pl.when(pl.program_id(1) == 0)
    def _():
        st_ref[...] = jnp.zeros_like(st_ref)

    rows = t_chunk * SUBLANES
    u = u_ref[...].reshape(rows, LANES)
    bu_ref[...] = jnp.dot(u.astype(BF16), bcat_ref[0], preferred_element_type=F32)
    ar = jnp.broadcast_to(ar_ref[0], (SUBLANES, STATE_COLS))
    ai = jnp.broadcast_to(ai_ref[0], (SUBLANES, STATE_COLS))

    def body(t, carry):
        xr, xi = carry
        r0 = pl.multiple_of(t * SUBLANES, SUBLANES)
        bur = bu_ref[pl.ds(r0, SUBLANES), 0:STATE_COLS]
        bui = bu_ref[pl.ds(r0, SUBLANES), STATE_COLS:2 * STATE_COLS]
        nxr = ar * xr - ai * xi + bur
        nxi = ar * xi + ai * xr + bui
        xs_ref[pl.ds(r0, SUBLANES), 0:STATE_COLS] = nxr
        xs_ref[pl.ds(r0, SUBLANES), STATE_COLS:2 * STATE_COLS] = nxi
        return nxr, nxi

    xr, xi = lax.fori_loop(0, t_chunk, body, (st_ref[0], st_ref[1]), unroll=8)
    st_ref[0] = xr
    st_ref[1] = xi
    y = jnp.dot(xs_ref[...].astype(BF16), ccat_ref[0], preferred_element_type=F32)
    y = y + d_ref[...] * u
    y_ref[...] = jax.nn.gelu(y).reshape(t_chunk, SUBLANES, LANES)


def _s5_discretise(lam_re, lam_im, log_step, b_re, b_im, c_re, c_im):
    groups = lam_re.shape[0]
    nblk = groups // GROUPS_PER_BLOCK
    lr = lam_re.astype(F32)
    li = lam_im.astype(F32)
    dt = jnp.exp(log_step.astype(F32))[:, None]
    mag = jnp.exp(lr * dt)
    ar = mag * jnp.cos(li * dt)
    ai = mag * jnp.sin(li * dt)
    den = lr * lr + li * li
    cr = ((ar - 1.0) * lr + ai * li) / den
    ci = (ai * lr - (ar - 1.0) * li) / den
    br = b_re.astype(F32)
    bi = b_im.astype(F32)
    bbar_re = cr[..., None] * br - ci[..., None] * bi
    bbar_im = cr[..., None] * bi + ci[..., None] * br
    eye = jnp.eye(GROUPS_PER_BLOCK, dtype=F32)

    def blockdiag_in(bbar):
        b4 = bbar.reshape(nblk, GROUPS_PER_BLOCK, SSM_STATE, SSM_GROUP)
        out = jnp.einsum('jgph,gk->jghkp', b4, eye)
        return out.reshape(nblk, LANES, STATE_COLS)

    def blockdiag_out(c):
        c4 = c.reshape(nblk, GROUPS_PER_BLOCK, SSM_GROUP, SSM_STATE)
        out = jnp.einsum('jghp,gk->jgpkh', c4, eye)
        return out.reshape(nblk, STATE_COLS, LANES)

    bcat = jnp.concatenate([blockdiag_in(bbar_re), blockdiag_in(bbar_im)], axis=2)
    ccat = jnp.concatenate([blockdiag_out(c_re.astype(F32)),
                            -blockdiag_out(c_im.astype(F32))], axis=1)
    return (bcat.astype(BF16), ccat.astype(BF16),
            ar.reshape(nblk, 1, STATE_COLS), ai.reshape(nblk, 1, STATE_COLS))


def _s5(u_tm, bcat, ccat, ar, ai, d_skip, *, bsz, t_chunk):
    seqlen = u_tm.shape[0]
    width = u_tm.shape[1] // bsz
    assert bsz == SUBLANES and seqlen % t_chunk == 0 and width % LANES == 0
    nblk = width // LANES
    rows = t_chunk * SUBLANES
    u3 = u_tm.reshape(seqlen, bsz, width)
    out = pl.pallas_call(
        functools.partial(_s5_kernel, t_chunk=t_chunk),
        grid=(nblk, seqlen // t_chunk),
        in_specs=[pl.BlockSpec((t_chunk, bsz, LANES), lambda j, c: (c, 0, j)),
                  pl.BlockSpec((1, LANES, 2 * STATE_COLS), lambda j, c: (j, 0, 0)),
                  pl.BlockSpec((1, 2 * STATE_COLS, LANES), lambda j, c: (j, 0, 0)),
                  pl.BlockSpec((1, 1, STATE_COLS), lambda j, c: (j, 0, 0)),
                  pl.BlockSpec((1, 1, STATE_COLS), lambda j, c: (j, 0, 0)),
                  pl.BlockSpec((1, LANES), lambda j, c: (0, j))],
        out_specs=pl.BlockSpec((t_chunk, bsz, LANES), lambda j, c: (c, 0, j)),
        out_shape=jax.ShapeDtypeStruct((seqlen, bsz, width), F32),
        scratch_shapes=[pltpu.VMEM((rows, 2 * STATE_COLS), F32),
                        pltpu.VMEM((rows, 2 * STATE_COLS), F32),
                        pltpu.VMEM((2, SUBLANES, STATE_COLS), F32)],
        compiler_params=_params(("parallel", "arbitrary")),
        name="s5_scan",
    )(u3, bcat, ccat, ar, ai, d_skip.reshape(1, width).astype(F32))
    return out.reshape(seqlen, bsz * width)


def _post(o, g_ref, h_ref, out_ref):
    ms = jnp.mean(o * o, axis=-1, keepdims=True)
    out_ref[...] = h_ref[...] + o * lax.rsqrt(ms + EPS) * g_ref[...]


def _glu_out_kernel(yg_ref, z_ref, memo_ref, h_ref, wglu_ref, bglu_ref,
                    wmain_ref, wmem_ref, g_ref, gkv_ref, gnext_ref, wf_ref, bf_ref,
                    out_ref, xkv_ref, xnext_ref, fp_ref, carry_ref):
    @pl.when(pl.program_id(1) == 0)
    def _():
        carry_ref[...] = jnp.zeros_like(carry_ref)

    yg = yg_ref[...]
    t = jnp.dot(yg.astype(BF16), wglu_ref[...], preferred_element_type=F32) + bglu_ref[...]
    z = z_ref[...].astype(F32)
    main = yg * jax.nn.sigmoid(t) * (z * jax.nn.sigmoid(z))
    o = jnp.dot(main.astype(BF16), wmain_ref[...], preferred_element_type=F32)
    o = o + jnp.dot(memo_ref[...], wmem_ref[...], preferred_element_type=F32)
    _post(o, g_ref, h_ref, out_ref)
    h1 = out_ref[...]
    r = h1 * lax.rsqrt(jnp.mean(h1 * h1, axis=-1, keepdims=True) + EPS)
    xnext_ref[...] = (r * gnext_ref[...]).astype(BF16)
    xkv = r * gkv_ref[...]
    xkv_ref[...] = xkv.astype(BF16)
    _fgate_block(xkv, wf_ref, bf_ref, carry_ref, fp_ref)


def _out_kernel(main_ref, memo_ref, h_ref, wmain_ref, wmem_ref, g_ref, out_ref):
    o = jnp.dot(main_ref[...], wmain_ref[...], preferred_element_type=F32)
    o = o + jnp.dot(memo_ref[...], wmem_ref[...], preferred_element_type=F32)
    _post(o, g_ref, h_ref, out_ref)


def _resident(shape):
    return pl.BlockSpec(shape, lambda *_: (0,) * len(shape), pipeline_mode=pl.Buffered(1))


def _glu_out(yg_tm, rest, memo, h2d, w_glu, b_glu, w_out, g, g_kv, g_next, w_fgate, b_fgate,
             *, bsz, tl):
    m, d = h2d.shape
    seqlen = m // bsz
    main_w = w_glu.shape[0]
    mem_w = w_out.shape[0] - main_w
    heads = w_fgate.shape[1]
    assert heads <= LANES and seqlen % tl == 0
    wpad = jnp.zeros((d, LANES), F32).at[:, :heads].set(w_fgate.astype(F32))
    wcat = jnp.concatenate(_split2(wpad), axis=1)
    bpad = jnp.zeros((1, LANES), F32).at[0, :heads].set(b_fgate.astype(F32))
    per_b = seqlen // tl
    row = lambda b, i: (b * per_b + i, 0)
    vec = lambda v: v.reshape(1, -1).astype(F32)
    return pl.pallas_call(
        _glu_out_kernel,
        grid=(bsz, per_b),
        in_specs=[pl.BlockSpec((tl, main_w), lambda b, i: (i, b)),
                  pl.BlockSpec((tl, main_w), row),
                  pl.BlockSpec((tl, mem_w), row),
                  pl.BlockSpec((tl, d), row),
                  _resident((main_w, main_w)),
                  _resident((1, main_w)),
                  _resident((main_w, d)),
                  _resident((mem_w, d)),
                  _resident((1, d)),
                  _resident((1, d)),
                  _resident((1, d)),
                  _resident((d, 2 * LANES)),
                  _resident((1, LANES))],
        out_specs=[pl.BlockSpec((tl, d), row),
                   pl.BlockSpec((tl, d), row),
                   pl.BlockSpec((tl, d), row),
                   pl.BlockSpec((tl, 3 * LANES), row)],
        out_shape=[jax.ShapeDtypeStruct((m, d), F32),
                   jax.ShapeDtypeStruct((m, d), BF16),
                   jax.ShapeDtypeStruct((m, d), BF16),
                   jax.ShapeDtypeStruct((m, 3 * LANES), BF16)],
        scratch_shapes=[pltpu.VMEM((1, LANES), F32)],
        compiler_params=_params(("arbitrary", "arbitrary")),
        name="glu_out_proj",
    )(yg_tm, rest, memo, h2d, w_glu.astype(BF16), vec(b_glu),
      w_out[:main_w].astype(BF16), w_out[main_w:].astype(BF16), vec(g), vec(g_kv),
      vec(g_next), wcat, bpad)


def _out_proj(main, memo, h2d, w_out, g, *, tl):
    m, d = h2d.shape
    main_w = main.shape[1]
    mem_w = memo.shape[1]
    row = lambda i: (i, 0)
    return pl.pallas_call(
        _out_kernel,
        grid=(m // tl,),
        in_specs=[pl.BlockSpec((tl, main_w), row),
                  pl.BlockSpec((tl, mem_w), row),
                  pl.BlockSpec((tl, d), row),
                  _resident((main_w, d)),
                  _resident((mem_w, d)),
                  _resident((1, d))],
        out_specs=pl.BlockSpec((tl, d), row),
        out_shape=jax.ShapeDtypeStruct((m, d), F32),
        compiler_params=_params(("parallel",)),
        name="out_proj",
    )(main, memo, h2d, w_out[:main_w].astype(BF16), w_out[main_w:].astype(BF16),
      g.reshape(1, d).astype(F32))


def _fox_kernel(q_ref, k_ref, v_ref, fp_ref, rep_ref, z_ref, o_ref,
                fkb_ref, vt_ref, st_ref, m_ref, l_ref, acc_ref, *, tq):
    seqlen = q_ref.shape[0]
    fkb_ref[...] = jnp.dot(fp_ref[...], rep_ref[0], preferred_element_type=F32)
    vt_ref[...] = v_ref[...].astype(F32).T.astype(BF16)
    key = lax.broadcasted_iota(jnp.int32, (tq, tq), 0)
    qry = lax.broadcasted_iota(jnp.int32, (tq, tq), 1)

    def scores(q, kj, slot):
        st_ref[slot] = lax.dot_general(k_ref[kj * tq:(kj + 1) * tq, :], q,
                                       (((1,), (1,)), ((), ())),
                                       preferred_element_type=F32)

    def softmax_pv(kj, slot, fq, masked):
        fkb = fkb_ref[kj * tq:(kj + 1) * tq, :]
        st = st_ref[slot] - jnp.concatenate([fkb] * (tq // LANES), axis=1)
        if masked:
            st = jnp.where(key <= qry, st, -jnp.inf)
        m_old = m_ref[...]
        m_new = jnp.maximum(m_old, jnp.max(st, axis=0, keepdims=True) + fq)
        alpha = jnp.exp2(m_old - m_new)
        pt = jnp.exp2(st + (fq - m_new))
        l_ref[...] = alpha * l_ref[...] + jnp.sum(pt, axis=0, keepdims=True)
        acc_ref[...] = alpha * acc_ref[...] + jnp.dot(
            vt_ref[:, kj * tq:(kj + 1) * tq], pt.astype(BF16), preferred_element_type=F32)
        m_ref[...] = m_new

    for qi in range(seqlen // tq):
        rows = slice(qi * tq, (qi + 1) * tq)
        q = q_ref[rows, :]
        fq = fkb_ref[rows, :].T[0:1, :]
        m_ref[...] = jnp.full_like(m_ref, -jnp.inf)
        l_ref[...] = jnp.zeros_like(l_ref)
        acc_ref[...] = jnp.zeros_like(acc_ref)
        scores(q, 0, 0)
        for kj in range(qi + 1):
            if kj < qi:
                scores(q, kj + 1, (kj + 1) % 2)
            softmax_pv(kj, kj % 2, fq, masked=(kj == qi))
        z = z_ref[rows, :].astype(F32)
        o = (acc_ref[...] / l_ref[...]).T
        o_ref[rows, :] = (o * (z * jax.nn.sigmoid(z))).astype(o_ref.dtype)


def _fox_attn(proj, kv, fparts, *, bsz, heads, tq):
    m = proj.shape[0]
    seqlen = m // bsz
    assert seqlen % tq == 0
    nparts = fparts.shape[1] // LANES
    sel = (jnp.arange(nparts * LANES)[None, :] % LANES) == jnp.arange(heads)[:, None]
    rep = jnp.broadcast_to(sel[:, :, None], (heads, nparts * LANES, LANES)).astype(BF16)
    return pl.pallas_call(
        functools.partial(_fox_kernel, tq=tq),
        grid=(bsz, heads),
        in_specs=[pl.BlockSpec((seqlen, HEAD_DIM), lambda b, h: (b, h)),
                  pl.BlockSpec((seqlen, HEAD_DIM), lambda b, h: (b, h)),
                  pl.BlockSpec((seqlen, HEAD_DIM), lambda b, h: (b, heads + h)),
                  pl.BlockSpec((seqlen, nparts * LANES), lambda b, h: (b, 0)),
                  pl.BlockSpec((1, nparts * LANES, LANES), lambda b, h: (h, 0, 0)),
                  pl.BlockSpec((seqlen, HEAD_DIM), lambda b, h: (b, heads + h))],
        out_specs=pl.BlockSpec((seqlen, HEAD_DIM), lambda b, h: (b, h)),
        out_shape=jax.ShapeDtypeStruct((m, heads * HEAD_DIM), BF16),
        scratch_shapes=[pltpu.VMEM((seqlen, LANES), F32),
                        pltpu.VMEM((HEAD_DIM, seqlen), BF16),
                        pltpu.VMEM((2, tq, tq), F32),
                        pltpu.VMEM((1, tq), F32),
                        pltpu.VMEM((1, tq), F32),
                        pltpu.VMEM((HEAD_DIM, tq), F32)],
        compiler_params=_params(("arbitrary", "arbitrary")),
        name="fox_attn",
    )(proj, kv, kv, fparts, rep, proj)


def kernel(x, mem, pre_norm_g, post_norm_g, w_in_a, lam_re, lam_im, log_step, b_re, b_im,
           c_re, c_im, d_skip, w_glu, b_glu, kv_norm_g, w_kv, w_fgate, b_fgate, w_in_b,
           mem_norm_g, w_mem_kv, w_out):
    bsz, seqlen, d = x.shape
    n_mem = mem.shape[1]
    main_w = w_glu.shape[1]
    mem_w = w_out.shape[1] - main_w
    heads = main_w // HEAD_DIM
    scale = HEAD_DIM ** -0.5
    x2d = x.reshape(bsz * seqlen, d)
    mem2d = mem.reshape(bsz * n_mem, d)

    ones = jnp.ones((main_w,), F32)
    mem_scale = jnp.full((mem_w,), scale, F32)
    cs_a = jnp.concatenate([ones, ones, mem_scale, ones[:mem_w]])
    cs_b = jnp.concatenate([ones * (scale * LOG2E), ones, mem_scale, ones[:mem_w]])
    xa = _prenorm(x2d, pre_norm_g[0], tm=512)
    u_tm = _matmul(xa, w_in_a[0], cs_a, col0=0, n=main_w, out_dtype=F32, tm=1024, tn=768,
                   time_major_batches=bsz, name="in_proj_a_u")
    rest_a = _matmul(xa, w_in_a[0], cs_a, col0=main_w, n=main_w + 2 * mem_w, out_dtype=BF16,
                     tm=1024, tn=512, name="in_proj_a_rest")
    kvm0 = _norm_matmul(mem2d, mem_norm_g[0], w_mem_kv[0].astype(BF16), out_dtype=BF16,
                        tm=512, tn=512, name="mem_kv0")
    memo0 = _mem_attn(rest_a, main_w // mem_w, main_w // mem_w + 1, kvm0,
                      bsz=bsz, tq=512, name="mem_attn0")
    bcat, ccat, ar, ai = _s5_discretise(lam_re[0], lam_im[0], log_step[0], b_re[0], b_im[0],
                                        c_re[0], c_im[0])
    yg_tm = _s5(u_tm, bcat, ccat, ar, ai, d_skip[0], bsz=bsz, t_chunk=128)
    h1, xkv, xb, fparts = _glu_out(
        yg_tm, rest_a, memo0, x2d, w_glu[0], b_glu[0], w_out[0], post_norm_g[0],
        kv_norm_g, pre_norm_g[1], w_fgate, b_fgate, bsz=bsz, tl=256)

    kv = _matmul(xkv, w_kv, jnp.ones((w_kv.shape[1],), F32), col0=0, n=w_kv.shape[1],
                 out_dtype=BF16, tm=1024, tn=1024, name="kv_proj")

    proj_b = _matmul(xb, w_in_b[0], cs_b, col0=0, n=w_in_b.shape[2], out_dtype=BF16,
                     tm=1024, tn=1024, name="in_proj_b")
    kvm1 = _norm_matmul(mem2d, mem_norm_g[1], w_mem_kv[1].astype(BF16), out_dtype=BF16,
                        tm=512, tn=512, name="mem_kv1")
    memo1 = _mem_attn(proj_b, 2 * main_w // mem_w, 2 * main_w // mem_w + 1, kvm1,
                      bsz=bsz, tq=512, name="mem_attn1")
    att = _fox_attn(proj_b, kv, fparts, bsz=bsz, heads=heads, tq=512)
    out = _out_proj(att, memo1, h1, w_out[1], post_norm_g[1], tl=512)
    return out.reshape(bsz, seqlen, d)
```

```python
import functools
import math

import jax
import jax.numpy as jnp
from jax import lax
from jax.experimental import pallas as pl
from jax.experimental.pallas import tpu as pltpu

F32 = jnp.float32
BF16 = jnp.bfloat16

EPS = 1e-6
LOG2E = 1.4426950408889634
HEAD_DIM = 128
SSM_GROUP = 16
SSM_STATE = 64
MEM_HEADS = 4
LANES = 128
SUBLANES = 8
GROUPS_PER_BLOCK = LANES // SSM_GROUP
STATE_COLS = GROUPS_PER_BLOCK * SSM_STATE
VMEM_LIMIT = 56 * 1024 * 1024


def _params(sem, vmem=VMEM_LIMIT):
    return pltpu.CompilerParams(dimension_semantics=sem, vmem_limit_bytes=vmem)


def _norm_matmul_kernel(x_ref, g_ref, w_ref, o_ref, xn_ref):
    @pl.when(pl.program_id(1) == 0)
    def _():
        x = x_ref[...]
        ms = jnp.mean(x * x, axis=-1, keepdims=True)
        xn_ref[...] = (x * lax.rsqrt(ms + EPS) * g_ref[...]).astype(BF16)

    o_ref[...] = jnp.dot(xn_ref[...], w_ref[...],
                         preferred_element_type=F32).astype(o_ref.dtype)


def _norm_matmul(x2d, g, w, *, out_dtype, tm, tn, time_major_batches=None, name):
    m, d = x2d.shape
    n = w.shape[1]
    assert m % tm == 0 and n % tn == 0
    nj = n // tn
    if time_major_batches is None:
        out_shape = jax.ShapeDtypeStruct((m, n), out_dtype)
        out_spec = pl.BlockSpec((tm, tn), lambda i, j: (i, j))
    else:
        bsz = time_major_batches
        seqlen = m // bsz
        assert seqlen % tm == 0
        per_b = seqlen // tm
        out_shape = jax.ShapeDtypeStruct((seqlen, bsz * n), out_dtype)
        out_spec = pl.BlockSpec((tm, tn), lambda i, j: (i % per_b, (i // per_b) * nj + j))
    return pl.pallas_call(
        _norm_matmul_kernel,
        grid=(m // tm, nj),
        in_specs=[pl.BlockSpec((tm, d), lambda i, j: (i, 0)),
                  pl.BlockSpec((1, d), lambda i, j: (0, 0)),
                  pl.BlockSpec((d, tn), lambda i, j: (0, j))],
        out_specs=out_spec,
        out_shape=out_shape,
        scratch_shapes=[pltpu.VMEM((tm, d), BF16)],
        compiler_params=_params(("parallel", "arbitrary")),
        name=name,
    )(x2d, g.reshape(1, d), w)


def _prenorm_kernel(x_ref, g_ref, o_ref):
    x = x_ref[...]
    ms = jnp.mean(x * x, axis=-1, keepdims=True)
    o_ref[...] = (x * lax.rsqrt(ms + EPS) * g_ref[...]).astype(o_ref.dtype)


def _prenorm(x2d, g, *, tm):
    m, d = x2d.shape
    return pl.pallas_call(
        _prenorm_kernel,
        grid=(m // tm,),
        in_specs=[pl.BlockSpec((tm, d), lambda i: (i, 0)),
                  pl.BlockSpec((1, d), lambda i: (0, 0))],
        out_specs=pl.BlockSpec((tm, d), lambda i: (i, 0)),
        out_shape=jax.ShapeDtypeStruct((m, d), BF16),
        compiler_params=_params(("parallel",)),
        name="prenorm",
    )(x2d, g.reshape(1, d).astype(F32))


def _matmul_kernel(x_ref, w_ref, cs_ref, o_ref, wb_ref):
    @pl.when(pl.program_id(1) == 0)
    def _():
        wb_ref[...] = (w_ref[...] * cs_ref[...]).astype(BF16)

    o_ref[...] = jnp.dot(x_ref[...], wb_ref[...],
                         preferred_element_type=F32).astype(o_ref.dtype)


def _matmul(xn, w, colscale, *, col0, n, out_dtype, tm, tn, time_major_batches=None, name):
    m, d = xn.shape
    assert m % tm == 0 and n % tn == 0 and col0 % tn == 0
    nj = n // tn
    j0 = col0 // tn
    if time_major_batches is None:
        out_shape = jax.ShapeDtypeStruct((m, n), out_dtype)
        out_spec = pl.BlockSpec((tm, tn), lambda j, i: (i, j))
    else:
        bsz = time_major_batches
        seqlen = m // bsz
        assert seqlen % tm == 0
        per_b = seqlen // tm
        out_shape = jax.ShapeDtypeStruct((seqlen, bsz * n), out_dtype)
        out_spec = pl.BlockSpec((tm, tn), lambda j, i: (i % per_b, (i // per_b) * nj + j))
    return pl.pallas_call(
        _matmul_kernel,
        grid=(nj, m // tm),
        in_specs=[pl.BlockSpec((tm, d), lambda j, i: (i, 0)),
                  pl.BlockSpec((d, tn), lambda j, i: (0, j0 + j), pipeline_mode=pl.Buffered(1)),
                  pl.BlockSpec((1, tn), lambda j, i: (0, j0 + j))],
        out_specs=out_spec,
        out_shape=out_shape,
        scratch_shapes=[pltpu.VMEM((d, tn), BF16)],
        compiler_params=_params(("arbitrary", "arbitrary")),
        name=name,
    )(xn, w, colscale.reshape(1, -1).astype(F32))


def _split2(x):
    hi = x.astype(BF16)
    return hi, (x - hi.astype(F32)).astype(BF16)


def _split3(x):
    hi = x.astype(BF16)
    r = x - hi.astype(F32)
    mid = r.astype(BF16)
    return hi, mid, (r - mid.astype(F32)).astype(BF16)


def _fgate_block(xn, wcat_ref, b_ref, carry_ref, o_ref):
    tl = xn.shape[0]
    xh, xl = _split2(xn)
    both = jnp.dot(xh, wcat_ref[...], preferred_element_type=F32)
    logit = (both[:, :LANES] + both[:, LANES:]
             + jnp.dot(xl, wcat_ref[:, :LANES], preferred_element_type=F32)) + b_ref[...]
    logf = jnp.minimum(logit, 0.0) - jnp.log(1.0 + jnp.exp(-jnp.abs(logit)))
    t_idx = lax.broadcasted_iota(jnp.int32, (tl, tl), 0)
    s_idx = lax.broadcasted_iota(jnp.int32, (tl, tl), 1)
    tril = (s_idx <= t_idx).astype(BF16)
    csum = carry_ref[...]
    for part in _split3(logf):
        csum = csum + jnp.dot(tril, part, preferred_element_type=F32)
    carry_ref[...] = csum[tl - 1:tl, :]
    for n, part in enumerate(_split3(csum * LOG2E)):
        o_ref[:, n * LANES:(n + 1) * LANES] = part


def _mem_attn_kernel(q_ref, zm_ref, kv_ref, o_ref):
    width = MEM_HEADS * HEAD_DIM
    for h in range(MEM_HEADS):
        lo, hi = h * HEAD_DIM, (h + 1) * HEAD_DIM
        q = q_ref[:, lo:hi]
        k = kv_ref[:, lo:hi]
        v = kv_ref[:, width + lo:width + hi]
        s = lax.dot_general(q, k, (((1,), (1,)), ((), ())), preferred_element_type=F32)
        m = jnp.max(s, axis=-1, keepdims=True)
        p = jnp.exp(s - m)
        l = jnp.sum(p, axis=-1, keepdims=True)
        o = jnp.dot(p.astype(BF16), v, preferred_element_type=F32) / l
        zm = zm_ref[:, lo:hi].astype(F32)
        o_ref[:, lo:hi] = (o * (zm * jax.nn.sigmoid(zm))).astype(o_ref.dtype)


def _mem_attn(proj, q_blk, zm_blk, kvm, *, bsz, tq, name):
    m = proj.shape[0]
    seqlen = m // bsz
    n_mem = kvm.shape[0] // bsz
    width = MEM_HEADS * HEAD_DIM
    per_b = seqlen // tq
    return pl.pallas_call(
        _mem_attn_kernel,
        grid=(bsz, per_b),
        in_specs=[pl.BlockSpec((tq, width), lambda b, i: (b * per_b + i, q_blk)),
                  pl.BlockSpec((tq, width), lambda b, i: (b * per_b + i, zm_blk)),
                  pl.BlockSpec((n_mem, 2 * width), lambda b, i: (b, 0))],
        out_specs=pl.BlockSpec((tq, width), lambda b, i: (b * per_b + i, 0)),
        out_shape=jax.ShapeDtypeStruct((m, width), BF16),
        compiler_params=_params(("parallel", "arbitrary")),
        name=name,
    )(proj, proj, kvm)


def _s5_kernel(u_ref, bcat_ref, ccat_ref, ar_ref, ai_ref, d_ref, y_ref,
               bu_ref, xs_ref, st_ref, *, t_chunk):
    @pl.when(pl.program_id(1) == 0)
    def _():
        st_ref[...] = jnp.zeros_like(st_ref)

    rows = t_chunk * SUBLANES
    u = u_ref[...].reshape(rows, LANES)
    bu_ref[...] = jnp.dot(u.astype(BF16), bcat_ref[0], preferred_element_type=F32)
    ar = jnp.broadcast_to(ar_ref[0], (SUBLANES, STATE_COLS))
    ai = jnp.broadcast_to(ai_ref[0], (SUBLANES, STATE_COLS))

    def body(t, carry):
        xr, xi = carry
        r0 = pl.multiple_of(t * SUBLANES, SUBLANES)
        bur = bu_ref[pl.ds(r0, SUBLANES), 0:STATE_COLS]
        bui = bu_ref[pl.ds(r0, SUBLANES), STATE_COLS:2 * STATE_COLS]
        nxr = ar * xr - ai * xi + bur
        nxi = ar * xi + ai * xr + bui
        xs_ref[pl.ds(r0, SUBLANES), 0:STATE_COLS] = nxr
        xs_ref[pl.ds(r0, SUBLANES), STATE_COLS:2 * STATE_COLS] = nxi
        return nxr, nxi

    xr, xi = lax.fori_loop(0, t_chunk, body, (st_ref[0], st_ref[1]), unroll=True)
    st_ref[0] = xr
    st_ref[1] = xi
    y = jnp.dot(xs_ref[...].astype(BF16), ccat_ref[0], preferred_element_type=F32)
    y = y + d_ref[...] * u
    y_ref[...] = jax.nn.gelu(y).reshape(t_chunk, SUBLANES, LANES)


def _s5_discretise(lam_re, lam_im, log_step, b_re, b_im, c_re, c_im):
    groups = lam_re.shape[0]
    nblk = groups // GROUPS_PER_BLOCK
    lr = lam_re.astype(F32)
    li = lam_im.astype(F32)
    dt = jnp.exp(log_step.astype(F32))[:, None]
    mag = jnp.exp(lr * dt)
    ar = mag * jnp.cos(li * dt)
    ai = mag * jnp.sin(li * dt)
    den = lr * lr + li * li
    cr = ((ar - 1.0) * lr + ai * li) / den
    ci = (ai * lr - (ar - 1.0) * li) / den
    br = b_re.astype(F32)
    bi = b_im.astype(F32)
    bbar_re = cr[..., None] * br - ci[..., None] * bi
    bbar_im = cr[..., None] * bi + ci[..., None] * br
    eye = jnp.eye(GROUPS_PER_BLOCK, dtype=F32)

    def blockdiag_in(bbar):
        b4 = bbar.reshape(nblk, GROUPS_PER_BLOCK, SSM_STATE, SSM_GROUP)
        out = jnp.einsum('jgph,gk->jghkp', b4, eye)
        return out.reshape(nblk, LANES, STATE_COLS)

    def blockdiag_out(c):
        c4 = c.reshape(nblk, GROUPS_PER_BLOCK, SSM_GROUP, SSM_STATE)
        out = jnp.einsum('jghp,gk->jgpkh', c4, eye)
        return out.reshape(nblk, STATE_COLS, LANES)

    bcat = jnp.concatenate([blockdiag_in(bbar_re), blockdiag_in(bbar_im)], axis=2)
    ccat = jnp.concatenate([blockdiag_out(c_re.astype(F32)),
                            -blockdiag_out(c_im.astype(F32))], axis=1)
    return (bcat.astype(BF16), ccat.astype(BF16),
            ar.reshape(nblk, 1, STATE_COLS), ai.reshape(nblk, 1, STATE_COLS))


def _s5(u_tm, bcat, ccat, ar, ai, d_skip, *, bsz, t_chunk):
    seqlen = u_tm.shape[0]
    width = u_tm.shape[1] // bsz
    assert bsz == SUBLANES and seqlen % t_chunk == 0 and width % LANES == 0
    nblk = width // LANES
    rows = t_chunk * SUBLANES
    u3 = u_tm.reshape(seqlen, bsz, width)
    out = pl.pallas_call(
        functools.partial(_s5_kernel, t_chunk=t_chunk),
        grid=(nblk, seqlen // t_chunk),
        in_specs=[pl.BlockSpec((t_chunk, bsz, LANES), lambda j, c: (c, 0, j)),
                  pl.BlockSpec((1, LANES, 2 * STATE_COLS), lambda j, c: (j, 0, 0)),
                  pl.BlockSpec((1, 2 * STATE_COLS, LANES), lambda j, c: (j, 0, 0)),
                  pl.BlockSpec((1, 1, STATE_COLS), lambda j, c: (j, 0, 0)),
                  pl.BlockSpec((1, 1, STATE_COLS), lambda j, c: (j, 0, 0)),
                  pl.BlockSpec((1, LANES), lambda j, c: (0, j))],
        out_specs=pl.BlockSpec((t_chunk, bsz, LANES), lambda j, c: (c, 0, j)),
        out_shape=jax.ShapeDtypeStruct((seqlen, bsz, width), F32),
        scratch_shapes=[pltpu.VMEM((rows, 2 * STATE_COLS), F32),
                        pltpu.VMEM((rows, 2 * STATE_COLS), F32),
                        pltpu.VMEM((2, SUBLANES, STATE_COLS), F32)],
        compiler_params=_params(("parallel", "arbitrary")),
        name="s5_scan",
    )(u3, bcat, ccat, ar, ai, d_skip.reshape(1, width).astype(F32))
    return out.reshape(seqlen, bsz * width)


def _post(o, g_ref, h_ref, out_ref):
    ms = jnp.mean(o * o, axis=-1, keepdims=True)
    out_ref[...] = h_ref[...] + o * lax.rsqrt(ms + EPS) * g_ref[...]


def _glu_out_kernel(yg_ref, z_ref, memo_ref, h_ref, wglu_ref, bglu_ref,
                    wmain_ref, wmem_ref, g_ref, gkv_ref, gnext_ref, wf_ref, bf_ref,
                    out_ref, xkv_ref, xnext_ref, fp_ref, carry_ref):
    @pl.when(pl.program_id(1) == 0)
    def _():
        carry_ref[...] = jnp.zeros_like(carry_ref)

    yg = yg_ref[...]
    t = jnp.dot(yg.astype(BF16), wglu_ref[...], preferred_element_type=F32) + bglu_ref[...]
    z = z_ref[...].astype(F32)
    main = yg * jax.nn.sigmoid(t) * (z * jax.nn.sigmoid(z))
    o = jnp.dot(main.astype(BF16), wmain_ref[...], preferred_element_type=F32)
    o = o + jnp.dot(memo_ref[...], wmem_ref[...], preferred_element_type=F32)
    _post(o, g_ref, h_ref, out_ref)
    h1 = out_ref[...]
    r = h1 * lax.rsqrt(jnp.mean(h1 * h1, axis=-1, keepdims=True) + EPS)
    xnext_ref[...] = (r * gnext_ref[...]).astype(BF16)
    xkv = r * gkv_ref[...]
    xkv_ref[...] = xkv.astype(BF16)
    _fgate_block(xkv, wf_ref, bf_ref, carry_ref, fp_ref)


def _out_kernel(main_ref, memo_ref, h_ref, wmain_ref, wmem_ref, g_ref, out_ref):
    o = jnp.dot(main_ref[...], wmain_ref[...], preferred_element_type=F32)
    o = o + jnp.dot(memo_ref[...], wmem_ref[...], preferred_element_type=F32)
    _post(o, g_ref, h_ref, out_ref)


def _resident(shape):
    return pl.BlockSpec(shape, lambda *_: (0,) * len(shape), pipeline_mode=pl.Buffered(1))


def _glu_out(yg_tm, rest, memo, h2d, w_glu, b_glu, w_out, g, g_kv, g_next, w_fgate, b_fgate,
             *, bsz, tl):
    m, d = h2d.shape
    seqlen = m // bsz
    main_w = w_glu.shape[0]
    mem_w = w_out.shape[0] - main_w
    heads = w_fgate.shape[1]
    assert heads <= LANES and seqlen % tl == 0
    wpad = jnp.zeros((d, LANES), F32).at[:, :heads].set(w_fgate.astype(F32))
    wcat = jnp.concatenate(_split2(wpad), axis=1)
    bpad = jnp.zeros((1, LANES), F32).at[0, :heads].set(b_fgate.astype(F32))
    per_b = seqlen // tl
    row = lambda b, i: (b * per_b + i, 0)
    vec = lambda v: v.reshape(1, -1).astype(F32)
    return pl.pallas_call(
        _glu_out_kernel,
        grid=(bsz, per_b),
        in_specs=[pl.BlockSpec((tl, main_w), lambda b, i: (i, b)),
                  pl.BlockSpec((tl, main_w), row),
                  pl.BlockSpec((tl, mem_w), row),
                  pl.BlockSpec((tl, d), row),
                  _resident((main_w, main_w)),
                  _resident((1, main_w)),
                  _resident((main_w, d)),
                  _resident((mem_w, d)),
                  _resident((1, d)),
                  _resident((1, d)),
                  _resident((1, d)),
                  _resident((d, 2 * LANES)),
                  _resident((1, LANES))],
        out_specs=[pl.BlockSpec((tl, d), row),
                   pl.BlockSpec((tl, d), row),
                   pl.BlockSpec((tl, d), row),
                   pl.BlockSpec((tl, 3 * LANES), row)],
        out_shape=[jax.ShapeDtypeStruct((m, d), F32),
                   jax.ShapeDtypeStruct((m, d), BF16),
                   jax.ShapeDtypeStruct((m, d), BF16),
                   jax.ShapeDtypeStruct((m, 3 * LANES), BF16)],
        scratch_shapes=[pltpu.VMEM((1, LANES), F32)],
        compiler_params=_params(("arbitrary", "arbitrary")),
        name="glu_out_proj",
    )(yg_tm, rest, memo, h2d, w_glu.astype(BF16), vec(b_glu),
      w_out[:main_w].astype(BF16), w_out[main_w:].astype(BF16), vec(g), vec(g_kv),
      vec(g_next), wcat, bpad)


def _out_proj(main, memo, h2d, w_out, g, *, tl):
    m, d = h2d.shape
    main_w = main.shape[1]
    mem_w = memo.shape[1]
    row = lambda i: (i, 0)
    return pl.pallas_call(
        _out_kernel,
        grid=(m // tl,),
        in_specs=[pl.BlockSpec((tl, main_w), row),
                  pl.BlockSpec((tl, mem_w), row),
                  pl.BlockSpec((tl, d), row),
                  _resident((main_w, d)),
                  _resident((mem_w, d)),
                  _resident((1, d))],
        out_specs=pl.BlockSpec((tl, d), row),
        out_shape=jax.ShapeDtypeStruct((m, d), F32),
        compiler_params=_params(("parallel",)),
        name="out_proj",
    )(main, memo, h2d, w_out[:main_w].astype(BF16), w_out[main_w:].astype(BF16),
      g.reshape(1, d).astype(F32))


def _fox_kernel(q_ref, k_ref, v_ref, fp_ref, rep_ref, z_ref, o_ref,
                fkb_ref, vt_ref, st_ref, m_ref, l_ref, acc_ref, *, tq):
    seqlen = q_ref.shape[0]
    fkb_ref[...] = jnp.dot(fp_ref[...], rep_ref[0], preferred_element_type=F32)
    vt_ref[...] = v_ref[...].astype(F32).T.astype(BF16)
    key = lax.broadcasted_iota(jnp.int32, (tq, tq), 0)
    qry = lax.broadcasted_iota(jnp.int32, (tq, tq), 1)

    def scores(q, kj, slot):
        st_ref[slot] = lax.dot_general(k_ref[kj * tq:(kj + 1) * tq, :], q,
                                       (((1,), (1,)), ((), ())),
                                       preferred_element_type=F32)

    def softmax_pv(kj, slot, fq, masked):
        fkb = fkb_ref[kj * tq:(kj + 1) * tq, :]
        st = st_ref[slot] - jnp.concatenate([fkb] * (tq // LANES), axis=1)
        if masked:
            st = jnp.where(key <= qry, st, -jnp.inf)
        m_old = m_ref[...]
        m_new = jnp.maximum(m_old, jnp.max(st, axis=0, keepdims=True) + fq)
        alpha = jnp.exp2(m_old - m_new)
        pt = jnp.exp2(st + (fq - m_new))
        l_ref[...] = alpha * l_ref[...] + jnp.sum(pt, axis=0, keepdims=True)
        acc_ref[...] = alpha * acc_ref[...] + jnp.dot(
            vt_ref[:, kj * tq:(kj + 1) * tq], pt.astype(BF16), preferred_element_type=F32)
        m_ref[...] = m_new

    for qi in range(seqlen // tq):
        rows = slice(qi * tq, (qi + 1) * tq)
        q = q_ref[rows, :]
        fq = fkb_ref[rows, :].T[0:1, :]
        m_ref[...] = jnp.full_like(m_ref, -jnp.inf)
        l_ref[...] = jnp.zeros_like(l_ref)
        acc_ref[...] = jnp.zeros_like(acc_ref)
        scores(q, 0, 0)
        for kj in range(qi + 1):
            if kj < qi:
                scores(q, kj + 1, (kj + 1) % 2)
            softmax_pv(kj, kj % 2, fq, masked=(kj == qi))
        z = z_ref[rows, :].astype(F32)
        o = (acc_ref[...] / l_ref[...]).T
        o_ref[rows, :] = (o * (z * jax.nn.sigmoid(z))).astype(o_ref.dtype)


def _fox_attn(proj, kv, fparts, *, bsz, heads, tq):
    m = proj.shape[0]
    seqlen = m // bsz
    assert seqlen % tq == 0
    nparts = fparts.shape[1] // LANES
    sel = (jnp.arange(nparts * LANES)[None, :] % LANES) == jnp.arange(heads)[:, None]
    rep = jnp.broadcast_to(sel[:, :, None], (heads, nparts * LANES, LANES)).astype(BF16)
    return pl.pallas_call(
        functools.partial(_fox_kernel, tq=tq),
        grid=(bsz, heads),
        in_specs=[pl.BlockSpec((seqlen, HEAD_DIM), lambda b, h: (b, h)),
                  pl.BlockSpec((seqlen, HEAD_DIM), lambda b, h: (b, h)),
                  pl.BlockSpec((seqlen, HEAD_DIM), lambda b, h: (b, heads + h)),
                  pl.BlockSpec((seqlen, nparts * LANES), lambda b, h: (b, 0)),
                  pl.BlockSpec((1, nparts * LANES, LANES), lambda b, h: (h, 0, 0)),
                  pl.BlockSpec((seqlen, HEAD_DIM), lambda b, h: (b, heads + h))],
        out_specs=pl.BlockSpec((seqlen, HEAD_DIM), lambda b, h: (b, h)),
        out_shape=jax.ShapeDtypeStruct((m, heads * HEAD_DIM), BF16),
        scratch_shapes=[pltpu.VMEM((seqlen, LANES), F32),
                        pltpu.VMEM((HEAD_DIM, seqlen), BF16),
                        pltpu.VMEM((2, tq, tq), F32),
                        pltpu.VMEM((1, tq), F32),
                        pltpu.VMEM((1, tq), F32),
                        pltpu.VMEM((HEAD_DIM, tq), F32)],
        compiler_params=_params(("arbitrary", "arbitrary")),
        name="fox_attn",
    )(proj, kv, kv, fparts, rep, proj)


def kernel(x, mem, pre_norm_g, post_norm_g, w_in_a, lam_re, lam_im, log_step, b_re, b_im,
           c_re, c_im, d_skip, w_glu, b_glu, kv_norm_g, w_kv, w_fgate, b_fgate, w_in_b,
           mem_norm_g, w_mem_kv, w_out):
    bsz, seqlen, d = x.shape
    n_mem = mem.shape[1]
    main_w = w_glu.shape[1]
    mem_w = w_out.shape[1] - main_w
    heads = main_w // HEAD_DIM
    scale = HEAD_DIM ** -0.5
    x2d = x.reshape(bsz * seqlen, d)
    mem2d = mem.reshape(bsz * n_mem, d)

    ones = jnp.ones((main_w,), F32)
    mem_scale = jnp.full((mem_w,), scale, F32)
    cs_a = jnp.concatenate([ones, ones, mem_scale, ones[:mem_w]])
    cs_b = jnp.concatenate([ones * (scale * LOG2E), ones, mem_scale, ones[:mem_w]])
    xa = _prenorm(x2d, pre_norm_g[0], tm=512)
    u_tm = _matmul(xa, w_in_a[0], cs_a, col0=0, n=main_w, out_dtype=F32, tm=1024, tn=768,
                   time_major_batches=bsz, name="in_proj_a_u")
    rest_a = _matmul(xa, w_in_a[0], cs_a, col0=main_w, n=main_w + 2 * mem_w, out_dtype=BF16,
                     tm=2048, tn=512, name="in_proj_a_rest")
    kvm0 = _norm_matmul(mem2d, mem_norm_g[0], w_mem_kv[0].astype(BF16), out_dtype=BF16,
                        tm=512, tn=512, name="mem_kv0")
    memo0 = _mem_attn(rest_a, main_w // mem_w, main_w // mem_w + 1, kvm0,
                      bsz=bsz, tq=512, name="mem_attn0")
    bcat, ccat, ar, ai = _s5_discretise(lam_re[0], lam_im[0], log_step[0], b_re[0], b_im[0],
                                        c_re[0], c_im[0])
    yg_tm = _s5(u_tm, bcat, ccat, ar, ai, d_skip[0], bsz=bsz, t_chunk=128)
    h1, xkv, xb, fparts = _glu_out(
        yg_tm, rest_a, memo0, x2d, w_glu[0], b_glu[0], w_out[0], post_norm_g[0],
        kv_norm_g, pre_norm_g[1], w_fgate, b_fgate, bsz=bsz, tl=512)

    kv = _matmul(xkv, w_kv, jnp.ones((w_kv.shape[1],), F32), col0=0, n=w_kv.shape[1],
                 out_dtype=BF16, tm=1024, tn=1024, name="kv_proj")

    proj_b = _matmul(xb, w_in_b[0], cs_b, col0=0, n=w_in_b.shape[2], out_dtype=BF16,
                     tm=1024, tn=1024, name="in_proj_b")
    kvm1 = _norm_matmul(mem2d, mem_norm_g[1], w_mem_kv[1].astype(BF16), out_dtype=BF16,
                        tm=512, tn=512, name="mem_kv1")
    memo1 = _mem_attn(proj_b, 2 * main_w // mem_w, 2 * main_w // mem_w + 1, kvm1,
                      bsz=bsz, tq=512, name="mem_attn1")
    att = _fox_attn(proj_b, kv, fparts, bsz=bsz, heads=heads, tq=512)
    out = _out_proj(att, memo1, h1, w_out[1], post_norm_g[1], tl=512)
    return out.reshape(bsz, seqlen, d)
```

```python
import functools
import math

import jax
import jax.numpy as jnp
from jax import lax
from jax.experimental import pallas as pl
from jax.experimental.pallas import tpu as pltpu

F32 = jnp.float32
BF16 = jnp.bfloat16

EPS = 1e-6
LOG2E = 1.4426950408889634
HEAD_DIM = 128
SSM_GROUP = 16
SSM_STATE = 64
MEM_HEADS = 4
LANES = 128
SUBLANES = 8
GROUPS_PER_BLOCK = LANES // SSM_GROUP
STATE_COLS = GROUPS_PER_BLOCK * SSM_STATE
VMEM_LIMIT = 56 * 1024 * 1024


def _params(sem, vmem=VMEM_LIMIT):
    return pltpu.CompilerParams(dimension_semantics=sem, vmem_limit_bytes=vmem)


def _norm_matmul_kernel(x_ref, g_ref, w_ref, o_ref, xn_ref):
    @pl.when(pl.program_id(1) == 0)
    def _():
        x = x_ref[...]
        ms = jnp.mean(x * x, axis=-1, keepdims=True)
        xn_ref[...] = (x * lax.rsqrt(ms + EPS) * g_ref[...]).astype(BF16)

    o_ref[...] = jnp.dot(xn_ref[...], w_ref[...],
                         preferred_element_type=F32).astype(o_ref.dtype)


def _norm_matmul(x2d, g, w, *, out_dtype, tm, tn, time_major_batches=None, name):
    m, d = x2d.shape
    n = w.shape[1]
    assert m % tm == 0 and n % tn == 0
    nj = n // tn
    if time_major_batches is None:
        out_shape = jax.ShapeDtypeStruct((m, n), out_dtype)
        out_spec = pl.BlockSpec((tm, tn), lambda i, j: (i, j))
    else:
        bsz = time_major_batches
        seqlen = m // bsz
        assert seqlen % tm == 0
        per_b = seqlen // tm
        out_shape = jax.ShapeDtypeStruct((seqlen, bsz * n), out_dtype)
        out_spec = pl.BlockSpec((tm, tn), lambda i, j: (i % per_b, (i // per_b) * nj + j))
    return pl.pallas_call(
        _norm_matmul_kernel,
        grid=(m // tm, nj),
        in_specs=[pl.BlockSpec((tm, d), lambda i, j: (i, 0)),
                  pl.BlockSpec((1, d), lambda i, j: (0, 0)),
                  pl.BlockSpec((d, tn), lambda i, j: (0, j))],
        out_specs=out_spec,
        out_shape=out_shape,
        scratch_shapes=[pltpu.VMEM((tm, d), BF16)],
        compiler_params=_params(("parallel", "arbitrary")),
        name=name,
    )(x2d, g.reshape(1, d), w)


def _prenorm_kernel(x_ref, g_ref, o_ref):
    x = x_ref[...]
    ms = jnp.mean(x * x, axis=-1, keepdims=True)
    o_ref[...] = (x * lax.rsqrt(ms + EPS) * g_ref[...]).astype(o_ref.dtype)


def _prenorm(x2d, g, *, tm):
    m, d = x2d.shape
    return pl.pallas_call(
        _prenorm_kernel,
        grid=(m // tm,),
        in_specs=[pl.BlockSpec((tm, d), lambda i: (i, 0)),
                  pl.BlockSpec((1, d), lambda i: (0, 0))],
        out_specs=pl.BlockSpec((tm, d), lambda i: (i, 0)),
        out_shape=jax.ShapeDtypeStruct((m, d), BF16),
        compiler_params=_params(("parallel",)),
        name="prenorm",
    )(x2d, g.reshape(1, d).astype(F32))


def _matmul_kernel(x_ref, w_ref, cs_ref, o_ref, wb_ref):
    @pl.when(pl.program_id(1) == 0)
    def _():
        wb_ref[...] = (w_ref[...] * cs_ref[...]).astype(BF16)

    res = jnp.dot(x_ref[...], wb_ref[...], preferred_element_type=F32).astype(o_ref.dtype)
    if len(o_ref.shape) == 3:
        sw = o_ref.shape[2]
        for s in range(o_ref.shape[0]):
            o_ref[s] = res[:, s * sw:(s + 1) * sw]
    else:
        o_ref[...] = res


def _matmul(xn, w, colscale, *, col0, n, out_dtype, tm, tn, time_major_batches=None,
            slab_width=None, name):
    m, d = xn.shape
    assert m % tm == 0 and n % tn == 0 and col0 % tn == 0
    nj = n // tn
    j0 = col0 // tn
    if slab_width is not None:
        assert time_major_batches is None and tn % slab_width == 0
        per_tile = tn // slab_width
        out_shape = jax.ShapeDtypeStruct((n // slab_width, m, slab_width), out_dtype)
        out_spec = pl.BlockSpec((per_tile, tm, slab_width), lambda j, i: (j, i, 0))
    elif time_major_batches is None:
        out_shape = jax.ShapeDtypeStruct((m, n), out_dtype)
        out_spec = pl.BlockSpec((tm, tn), lambda j, i: (i, j))
    else:
        bsz = time_major_batches
        seqlen = m // bsz
        assert seqlen % tm == 0
        per_b = seqlen // tm
        out_shape = jax.ShapeDtypeStruct((seqlen, bsz * n), out_dtype)
        out_spec = pl.BlockSpec((tm, tn), lambda j, i: (i % per_b, (i // per_b) * nj + j))
    return pl.pallas_call(
        _matmul_kernel,
        grid=(nj, m // tm),
        in_specs=[pl.BlockSpec((tm, d), lambda j, i: (i, 0)),
                  pl.BlockSpec((d, tn), lambda j, i: (0, j0 + j), pipeline_mode=pl.Buffered(1)),
                  pl.BlockSpec((1, tn), lambda j, i: (0, j0 + j))],
        out_specs=out_spec,
        out_shape=out_shape,
        scratch_shapes=[pltpu.VMEM((d, tn), BF16)],
        compiler_params=_params(("arbitrary", "arbitrary")),
        name=name,
    )(xn, w, colscale.reshape(1, -1).astype(F32))


def _split2(x):
    hi = x.astype(BF16)
    return hi, (x - hi.astype(F32)).astype(BF16)


def _split3(x):
    hi = x.astype(BF16)
    r = x - hi.astype(F32)
    mid = r.astype(BF16)
    return hi, mid, (r - mid.astype(F32)).astype(BF16)


def _fgate_block(xn, wcat_ref, b_ref, carry_ref, o_ref, heads):
    tl = xn.shape[0]
    xh, xl = _split2(xn)
    both = jnp.dot(xh, wcat_ref[...], preferred_element_type=F32)
    logit = (both[:, :LANES] + both[:, LANES:]
             + jnp.dot(xl, wcat_ref[:, :LANES], preferred_element_type=F32)) + b_ref[...]
    logf = jnp.minimum(logit, 0.0) - jnp.log(1.0 + jnp.exp(-jnp.abs(logit)))
    t_idx = lax.broadcasted_iota(jnp.int32, (tl, tl), 0)
    s_idx = lax.broadcasted_iota(jnp.int32, (tl, tl), 1)
    tril = (s_idx <= t_idx).astype(BF16)
    csum = carry_ref[...]
    for part in _split3(logf):
        csum = csum + jnp.dot(tril, part, preferred_element_type=F32)
    carry_ref[...] = csum[tl - 1:tl, :]
    lane = lax.broadcasted_iota(jnp.int32, (tl, LANES), 1)
    packed = jnp.zeros((tl, LANES), F32)
    for n, part in reversed(list(enumerate(_split3(csum * LOG2E)))):
        shifted = part.astype(F32) if n == 0 else pltpu.roll(part.astype(F32), n * heads, 1)
        packed = jnp.where(lane < (n + 1) * heads, shifted, packed)
    o_ref[...] = packed.astype(BF16)


def _mem_attn_kernel(q_ref, zm_ref, kv_ref, o_ref):
    width = MEM_HEADS * HEAD_DIM
    for h in range(MEM_HEADS):
        lo, hi = h * HEAD_DIM, (h + 1) * HEAD_DIM
        q = q_ref[:, lo:hi]
        k = kv_ref[:, lo:hi]
        v = kv_ref[:, width + lo:width + hi]
        s = lax.dot_general(q, k, (((1,), (1,)), ((), ())), preferred_element_type=F32)
        m = jnp.max(s, axis=-1, keepdims=True)
        p = jnp.exp(s - m)
        l = jnp.sum(p, axis=-1, keepdims=True)
        o = jnp.dot(p.astype(BF16), v, preferred_element_type=F32) / l
        zm = zm_ref[:, lo:hi].astype(F32)
        o_ref[:, lo:hi] = (o * (zm * jax.nn.sigmoid(zm))).astype(o_ref.dtype)


def _mem_attn(proj, q_blk, zm_blk, kvm, *, bsz, tq, name):
    width = MEM_HEADS * HEAD_DIM
    if proj.ndim == 3:
        m = proj.shape[1]
        spec = lambda blk: pl.BlockSpec((None, tq, width), lambda b, i: (blk, b * per_b + i, 0))
    else:
        m = proj.shape[0]
        spec = lambda blk: pl.BlockSpec((tq, width), lambda b, i: (b * per_b + i, blk))
    seqlen = m // bsz
    n_mem = kvm.shape[0] // bsz
    per_b = seqlen // tq
    return pl.pallas_call(
        _mem_attn_kernel,
        grid=(bsz, per_b),
        in_specs=[spec(q_blk), spec(zm_blk),
                  pl.BlockSpec((n_mem, 2 * width), lambda b, i: (b, 0))],
        out_specs=pl.BlockSpec((tq, width), lambda b, i: (b * per_b + i, 0)),
        out_shape=jax.ShapeDtypeStruct((m, width), BF16),
        compiler_params=_params(("parallel", "arbitrary")),
        name=name,
    )(proj, proj, kvm)


def _s5_kernel(u_ref, bcat_ref, ccat_ref, ar_ref, ai_ref, d_ref, y_ref,
               bu_ref, xs_ref, st_ref, *, t_chunk):
    @pl.when(pl.program_id(1) == 0)
    def _():
        st_ref[...] = jnp.zeros_like(st_ref)

    rows = t_chunk * SUBLANES
    u = u_ref[...].reshape(rows, LANES)
    bu_ref[...] = jnp.dot(u.astype(BF16), bcat_ref[0], preferred_element_type=F32)
    ar = jnp.broadcast_to(ar_ref[0], (SUBLANES, STATE_COLS))
    ai = jnp.broadcast_to(ai_ref[0], (SUBLANES, STATE_COLS))

    def body(t, carry):
        xr, xi = carry
        r0 = pl.multiple_of(t * SUBLANES, SUBLANES)
        bur = bu_ref[pl.ds(r0, SUBLANES), 0:STATE_COLS]
        bui = bu_ref[pl.ds(r0, SUBLANES), STATE_COLS:2 * STATE_COLS]
        nxr = ar * xr - ai * xi + bur
        nxi = ar * xi + ai * xr + bui
        xs_ref[pl.ds(r0, SUBLANES), 0:STATE_COLS] = nxr
        xs_ref[pl.ds(r0, SUBLANES), STATE_COLS:2 * STATE_COLS] = nxi
        return nxr, nxi

    xr, xi = lax.fori_loop(0, t_chunk, body, (st_ref[0], st_ref[1]), unroll=True)
    st_ref[0] = xr
    st_ref[1] = xi
    y = jnp.dot(xs_ref[...].astype(BF16), ccat_ref[0], preferred_element_type=F32)
    y = y + d_ref[...] * u
    y_ref[...] = jax.nn.gelu(y).reshape(t_chunk, SUBLANES, LANES)


def _s5_discretise(lam_re, lam_im, log_step, b_re, b_im, c_re, c_im):
    groups = lam_re.shape[0]
    nblk = groups // GROUPS_PER_BLOCK
    lr = lam_re.astype(F32)
    li = lam_im.astype(F32)
    dt = jnp.exp(log_step.astype(F32))[:, None]
    mag = jnp.exp(lr * dt)
    ar = mag * jnp.cos(li * dt)
    ai = mag * jnp.sin(li * dt)
    den = lr * lr + li * li
    cr = ((ar - 1.0) * lr + ai * li) / den
    ci = (ai * lr - (ar - 1.0) * li) / den
    br = b_re.astype(F32)
    bi = b_im.astype(F32)
    bbar_re = cr[..., None] * br - ci[..., None] * bi
    bbar_im = cr[..., None] * bi + ci[..., None] * br
    eye = jnp.eye(GROUPS_PER_BLOCK, dtype=F32)

    def blockdiag_in(bbar):
        b4 = bbar.reshape(nblk, GROUPS_PER_BLOCK, SSM_STATE, SSM_GROUP)
        out = jnp.einsum('jgph,gk->jghkp', b4, eye)
        return out.reshape(nblk, LANES, STATE_COLS)

    def blockdiag_out(c):
        c4 = c.reshape(nblk, GROUPS_PER_BLOCK, SSM_GROUP, SSM_STATE)
        out = jnp.einsum('jghp,gk->jgpkh', c4, eye)
        return out.reshape(nblk, STATE_COLS, LANES)

    bcat = jnp.concatenate([blockdiag_in(bbar_re), blockdiag_in(bbar_im)], axis=2)
    ccat = jnp.concatenate([blockdiag_out(c_re.astype(F32)),
                            -blockdiag_out(c_im.astype(F32))], axis=1)
    return (bcat.astype(BF16), ccat.astype(BF16),
            ar.reshape(nblk, 1, STATE_COLS), ai.reshape(nblk, 1, STATE_COLS))


def _s5(u_tm, bcat, ccat, ar, ai, d_skip, *, bsz, t_chunk):
    seqlen = u_tm.shape[0]
    width = u_tm.shape[1] // bsz
    assert bsz == SUBLANES and seqlen % t_chunk == 0 and width % LANES == 0
    nblk = width // LANES
    rows = t_chunk * SUBLANES
    u3 = u_tm.reshape(seqlen, bsz, width)
    out = pl.pallas_call(
        functools.partial(_s5_kernel, t_chunk=t_chunk),
        grid=(nblk, seqlen // t_chunk),
        in_specs=[pl.BlockSpec((t_chunk, bsz, LANES), lambda j, c: (c, 0, j)),
                  pl.BlockSpec((1, LANES, 2 * STATE_COLS), lambda j, c: (j, 0, 0)),
                  pl.BlockSpec((1, 2 * STATE_COLS, LANES), lambda j, c: (j, 0, 0)),
                  pl.BlockSpec((1, 1, STATE_COLS), lambda j, c: (j, 0, 0)),
                  pl.BlockSpec((1, 1, STATE_COLS), lambda j, c: (j, 0, 0)),
                  pl.BlockSpec((1, LANES), lambda j, c: (0, j))],
        out_specs=pl.BlockSpec((t_chunk, bsz, LANES), lambda j, c: (c, 0, j)),
        out_shape=jax.ShapeDtypeStruct((seqlen, bsz, width), F32),
        scratch_shapes=[pltpu.VMEM((rows, 2 * STATE_COLS), F32),
                        pltpu.VMEM((rows, 2 * STATE_COLS), F32),
                        pltpu.VMEM((2, SUBLANES, STATE_COLS), F32)],
        compiler_params=_params(("parallel", "arbitrary")),
        name="s5_scan",
    )(u3, bcat, ccat, ar, ai, d_skip.reshape(1, width).astype(F32))
    return out.reshape(seqlen, bsz * width)


def _post(o, g_ref, h_ref, out_ref):
    ms = jnp.mean(o * o, axis=-1, keepdims=True)
    out_ref[...] = h_ref[...] + o * lax.rsqrt(ms + EPS) * g_ref[...]


def _glu_out_kernel(yg_ref, z_ref, memo_ref, h_ref, wglu_ref, bglu_ref,
                    wmain_ref, wmem_ref, g_ref, gkv_ref, gnext_ref, wf_ref, bf_ref,
                    out_ref, xkv_ref, xnext_ref, fp_ref, carry_ref, *, heads):
    @pl.when(pl.program_id(1) == 0)
    def _():
        carry_ref[...] = jnp.zeros_like(carry_ref)

    yg = yg_ref[...]
    t = jnp.dot(yg.astype(BF16), wglu_ref[...], preferred_element_type=F32) + bglu_ref[...]
    z = jnp.concatenate([z_ref[s] for s in range(z_ref.shape[0])], axis=1).astype(F32)
    main = yg * jax.nn.sigmoid(t) * (z * jax.nn.sigmoid(z))
    o = jnp.dot(main.astype(BF16), wmain_ref[...], preferred_element_type=F32)
    o = o + jnp.dot(memo_ref[...], wmem_ref[...], preferred_element_type=F32)
    _post(o, g_ref, h_ref, out_ref)
    h1 = out_ref[...]
    r = h1 * lax.rsqrt(jnp.mean(h1 * h1, axis=-1, keepdims=True) + EPS)
    xnext_ref[...] = (r * gnext_ref[...]).astype(BF16)
    xkv = r * gkv_ref[...]
    xkv_ref[...] = xkv.astype(BF16)
    _fgate_block(xkv, wf_ref, bf_ref, carry_ref, fp_ref, heads)


def _out_kernel(main_ref, memo_ref, h_ref, wmain_ref, wmem_ref, g_ref, out_ref):
    o = jnp.dot(main_ref[...], wmain_ref[...], preferred_element_type=F32)
    o = o + jnp.dot(memo_ref[...], wmem_ref[...], preferred_element_type=F32)
    _post(o, g_ref, h_ref, out_ref)


def _resident(shape):
    return pl.BlockSpec(shape, lambda *_: (0,) * len(shape), pipeline_mode=pl.Buffered(1))


def _glu_out(yg_tm, rest, memo, h2d, w_glu, b_glu, w_out, g, g_kv, g_next, w_fgate, b_fgate,
             *, bsz, tl):
    m, d = h2d.shape
    seqlen = m // bsz
    main_w = w_glu.shape[0]
    mem_w = w_out.shape[0] - main_w
    heads = w_fgate.shape[1]
    assert 3 * heads <= LANES and seqlen % tl == 0
    wpad = jnp.zeros((d, LANES), F32).at[:, :heads].set(w_fgate.astype(F32))
    wcat = jnp.concatenate(_split2(wpad), axis=1)
    bpad = jnp.zeros((1, LANES), F32).at[0, :heads].set(b_fgate.astype(F32))
    per_b = seqlen // tl
    row = lambda b, i: (b * per_b + i, 0)
    vec = lambda v: v.reshape(1, -1).astype(F32)
    return pl.pallas_call(
        functools.partial(_glu_out_kernel, heads=heads),
        grid=(bsz, per_b),
        in_specs=[pl.BlockSpec((tl, main_w), lambda b, i: (i, b)),
                  pl.BlockSpec((main_w // rest.shape[2], tl, rest.shape[2]),
                               lambda b, i: (0, b * per_b + i, 0)),
                  pl.BlockSpec((tl, mem_w), row),
                  pl.BlockSpec((tl, d), row),
                  _resident((main_w, main_w)),
                  _resident((1, main_w)),
                  _resident((main_w, d)),
                  _resident((mem_w, d)),
                  _resident((1, d)),
                  _resident((1, d)),
                  _resident((1, d)),
                  _resident((d, 2 * LANES)),
                  _resident((1, LANES))],
        out_specs=[pl.BlockSpec((tl, d), row),
                   pl.BlockSpec((tl, d), row),
                   pl.BlockSpec((tl, d), row),
                   pl.BlockSpec((tl, LANES), row)],
        out_shape=[jax.ShapeDtypeStruct((m, d), F32),
                   jax.ShapeDtypeStruct((m, d), BF16),
                   jax.ShapeDtypeStruct((m, d), BF16),
                   jax.ShapeDtypeStruct((m, LANES), BF16)],
        scratch_shapes=[pltpu.VMEM((1, LANES), F32)],
        compiler_params=_params(("arbitrary", "arbitrary")),
        name="glu_out_proj",
    )(yg_tm, rest, memo, h2d, w_glu.astype(BF16), vec(b_glu),
      w_out[:main_w].astype(BF16), w_out[main_w:].astype(BF16), vec(g), vec(g_kv),
      vec(g_next), wcat, bpad)


def _out_proj(main, memo, h2d, w_out, g, *, tl):
    m, d = h2d.shape
    main_w = main.shape[1]
    mem_w = memo.shape[1]
    row = lambda i: (i, 0)
    return pl.pallas_call(
        _out_kernel,
        grid=(m // tl,),
        in_specs=[pl.BlockSpec((tl, main_w), row),
                  pl.BlockSpec((tl, mem_w), row),
                  pl.BlockSpec((tl, d), row),
                  _resident((main_w, d)),
                  _resident((mem_w, d)),
                  _resident((1, d))],
        out_specs=pl.BlockSpec((tl, d), row),
        out_shape=jax.ShapeDtypeStruct((m, d), F32),
        compiler_params=_params(("parallel",)),
        name="out_proj",
    )(main, memo, h2d, w_out[:main_w].astype(BF16), w_out[main_w:].astype(BF16),
      g.reshape(1, d).astype(F32))


FOX_CHUNK = 64


def _fox_kernel(q_ref, k_ref, v_ref, fp_ref, sel_ref, cst_ref, z_ref, o_ref,
                ka_ref, qa_ref, vt_ref, st_ref, pt_ref, m_ref, l_ref, acc_ref, *, tq):
    seqlen = q_ref.shape[0]
    ncg = tq // LANES
    nch = tq // FOX_CHUNK
    ext = jnp.dot(fp_ref[...], sel_ref[0], preferred_element_type=F32)
    ka_ref[:, :HEAD_DIM] = k_ref[...]
    ka_ref[:, HEAD_DIM:] = (ext[:, :HEAD_DIM] + cst_ref[0:1, :]).astype(BF16)
    qa_ref[:, :HEAD_DIM] = q_ref[...]
    qa_ref[:, HEAD_DIM:] = (ext[:, HEAD_DIM:] + cst_ref[1:2, :]).astype(BF16)
    vt_ref[...] = v_ref[...].astype(F32).T.astype(BF16)
    for ci in range(nch):
        for g in range(ncg):
            if g * LANES + LANES - 1 < ci * FOX_CHUNK:
                pt_ref[1, ci * FOX_CHUNK:(ci + 1) * FOX_CHUNK, g * LANES:(g + 1) * LANES] = (
                    jnp.zeros((FOX_CHUNK, LANES), BF16))

    def scores(qi, kj, slot):
        st_ref[slot] = lax.dot_general(ka_ref[kj * tq:(kj + 1) * tq, :],
                                       qa_ref[qi * tq:(qi + 1) * tq, :],
                                       (((1,), (1,)), ((), ())),
                                       preferred_element_type=F32)

    def piece(slot, ci, g, masked):
        r0, c0 = ci * FOX_CHUNK, g * LANES
        if masked and c0 + LANES - 1 < r0:
            return None
        x = st_ref[slot, r0:r0 + FOX_CHUNK, c0:c0 + LANES]
        if masked and c0 < r0 + FOX_CHUNK - 1:
            key = r0 + lax.broadcasted_iota(jnp.int32, (FOX_CHUNK, LANES), 0)
            qry = c0 + lax.broadcasted_iota(jnp.int32, (FOX_CHUNK, LANES), 1)
            x = jnp.where(key <= qry, x, -jnp.inf)
        return x

    def fold(x, op):
        return op(x.reshape(FOX_CHUNK // SUBLANES, SUBLANES, LANES), axis=0)

    def softmax_pv(kj, slot, masked, first):
        mx = [None] * ncg
        for ci in range(nch):
            for g in range(ncg):
                x = piece(slot, ci, g, masked)
                if x is not None:
                    f = fold(x, jnp.max)
                    mx[g] = f if mx[g] is None else jnp.maximum(mx[g], f)
        m_new = jnp.concatenate([jnp.max(v, axis=0, keepdims=True) for v in mx], axis=1)
        if not first:
            m_old = m_ref[...]
            m_new = jnp.maximum(m_old, m_new)
            alpha = jnp.exp2(m_old - m_new)
        m_ref[...] = m_new
        ls = [None] * ncg
        for ci in range(nch):
            for g in range(ncg):
                r0, c0 = ci * FOX_CHUNK, g * LANES
                x = piece(slot, ci, g, masked)
                if x is None:
                    continue
                p = jnp.exp2(x - m_new[:, c0:c0 + LANES])
                f = fold(p, jnp.sum)
                ls[g] = f if ls[g] is None else ls[g] + f
                pt_ref[int(masked), r0:r0 + FOX_CHUNK, c0:c0 + LANES] = p.astype(BF16)
        l_new = jnp.concatenate([jnp.sum(v, axis=0, keepdims=True) for v in ls], axis=1)
        pv = jnp.dot(vt_ref[:, kj * tq:(kj + 1) * tq], pt_ref[int(masked)],
                     preferred_element_type=F32)
        if first:
            l_ref[...] = l_new
            acc_ref[...] = pv
        else:
            l_ref[...] = alpha * l_ref[...] + l_new
            acc_ref[...] = alpha * acc_ref[...] + pv

    for qi in range(seqlen // tq):
        rows = slice(qi * tq, (qi + 1) * tq)
        scores(qi, 0, 0)
        for kj in range(qi + 1):
            if kj < qi:
                scores(qi, kj + 1, (kj + 1) % 2)
            softmax_pv(kj, kj % 2, masked=(kj == qi), first=(kj == 0))
        z = z_ref[rows, :].astype(F32)
        o = (acc_ref[...] / l_ref[...]).T
        o_ref[rows, :] = (o * (z * jax.nn.sigmoid(z))).astype(o_ref.dtype)


def _fox_attn(proj, kv, fparts, *, bsz, heads, tq):
    m = proj.shape[0]
    seqlen = m // bsz
    nterms = 3
    assert seqlen % tq == 0 and tq % FOX_CHUNK == 0 and nterms * heads <= LANES
    h_idx = jnp.arange(heads)[:, None, None]
    r_idx = jnp.arange(LANES)[None, :, None]
    c_idx = jnp.arange(2 * HEAD_DIM)[None, None, :]
    key_side = (c_idx < nterms) & (r_idx == c_idx * heads + h_idx)
    qc = c_idx - HEAD_DIM - nterms
    qry_side = (qc >= 0) & (qc < nterms) & (r_idx == qc * heads + h_idx)
    sel = (qry_side.astype(F32) - key_side.astype(F32)).astype(BF16)
    col = jnp.arange(HEAD_DIM)
    cst = jnp.zeros((SUBLANES, HEAD_DIM), F32)
    cst = cst.at[0].set(((col >= nterms) & (col < 2 * nterms)).astype(F32))
    cst = cst.at[1].set((col < nterms).astype(F32))
    return pl.pallas_call(
        functools.partial(_fox_kernel, tq=tq),
        grid=(bsz, heads),
        in_specs=[pl.BlockSpec((seqlen, HEAD_DIM), lambda b, h: (b, h)),
                  pl.BlockSpec((seqlen, HEAD_DIM), lambda b, h: (b, h)),
                  pl.BlockSpec((seqlen, HEAD_DIM), lambda b, h: (b, heads + h)),
                  pl.BlockSpec((seqlen, LANES), lambda b, h: (b, 0)),
                  pl.BlockSpec((1, LANES, 2 * HEAD_DIM), lambda b, h: (h, 0, 0)),
                  pl.BlockSpec((SUBLANES, HEAD_DIM), lambda b, h: (0, 0)),
                  pl.BlockSpec((seqlen, HEAD_DIM), lambda b, h: (b, heads + h))],
        out_specs=pl.BlockSpec((seqlen, HEAD_DIM), lambda b, h: (b, h)),
        out_shape=jax.ShapeDtypeStruct((m, heads * HEAD_DIM), BF16),
        scratch_shapes=[pltpu.VMEM((seqlen, 2 * HEAD_DIM), BF16),
                        pltpu.VMEM((seqlen, 2 * HEAD_DIM), BF16),
                        pltpu.VMEM((HEAD_DIM, seqlen), BF16),
                        pltpu.VMEM((2, tq, tq), F32),
                        pltpu.VMEM((2, tq, tq), BF16),
                        pltpu.VMEM((1, tq), F32),
                        pltpu.VMEM((1, tq), F32),
                        pltpu.VMEM((HEAD_DIM, tq), F32)],
        compiler_params=_params(("arbitrary", "arbitrary")),
        name="fox_attn",
    )(proj, kv, kv, fparts, sel, cst, proj)


def kernel(x, mem, pre_norm_g, post_norm_g, w_in_a, lam_re, lam_im, log_step, b_re, b_im,
           c_re, c_im, d_skip, w_glu, b_glu, kv_norm_g, w_kv, w_fgate, b_fgate, w_in_b,
           mem_norm_g, w_mem_kv, w_out):
    bsz, seqlen, d = x.shape
    n_mem = mem.shape[1]
    main_w = w_glu.shape[1]
    mem_w = w_out.shape[1] - main_w
    heads = main_w // HEAD_DIM
    scale = HEAD_DIM ** -0.5
    x2d = x.reshape(bsz * seqlen, d)
    mem2d = mem.reshape(bsz * n_mem, d)

    ones = jnp.ones((main_w,), F32)
    mem_scale = jnp.full((mem_w,), scale, F32)
    cs_a = jnp.concatenate([ones, ones, mem_scale, ones[:mem_w]])
    cs_b = jnp.concatenate([ones * (scale * LOG2E), ones, mem_scale, ones[:mem_w]])
    xa = _prenorm(x2d, pre_norm_g[0], tm=512)
    u_tm = _matmul(xa, w_in_a[0], cs_a, col0=0, n=main_w, out_dtype=F32, tm=1024, tn=768,
                   time_major_batches=bsz, name="in_proj_a_u")
    rest_a = _matmul(xa, w_in_a[0], cs_a, col0=main_w, n=main_w + 2 * mem_w, out_dtype=BF16,
                     tm=1024, tn=512, slab_width=mem_w, name="in_proj_a_rest")
    kvm0 = _norm_matmul(mem2d, mem_norm_g[0], w_mem_kv[0].astype(BF16), out_dtype=BF16,
                        tm=512, tn=512, name="mem_kv0")
    memo0 = _mem_attn(rest_a, main_w // mem_w, main_w // mem_w + 1, kvm0,
                      bsz=bsz, tq=512, name="mem_attn0")
    bcat, ccat, ar, ai = _s5_discretise(lam_re[0], lam_im[0], log_step[0], b_re[0], b_im[0],
                                        c_re[0], c_im[0])
    yg_tm = _s5(u_tm, bcat, ccat, ar, ai, d_skip[0], bsz=bsz, t_chunk=128)
    h1, xkv, xb, fparts = _glu_out(
        yg_tm, rest_a, memo0, x2d, w_glu[0], b_glu[0], w_out[0], post_norm_g[0],
        kv_norm_g, pre_norm_g[1], w_fgate, b_fgate, bsz=bsz, tl=256)

    kv = _matmul(xkv, w_kv, jnp.ones((w_kv.shape[1],), F32), col0=0, n=w_kv.shape[1],
                 out_dtype=BF16, tm=2048, tn=1024, name="kv_proj")

    proj_b = _matmul(xb, w_in_b[0], cs_b, col0=0, n=w_in_b.shape[2], out_dtype=BF16,
                     tm=2048, tn=1024, name="in_proj_b")
    kvm1 = _norm_matmul(mem2d, mem_norm_g[1], w_mem_kv[1].astype(BF16), out_dtype=BF16,
                        tm=512, tn=512, name="mem_kv1")
    memo1 = _mem_attn(proj_b, 2 * main_w // mem_w, 2 * main_w // mem_w + 1, kvm1,
                      bsz=bsz, tq=512, name="mem_attn1")
    att = _fox_attn(proj_b, kv, fparts, bsz=bsz, heads=heads, tq=512)
    out = _out_proj(att, memo1, h1, w_out[1], post_norm_g[1], tl=512)
    return out.reshape(bsz, seqlen, d)
```

```python
import functools
import math

import jax
import jax.numpy as jnp
from jax import lax
from jax.experimental import pallas as pl
from jax.experimental.pallas import tpu as pltpu

F32 = jnp.float32
BF16 = jnp.bfloat16

EPS = 1e-6
LOG2E = 1.4426950408889634
HEAD_DIM = 128
SSM_GROUP = 16
SSM_STATE = 64
MEM_HEADS = 4
LANES = 128
SUBLANES = 8
GROUPS_PER_BLOCK = LANES // SSM_GROUP
STATE_COLS = GROUPS_PER_BLOCK * SSM_STATE
VMEM_LIMIT = 56 * 1024 * 1024


def _params(sem, vmem=VMEM_LIMIT):
    return pltpu.CompilerParams(dimension_semantics=sem, vmem_limit_bytes=vmem)


def _norm_matmul_kernel(x_ref, g_ref, w_ref, o_ref, xn_ref):
    @pl.when(pl.program_id(1) == 0)
    def _():
        x = x_ref[...]
        ms = jnp.mean(x * x, axis=-1, keepdims=True)
        xn_ref[...] = (x * lax.rsqrt(ms + EPS) * g_ref[...]).astype(BF16)

    o_ref[...] = jnp.dot(xn_ref[...], w_ref[...],
                         preferred_element_type=F32).astype(o_ref.dtype)


def _norm_matmul(x2d, g, w, *, out_dtype, tm, tn, time_major_batches=None, name):
    m, d = x2d.shape
    n = w.shape[1]
    assert m % tm == 0 and n % tn == 0
    nj = n // tn
    if time_major_batches is None:
        out_shape = jax.ShapeDtypeStruct((m, n), out_dtype)
        out_spec = pl.BlockSpec((tm, tn), lambda i, j: (i, j))
    else:
        bsz = time_major_batches
        seqlen = m // bsz
        assert seqlen % tm == 0
        per_b = seqlen // tm
        out_shape = jax.ShapeDtypeStruct((seqlen, bsz * n), out_dtype)
        out_spec = pl.BlockSpec((tm, tn), lambda i, j: (i % per_b, (i // per_b) * nj + j))
    return pl.pallas_call(
        _norm_matmul_kernel,
        grid=(m // tm, nj),
        in_specs=[pl.BlockSpec((tm, d), lambda i, j: (i, 0)),
                  pl.BlockSpec((1, d), lambda i, j: (0, 0)),
                  pl.BlockSpec((d, tn), lambda i, j: (0, j))],
        out_specs=out_spec,
        out_shape=out_shape,
        scratch_shapes=[pltpu.VMEM((tm, d), BF16)],
        compiler_params=_params(("parallel", "arbitrary")),
        name=name,
    )(x2d, g.reshape(1, d), w)


def _prenorm_kernel(x_ref, g_ref, o_ref):
    x = x_ref[...]
    ms = jnp.mean(x * x, axis=-1, keepdims=True)
    o_ref[...] = (x * lax.rsqrt(ms + EPS) * g_ref[...]).astype(o_ref.dtype)


def _prenorm(x2d, g, *, tm):
    m, d = x2d.shape
    return pl.pallas_call(
        _prenorm_kernel,
        grid=(m // tm,),
        in_specs=[pl.BlockSpec((tm, d), lambda i: (i, 0)),
                  pl.BlockSpec((1, d), lambda i: (0, 0))],
        out_specs=pl.BlockSpec((tm, d), lambda i: (i, 0)),
        out_shape=jax.ShapeDtypeStruct((m, d), BF16),
        compiler_params=_params(("parallel",)),
        name="prenorm",
    )(x2d, g.reshape(1, d).astype(F32))


def _matmul_kernel(x_ref, w_ref, cs_ref, o_ref, wb_ref):
    @pl.when(pl.program_id(1) == 0)
    def _():
        wb_ref[...] = (w_ref[...] * cs_ref[...]).astype(BF16)

    res = jnp.dot(x_ref[...], wb_ref[...], preferred_element_type=F32).astype(o_ref.dtype)
    if len(o_ref.shape) == 3:
        sw = o_ref.shape[2]
        for s in range(o_ref.shape[0]):
            o_ref[s] = res[:, s * sw:(s + 1) * sw]
    else:
        o_ref[...] = res


def _matmul(xn, w, colscale, *, col0, n, out_dtype, tm, tn, time_major_batches=None,
            slab_width=None, name):
    m, d = xn.shape
    assert m % tm == 0 and n % tn == 0 and col0 % tn == 0
    nj = n // tn
    j0 = col0 // tn
    if slab_width is not None:
        assert time_major_batches is None and tn % slab_width == 0
        per_tile = tn // slab_width
        out_shape = jax.ShapeDtypeStruct((n // slab_width, m, slab_width), out_dtype)
        out_spec = pl.BlockSpec((per_tile, tm, slab_width), lambda j, i: (j, i, 0))
    elif time_major_batches is None:
        out_shape = jax.ShapeDtypeStruct((m, n), out_dtype)
        out_spec = pl.BlockSpec((tm, tn), lambda j, i: (i, j))
    else:
        bsz = time_major_batches
        seqlen = m // bsz
        assert seqlen % tm == 0
        per_b = seqlen // tm
        out_shape = jax.ShapeDtypeStruct((seqlen, bsz * n), out_dtype)
        out_spec = pl.BlockSpec((tm, tn), lambda j, i: (i % per_b, (i // per_b) * nj + j))
    return pl.pallas_call(
        _matmul_kernel,
        grid=(nj, m // tm),
        in_specs=[pl.BlockSpec((tm, d), lambda j, i: (i, 0)),
                  pl.BlockSpec((d, tn), lambda j, i: (0, j0 + j), pipeline_mode=pl.Buffered(1)),
                  pl.BlockSpec((1, tn), lambda j, i: (0, j0 + j))],
        out_specs=out_spec,
        out_shape=out_shape,
        scratch_shapes=[pltpu.VMEM((d, tn), BF16)],
        compiler_params=_params(("arbitrary", "arbitrary")),
        name=name,
    )(xn, w, colscale.reshape(1, -1).astype(F32))


def _split2(x):
    hi = x.astype(BF16)
    return hi, (x - hi.astype(F32)).astype(BF16)


def _split3(x):
    hi = x.astype(BF16)
    r = x - hi.astype(F32)
    mid = r.astype(BF16)
    return hi, mid, (r - mid.astype(F32)).astype(BF16)


def _fgate_block(xn, wcat_ref, b_ref, carry_ref, o_ref, heads):
    tl = xn.shape[0]
    xh, xl = _split2(xn)
    both = jnp.dot(xh, wcat_ref[...], preferred_element_type=F32)
    logit = (both[:, :LANES] + both[:, LANES:]
             + jnp.dot(xl, wcat_ref[:, :LANES], preferred_element_type=F32)) + b_ref[...]
    logf = jnp.minimum(logit, 0.0) - jnp.log(1.0 + jnp.exp(-jnp.abs(logit)))
    t_idx = lax.broadcasted_iota(jnp.int32, (tl, tl), 0)
    s_idx = lax.broadcasted_iota(jnp.int32, (tl, tl), 1)
    tril = (s_idx <= t_idx).astype(BF16)
    csum = carry_ref[...]
    for part in _split3(logf):
        csum = csum + jnp.dot(tril, part, preferred_element_type=F32)
    carry_ref[...] = csum[tl - 1:tl, :]
    lane = lax.broadcasted_iota(jnp.int32, (tl, LANES), 1)
    packed = jnp.zeros((tl, LANES), F32)
    for n, part in reversed(list(enumerate(_split3(csum * LOG2E)))):
        shifted = part.astype(F32) if n == 0 else pltpu.roll(part.astype(F32), n * heads, 1)
        packed = jnp.where(lane < (n + 1) * heads, shifted, packed)
    o_ref[...] = packed.astype(BF16)


def _mem_attn_kernel(q_ref, zm_ref, kv_ref, o_ref):
    width = MEM_HEADS * HEAD_DIM
    for h in range(MEM_HEADS):
        lo, hi = h * HEAD_DIM, (h + 1) * HEAD_DIM
        q = q_ref[:, lo:hi]
        k = kv_ref[:, lo:hi]
        v = kv_ref[:, width + lo:width + hi]
        s = lax.dot_general(q, k, (((1,), (1,)), ((), ())), preferred_element_type=F32)
        m = jnp.max(s, axis=-1, keepdims=True)
        p = jnp.exp(s - m)
        l = jnp.sum(p, axis=-1, keepdims=True)
        o = jnp.dot(p.astype(BF16), v, preferred_element_type=F32) / l
        zm = zm_ref[:, lo:hi].astype(F32)
        o_ref[:, lo:hi] = (o * (zm * jax.nn.sigmoid(zm))).astype(o_ref.dtype)


def _mem_attn(proj, q_blk, zm_blk, kvm, *, bsz, tq, name):
    width = MEM_HEADS * HEAD_DIM
    if proj.ndim == 3:
        m = proj.shape[1]
        spec = lambda blk: pl.BlockSpec((None, tq, width), lambda b, i: (blk, b * per_b + i, 0))
    else:
        m = proj.shape[0]
        spec = lambda blk: pl.BlockSpec((tq, width), lambda b, i: (b * per_b + i, blk))
    seqlen = m // bsz
    n_mem = kvm.shape[0] // bsz
    per_b = seqlen // tq
    return pl.pallas_call(
        _mem_attn_kernel,
        grid=(bsz, per_b),
        in_specs=[spec(q_blk), spec(zm_blk),
                  pl.BlockSpec((n_mem, 2 * width), lambda b, i: (b, 0))],
        out_specs=pl.BlockSpec((tq, width), lambda b, i: (b * per_b + i, 0)),
        out_shape=jax.ShapeDtypeStruct((m, width), BF16),
        compiler_params=_params(("parallel", "arbitrary")),
        name=name,
    )(proj, proj, kvm)


def _s5_kernel(u_ref, win_ref, wout_ref, wt_ref, ar_ref, ai_ref, d_ref, y_ref,
               bu_ref, xs_ref, st_ref, *, n_pairs):
    @pl.when(pl.program_id(1) == 0)
    def _():
        st_ref[...] = jnp.zeros_like(st_ref)

    rows = n_pairs * SUBLANES
    u0 = u_ref[:, 0].reshape(rows, LANES)
    u1 = u_ref[:, 1].reshape(rows, LANES)
    ucat = jnp.concatenate([u0, u1], axis=1).astype(BF16)
    bu_ref[...] = jnp.dot(ucat, win_ref[0], preferred_element_type=F32)
    ar = jnp.broadcast_to(ar_ref[0], (SUBLANES, STATE_COLS))
    ai = jnp.broadcast_to(ai_ref[0], (SUBLANES, STATE_COLS))

    def body(c, carry):
        xr, xi = carry
        r0 = pl.multiple_of(c * SUBLANES, SUBLANES)
        xs_ref[pl.ds(r0, SUBLANES), 0:STATE_COLS] = xr
        xs_ref[pl.ds(r0, SUBLANES), STATE_COLS:2 * STATE_COLS] = xi
        bur = bu_ref[pl.ds(r0, SUBLANES), 0:STATE_COLS]
        bui = bu_ref[pl.ds(r0, SUBLANES), STATE_COLS:2 * STATE_COLS]
        return ar * xr - ai * xi + bur, ar * xi + ai * xr + bui

    xr, xi = lax.fori_loop(0, n_pairs, body, (st_ref[0], st_ref[1]), unroll=True)
    st_ref[0] = xr
    st_ref[1] = xi
    y = (jnp.dot(xs_ref[...].astype(BF16), wout_ref[0], preferred_element_type=F32)
         + jnp.dot(ucat, wt_ref[0], preferred_element_type=F32))
    d = d_ref[...]
    y_ref[:, 0] = jax.nn.gelu(y[:, :LANES] + d * u0).reshape(n_pairs, SUBLANES, LANES)
    y_ref[:, 1] = jax.nn.gelu(y[:, LANES:] + d * u1).reshape(n_pairs, SUBLANES, LANES)


def _s5_discretise(lam_re, lam_im, log_step, b_re, b_im, c_re, c_im):
    groups = lam_re.shape[0]
    nblk = groups // GROUPS_PER_BLOCK
    lr = lam_re.astype(F32)
    li = lam_im.astype(F32)
    dt = jnp.exp(log_step.astype(F32))[:, None]
    mag = jnp.exp(lr * dt)
    ar = mag * jnp.cos(li * dt)
    ai = mag * jnp.sin(li * dt)
    den = lr * lr + li * li
    cr = ((ar - 1.0) * lr + ai * li) / den
    ci = (ai * lr - (ar - 1.0) * li) / den
    br = b_re.astype(F32)
    bi = b_im.astype(F32)
    bbar_re = cr[..., None] * br - ci[..., None] * bi
    bbar_im = cr[..., None] * bi + ci[..., None] * br
    a2r = ar * ar - ai * ai
    a2i = 2.0 * ar * ai
    ab_re = ar[..., None] * bbar_re - ai[..., None] * bbar_im
    ab_im = ar[..., None] * bbar_im + ai[..., None] * bbar_re
    cre = c_re.astype(F32)
    cim = c_im.astype(F32)
    ca_re = cre * ar[:, None, :] - cim * ai[:, None, :]
    ca_im = cre * ai[:, None, :] + cim * ar[:, None, :]
    ca2_re = cre * a2r[:, None, :] - cim * a2i[:, None, :]
    ca2_im = cre * a2i[:, None, :] + cim * a2r[:, None, :]
    k0 = jnp.einsum('ghp,gpk->ghk', cre, bbar_re) - jnp.einsum('ghp,gpk->ghk', cim, bbar_im)
    k1 = jnp.einsum('ghp,gpk->ghk', cre, ab_re) - jnp.einsum('ghp,gpk->ghk', cim, ab_im)
    eye = jnp.eye(GROUPS_PER_BLOCK, dtype=F32)

    def blockdiag_in(b):
        b4 = b.reshape(nblk, GROUPS_PER_BLOCK, SSM_STATE, SSM_GROUP)
        return jnp.einsum('jgph,gk->jghkp', b4, eye).reshape(nblk, LANES, STATE_COLS)

    def blockdiag_out(c):
        c4 = c.reshape(nblk, GROUPS_PER_BLOCK, SSM_GROUP, SSM_STATE)
        return jnp.einsum('jghp,gk->jgpkh', c4, eye).reshape(nblk, STATE_COLS, LANES)

    def blockdiag_mix(k):
        k4 = k.reshape(nblk, GROUPS_PER_BLOCK, SSM_GROUP, SSM_GROUP)
        return jnp.einsum('jghk,gm->jgkmh', k4, eye).reshape(nblk, LANES, LANES)

    win = jnp.concatenate(
        [jnp.concatenate([blockdiag_in(ab_re), blockdiag_in(ab_im)], axis=2),
         jnp.concatenate([blockdiag_in(bbar_re), blockdiag_in(bbar_im)], axis=2)], axis=1)
    wout = jnp.concatenate(
        [jnp.concatenate([blockdiag_out(ca_re), blockdiag_out(ca2_re)], axis=2),
         jnp.concatenate([-blockdiag_out(ca_im), -blockdiag_out(ca2_im)], axis=2)], axis=1)
    m0 = blockdiag_mix(k0)
    wt = jnp.concatenate(
        [jnp.concatenate([m0, blockdiag_mix(k1)], axis=2),
         jnp.concatenate([jnp.zeros_like(m0), m0], axis=2)], axis=1)
    return (win.astype(BF16), wout.astype(BF16), wt.astype(BF16),
            a2r.reshape(nblk, 1, STATE_COLS), a2i.reshape(nblk, 1, STATE_COLS))


def _s5(u_tm, win, wout, wt, a2r, a2i, d_skip, *, bsz, n_pairs):
    seqlen = u_tm.shape[0]
    width = u_tm.shape[1] // bsz
    assert bsz == SUBLANES and seqlen % (2 * n_pairs) == 0 and width % LANES == 0
    nblk = width // LANES
    rows = n_pairs * SUBLANES
    u4 = u_tm.reshape(seqlen // 2, 2, bsz, width)
    blk = pl.BlockSpec((n_pairs, 2, bsz, LANES), lambda j, c: (c, 0, 0, j))
    per_blk = lambda shape: pl.BlockSpec((1,) + shape, lambda j, c: (j, 0, 0))
    out = pl.pallas_call(
        functools.partial(_s5_kernel, n_pairs=n_pairs),
        grid=(nblk, seqlen // (2 * n_pairs)),
        in_specs=[blk,
                  per_blk((2 * LANES, 2 * STATE_COLS)),
                  per_blk((2 * STATE_COLS, 2 * LANES)),
                  per_blk((2 * LANES, 2 * LANES)),
                  per_blk((1, STATE_COLS)),
                  per_blk((1, STATE_COLS)),
                  pl.BlockSpec((1, LANES), lambda j, c: (0, j))],
        out_specs=blk,
        out_shape=jax.ShapeDtypeStruct((seqlen // 2, 2, bsz, width), F32),
        scratch_shapes=[pltpu.VMEM((rows, 2 * STATE_COLS), F32),
                        pltpu.VMEM((rows, 2 * STATE_COLS), F32),
                        pltpu.VMEM((2, SUBLANES, STATE_COLS), F32)],
        compiler_params=_params(("arbitrary", "arbitrary")),
        name="s5_scan",
    )(u4, win, wout, wt, a2r, a2i, d_skip.reshape(1, width).astype(F32))
    return out.reshape(seqlen, bsz * width)


def _post(o, g_ref, h_ref, out_ref):
    ms = jnp.mean(o * o, axis=-1, keepdims=True)
    out_ref[...] = h_ref[...] + o * lax.rsqrt(ms + EPS) * g_ref[...]


def _glu_out_kernel(yg_ref, z_ref, memo_ref, h_ref, wglu_ref, bglu_ref,
                    wmain_ref, wmem_ref, g_ref, gkv_ref, gnext_ref, wf_ref, bf_ref,
                    out_ref, xkv_ref, xnext_ref, fp_ref, carry_ref, *, heads):
    @pl.when(pl.program_id(1) == 0)
    def _():
        carry_ref[...] = jnp.zeros_like(carry_ref)

    yg = yg_ref[...]
    t = jnp.dot(yg.astype(BF16), wglu_ref[...], preferred_element_type=F32) + bglu_ref[...]
    z = jnp.concatenate([z_ref[s] for s in range(z_ref.shape[0])], axis=1).astype(F32)
    main = yg * jax.nn.sigmoid(t) * (z * jax.nn.sigmoid(z))
    o = jnp.dot(main.astype(BF16), wmain_ref[...], preferred_element_type=F32)
    o = o + jnp.dot(memo_ref[...], wmem_ref[...], preferred_element_type=F32)
    _post(o, g_ref, h_ref, out_ref)
    h1 = out_ref[...]
    r = h1 * lax.rsqrt(jnp.mean(h1 * h1, axis=-1, keepdims=True) + EPS)
    xnext_ref[...] = (r * gnext_ref[...]).astype(BF16)
    xkv = r * gkv_ref[...]
    xkv_ref[...] = xkv.astype(BF16)
    _fgate_block(xkv, wf_ref, bf_ref, carry_ref, fp_ref, heads)


def _out_kernel(main_ref, memo_ref, h_ref, wmain_ref, wmem_ref, g_ref, out_ref):
    o = jnp.dot(main_ref[...], wmain_ref[...], preferred_element_type=F32)
    o = o + jnp.dot(memo_ref[...], wmem_ref[...], preferred_element_type=F32)
    _post(o, g_ref, h_ref, out_ref)


def _resident(shape):
    return pl.BlockSpec(shape, lambda *_: (0,) * len(shape), pipeline_mode=pl.Buffered(1))


def _glu_out(yg_tm, rest, memo, h2d, w_glu, b_glu, w_out, g, g_kv, g_next, w_fgate, b_fgate,
             *, bsz, tl):
    m, d = h2d.shape
    seqlen = m // bsz
    main_w = w_glu.shape[0]
    mem_w = w_out.shape[0] - main_w
    heads = w_fgate.shape[1]
    assert 3 * heads <= LANES and seqlen % tl == 0
    wpad = jnp.zeros((d, LANES), F32).at[:, :heads].set(w_fgate.astype(F32))
    wcat = jnp.concatenate(_split2(wpad), axis=1)
    bpad = jnp.zeros((1, LANES), F32).at[0, :heads].set(b_fgate.astype(F32))
    per_b = seqlen // tl
    row = lambda b, i: (b * per_b + i, 0)
    vec = lambda v: v.reshape(1, -1).astype(F32)
    return pl.pallas_call(
        functools.partial(_glu_out_kernel, heads=heads),
        grid=(bsz, per_b),
        in_specs=[pl.BlockSpec((tl, main_w), lambda b, i: (i, b)),
                  pl.BlockSpec((main_w // rest.shape[2], tl, rest.shape[2]),
                               lambda b, i: (0, b * per_b + i, 0)),
                  pl.BlockSpec((tl, mem_w), row),
                  pl.BlockSpec((tl, d), row),
                  _resident((main_w, main_w)),
                  _resident((1, main_w)),
                  _resident((main_w, d)),
                  _resident((mem_w, d)),
                  _resident((1, d)),
                  _resident((1, d)),
                  _resident((1, d)),
                  _resident((d, 2 * LANES)),
                  _resident((1, LANES))],
        out_specs=[pl.BlockSpec((tl, d), row),
                   pl.BlockSpec((tl, d), row),
                   pl.BlockSpec((tl, d), row),
                   pl.BlockSpec((tl, LANES), row)],
        out_shape=[jax.ShapeDtypeStruct((m, d), F32),
                   jax.ShapeDtypeStruct((m, d), BF16),
                   jax.ShapeDtypeStruct((m, d), BF16),
                   jax.ShapeDtypeStruct((m, LANES), BF16)],
        scratch_shapes=[pltpu.VMEM((1, LANES), F32)],
        compiler_params=_params(("arbitrary", "arbitrary")),
        name="glu_out_proj",
    )(yg_tm, rest, memo, h2d, w_glu.astype(BF16), vec(b_glu),
      w_out[:main_w].astype(BF16), w_out[main_w:].astype(BF16), vec(g), vec(g_kv),
      vec(g_next), wcat, bpad)


def _out_proj(main, memo, h2d, w_out, g, *, tl):
    m, d = h2d.shape
    main_w = main.shape[1]
    mem_w = memo.shape[1]
    row = lambda i: (i, 0)
    return pl.pallas_call(
        _out_kernel,
        grid=(m // tl,),
        in_specs=[pl.BlockSpec((tl, main_w), row),
                  pl.BlockSpec((tl, mem_w), row),
                  pl.BlockSpec((tl, d), row),
                  _resident((main_w, d)),
                  _resident((mem_w, d)),
                  _resident((1, d))],
        out_specs=pl.BlockSpec((tl, d), row),
        out_shape=jax.ShapeDtypeStruct((m, d), F32),
        compiler_params=_params(("parallel",)),
        name="out_proj",
    )(main, memo, h2d, w_out[:main_w].astype(BF16), w_out[main_w:].astype(BF16),
      g.reshape(1, d).astype(F32))


FOX_CHUNK = 64


def _fox_kernel(q_ref, k_ref, v_ref, fp_ref, sel_ref, cst_ref, z_ref, o_ref,
                ka_ref, qa_ref, vt_ref, st_ref, pt_ref, m_ref, l_ref, acc_ref, *, tq):
    seqlen = q_ref.shape[0]
    ncg = tq // LANES
    nch = tq // FOX_CHUNK
    ext = jnp.dot(fp_ref[...], sel_ref[0], preferred_element_type=F32)
    ka_ref[:, :HEAD_DIM] = k_ref[...]
    ka_ref[:, HEAD_DIM:] = (ext[:, :HEAD_DIM] + cst_ref[0:1, :]).astype(BF16)
    qa_ref[:, :HEAD_DIM] = q_ref[...]
    qa_ref[:, HEAD_DIM:] = (ext[:, HEAD_DIM:] + cst_ref[1:2, :]).astype(BF16)
    vt_ref[...] = v_ref[...].astype(F32).T.astype(BF16)
    for ci in range(nch):
        for g in range(ncg):
            if g * LANES + LANES - 1 < ci * FOX_CHUNK:
                pt_ref[1, ci * FOX_CHUNK:(ci + 1) * FOX_CHUNK, g * LANES:(g + 1) * LANES] = (
                    jnp.zeros((FOX_CHUNK, LANES), BF16))

    def scores(qi, kj, slot):
        st_ref[slot] = lax.dot_general(ka_ref[kj * tq:(kj + 1) * tq, :],
                                       qa_ref[qi * tq:(qi + 1) * tq, :],
                                       (((1,), (1,)), ((), ())),
                                       preferred_element_type=F32)

    def piece(slot, ci, g, masked):
        r0, c0 = ci * FOX_CHUNK, g * LANES
        if masked and c0 + LANES - 1 < r0:
            return None
        x = st_ref[slot, r0:r0 + FOX_CHUNK, c0:c0 + LANES]
        if masked and c0 < r0 + FOX_CHUNK - 1:
            key = r0 + lax.broadcasted_iota(jnp.int32, (FOX_CHUNK, LANES), 0)
            qry = c0 + lax.broadcasted_iota(jnp.int32, (FOX_CHUNK, LANES), 1)
            x = jnp.where(key <= qry, x, -jnp.inf)
        return x

    def fold(x, op):
        return op(x.reshape(FOX_CHUNK // SUBLANES, SUBLANES, LANES), axis=0)

    def softmax_pv(kj, slot, masked, first):
        mx = [None] * ncg
        for ci in range(nch):
            for g in range(ncg):
                x = piece(slot, ci, g, masked)
                if x is not None:
                    f = fold(x, jnp.max)
                    mx[g] = f if mx[g] is None else jnp.maximum(mx[g], f)
        m_new = jnp.concatenate([jnp.max(v, axis=0, keepdims=True) for v in mx], axis=1)
        if not first:
            m_old = m_ref[...]
            m_new = jnp.maximum(m_old, m_new)
            alpha = jnp.exp2(m_old - m_new)
        m_ref[...] = m_new
        ls = [None] * ncg
        for ci in range(nch):
            for g in range(ncg):
                r0, c0 = ci * FOX_CHUNK, g * LANES
                x = piece(slot, ci, g, masked)
                if x is None:
                    continue
                p = jnp.exp2(x - m_new[:, c0:c0 + LANES])
                f = fold(p, jnp.sum)
                ls[g] = f if ls[g] is None else ls[g] + f
                pt_ref[int(masked), r0:r0 + FOX_CHUNK, c0:c0 + LANES] = p.astype(BF16)
        l_new = jnp.concatenate([jnp.sum(v, axis=0, keepdims=True) for v in ls], axis=1)
        pv = jnp.dot(vt_ref[:, kj * tq:(kj + 1) * tq], pt_ref[int(masked)],
                     preferred_element_type=F32)
        if first:
            l_ref[...] = l_new
            acc_ref[...] = pv
        else:
            l_ref[...] = alpha * l_ref[...] + l_new
            acc_ref[...] = alpha * acc_ref[...] + pv

    for qi in range(seqlen // tq):
        rows = slice(qi * tq, (qi + 1) * tq)
        scores(qi, 0, 0)
        for kj in range(qi + 1):
            if kj < qi:
                scores(qi, kj + 1, (kj + 1) % 2)
            softmax_pv(kj, kj % 2, masked=(kj == qi), first=(kj == 0))
        z = z_ref[rows, :].astype(F32)
        o = (acc_ref[...] / l_ref[...]).T
        o_ref[rows, :] = (o * (z * jax.nn.sigmoid(z))).astype(o_ref.dtype)


def _fox_attn(proj, kv, fparts, *, bsz, heads, tq):
    m = proj.shape[0]
    seqlen = m // bsz
    nterms = 3
    assert seqlen % tq == 0 and tq % FOX_CHUNK == 0 and nterms * heads <= LANES
    h_idx = jnp.arange(heads)[:, None, None]
    r_idx = jnp.arange(LANES)[None, :, None]
    c_idx = jnp.arange(2 * HEAD_DIM)[None, None, :]
    key_side = (c_idx < nterms) & (r_idx == c_idx * heads + h_idx)
    qc = c_idx - HEAD_DIM - nterms
    qry_side = (qc >= 0) & (qc < nterms) & (r_idx == qc * heads + h_idx)
    sel = (qry_side.astype(F32) - key_side.astype(F32)).astype(BF16)
    col = jnp.arange(HEAD_DIM)
    cst = jnp.zeros((SUBLANES, HEAD_DIM), F32)
    cst = cst.at[0].set(((col >= nterms) & (col < 2 * nterms)).astype(F32))
    cst = cst.at[1].set((col < nterms).astype(F32))
    return pl.pallas_call(
        functools.partial(_fox_kernel, tq=tq),
        grid=(bsz, heads),
        in_specs=[pl.BlockSpec((seqlen, HEAD_DIM), lambda b, h: (b, h)),
                  pl.BlockSpec((seqlen, HEAD_DIM), lambda b, h: (b, h)),
                  pl.BlockSpec((seqlen, HEAD_DIM), lambda b, h: (b, heads + h)),
                  pl.BlockSpec((seqlen, LANES), lambda b, h: (b, 0)),
                  pl.BlockSpec((1, LANES, 2 * HEAD_DIM), lambda b, h: (h, 0, 0)),
                  pl.BlockSpec((SUBLANES, HEAD_DIM), lambda b, h: (0, 0)),
                  pl.BlockSpec((seqlen, HEAD_DIM), lambda b, h: (b, heads + h))],
        out_specs=pl.BlockSpec((seqlen, HEAD_DIM), lambda b, h: (b, h)),
        out_shape=jax.ShapeDtypeStruct((m, heads * HEAD_DIM), BF16),
        scratch_shapes=[pltpu.VMEM((seqlen, 2 * HEAD_DIM), BF16),
                        pltpu.VMEM((seqlen, 2 * HEAD_DIM), BF16),
                        pltpu.VMEM((HEAD_DIM, seqlen), BF16),
                        pltpu.VMEM((2, tq, tq), F32),
                        pltpu.VMEM((2, tq, tq), BF16),
                        pltpu.VMEM((1, tq), F32),
                        pltpu.VMEM((1, tq), F32),
                        pltpu.VMEM((HEAD_DIM, tq), F32)],
        compiler_params=_params(("arbitrary", "arbitrary")),
        name="fox_attn",
    )(proj, kv, kv, fparts, sel, cst, proj)


def kernel(x, mem, pre_norm_g, post_norm_g, w_in_a, lam_re, lam_im, log_step, b_re, b_im,
           c_re, c_im, d_skip, w_glu, b_glu, kv_norm_g, w_kv, w_fgate, b_fgate, w_in_b,
           mem_norm_g, w_mem_kv, w_out):
    bsz, seqlen, d = x.shape
    n_mem = mem.shape[1]
    main_w = w_glu.shape[1]
    mem_w = w_out.shape[1] - main_w
    heads = main_w // HEAD_DIM
    scale = HEAD_DIM ** -0.5
    x2d = x.reshape(bsz * seqlen, d)
    mem2d = mem.reshape(bsz * n_mem, d)

    ones = jnp.ones((main_w,), F32)
    mem_scale = jnp.full((mem_w,), scale, F32)
    cs_a = jnp.concatenate([ones, ones, mem_scale, ones[:mem_w]])
    cs_b = jnp.concatenate([ones * (scale * LOG2E), ones, mem_scale, ones[:mem_w]])
    xa = _prenorm(x2d, pre_norm_g[0], tm=512)
    u_tm = _matmul(xa, w_in_a[0], cs_a, col0=0, n=main_w, out_dtype=F32, tm=1024, tn=768,
                   time_major_batches=bsz, name="in_proj_a_u")
    rest_a = _matmul(xa, w_in_a[0], cs_a, col0=main_w, n=main_w + 2 * mem_w, out_dtype=BF16,
                     tm=1024, tn=512, slab_width=mem_w, name="in_proj_a_rest")
    kvm0 = _norm_matmul(mem2d, mem_norm_g[0], w_mem_kv[0].astype(BF16), out_dtype=BF16,
                        tm=512, tn=512, name="mem_kv0")
    memo0 = _mem_attn(rest_a, main_w // mem_w, main_w // mem_w + 1, kvm0,
                      bsz=bsz, tq=512, name="mem_attn0")
    s5_ops = _s5_discretise(lam_re[0], lam_im[0], log_step[0], b_re[0], b_im[0],
                            c_re[0], c_im[0])
    yg_tm = _s5(u_tm, *s5_ops, d_skip[0], bsz=bsz, n_pairs=128)
    h1, xkv, xb, fparts = _glu_out(
        yg_tm, rest_a, memo0, x2d, w_glu[0], b_glu[0], w_out[0], post_norm_g[0],
        kv_norm_g, pre_norm_g[1], w_fgate, b_fgate, bsz=bsz, tl=256)

    kv = _matmul(xkv, w_kv, jnp.ones((w_kv.shape[1],), F32), col0=0, n=w_kv.shape[1],
                 out_dtype=BF16, tm=2048, tn=1024, name="kv_proj")

    proj_b = _matmul(xb, w_in_b[0], cs_b, col0=0, n=w_in_b.shape[2], out_dtype=BF16,
                     tm=2048, tn=1024, name="in_proj_b")
    kvm1 = _norm_matmul(mem2d, mem_norm_g[1], w_mem_kv[1].astype(BF16), out_dtype=BF16,
                        tm=512, tn=512, name="mem_kv1")
    memo1 = _mem_attn(proj_b, 2 * main_w // mem_w, 2 * main_w // mem_w + 1, kvm1,
                      bsz=bsz, tq=512, name="mem_attn1")
    att = _fox_attn(proj_b, kv, fparts, bsz=bsz, heads=heads, tq=512)
    out = _out_proj(att, memo1, h1, w_out[1], post_norm_g[1], tl=512)
    return out.reshape(bsz, seqlen, d)
```

```python
import functools
import math

import jax
import jax.numpy as jnp
from jax import lax
from jax.experimental import pallas as pl
from jax.experimental.pallas import tpu as pltpu

F32 = jnp.float32
BF16 = jnp.bfloat16

EPS = 1e-6
LOG2E = 1.4426950408889634
HEAD_DIM = 128
SSM_GROUP = 16
SSM_STATE = 64
MEM_HEADS = 4
LANES = 128
SUBLANES = 8
GROUPS_PER_BLOCK = LANES // SSM_GROUP
STATE_COLS = GROUPS_PER_BLOCK * SSM_STATE
VMEM_LIMIT = 56 * 1024 * 1024


def _params(sem, vmem=VMEM_LIMIT):
    return pltpu.CompilerParams(dimension_semantics=sem, vmem_limit_bytes=vmem)


def _norm_matmul_kernel(x_ref, g_ref, w_ref, o_ref, xn_ref):
    @pl.when(pl.program_id(1) == 0)
    def _():
        x = x_ref[...]
        ms = jnp.mean(x * x, axis=-1, keepdims=True)
        xn_ref[...] = (x * lax.rsqrt(ms + EPS) * g_ref[...]).astype(BF16)

    o_ref[...] = jnp.dot(xn_ref[...], w_ref[...],
                         preferred_element_type=F32).astype(o_ref.dtype)


def _norm_matmul(x2d, g, w, *, out_dtype, tm, tn, time_major_batches=None, name):
    m, d = x2d.shape
    n = w.shape[1]
    assert m % tm == 0 and n % tn == 0
    nj = n // tn
    if time_major_batches is None:
        out_shape = jax.ShapeDtypeStruct((m, n), out_dtype)
        out_spec = pl.BlockSpec((tm, tn), lambda i, j: (i, j))
    else:
        bsz = time_major_batches
        seqlen = m // bsz
        assert seqlen % tm == 0
        per_b = seqlen // tm
        out_shape = jax.ShapeDtypeStruct((seqlen, bsz * n), out_dtype)
        out_spec = pl.BlockSpec((tm, tn), lambda i, j: (i % per_b, (i // per_b) * nj + j))
    return pl.pallas_call(
        _norm_matmul_kernel,
        grid=(m // tm, nj),
        in_specs=[pl.BlockSpec((tm, d), lambda i, j: (i, 0)),
                  pl.BlockSpec((1, d), lambda i, j: (0, 0)),
                  pl.BlockSpec((d, tn), lambda i, j: (0, j))],
        out_specs=out_spec,
        out_shape=out_shape,
        scratch_shapes=[pltpu.VMEM((tm, d), BF16)],
        compiler_params=_params(("parallel", "arbitrary")),
        name=name,
    )(x2d, g.reshape(1, d), w)


def _prenorm_kernel(x_ref, g_ref, o_ref):
    x = x_ref[...]
    ms = jnp.mean(x * x, axis=-1, keepdims=True)
    o_ref[...] = (x * lax.rsqrt(ms + EPS) * g_ref[...]).astype(o_ref.dtype)


def _prenorm(x2d, g, *, tm):
    m, d = x2d.shape
    return pl.pallas_call(
        _prenorm_kernel,
        grid=(m // tm,),
        in_specs=[pl.BlockSpec((tm, d), lambda i: (i, 0)),
                  pl.BlockSpec((1, d), lambda i: (0, 0))],
        out_specs=pl.BlockSpec((tm, d), lambda i: (i, 0)),
        out_shape=jax.ShapeDtypeStruct((m, d), BF16),
        compiler_params=_params(("parallel",)),
        name="prenorm",
    )(x2d, g.reshape(1, d).astype(F32))


def _matmul_kernel(x_ref, w_ref, cs_ref, o_ref, wb_ref):
    @pl.when(pl.program_id(1) == 0)
    def _():
        wb_ref[...] = (w_ref[...] * cs_ref[...]).astype(BF16)

    res = jnp.dot(x_ref[...], wb_ref[...], preferred_element_type=F32).astype(o_ref.dtype)
    if len(o_ref.shape) == 3:
        sw = o_ref.shape[2]
        for s in range(o_ref.shape[0]):
            o_ref[s] = res[:, s * sw:(s + 1) * sw]
    else:
        o_ref[...] = res


def _matmul(xn, w, colscale, *, col0, n, out_dtype, tm, tn, time_major_batches=None,
            slab_width=None, name):
    m, d = xn.shape
    assert m % tm == 0 and n % tn == 0 and col0 % tn == 0
    nj = n // tn
    j0 = col0 // tn
    if slab_width is not None:
        assert time_major_batches is None and tn % slab_width == 0
        per_tile = tn // slab_width
        out_shape = jax.ShapeDtypeStruct((n // slab_width, m, slab_width), out_dtype)
        out_spec = pl.BlockSpec((per_tile, tm, slab_width), lambda j, i: (j, i, 0))
    elif time_major_batches is None:
        out_shape = jax.ShapeDtypeStruct((m, n), out_dtype)
        out_spec = pl.BlockSpec((tm, tn), lambda j, i: (i, j))
    else:
        bsz = time_major_batches
        seqlen = m // bsz
        assert seqlen % tm == 0
        per_b = seqlen // tm
        out_shape = jax.ShapeDtypeStruct((seqlen, bsz * n), out_dtype)
        out_spec = pl.BlockSpec((tm, tn), lambda j, i: (i % per_b, (i // per_b) * nj + j))
    return pl.pallas_call(
        _matmul_kernel,
        grid=(nj, m // tm),
        in_specs=[pl.BlockSpec((tm, d), lambda j, i: (i, 0)),
                  pl.BlockSpec((d, tn), lambda j, i: (0, j0 + j), pipeline_mode=pl.Buffered(1)),
                  pl.BlockSpec((1, tn), lambda j, i: (0, j0 + j))],
        out_specs=out_spec,
        out_shape=out_shape,
        scratch_shapes=[pltpu.VMEM((d, tn), BF16)],
        compiler_params=_params(("arbitrary", "arbitrary")),
        name=name,
    )(xn, w, colscale.reshape(1, -1).astype(F32))


def _split2(x):
    hi = x.astype(BF16)
    return hi, (x - hi.astype(F32)).astype(BF16)


def _split3(x):
    hi = x.astype(BF16)
    r = x - hi.astype(F32)
    mid = r.astype(BF16)
    return hi, mid, (r - mid.astype(F32)).astype(BF16)


def _fgate_block(xn, wcat_ref, b_ref, carry_ref, o_ref, heads):
    tl = xn.shape[0]
    xh, xl = _split2(xn)
    both = jnp.dot(xh, wcat_ref[...], preferred_element_type=F32)
    logit = (both[:, :LANES] + both[:, LANES:]
             + jnp.dot(xl, wcat_ref[:, :LANES], preferred_element_type=F32)) + b_ref[...]
    logf = jnp.minimum(logit, 0.0) - jnp.log(1.0 + jnp.exp(-jnp.abs(logit)))
    t_idx = lax.broadcasted_iota(jnp.int32, (tl, tl), 0)
    s_idx = lax.broadcasted_iota(jnp.int32, (tl, tl), 1)
    tril = (s_idx <= t_idx).astype(BF16)
    csum = carry_ref[...]
    for part in _split3(logf):
        csum = csum + jnp.dot(tril, part, preferred_element_type=F32)
    carry_ref[...] = csum[tl - 1:tl, :]
    lane = lax.broadcasted_iota(jnp.int32, (tl, LANES), 1)
    packed = jnp.zeros((tl, LANES), F32)
    for n, part in reversed(list(enumerate(_split3(csum * LOG2E)))):
        shifted = part.astype(F32) if n == 0 else pltpu.roll(part.astype(F32), n * heads, 1)
        packed = jnp.where(lane < (n + 1) * heads, shifted, packed)
    o_ref[...] = packed.astype(BF16)


def _mem_attn_kernel(q_ref, zm_ref, kv_ref, o_ref):
    width = MEM_HEADS * HEAD_DIM
    for h in range(MEM_HEADS):
        lo, hi = h * HEAD_DIM, (h + 1) * HEAD_DIM
        q = q_ref[h]
        k = kv_ref[:, lo:hi]
        v = kv_ref[:, width + lo:width + hi]
        s = lax.dot_general(q, k, (((1,), (1,)), ((), ())), preferred_element_type=F32)
        m = jnp.max(s, axis=-1, keepdims=True)
        p = jnp.exp(s - m)
        l = jnp.sum(p, axis=-1, keepdims=True)
        o = jnp.dot(p.astype(BF16), v, preferred_element_type=F32) / l
        zm = zm_ref[h].astype(F32)
        o_ref[:, lo:hi] = (o * (zm * jax.nn.sigmoid(zm))).astype(o_ref.dtype)


def _mem_attn(qz, kvm, *, bsz, tq, name):
    width = MEM_HEADS * HEAD_DIM
    m = qz.shape[1]
    seqlen = m // bsz
    n_mem = kvm.shape[0] // bsz
    per_b = seqlen // tq
    spec = lambda blk: pl.BlockSpec((MEM_HEADS, tq, HEAD_DIM),
                                    lambda b, i: (blk, b * per_b + i, 0))
    return pl.pallas_call(
        _mem_attn_kernel,
        grid=(bsz, per_b),
        in_specs=[spec(0), spec(1),
                  pl.BlockSpec((n_mem, 2 * width), lambda b, i: (b, 0))],
        out_specs=pl.BlockSpec((tq, width), lambda b, i: (b * per_b + i, 0)),
        out_shape=jax.ShapeDtypeStruct((m, width), BF16),
        compiler_params=_params(("parallel", "arbitrary")),
        name=name,
    )(qz, qz, kvm)


def _s5_kernel(u_ref, win_ref, wout_ref, wt_ref, ar_ref, ai_ref, d_ref, y_ref,
               bu_ref, xs_ref, st_ref, *, n_pairs):
    @pl.when(pl.program_id(1) == 0)
    def _():
        st_ref[...] = jnp.zeros_like(st_ref)

    rows = n_pairs * SUBLANES
    upair = u_ref[...].reshape(n_pairs, 2, SUBLANES, LANES)
    u0 = upair[:, 0].reshape(rows, LANES)
    u1 = upair[:, 1].reshape(rows, LANES)
    ucat = jnp.concatenate([u0, u1], axis=1).astype(BF16)
    bu_ref[...] = jnp.dot(ucat, win_ref[0], preferred_element_type=F32)
    ar = jnp.broadcast_to(ar_ref[0], (SUBLANES, STATE_COLS))
    ai = jnp.broadcast_to(ai_ref[0], (SUBLANES, STATE_COLS))

    def body(c, carry):
        xr, xi = carry
        r0 = pl.multiple_of(c * SUBLANES, SUBLANES)
        xs_ref[pl.ds(r0, SUBLANES), 0:STATE_COLS] = xr
        xs_ref[pl.ds(r0, SUBLANES), STATE_COLS:2 * STATE_COLS] = xi
        bur = bu_ref[pl.ds(r0, SUBLANES), 0:STATE_COLS]
        bui = bu_ref[pl.ds(r0, SUBLANES), STATE_COLS:2 * STATE_COLS]
        return ar * xr - ai * xi + bur, ar * xi + ai * xr + bui

    xr, xi = lax.fori_loop(0, n_pairs, body, (st_ref[0], st_ref[1]), unroll=True)
    st_ref[0] = xr
    st_ref[1] = xi
    y = (jnp.dot(xs_ref[...].astype(BF16), wout_ref[0], preferred_element_type=F32)
         + jnp.dot(ucat, wt_ref[0], preferred_element_type=F32))
    d = d_ref[...]
    y0 = jax.nn.gelu(y[:, :LANES] + d * u0).reshape(n_pairs, SUBLANES, LANES)
    y1 = jax.nn.gelu(y[:, LANES:] + d * u1).reshape(n_pairs, SUBLANES, LANES)
    y_ref[...] = jnp.stack([y0, y1], axis=1).reshape(2 * n_pairs, SUBLANES, LANES)


def _s5_discretise(lam_re, lam_im, log_step, b_re, b_im, c_re, c_im):
    groups = lam_re.shape[0]
    nblk = groups // GROUPS_PER_BLOCK
    lr = lam_re.astype(F32)
    li = lam_im.astype(F32)
    dt = jnp.exp(log_step.astype(F32))[:, None]
    mag = jnp.exp(lr * dt)
    ar = mag * jnp.cos(li * dt)
    ai = mag * jnp.sin(li * dt)
    den = lr * lr + li * li
    cr = ((ar - 1.0) * lr + ai * li) / den
    ci = (ai * lr - (ar - 1.0) * li) / den
    br = b_re.astype(F32)
    bi = b_im.astype(F32)
    bbar_re = cr[..., None] * br - ci[..., None] * bi
    bbar_im = cr[..., None] * bi + ci[..., None] * br
    a2r = ar * ar - ai * ai
    a2i = 2.0 * ar * ai
    ab_re = ar[..., None] * bbar_re - ai[..., None] * bbar_im
    ab_im = ar[..., None] * bbar_im + ai[..., None] * bbar_re
    cre = c_re.astype(F32)
    cim = c_im.astype(F32)
    ca_re = cre * ar[:, None, :] - cim * ai[:, None, :]
    ca_im = cre * ai[:, None, :] + cim * ar[:, None, :]
    ca2_re = cre * a2r[:, None, :] - cim * a2i[:, None, :]
    ca2_im = cre * a2i[:, None, :] + cim * a2r[:, None, :]
    k0 = jnp.einsum('ghp,gpk->ghk', cre, bbar_re) - jnp.einsum('ghp,gpk->ghk', cim, bbar_im)
    k1 = jnp.einsum('ghp,gpk->ghk', cre, ab_re) - jnp.einsum('ghp,gpk->ghk', cim, ab_im)
    eye = jnp.eye(GROUPS_PER_BLOCK, dtype=F32)

    def blockdiag_in(b):
        b4 = b.reshape(nblk, GROUPS_PER_BLOCK, SSM_STATE, SSM_GROUP)
        return jnp.einsum('jgph,gk->jghkp', b4, eye).reshape(nblk, LANES, STATE_COLS)

    def blockdiag_out(c):
        c4 = c.reshape(nblk, GROUPS_PER_BLOCK, SSM_GROUP, SSM_STATE)
        return jnp.einsum('jghp,gk->jgpkh', c4, eye).reshape(nblk, STATE_COLS, LANES)

    def blockdiag_mix(k):
        k4 = k.reshape(nblk, GROUPS_PER_BLOCK, SSM_GROUP, SSM_GROUP)
        return jnp.einsum('jghk,gm->jgkmh', k4, eye).reshape(nblk, LANES, LANES)

    win = jnp.concatenate(
        [jnp.concatenate([blockdiag_in(ab_re), blockdiag_in(ab_im)], axis=2),
         jnp.concatenate([blockdiag_in(bbar_re), blockdiag_in(bbar_im)], axis=2)], axis=1)
    wout = jnp.concatenate(
        [jnp.concatenate([blockdiag_out(ca_re), blockdiag_out(ca2_re)], axis=2),
         jnp.concatenate([-blockdiag_out(ca_im), -blockdiag_out(ca2_im)], axis=2)], axis=1)
    m0 = blockdiag_mix(k0)
    wt = jnp.concatenate(
        [jnp.concatenate([m0, blockdiag_mix(k1)], axis=2),
         jnp.concatenate([jnp.zeros_like(m0), m0], axis=2)], axis=1)
    return (win.astype(BF16), wout.astype(BF16), wt.astype(BF16),
            a2r.reshape(nblk, 1, STATE_COLS), a2i.reshape(nblk, 1, STATE_COLS))


def _s5(u_tm, win, wout, wt, a2r, a2i, d_skip, *, bsz, n_pairs):
    seqlen = u_tm.shape[0]
    width = u_tm.shape[1] // bsz
    assert bsz == SUBLANES and seqlen % (2 * n_pairs) == 0 and width % LANES == 0
    nblk = width // LANES
    rows = n_pairs * SUBLANES
    u3 = u_tm.reshape(seqlen, bsz, width)
    blk = pl.BlockSpec((2 * n_pairs, bsz, LANES), lambda j, c: (c, 0, j))
    per_blk = lambda shape: pl.BlockSpec((1,) + shape, lambda j, c: (j, 0, 0))
    out = pl.pallas_call(
        functools.partial(_s5_kernel, n_pairs=n_pairs),
        grid=(nblk, seqlen // (2 * n_pairs)),
        in_specs=[blk,
                  per_blk((2 * LANES, 2 * STATE_COLS)),
                  per_blk((2 * STATE_COLS, 2 * LANES)),
                  per_blk((2 * LANES, 2 * LANES)),
                  per_blk((1, STATE_COLS)),
                  per_blk((1, STATE_COLS)),
                  pl.BlockSpec((1, LANES), lambda j, c: (0, j))],
        out_specs=blk,
        out_shape=jax.ShapeDtypeStruct((seqlen, bsz, width), F32),
        scratch_shapes=[pltpu.VMEM((rows, 2 * STATE_COLS), F32),
                        pltpu.VMEM((rows, 2 * STATE_COLS), F32),
                        pltpu.VMEM((2, SUBLANES, STATE_COLS), F32)],
        compiler_params=_params(("arbitrary", "arbitrary")),
        name="s5_scan",
    )(u3, win, wout, wt, a2r, a2i, d_skip.reshape(1, width).astype(F32))
    return out.reshape(seqlen, bsz * width)


def _post(o, g_ref, h_ref, out_ref):
    ms = jnp.mean(o * o, axis=-1, keepdims=True)
    out_ref[...] = h_ref[...] + o * lax.rsqrt(ms + EPS) * g_ref[...]


def _glu_out_kernel(yg_ref, z_ref, memo_ref, h_ref, wglu_ref, bglu_ref,
                    wmain_ref, wmem_ref, g_ref, gkv_ref, gnext_ref, wf_ref, bf_ref,
                    out_ref, xkv_ref, xnext_ref, fp_ref, carry_ref, *, heads):
    @pl.when(pl.program_id(1) == 0)
    def _():
        carry_ref[...] = jnp.zeros_like(carry_ref)

    yg = yg_ref[...]
    t = jnp.dot(yg.astype(BF16), wglu_ref[...], preferred_element_type=F32) + bglu_ref[...]
    z = z_ref[...].astype(F32)
    main = yg * jax.nn.sigmoid(t) * (z * jax.nn.sigmoid(z))
    o = jnp.dot(main.astype(BF16), wmain_ref[...], preferred_element_type=F32)
    o = o + jnp.dot(memo_ref[...], wmem_ref[...], preferred_element_type=F32)
    _post(o, g_ref, h_ref, out_ref)
    h1 = out_ref[...]
    r = h1 * lax.rsqrt(jnp.mean(h1 * h1, axis=-1, keepdims=True) + EPS)
    xnext_ref[...] = (r * gnext_ref[...]).astype(BF16)
    xkv = r * gkv_ref[...]
    xkv_ref[...] = xkv.astype(BF16)
    _fgate_block(xkv, wf_ref, bf_ref, carry_ref, fp_ref, heads)


def _out_kernel(main_ref, memo_ref, h_ref, wmain_ref, wmem_ref, g_ref, out_ref):
    main = jnp.concatenate([main_ref[s] for s in range(main_ref.shape[0])], axis=1)
    o = jnp.dot(main, wmain_ref[...], preferred_element_type=F32)
    o = o + jnp.dot(memo_ref[...], wmem_ref[...], preferred_element_type=F32)
    _post(o, g_ref, h_ref, out_ref)


def _resident(shape):
    return pl.BlockSpec(shape, lambda *_: (0,) * len(shape), pipeline_mode=pl.Buffered(1))


def _glu_out(yg_tm, rest, memo, h2d, w_glu, b_glu, w_out, g, g_kv, g_next, w_fgate, b_fgate,
             *, bsz, tl):
    m, d = h2d.shape
    seqlen = m // bsz
    main_w = w_glu.shape[0]
    mem_w = w_out.shape[0] - main_w
    heads = w_fgate.shape[1]
    assert 3 * heads <= LANES and seqlen % tl == 0
    wpad = jnp.zeros((d, LANES), F32).at[:, :heads].set(w_fgate.astype(F32))
    wcat = jnp.concatenate(_split2(wpad), axis=1)
    bpad = jnp.zeros((1, LANES), F32).at[0, :heads].set(b_fgate.astype(F32))
    per_b = seqlen // tl
    row = lambda b, i: (b * per_b + i, 0)
    vec = lambda v: v.reshape(1, -1).astype(F32)
    return pl.pallas_call(
        functools.partial(_glu_out_kernel, heads=heads),
        grid=(bsz, per_b),
        in_specs=[pl.BlockSpec((tl, main_w), lambda b, i: (i, b)),
                  pl.BlockSpec((tl, main_w), row),
                  pl.BlockSpec((tl, mem_w), row),
                  pl.BlockSpec((tl, d), row),
                  _resident((main_w, main_w)),
                  _resident((1, main_w)),
                  _resident((main_w, d)),
                  _resident((mem_w, d)),
                  _resident((1, d)),
                  _resident((1, d)),
                  _resident((1, d)),
                  _resident((d, 2 * LANES)),
                  _resident((1, LANES))],
        out_specs=[pl.BlockSpec((tl, d), row),
                   pl.BlockSpec((tl, d), row),
                   pl.BlockSpec((tl, d), row),
                   pl.BlockSpec((tl, LANES), row)],
        out_shape=[jax.ShapeDtypeStruct((m, d), F32),
                   jax.ShapeDtypeStruct((m, d), BF16),
                   jax.ShapeDtypeStruct((m, d), BF16),
                   jax.ShapeDtypeStruct((m, LANES), BF16)],
        scratch_shapes=[pltpu.VMEM((1, LANES), F32)],
        compiler_params=_params(("arbitrary", "arbitrary")),
        name="glu_out_proj",
    )(yg_tm, rest, memo, h2d, w_glu.astype(BF16), vec(b_glu),
      w_out[:main_w].astype(BF16), w_out[main_w:].astype(BF16), vec(g), vec(g_kv),
      vec(g_next), wcat, bpad)


def _out_proj(main, memo, h2d, w_out, g, *, tl):
    m, d = h2d.shape
    heads, _, dh = main.shape
    main_w = heads * dh
    mem_w = memo.shape[1]
    row = lambda i: (i, 0)
    return pl.pallas_call(
        _out_kernel,
        grid=(m // tl,),
        in_specs=[pl.BlockSpec((heads, tl, dh), lambda i: (0, i, 0)),
                  pl.BlockSpec((tl, mem_w), row),
                  pl.BlockSpec((tl, d), row),
                  _resident((main_w, d)),
                  _resident((mem_w, d)),
                  _resident((1, d))],
        out_specs=pl.BlockSpec((tl, d), row),
        out_shape=jax.ShapeDtypeStruct((m, d), F32),
        compiler_params=_params(("parallel",)),
        name="out_proj",
    )(main, memo, h2d, w_out[:main_w].astype(BF16), w_out[main_w:].astype(BF16),
      g.reshape(1, d).astype(F32))


FOX_CHUNK = 64


def _fox_kernel(q_ref, k_ref, v_ref, fp_ref, sel_ref, cst_ref, z_ref, o_ref,
                ka_ref, qa_ref, vt_ref, st_ref, pt_ref, m_ref, l_ref, acc_ref, *, tq):
    seqlen = q_ref.shape[0]
    ncg = tq // LANES
    nch = tq // FOX_CHUNK
    ext = jnp.dot(fp_ref[...], sel_ref[0], preferred_element_type=F32)
    ka_ref[:, :HEAD_DIM] = k_ref[...]
    ka_ref[:, HEAD_DIM:] = (ext[:, :HEAD_DIM] + cst_ref[0:1, :]).astype(BF16)
    qa_ref[:, :HEAD_DIM] = q_ref[...]
    qa_ref[:, HEAD_DIM:] = (ext[:, HEAD_DIM:] + cst_ref[1:2, :]).astype(BF16)
    vt_ref[...] = v_ref[...].astype(F32).T.astype(BF16)
    for ci in range(nch):
        for g in range(ncg):
            if g * LANES + LANES - 1 < ci * FOX_CHUNK:
                pt_ref[1, ci * FOX_CHUNK:(ci + 1) * FOX_CHUNK, g * LANES:(g + 1) * LANES] = (
                    jnp.zeros((FOX_CHUNK, LANES), BF16))

    def scores(qi, kj, slot):
        st_ref[slot] = lax.dot_general(ka_ref[kj * tq:(kj + 1) * tq, :],
                                       qa_ref[qi * tq:(qi + 1) * tq, :],
                                       (((1,), (1,)), ((), ())),
                                       preferred_element_type=F32)

    def piece(slot, ci, g, masked):
        r0, c0 = ci * FOX_CHUNK, g * LANES
        if masked and c0 + LANES - 1 < r0:
            return None
        x = st_ref[slot, r0:r0 + FOX_CHUNK, c0:c0 + LANES]
        if masked and c0 < r0 + FOX_CHUNK - 1:
            key = r0 + lax.broadcasted_iota(jnp.int32, (FOX_CHUNK, LANES), 0)
            qry = c0 + lax.broadcasted_iota(jnp.int32, (FOX_CHUNK, LANES), 1)
            x = jnp.where(key <= qry, x, -jnp.inf)
        return x

    def fold(x, op):
        return op(x.reshape(FOX_CHUNK // SUBLANES, SUBLANES, LANES), axis=0)

    def softmax_pv(kj, slot, masked, first):
        mx = [None] * ncg
        for ci in range(nch):
            for g in range(ncg):
                x = piece(slot, ci, g, masked)
                if x is not None:
                    f = fold(x, jnp.max)
                    mx[g] = f if mx[g] is None else jnp.maximum(mx[g], f)
        m_new = jnp.concatenate([jnp.max(v, axis=0, keepdims=True) for v in mx], axis=1)
        if not first:
            m_old = m_ref[...]
            m_new = jnp.maximum(m_old, m_new)
            alpha = jnp.exp2(m_old - m_new)
        m_ref[...] = m_new
        ls = [None] * ncg
        for ci in range(nch):
            for g in range(ncg):
                r0, c0 = ci * FOX_CHUNK, g * LANES
                x = piece(slot, ci, g, masked)
                if x is None:
                    continue
                p = jnp.exp2(x - m_new[:, c0:c0 + LANES])
                f = fold(p, jnp.sum)
                ls[g] = f if ls[g] is None else ls[g] + f
                pt_ref[int(masked), r0:r0 + FOX_CHUNK, c0:c0 + LANES] = p.astype(BF16)
        l_new = jnp.concatenate([jnp.sum(v, axis=0, keepdims=True) for v in ls], axis=1)
        pv = jnp.dot(vt_ref[:, kj * tq:(kj + 1) * tq], pt_ref[int(masked)],
                     preferred_element_type=F32)
        if first:
            l_ref[...] = l_new
            acc_ref[...] = pv
        else:
            l_ref[...] = alpha * l_ref[...] + l_new
            acc_ref[...] = alpha * acc_ref[...] + pv

    for qi in range(seqlen // tq):
        rows = slice(qi * tq, (qi + 1) * tq)
        scores(qi, 0, 0)
        for kj in range(qi + 1):
            if kj < qi:
                scores(qi, kj + 1, (kj + 1) % 2)
            softmax_pv(kj, kj % 2, masked=(kj == qi), first=(kj == 0))
        z = z_ref[rows, :].astype(F32)
        o = (acc_ref[...] / l_ref[...]).T
        o_ref[rows, :] = (o * (z * jax.nn.sigmoid(z))).astype(o_ref.dtype)


def _fox_attn(q, k, v, z, fparts, *, bsz, tq):
    heads, m, _ = q.shape
    seqlen = m // bsz
    nterms = 3
    assert seqlen % tq == 0 and tq % FOX_CHUNK == 0 and nterms * heads <= LANES
    h_idx = jnp.arange(heads)[:, None, None]
    r_idx = jnp.arange(LANES)[None, :, None]
    c_idx = jnp.arange(2 * HEAD_DIM)[None, None, :]
    key_side = (c_idx < nterms) & (r_idx == c_idx * heads + h_idx)
    qc = c_idx - HEAD_DIM - nterms
    qry_side = (qc >= 0) & (qc < nterms) & (r_idx == qc * heads + h_idx)
    sel = (qry_side.astype(F32) - key_side.astype(F32)).astype(BF16)
    col = jnp.arange(HEAD_DIM)
    cst = jnp.zeros((SUBLANES, HEAD_DIM), F32)
    cst = cst.at[0].set(((col >= nterms) & (col < 2 * nterms)).astype(F32))
    cst = cst.at[1].set((col < nterms).astype(F32))
    head_seq = pl.BlockSpec((None, seqlen, HEAD_DIM), lambda b, h: (h, b, 0))
    return pl.pallas_call(
        functools.partial(_fox_kernel, tq=tq),
        grid=(bsz, heads),
        in_specs=[head_seq, head_seq, head_seq,
                  pl.BlockSpec((seqlen, LANES), lambda b, h: (b, 0)),
                  pl.BlockSpec((1, LANES, 2 * HEAD_DIM), lambda b, h: (h, 0, 0)),
                  pl.BlockSpec((SUBLANES, HEAD_DIM), lambda b, h: (0, 0)),
                  head_seq],
        out_specs=head_seq,
        out_shape=jax.ShapeDtypeStruct((heads, m, HEAD_DIM), BF16),
        scratch_shapes=[pltpu.VMEM((seqlen, 2 * HEAD_DIM), BF16),
                        pltpu.VMEM((seqlen, 2 * HEAD_DIM), BF16),
                        pltpu.VMEM((HEAD_DIM, seqlen), BF16),
                        pltpu.VMEM((2, tq, tq), F32),
                        pltpu.VMEM((2, tq, tq), BF16),
                        pltpu.VMEM((1, tq), F32),
                        pltpu.VMEM((1, tq), F32),
                        pltpu.VMEM((HEAD_DIM, tq), F32)],
        compiler_params=_params(("arbitrary", "arbitrary")),
        name="fox_attn",
    )(q, k, v, fparts, sel, cst, z)


def kernel(x, mem, pre_norm_g, post_norm_g, w_in_a, lam_re, lam_im, log_step, b_re, b_im,
           c_re, c_im, d_skip, w_glu, b_glu, kv_norm_g, w_kv, w_fgate, b_fgate, w_in_b,
           mem_norm_g, w_mem_kv, w_out):
    bsz, seqlen, d = x.shape
    n_mem = mem.shape[1]
    main_w = w_glu.shape[1]
    mem_w = w_out.shape[1] - main_w
    heads = main_w // HEAD_DIM
    scale = HEAD_DIM ** -0.5
    x2d = x.reshape(bsz * seqlen, d)
    mem2d = mem.reshape(bsz * n_mem, d)

    ones = jnp.ones((main_w,), F32)
    mem_scale = jnp.full((mem_w,), scale, F32)
    cs_a = jnp.concatenate([ones, ones, mem_scale, ones[:mem_w]])
    cs_b = jnp.concatenate([ones * (scale * LOG2E), ones, mem_scale, ones[:mem_w]])
    xa = _prenorm(x2d, pre_norm_g[0], tm=512)
    proj = functools.partial(_matmul, out_dtype=BF16, tm=1024)
    u_tm = _matmul(xa, w_in_a[0], cs_a, col0=0, n=main_w, out_dtype=F32, tm=1024, tn=main_w,
                   time_major_batches=bsz, name="in_proj_a_u")
    z_a = proj(xa, w_in_a[0], cs_a, col0=main_w, n=main_w, tn=main_w, name="in_proj_a_z")
    qz_a = proj(xa, w_in_a[0], cs_a, col0=2 * main_w, n=2 * mem_w, tn=2 * mem_w,
                slab_width=HEAD_DIM, name="in_proj_a_mem")
    kvm0 = _norm_matmul(mem2d, mem_norm_g[0], w_mem_kv[0].astype(BF16), out_dtype=BF16,
                        tm=512, tn=512, name="mem_kv0")
    memo0 = _mem_attn(qz_a, kvm0, bsz=bsz, tq=512, name="mem_attn0")
    s5_ops = _s5_discretise(lam_re[0], lam_im[0], log_step[0], b_re[0], b_im[0],
                            c_re[0], c_im[0])
    yg_tm = _s5(u_tm, *s5_ops, d_skip[0], bsz=bsz, n_pairs=128)
    h1, xkv, xb, fparts = _glu_out(
        yg_tm, z_a, memo0, x2d, w_glu[0], b_glu[0], w_out[0], post_norm_g[0],
        kv_norm_g, pre_norm_g[1], w_fgate, b_fgate, bsz=bsz, tl=256)

    cs_kv = jnp.ones((w_kv.shape[1],), F32)
    k_sh = proj(xkv, w_kv, cs_kv, col0=0, n=main_w, tn=main_w, slab_width=HEAD_DIM,
                name="k_proj")
    v_sh = proj(xkv, w_kv, cs_kv, col0=main_w, n=main_w, tn=main_w, slab_width=HEAD_DIM,
                name="v_proj")

    q_b = proj(xb, w_in_b[0], cs_b, col0=0, n=main_w, tn=main_w, slab_width=HEAD_DIM,
               name="in_proj_b_q")
    z_b = proj(xb, w_in_b[0], cs_b, col0=main_w, n=main_w, tn=main_w, slab_width=HEAD_DIM,
               name="in_proj_b_z")
    qz_b = proj(xb, w_in_b[0], cs_b, col0=2 * main_w, n=2 * mem_w, tn=2 * mem_w,
                slab_width=HEAD_DIM, name="in_proj_b_mem")
    kvm1 = _norm_matmul(mem2d, mem_norm_g[1], w_mem_kv[1].astype(BF16), out_dtype=BF16,
                        tm=512, tn=512, name="mem_kv1")
    memo1 = _mem_attn(qz_b, kvm1, bsz=bsz, tq=512, name="mem_attn1")
    att = _fox_attn(q_b, k_sh, v_sh, z_b, fparts, bsz=bsz, tq=512)
    out = _out_proj(att, memo1, h1, w_out[1], post_norm_g[1], tl=512)
    return out.reshape(bsz, seqlen, d)
```

```python
import functools
import math

import jax
import jax.numpy as jnp
from jax import lax
from jax.experimental import pallas as pl
from jax.experimental.pallas import tpu as pltpu

F32 = jnp.float32
BF16 = jnp.bfloat16

EPS = 1e-6
LOG2E = 1.4426950408889634
HEAD_DIM = 128
SSM_GROUP = 16
SSM_STATE = 64
MEM_HEADS = 4
LANES = 128
SUBLANES = 8
GROUPS_PER_BLOCK = LANES // SSM_GROUP
STATE_COLS = GROUPS_PER_BLOCK * SSM_STATE
VMEM_LIMIT = 56 * 1024 * 1024


def _params(sem, vmem=VMEM_LIMIT):
    return pltpu.CompilerParams(dimension_semantics=sem, vmem_limit_bytes=vmem)


def _norm_matmul_kernel(x_ref, g_ref, w_ref, o_ref, xn_ref):
    @pl.when(pl.program_id(1) == 0)
    def _():
        x = x_ref[...]
        ms = jnp.mean(x * x, axis=-1, keepdims=True)
        xn_ref[...] = (x * lax.rsqrt(ms + EPS) * g_ref[...]).astype(BF16)

    o_ref[...] = jnp.dot(xn_ref[...], w_ref[...],
                         preferred_element_type=F32).astype(o_ref.dtype)


def _norm_matmul(x2d, g, w, *, out_dtype, tm, tn, time_major_batches=None, name):
    m, d = x2d.shape
    n = w.shape[1]
    assert m % tm == 0 and n % tn == 0
    nj = n // tn
    if time_major_batches is None:
        out_shape = jax.ShapeDtypeStruct((m, n), out_dtype)
        out_spec = pl.BlockSpec((tm, tn), lambda i, j: (i, j))
    else:
        bsz = time_major_batches
        seqlen = m // bsz
        assert seqlen % tm == 0
        per_b = seqlen // tm
        out_shape = jax.ShapeDtypeStruct((seqlen, bsz * n), out_dtype)
        out_spec = pl.BlockSpec((tm, tn), lambda i, j: (i % per_b, (i // per_b) * nj + j))
    return pl.pallas_call(
        _norm_matmul_kernel,
        grid=(m // tm, nj),
        in_specs=[pl.BlockSpec((tm, d), lambda i, j: (i, 0)),
                  pl.BlockSpec((1, d), lambda i, j: (0, 0)),
                  pl.BlockSpec((d, tn), lambda i, j: (0, j))],
        out_specs=out_spec,
        out_shape=out_shape,
        scratch_shapes=[pltpu.VMEM((tm, d), BF16)],
        compiler_params=_params(("parallel", "arbitrary")),
        name=name,
    )(x2d, g.reshape(1, d), w)


def _prenorm_kernel(x_ref, g_ref, o_ref):
    x = x_ref[...]
    ms = jnp.mean(x * x, axis=-1, keepdims=True)
    o_ref[...] = (x * lax.rsqrt(ms + EPS) * g_ref[...]).astype(o_ref.dtype)


def _prenorm(x2d, g, *, tm):
    m, d = x2d.shape
    return pl.pallas_call(
        _prenorm_kernel,
        grid=(m // tm,),
        in_specs=[pl.BlockSpec((tm, d), lambda i: (i, 0)),
                  pl.BlockSpec((1, d), lambda i: (0, 0))],
        out_specs=pl.BlockSpec((tm, d), lambda i: (i, 0)),
        out_shape=jax.ShapeDtypeStruct((m, d), BF16),
        compiler_params=_params(("parallel",)),
        name="prenorm",
    )(x2d, g.reshape(1, d).astype(F32))


def _matmul_kernel(x_ref, w_ref, cs_ref, o_ref, wb_ref):
    @pl.when(pl.program_id(1) == 0)
    def _():
        wb_ref[...] = (w_ref[...] * cs_ref[...]).astype(BF16)

    res = jnp.dot(x_ref[...], wb_ref[...], preferred_element_type=F32).astype(o_ref.dtype)
    if len(o_ref.shape) == 3:
        sw = o_ref.shape[2]
        for s in range(o_ref.shape[0]):
            o_ref[s] = res[:, s * sw:(s + 1) * sw]
    else:
        o_ref[...] = res


def _matmul(xn, w, colscale, *, col0, n, out_dtype, tm, tn, time_major_batches=None,
            slab_width=None, name):
    m, d = xn.shape
    assert m % tm == 0 and n % tn == 0 and col0 % tn == 0
    nj = n // tn
    j0 = col0 // tn
    if slab_width is not None:
        assert time_major_batches is None and tn % slab_width == 0
        per_tile = tn // slab_width
        out_shape = jax.ShapeDtypeStruct((n // slab_width, m, slab_width), out_dtype)
        out_spec = pl.BlockSpec((per_tile, tm, slab_width), lambda j, i: (j, i, 0))
    elif time_major_batches is None:
        out_shape = jax.ShapeDtypeStruct((m, n), out_dtype)
        out_spec = pl.BlockSpec((tm, tn), lambda j, i: (i, j))
    else:
        bsz = time_major_batches
        seqlen = m // bsz
        assert seqlen % tm == 0
        per_b = seqlen // tm
        out_shape = jax.ShapeDtypeStruct((seqlen, bsz * n), out_dtype)
        out_spec = pl.BlockSpec((tm, tn), lambda j, i: (i % per_b, (i // per_b) * nj + j))
    return pl.pallas_call(
        _matmul_kernel,
        grid=(nj, m // tm),
        in_specs=[pl.BlockSpec((tm, d), lambda j, i: (i, 0)),
                  pl.BlockSpec((d, tn), lambda j, i: (0, j0 + j), pipeline_mode=pl.Buffered(1)),
                  pl.BlockSpec((1, tn), lambda j, i: (0, j0 + j))],
        out_specs=out_spec,
        out_shape=out_shape,
        scratch_shapes=[pltpu.VMEM((d, tn), BF16)],
        compiler_params=_params(("arbitrary", "arbitrary")),
        name=name,
    )(xn, w, colscale.reshape(1, -1).astype(F32))


def _split2(x):
    hi = x.astype(BF16)
    return hi, (x - hi.astype(F32)).astype(BF16)


def _split3(x):
    hi = x.astype(BF16)
    r = x - hi.astype(F32)
    mid = r.astype(BF16)
    return hi, mid, (r - mid.astype(F32)).astype(BF16)


def _fgate_block(xn, wcat_ref, b_ref, carry_ref, heads):
    tl = xn.shape[0]
    xh, xl = _split2(xn)
    both = jnp.dot(xh, wcat_ref[...], preferred_element_type=F32)
    logit = (both[:, :LANES] + both[:, LANES:]
             + jnp.dot(xl, wcat_ref[:, :LANES], preferred_element_type=F32)) + b_ref[...]
    logf = jnp.minimum(logit, 0.0) - jnp.log(1.0 + jnp.exp(-jnp.abs(logit)))
    t_idx = lax.broadcasted_iota(jnp.int32, (tl, tl), 0)
    s_idx = lax.broadcasted_iota(jnp.int32, (tl, tl), 1)
    tril = (s_idx <= t_idx).astype(BF16)
    csum = carry_ref[...]
    for part in _split3(logf):
        csum = csum + jnp.dot(tril, part, preferred_element_type=F32)
    carry_ref[...] = csum[tl - 1:tl, :]
    lane = lax.broadcasted_iota(jnp.int32, (tl, LANES), 1)
    packed = jnp.zeros((tl, LANES), F32)
    for n, part in reversed(list(enumerate(_split3(csum * LOG2E)))):
        shifted = part.astype(F32) if n == 0 else pltpu.roll(part.astype(F32), n * heads, 1)
        packed = jnp.where(lane < (n + 1) * heads, shifted, packed)
    return packed.astype(BF16)


def _mem_attn_kernel(q_ref, zm_ref, kv_ref, o_ref):
    width = MEM_HEADS * HEAD_DIM
    heads = range(MEM_HEADS)
    cols = [slice(h * HEAD_DIM, (h + 1) * HEAD_DIM) for h in heads]
    st = [lax.dot_general(kv_ref[:, cols[h]], q_ref[h], (((1,), (1,)), ((), ())),
                          preferred_element_type=F32) for h in heads]
    vt = [kv_ref[:, width + h * HEAD_DIM:width + (h + 1) * HEAD_DIM].astype(F32).T.astype(BF16)
          for h in heads]
    p = [jnp.exp2(st[h] - jnp.max(st[h], axis=0, keepdims=True)) for h in heads]
    inv = [1.0 / jnp.sum(p[h], axis=0, keepdims=True) for h in heads]
    ot = [jnp.dot(vt[h], p[h].astype(BF16), preferred_element_type=F32) * inv[h] for h in heads]
    for h in heads:
        zm = zm_ref[h].astype(F32)
        o_ref[:, cols[h]] = (ot[h].T * (zm * jax.nn.sigmoid(zm))).astype(o_ref.dtype)


def _mem_attn(qz, kvm, *, bsz, tq, name):
    width = MEM_HEADS * HEAD_DIM
    m = qz.shape[1]
    seqlen = m // bsz
    n_mem = kvm.shape[0] // bsz
    per_b = seqlen // tq
    spec = lambda blk: pl.BlockSpec((MEM_HEADS, tq, HEAD_DIM),
                                    lambda b, i: (blk, b * per_b + i, 0))
    return pl.pallas_call(
        _mem_attn_kernel,
        grid=(bsz, per_b),
        in_specs=[spec(0), spec(1),
                  pl.BlockSpec((n_mem, 2 * width), lambda b, i: (b, 0))],
        out_specs=pl.BlockSpec((tq, width), lambda b, i: (b * per_b + i, 0)),
        out_shape=jax.ShapeDtypeStruct((m, width), BF16),
        compiler_params=_params(("parallel", "arbitrary")),
        name=name,
    )(qz, qz, kvm)


def _s5_kernel(u_ref, win_ref, wout_ref, wt_ref, ar_ref, ai_ref, d_ref, y_ref,
               bu_ref, xs_ref, st_ref, *, n_pairs):
    @pl.when(pl.program_id(1) == 0)
    def _():
        st_ref[...] = jnp.zeros_like(st_ref)

    nblk = win_ref.shape[0]
    rows = n_pairs * SUBLANES
    u0, u1, ucat, a2, state = [], [], [], [], []
    for k in range(nblk):
        upair = u_ref[:, :, k * LANES:(k + 1) * LANES].reshape(n_pairs, 2, SUBLANES, LANES)
        u0.append(upair[:, 0].reshape(rows, LANES))
        u1.append(upair[:, 1].reshape(rows, LANES))
        ucat.append(jnp.concatenate([u0[k], u1[k]], axis=1).astype(BF16))
        bu_ref[k] = jnp.dot(ucat[k], win_ref[k], preferred_element_type=F32)
        a2.append((jnp.broadcast_to(ar_ref[k], (SUBLANES, STATE_COLS)),
                   jnp.broadcast_to(ai_ref[k], (SUBLANES, STATE_COLS))))
        state.append((st_ref[k, 0], st_ref[k, 1]))

    held = [None] * nblk
    for c in range(n_pairs):
        for k in range(nblk):
            xr, xi = state[k]
            cur = jnp.concatenate([xr, xi], axis=1)
            if c % 2 == 1:
                xs_ref[k, (c - 1) * SUBLANES:(c + 1) * SUBLANES, :] = jnp.concatenate(
                    [held[k], cur], axis=0).astype(BF16)
            held[k] = cur
            bur = bu_ref[k, c * SUBLANES:(c + 1) * SUBLANES, 0:STATE_COLS]
            bui = bu_ref[k, c * SUBLANES:(c + 1) * SUBLANES, STATE_COLS:2 * STATE_COLS]
            ar, ai = a2[k]
            state[k] = (ar * xr - ai * xi + bur, ar * xi + ai * xr + bui)

    for k in range(nblk):
        st_ref[k, 0] = state[k][0]
        st_ref[k, 1] = state[k][1]
        y = (jnp.dot(xs_ref[k], wout_ref[k], preferred_element_type=F32)
             + jnp.dot(ucat[k], wt_ref[k], preferred_element_type=F32))
        d = d_ref[:, k * LANES:(k + 1) * LANES]
        y0 = jax.nn.gelu(y[:, :LANES] + d * u0[k]).reshape(n_pairs, SUBLANES, LANES)
        y1 = jax.nn.gelu(y[:, LANES:] + d * u1[k]).reshape(n_pairs, SUBLANES, LANES)
        y_ref[:, :, k * LANES:(k + 1) * LANES] = jnp.stack([y0, y1], axis=1).reshape(
            2 * n_pairs, SUBLANES, LANES)


def _s5_discretise(lam_re, lam_im, log_step, b_re, b_im, c_re, c_im):
    groups = lam_re.shape[0]
    nblk = groups // GROUPS_PER_BLOCK
    lr = lam_re.astype(F32)
    li = lam_im.astype(F32)
    dt = jnp.exp(log_step.astype(F32))[:, None]
    mag = jnp.exp(lr * dt)
    ar = mag * jnp.cos(li * dt)
    ai = mag * jnp.sin(li * dt)
    den = lr * lr + li * li
    cr = ((ar - 1.0) * lr + ai * li) / den
    ci = (ai * lr - (ar - 1.0) * li) / den
    br = b_re.astype(F32)
    bi = b_im.astype(F32)
    bbar_re = cr[..., None] * br - ci[..., None] * bi
    bbar_im = cr[..., None] * bi + ci[..., None] * br
    a2r = ar * ar - ai * ai
    a2i = 2.0 * ar * ai
    ab_re = ar[..., None] * bbar_re - ai[..., None] * bbar_im
    ab_im = ar[..., None] * bbar_im + ai[..., None] * bbar_re
    cre = c_re.astype(F32)
    cim = c_im.astype(F32)
    ca_re = cre * ar[:, None, :] - cim * ai[:, None, :]
    ca_im = cre * ai[:, None, :] + cim * ar[:, None, :]
    ca2_re = cre * a2r[:, None, :] - cim * a2i[:, None, :]
    ca2_im = cre * a2i[:, None, :] + cim * a2r[:, None, :]
    k0 = jnp.einsum('ghp,gpk->ghk', cre, bbar_re) - jnp.einsum('ghp,gpk->ghk', cim, bbar_im)
    k1 = jnp.einsum('ghp,gpk->ghk', cre, ab_re) - jnp.einsum('ghp,gpk->ghk', cim, ab_im)
    eye = jnp.eye(GROUPS_PER_BLOCK, dtype=F32)

    def blockdiag_in(b):
        b4 = b.reshape(nblk, GROUPS_PER_BLOCK, SSM_STATE, SSM_GROUP)
        return jnp.einsum('jgph,gk->jghkp', b4, eye).reshape(nblk, LANES, STATE_COLS)

    def blockdiag_out(c):
        c4 = c.reshape(nblk, GROUPS_PER_BLOCK, SSM_GROUP, SSM_STATE)
        return jnp.einsum('jghp,gk->jgpkh', c4, eye).reshape(nblk, STATE_COLS, LANES)

    def blockdiag_mix(k):
        k4 = k.reshape(nblk, GROUPS_PER_BLOCK, SSM_GROUP, SSM_GROUP)
        return jnp.einsum('jghk,gm->jgkmh', k4, eye).reshape(nblk, LANES, LANES)

    win = jnp.concatenate(
        [jnp.concatenate([blockdiag_in(ab_re), blockdiag_in(ab_im)], axis=2),
         jnp.concatenate([blockdiag_in(bbar_re), blockdiag_in(bbar_im)], axis=2)], axis=1)
    wout = jnp.concatenate(
        [jnp.concatenate([blockdiag_out(ca_re), blockdiag_out(ca2_re)], axis=2),
         jnp.concatenate([-blockdiag_out(ca_im), -blockdiag_out(ca2_im)], axis=2)], axis=1)
    m0 = blockdiag_mix(k0)
    wt = jnp.concatenate(
        [jnp.concatenate([m0, blockdiag_mix(k1)], axis=2),
         jnp.concatenate([jnp.zeros_like(m0), m0], axis=2)], axis=1)
    return (win.astype(BF16), wout.astype(BF16), wt.astype(BF16),
            a2r.reshape(nblk, 1, STATE_COLS), a2i.reshape(nblk, 1, STATE_COLS))


def _s5(u_tm, win, wout, wt, a2r, a2i, d_skip, *, bsz, n_pairs, blocks_per_step):
    seqlen = u_tm.shape[0]
    width = u_tm.shape[1] // bsz
    assert bsz == SUBLANES and seqlen % (2 * n_pairs) == 0 and width % LANES == 0
    nblk = width // LANES
    assert nblk % blocks_per_step == 0 and n_pairs % 2 == 0
    nb = blocks_per_step
    rows = n_pairs * SUBLANES
    u3 = u_tm.reshape(seqlen, bsz, width)
    blk = pl.BlockSpec((2 * n_pairs, bsz, nb * LANES), lambda j, c: (c, 0, j))
    per_blk = lambda shape: pl.BlockSpec((nb,) + shape, lambda j, c: (j, 0, 0))
    out = pl.pallas_call(
        functools.partial(_s5_kernel, n_pairs=n_pairs),
        grid=(nblk // nb, seqlen // (2 * n_pairs)),
        in_specs=[blk,
                  per_blk((2 * LANES, 2 * STATE_COLS)),
                  per_blk((2 * STATE_COLS, 2 * LANES)),
                  per_blk((2 * LANES, 2 * LANES)),
                  per_blk((1, STATE_COLS)),
                  per_blk((1, STATE_COLS)),
                  pl.BlockSpec((1, nb * LANES), lambda j, c: (0, j))],
        out_specs=blk,
        out_shape=jax.ShapeDtypeStruct((seqlen, bsz, width), F32),
        scratch_shapes=[pltpu.VMEM((nb, rows, 2 * STATE_COLS), F32),
                        pltpu.VMEM((nb, rows, 2 * STATE_COLS), BF16),
                        pltpu.VMEM((nb, 2, SUBLANES, STATE_COLS), F32)],
        compiler_params=_params(("arbitrary", "arbitrary")),
        name="s5_scan",
    )(u3, win, wout, wt, a2r, a2i, d_skip.reshape(1, width).astype(F32))
    return out.reshape(seqlen, bsz * width)


def _post(o, g_ref, h_ref, out_ref):
    ms = jnp.mean(o * o, axis=-1, keepdims=True)
    out_ref[...] = h_ref[...] + o * lax.rsqrt(ms + EPS) * g_ref[...]


def _glu_out_kernel(yg_ref, z_ref, memo_ref, h_ref, wglu_ref, bglu_ref,
                    wmain_ref, wmem_ref, g_ref, gkv_ref, gnext_ref, wf_ref, bf_ref,
                    out_ref, xkv_ref, xnext_ref, fp_ref, carry_ref, *, heads, sub):
    @pl.when(pl.program_id(1) == 0)
    def _():
        carry_ref[...] = jnp.zeros_like(carry_ref)

    for s in range(out_ref.shape[0] // sub):
        rows = slice(s * sub, (s + 1) * sub)
        yg = yg_ref[rows, :]
        t = jnp.dot(yg.astype(BF16), wglu_ref[...], preferred_element_type=F32) + bglu_ref[...]
        z = z_ref[rows, :].astype(F32)
        main = yg * jax.nn.sigmoid(t) * (z * jax.nn.sigmoid(z))
        o = jnp.dot(main.astype(BF16), wmain_ref[...], preferred_element_type=F32)
        o = o + jnp.dot(memo_ref[rows, :], wmem_ref[...], preferred_element_type=F32)
        ms = jnp.mean(o * o, axis=-1, keepdims=True)
        h1 = h_ref[rows, :] + o * lax.rsqrt(ms + EPS) * g_ref[...]
        out_ref[rows, :] = h1
        r = h1 * lax.rsqrt(jnp.mean(h1 * h1, axis=-1, keepdims=True) + EPS)
        xnext_ref[rows, :] = (r * gnext_ref[...]).astype(BF16)
        xkv = r * gkv_ref[...]
        xkv_ref[rows, :] = xkv.astype(BF16)
        fp_ref[rows, :] = _fgate_block(xkv, wf_ref, bf_ref, carry_ref, heads)


def _out_kernel(main_ref, memo_ref, h_ref, wmain_ref, wmem_ref, g_ref, out_ref):
    main = jnp.concatenate([main_ref[s] for s in range(main_ref.shape[0])], axis=1)
    o = jnp.dot(main, wmain_ref[...], preferred_element_type=F32)
    o = o + jnp.dot(memo_ref[...], wmem_ref[...], preferred_element_type=F32)
    _post(o, g_ref, h_ref, out_ref)


def _resident(shape):
    return pl.BlockSpec(shape, lambda *_: (0,) * len(shape), pipeline_mode=pl.Buffered(1))


def _glu_out(yg_tm, rest, memo, h2d, w_glu, b_glu, w_out, g, g_kv, g_next, w_fgate, b_fgate,
             *, bsz, tl, sub):
    assert tl % sub == 0
    m, d = h2d.shape
    seqlen = m // bsz
    main_w = w_glu.shape[0]
    mem_w = w_out.shape[0] - main_w
    heads = w_fgate.shape[1]
    assert 3 * heads <= LANES and seqlen % tl == 0
    wpad = jnp.zeros((d, LANES), F32).at[:, :heads].set(w_fgate.astype(F32))
    wcat = jnp.concatenate(_split2(wpad), axis=1)
    bpad = jnp.zeros((1, LANES), F32).at[0, :heads].set(b_fgate.astype(F32))
    per_b = seqlen // tl
    row = lambda b, i: (b * per_b + i, 0)
    vec = lambda v: v.reshape(1, -1).astype(F32)
    return pl.pallas_call(
        functools.partial(_glu_out_kernel, heads=heads, sub=sub),
        grid=(bsz, per_b),
        in_specs=[pl.BlockSpec((tl, main_w), lambda b, i: (i, b)),
                  pl.BlockSpec((tl, main_w), row),
                  pl.BlockSpec((tl, mem_w), row),
                  pl.BlockSpec((tl, d), row),
                  _resident((main_w, main_w)),
                  _resident((1, main_w)),
                  _resident((main_w, d)),
                  _resident((mem_w, d)),
                  _resident((1, d)),
                  _resident((1, d)),
                  _resident((1, d)),
                  _resident((d, 2 * LANES)),
                  _resident((1, LANES))],
        out_specs=[pl.BlockSpec((tl, d), row),
                   pl.BlockSpec((tl, d), row),
                   pl.BlockSpec((tl, d), row),
                   pl.BlockSpec((tl, LANES), row)],
        out_shape=[jax.ShapeDtypeStruct((m, d), F32),
                   jax.ShapeDtypeStruct((m, d), BF16),
                   jax.ShapeDtypeStruct((m, d), BF16),
                   jax.ShapeDtypeStruct((m, LANES), BF16)],
        scratch_shapes=[pltpu.VMEM((1, LANES), F32)],
        compiler_params=_params(("arbitrary", "arbitrary")),
        name="glu_out_proj",
    )(yg_tm, rest, memo, h2d, w_glu.astype(BF16), vec(b_glu),
      w_out[:main_w].astype(BF16), w_out[main_w:].astype(BF16), vec(g), vec(g_kv),
      vec(g_next), wcat, bpad)


def _out_proj(main, memo, h2d, w_out, g, *, tl):
    m, d = h2d.shape
    heads, _, dh = main.shape
    main_w = heads * dh
    mem_w = memo.shape[1]
    row = lambda i: (i, 0)
    return pl.pallas_call(
        _out_kernel,
        grid=(m // tl,),
        in_specs=[pl.BlockSpec((heads, tl, dh), lambda i: (0, i, 0)),
                  pl.BlockSpec((tl, mem_w), row),
                  pl.BlockSpec((tl, d), row),
                  _resident((main_w, d)),
                  _resident((mem_w, d)),
                  _resident((1, d))],
        out_specs=pl.BlockSpec((tl, d), row),
        out_shape=jax.ShapeDtypeStruct((m, d), F32),
        compiler_params=_params(("parallel",)),
        name="out_proj",
    )(main, memo, h2d, w_out[:main_w].astype(BF16), w_out[main_w:].astype(BF16),
      g.reshape(1, d).astype(F32))


FOX_CHUNK = 64


def _fox_kernel(q_ref, k_ref, v_ref, fp_ref, sel_ref, cst_ref, z_ref, o_ref,
                ka_ref, qa_ref, vt_ref, st_ref, pt_ref, m_ref, l_ref, acc_ref, *, tq):
    seqlen = q_ref.shape[0]
    ncg = tq // LANES
    nch = tq // FOX_CHUNK
    ext = jnp.dot(fp_ref[...], sel_ref[0], preferred_element_type=F32)
    ka_ref[:, :HEAD_DIM] = k_ref[...]
    ka_ref[:, HEAD_DIM:] = (ext[:, :HEAD_DIM] + cst_ref[0:1, :]).astype(BF16)
    qa_ref[:, :HEAD_DIM] = q_ref[...]
    qa_ref[:, HEAD_DIM:] = (ext[:, HEAD_DIM:] + cst_ref[1:2, :]).astype(BF16)
    vt_ref[...] = v_ref[...].astype(F32).T.astype(BF16)
    for ci in range(nch):
        for g in range(ncg):
            if g * LANES + LANES - 1 < ci * FOX_CHUNK:
                pt_ref[1, ci * FOX_CHUNK:(ci + 1) * FOX_CHUNK, g * LANES:(g + 1) * LANES] = (
                    jnp.zeros((FOX_CHUNK, LANES), BF16))

    def scores(qi, kj, slot):
        st_ref[slot] = lax.dot_general(ka_ref[kj * tq:(kj + 1) * tq, :],
                                       qa_ref[qi * tq:(qi + 1) * tq, :],
                                       (((1,), (1,)), ((), ())),
                                       preferred_element_type=F32)

    def piece(slot, ci, g, masked):
        r0, c0 = ci * FOX_CHUNK, g * LANES
        if masked and c0 + LANES - 1 < r0:
            return None
        x = st_ref[slot, r0:r0 + FOX_CHUNK, c0:c0 + LANES]
        if masked and c0 < r0 + FOX_CHUNK - 1:
            key = r0 + lax.broadcasted_iota(jnp.int32, (FOX_CHUNK, LANES), 0)
            qry = c0 + lax.broadcasted_iota(jnp.int32, (FOX_CHUNK, LANES), 1)
            x = jnp.where(key <= qry, x, -jnp.inf)
        return x

    def fold(x, op):
        return op(x.reshape(FOX_CHUNK // SUBLANES, SUBLANES, LANES), axis=0)

    def softmax_pv(kj, slot, masked, first):
        mx = [None] * ncg
        for ci in range(nch):
            for g in range(ncg):
                x = piece(slot, ci, g, masked)
                if x is not None:
                    f = fold(x, jnp.max)
                    mx[g] = f if mx[g] is None else jnp.maximum(mx[g], f)
        m_new = jnp.concatenate([jnp.max(v, axis=0, keepdims=True) for v in mx], axis=1)
        if not first:
            m_old = m_ref[...]
            m_new = jnp.maximum(m_old, m_new)
            alpha = jnp.exp2(m_old - m_new)
        m_ref[...] = m_new
        ls = [None] * ncg
        for ci in range(nch):
            for g in range(ncg):
                r0, c0 = ci * FOX_CHUNK, g * LANES
                x = piece(slot, ci, g, masked)
                if x is None:
                    continue
                p = jnp.exp2(x - m_new[:, c0:c0 + LANES])
                f = fold(p, jnp.sum)
                ls[g] = f if ls[g] is None else ls[g] + f
                pt_ref[int(masked), r0:r0 + FOX_CHUNK, c0:c0 + LANES] = p.astype(BF16)
        l_new = jnp.concatenate([jnp.sum(v, axis=0, keepdims=True) for v in ls], axis=1)
        pv = jnp.dot(vt_ref[:, kj * tq:(kj + 1) * tq], pt_ref[int(masked)],
                     preferred_element_type=F32)
        if first:
            l_ref[...] = l_new
            acc_ref[...] = pv
        else:
            l_ref[...] = alpha * l_ref[...] + l_new
            acc_ref[...] = alpha * acc_ref[...] + pv

    for qi in range(seqlen // tq):
        rows = slice(qi * tq, (qi + 1) * tq)
        scores(qi, 0, 0)
        for kj in range(qi + 1):
            if kj < qi:
                scores(qi, kj + 1, (kj + 1) % 2)
            softmax_pv(kj, kj % 2, masked=(kj == qi), first=(kj == 0))
        z = z_ref[rows, :].astype(F32)
        o = (acc_ref[...] / l_ref[...]).T
        o_ref[rows, :] = (o * (z * jax.nn.sigmoid(z))).astype(o_ref.dtype)


def _fox_attn(q, k, v, z, fparts, *, bsz, tq):
    heads, m, _ = q.shape
    seqlen = m // bsz
    nterms = 3
    assert seqlen % tq == 0 and tq % FOX_CHUNK == 0 and nterms * heads <= LANES
    h_idx = jnp.arange(heads)[:, None, None]
    r_idx = jnp.arange(LANES)[None, :, None]
    c_idx = jnp.arange(2 * HEAD_DIM)[None, None, :]
    key_side = (c_idx < nterms) & (r_idx == c_idx * heads + h_idx)
    qc = c_idx - HEAD_DIM - nterms
    qry_side = (qc >= 0) & (qc < nterms) & (r_idx == qc * heads + h_idx)
    sel = (qry_side.astype(F32) - key_side.astype(F32)).astype(BF16)
    col = jnp.arange(HEAD_DIM)
    cst = jnp.zeros((SUBLANES, HEAD_DIM), F32)
    cst = cst.at[0].set(((col >= nterms) & (col < 2 * nterms)).astype(F32))
    cst = cst.at[1].set((col < nterms).astype(F32))
    head_seq = pl.BlockSpec((None, seqlen, HEAD_DIM), lambda b, h: (h, b, 0))
    return pl.pallas_call(
        functools.partial(_fox_kernel, tq=tq),
        grid=(bsz, heads),
        in_specs=[head_seq, head_seq, head_seq,
                  pl.BlockSpec((seqlen, LANES), lambda b, h: (b, 0)),
                  pl.BlockSpec((1, LANES, 2 * HEAD_DIM), lambda b, h: (h, 0, 0)),
                  pl.BlockSpec((SUBLANES, HEAD_DIM), lambda b, h: (0, 0)),
                  head_seq],
        out_specs=head_seq,
        out_shape=jax.ShapeDtypeStruct((heads, m, HEAD_DIM), BF16),
        scratch_shapes=[pltpu.VMEM((seqlen, 2 * HEAD_DIM), BF16),
                        pltpu.VMEM((seqlen, 2 * HEAD_DIM), BF16),
                        pltpu.VMEM((HEAD_DIM, seqlen), BF16),
                        pltpu.VMEM((2, tq, tq), F32),
                        pltpu.VMEM((2, tq, tq), BF16),
                        pltpu.VMEM((1, tq), F32),
                        pltpu.VMEM((1, tq), F32),
                        pltpu.VMEM((HEAD_DIM, tq), F32)],
        compiler_params=_params(("arbitrary", "arbitrary")),
        name="fox_attn",
    )(q, k, v, fparts, sel, cst, z)


def kernel(x, mem, pre_norm_g, post_norm_g, w_in_a, lam_re, lam_im, log_step, b_re, b_im,
           c_re, c_im, d_skip, w_glu, b_glu, kv_norm_g, w_kv, w_fgate, b_fgate, w_in_b,
           mem_norm_g, w_mem_kv, w_out):
    bsz, seqlen, d = x.shape
    n_mem = mem.shape[1]
    main_w = w_glu.shape[1]
    mem_w = w_out.shape[1] - main_w
    heads = main_w // HEAD_DIM
    scale = HEAD_DIM ** -0.5
    x2d = x.reshape(bsz * seqlen, d)
    mem2d = mem.reshape(bsz * n_mem, d)

    ones = jnp.ones((main_w,), F32)
    mem_scale = jnp.full((mem_w,), scale * LOG2E, F32)
    cs_a = jnp.concatenate([ones, ones, mem_scale, ones[:mem_w]])
    cs_b = jnp.concatenate([ones * (scale * LOG2E), ones, mem_scale, ones[:mem_w]])
    xa = _prenorm(x2d, pre_norm_g[0], tm=512)
    proj = functools.partial(_matmul, out_dtype=BF16, tm=1024)
    u_tm = _matmul(xa, w_in_a[0], cs_a, col0=0, n=main_w, out_dtype=F32, tm=1024, tn=main_w,
                   time_major_batches=bsz, name="in_proj_a_u")
    z_a = proj(xa, w_in_a[0], cs_a, col0=main_w, n=main_w, tn=main_w, name="in_proj_a_z")
    qz_a = proj(xa, w_in_a[0], cs_a, col0=2 * main_w, n=2 * mem_w, tn=2 * mem_w,
                slab_width=HEAD_DIM, name="in_proj_a_mem")
    kvm0 = _norm_matmul(mem2d, mem_norm_g[0], w_mem_kv[0].astype(BF16), out_dtype=BF16,
                        tm=512, tn=512, name="mem_kv0")
    memo0 = _mem_attn(qz_a, kvm0, bsz=bsz, tq=512, name="mem_attn0")
    s5_ops = _s5_discretise(lam_re[0], lam_im[0], log_step[0], b_re[0], b_im[0],
                            c_re[0], c_im[0])
    yg_tm = _s5(u_tm, *s5_ops, d_skip[0], bsz=bsz, n_pairs=128, blocks_per_step=2)
    h1, xkv, xb, fparts = _glu_out(
        yg_tm, z_a, memo0, x2d, w_glu[0], b_glu[0], w_out[0], post_norm_g[0],
        kv_norm_g, pre_norm_g[1], w_fgate, b_fgate, bsz=bsz, tl=512, sub=256)

    cs_kv = jnp.ones((w_kv.shape[1],), F32)
    k_sh = proj(xkv, w_kv, cs_kv, col0=0, n=main_w, tn=main_w, slab_width=HEAD_DIM,
                name="k_proj")
    v_sh = proj(xkv, w_kv, cs_kv, col0=main_w, n=main_w, tn=main_w, slab_width=HEAD_DIM,
                name="v_proj")

    q_b = proj(xb, w_in_b[0], cs_b, col0=0, n=main_w, tn=main_w, slab_width=HEAD_DIM,
               name="in_proj_b_q")
    z_b = proj(xb, w_in_b[0], cs_b, col0=main_w, n=main_w, tn=main_w, slab_width=HEAD_DIM,
               name="in_proj_b_z")
    qz_b = proj(xb, w_in_b[0], cs_b, col0=2 * main_w, n=2 * mem_w, tn=2 * mem_w,
                slab_width=HEAD_DIM, name="in_proj_b_mem")
    kvm1 = _norm_matmul(mem2d, mem_norm_g[1], w_mem_kv[1].astype(BF16), out_dtype=BF16,
                        tm=512, tn=512, name="mem_kv1")
    memo1 = _mem_attn(qz_b, kvm1, bsz=bsz, tq=512, name="mem_attn1")
    att = _fox_attn(q_b, k_sh, v_sh, z_b, fparts, bsz=bsz, tq=512)
    out = _out_proj(att, memo1, h1, w_out[1], post_norm_g[1], tl=512)
    return out.reshape(bsz, seqlen, d)
```

```python
import functools
import math

import jax
import jax.numpy as jnp
from jax import lax
from jax.experimental import pallas as pl
from jax.experimental.pallas import tpu as pltpu

F32 = jnp.float32
BF16 = jnp.bfloat16

EPS = 1e-6
LOG2E = 1.4426950408889634
HEAD_DIM = 128
SSM_GROUP = 16
SSM_STATE = 64
MEM_HEADS = 4
LANES = 128
SUBLANES = 8
GROUPS_PER_BLOCK = LANES // SSM_GROUP
STATE_COLS = GROUPS_PER_BLOCK * SSM_STATE
VMEM_LIMIT = 56 * 1024 * 1024


def _params(sem, vmem=VMEM_LIMIT):
    return pltpu.CompilerParams(dimension_semantics=sem, vmem_limit_bytes=vmem)


def _norm_matmul_kernel(x_ref, g_ref, w_ref, o_ref, xn_ref):
    @pl.when(pl.program_id(1) == 0)
    def _():
        x = x_ref[...]
        ms = jnp.mean(x * x, axis=-1, keepdims=True)
        xn_ref[...] = (x * lax.rsqrt(ms + EPS) * g_ref[...]).astype(BF16)

    o_ref[...] = jnp.dot(xn_ref[...], w_ref[...].astype(BF16),
                         preferred_element_type=F32).astype(o_ref.dtype)


def _norm_matmul(x2d, g, w, *, out_dtype, tm, tn, time_major_batches=None, name):
    m, d = x2d.shape
    n = w.shape[1]
    assert m % tm == 0 and n % tn == 0
    nj = n // tn
    if time_major_batches is None:
        out_shape = jax.ShapeDtypeStruct((m, n), out_dtype)
        out_spec = pl.BlockSpec((tm, tn), lambda i, j: (i, j))
    else:
        bsz = time_major_batches
        seqlen = m // bsz
        assert seqlen % tm == 0
        per_b = seqlen // tm
        out_shape = jax.ShapeDtypeStruct((seqlen, bsz * n), out_dtype)
        out_spec = pl.BlockSpec((tm, tn), lambda i, j: (i % per_b, (i // per_b) * nj + j))
    return pl.pallas_call(
        _norm_matmul_kernel,
        grid=(m // tm, nj),
        in_specs=[pl.BlockSpec((tm, d), lambda i, j: (i, 0)),
                  pl.BlockSpec((1, d), lambda i, j: (0, 0)),
                  pl.BlockSpec((d, tn), lambda i, j: (0, j))],
        out_specs=out_spec,
        out_shape=out_shape,
        scratch_shapes=[pltpu.VMEM((tm, d), BF16)],
        compiler_params=_params(("parallel", "arbitrary")),
        name=name,
    )(x2d, g.reshape(1, d), w)


def _prenorm_kernel(x_ref, g_ref, o_ref):
    x = x_ref[...]
    ms = jnp.mean(x * x, axis=-1, keepdims=True)
    o_ref[...] = (x * lax.rsqrt(ms + EPS) * g_ref[...]).astype(o_ref.dtype)


def _prenorm(x2d, g, *, tm):
    m, d = x2d.shape
    return pl.pallas_call(
        _prenorm_kernel,
        grid=(m // tm,),
        in_specs=[pl.BlockSpec((tm, d), lambda i: (i, 0)),
                  pl.BlockSpec((1, d), lambda i: (0, 0))],
        out_specs=pl.BlockSpec((tm, d), lambda i: (i, 0)),
        out_shape=jax.ShapeDtypeStruct((m, d), BF16),
        compiler_params=_params(("parallel",)),
        name="prenorm",
    )(x2d, g.reshape(1, d).astype(F32))


def _matmul_kernel(x_ref, w_ref, cs_ref, o_ref, wb_ref):
    @pl.when(pl.program_id(1) == 0)
    def _():
        wb_ref[...] = (w_ref[...] * cs_ref[...]).astype(BF16)

    res = jnp.dot(x_ref[...], wb_ref[...], preferred_element_type=F32).astype(o_ref.dtype)
    if len(o_ref.shape) == 3:
        sw = o_ref.shape[2]
        for s in range(o_ref.shape[0]):
            o_ref[s] = res[:, s * sw:(s + 1) * sw]
    else:
        o_ref[...] = res


def _matmul(xn, w, colscale, *, col0, n, out_dtype, tm, tn, time_major_batches=None,
            slab_width=None, name):
    m, d = xn.shape
    assert m % tm == 0 and n % tn == 0 and col0 % tn == 0
    nj = n // tn
    j0 = col0 // tn
    if slab_width is not None:
        assert time_major_batches is None and tn % slab_width == 0
        per_tile = tn // slab_width
        out_shape = jax.ShapeDtypeStruct((n // slab_width, m, slab_width), out_dtype)
        out_spec = pl.BlockSpec((per_tile, tm, slab_width), lambda j, i: (j, i, 0))
    elif time_major_batches is None:
        out_shape = jax.ShapeDtypeStruct((m, n), out_dtype)
        out_spec = pl.BlockSpec((tm, tn), lambda j, i: (i, j))
    else:
        bsz = time_major_batches
        seqlen = m // bsz
        assert seqlen % tm == 0
        per_b = seqlen // tm
        out_shape = jax.ShapeDtypeStruct((seqlen, bsz * n), out_dtype)
        out_spec = pl.BlockSpec((tm, tn), lambda j, i: (i % per_b, (i // per_b) * nj + j))
    return pl.pallas_call(
        _matmul_kernel,
        grid=(nj, m // tm),
        in_specs=[pl.BlockSpec((tm, d), lambda j, i: (i, 0)),
                  pl.BlockSpec((d, tn), lambda j, i: (0, j0 + j), pipeline_mode=pl.Buffered(1)),
                  pl.BlockSpec((1, tn), lambda j, i: (0, j0 + j))],
        out_specs=out_spec,
        out_shape=out_shape,
        scratch_shapes=[pltpu.VMEM((d, tn), BF16)],
        compiler_params=_params(("arbitrary", "arbitrary")),
        name=name,
    )(xn, w, colscale.reshape(1, -1).astype(F32))


def _split2(x):
    hi = x.astype(BF16)
    return hi, (x - hi.astype(F32)).astype(BF16)


def _split3(x):
    hi = x.astype(BF16)
    r = x - hi.astype(F32)
    mid = r.astype(BF16)
    return hi, mid, (r - mid.astype(F32)).astype(BF16)


def _fgate_block(xn, wcat_ref, b_ref, carry_ref, heads):
    tl = xn.shape[0]
    xh, xl = _split2(xn)
    both = jnp.dot(xh, wcat_ref[...], preferred_element_type=F32)
    logit = (both[:, :LANES] + both[:, LANES:]
             + jnp.dot(xl, wcat_ref[:, :LANES], preferred_element_type=F32)) + b_ref[...]
    logf = jnp.minimum(logit, 0.0) - jnp.log(1.0 + jnp.exp(-jnp.abs(logit)))
    t_idx = lax.broadcasted_iota(jnp.int32, (tl, tl), 0)
    s_idx = lax.broadcasted_iota(jnp.int32, (tl, tl), 1)
    tril = (s_idx <= t_idx).astype(BF16)
    csum = carry_ref[...]
    for part in _split3(logf):
        csum = csum + jnp.dot(tril, part, preferred_element_type=F32)
    carry_ref[...] = csum[tl - 1:tl, :]
    lane = lax.broadcasted_iota(jnp.int32, (tl, LANES), 1)
    packed = jnp.zeros((tl, LANES), F32)
    for n, part in reversed(list(enumerate(_split3(csum * LOG2E)))):
        shifted = part.astype(F32) if n == 0 else pltpu.roll(part.astype(F32), n * heads, 1)
        packed = jnp.where(lane < (n + 1) * heads, shifted, packed)
    return packed.astype(BF16)


def _mem_attn_kernel(q_ref, zm_ref, kv_ref, o_ref):
    width = MEM_HEADS * HEAD_DIM
    heads = range(MEM_HEADS)
    cols = [slice(h * HEAD_DIM, (h + 1) * HEAD_DIM) for h in heads]
    st = [lax.dot_general(kv_ref[:, cols[h]], q_ref[h], (((1,), (1,)), ((), ())),
                          preferred_element_type=F32) for h in heads]
    vt = [kv_ref[:, width + h * HEAD_DIM:width + (h + 1) * HEAD_DIM].astype(F32).T.astype(BF16)
          for h in heads]
    p = [jnp.exp2(st[h] - jnp.max(st[h], axis=0, keepdims=True)) for h in heads]
    inv = [1.0 / jnp.sum(p[h], axis=0, keepdims=True) for h in heads]
    ot = [jnp.dot(vt[h], p[h].astype(BF16), preferred_element_type=F32) * inv[h] for h in heads]
    for h in heads:
        zm = zm_ref[h].astype(F32)
        o_ref[:, cols[h]] = (ot[h].T * (zm * jax.nn.sigmoid(zm))).astype(o_ref.dtype)


def _mem_attn(qz, kvm, *, bsz, tq, name):
    width = MEM_HEADS * HEAD_DIM
    m = qz.shape[1]
    seqlen = m // bsz
    n_mem = kvm.shape[0] // bsz
    per_b = seqlen // tq
    spec = lambda blk: pl.BlockSpec((MEM_HEADS, tq, HEAD_DIM),
                                    lambda b, i: (blk, b * per_b + i, 0))
    return pl.pallas_call(
        _mem_attn_kernel,
        grid=(bsz, per_b),
        in_specs=[spec(0), spec(1),
                  pl.BlockSpec((n_mem, 2 * width), lambda b, i: (b, 0))],
        out_specs=pl.BlockSpec((tq, width), lambda b, i: (b * per_b + i, 0)),
        out_shape=jax.ShapeDtypeStruct((m, width), BF16),
        compiler_params=_params(("parallel", "arbitrary")),
        name=name,
    )(qz, qz, kvm)


def _s5_kernel(u_ref, win_ref, wout_ref, wt_ref, ar_ref, ai_ref, d_ref, y_ref,
               bu_ref, xs_ref, st_ref, *, n_pairs):
    @pl.when(pl.program_id(1) == 0)
    def _():
        st_ref[...] = jnp.zeros_like(st_ref)

    nblk = win_ref.shape[0]
    rows = n_pairs * SUBLANES
    u0, u1, ucat, a2, state = [], [], [], [], []
    for k in range(nblk):
        upair = u_ref[:, :, k * LANES:(k + 1) * LANES].reshape(n_pairs, 2, SUBLANES, LANES)
        u0.append(upair[:, 0].reshape(rows, LANES))
        u1.append(upair[:, 1].reshape(rows, LANES))
        ucat.append(jnp.concatenate([u0[k], u1[k]], axis=1).astype(BF16))
        bu_ref[k] = jnp.dot(ucat[k], win_ref[k], preferred_element_type=F32)
        a2.append((jnp.broadcast_to(ar_ref[k], (SUBLANES, STATE_COLS)),
                   jnp.broadcast_to(ai_ref[k], (SUBLANES, STATE_COLS))))
        state.append((st_ref[k, 0], st_ref[k, 1]))

    held = [None] * nblk
    for c in range(n_pairs):
        for k in range(nblk):
            xr, xi = state[k]
            cur = jnp.concatenate([xr, xi], axis=1)
            if c % 2 == 1:
                xs_ref[k, (c - 1) * SUBLANES:(c + 1) * SUBLANES, :] = jnp.concatenate(
                    [held[k], cur], axis=0).astype(BF16)
            held[k] = cur
            bur = bu_ref[k, c * SUBLANES:(c + 1) * SUBLANES, 0:STATE_COLS]
            bui = bu_ref[k, c * SUBLANES:(c + 1) * SUBLANES, STATE_COLS:2 * STATE_COLS]
            ar, ai = a2[k]
            state[k] = (ar * xr - ai * xi + bur, ar * xi + ai * xr + bui)

    for k in range(nblk):
        st_ref[k, 0] = state[k][0]
        st_ref[k, 1] = state[k][1]
        y = (jnp.dot(xs_ref[k], wout_ref[k], preferred_element_type=F32)
             + jnp.dot(ucat[k], wt_ref[k], preferred_element_type=F32))
        d = d_ref[:, k * LANES:(k + 1) * LANES]
        y0 = jax.nn.gelu(y[:, :LANES] + d * u0[k]).reshape(n_pairs, SUBLANES, LANES)
        y1 = jax.nn.gelu(y[:, LANES:] + d * u1[k]).reshape(n_pairs, SUBLANES, LANES)
        y_ref[:, :, k * LANES:(k + 1) * LANES] = jnp.stack([y0, y1], axis=1).reshape(
            2 * n_pairs, SUBLANES, LANES)


def _s5_discretise(lam_re, lam_im, log_step, b_re, b_im, c_re, c_im):
    groups = lam_re.shape[0]
    nblk = groups // GROUPS_PER_BLOCK
    lr = lam_re.astype(F32)
    li = lam_im.astype(F32)
    dt = jnp.exp(log_step.astype(F32))[:, None]
    mag = jnp.exp(lr * dt)
    ar = mag * jnp.cos(li * dt)
    ai = mag * jnp.sin(li * dt)
    den = lr * lr + li * li
    cr = ((ar - 1.0) * lr + ai * li) / den
    ci = (ai * lr - (ar - 1.0) * li) / den
    br = b_re.astype(F32)
    bi = b_im.astype(F32)
    bbar_re = cr[..., None] * br - ci[..., None] * bi
    bbar_im = cr[..., None] * bi + ci[..., None] * br
    a2r = ar * ar - ai * ai
    a2i = 2.0 * ar * ai
    ab_re = ar[..., None] * bbar_re - ai[..., None] * bbar_im
    ab_im = ar[..., None] * bbar_im + ai[..., None] * bbar_re
    cre = c_re.astype(F32)
    cim = c_im.astype(F32)
    ca_re = cre * ar[:, None, :] - cim * ai[:, None, :]
    ca_im = cre * ai[:, None, :] + cim * ar[:, None, :]
    ca2_re = cre * a2r[:, None, :] - cim * a2i[:, None, :]
    ca2_im = cre * a2i[:, None, :] + cim * a2r[:, None, :]
    k0 = jnp.einsum('ghp,gpk->ghk', cre, bbar_re) - jnp.einsum('ghp,gpk->ghk', cim, bbar_im)
    k1 = jnp.einsum('ghp,gpk->ghk', cre, ab_re) - jnp.einsum('ghp,gpk->ghk', cim, ab_im)
    eye = jnp.eye(GROUPS_PER_BLOCK, dtype=F32)

    def per_block(x):
        return x.reshape((nblk, GROUPS_PER_BLOCK) + x.shape[1:])

    b_in = jnp.stack([jnp.stack([per_block(ab_re), per_block(ab_im)]),
                      jnp.stack([per_block(bbar_re), per_block(bbar_im)])])
    win = jnp.einsum('irjgph,gk->jighrkp', b_in, eye).reshape(
        nblk, 2 * LANES, 2 * STATE_COLS)
    c_out = jnp.stack([jnp.stack([per_block(ca_re), per_block(ca2_re)]),
                       jnp.stack([-per_block(ca_im), -per_block(ca2_im)])])
    wout = jnp.einsum('rijghp,gk->jrgpikh', c_out, eye).reshape(
        nblk, 2 * STATE_COLS, 2 * LANES)
    k0b, k1b = per_block(k0), per_block(k1)
    k_mix = jnp.stack([jnp.stack([k0b, k1b]),
                       jnp.stack([jnp.zeros_like(k0b), k0b])])
    wt = jnp.einsum('abjghk,gm->jagkbmh', k_mix, eye).reshape(nblk, 2 * LANES, 2 * LANES)
    return (win.astype(BF16), wout.astype(BF16), wt.astype(BF16),
            a2r.reshape(nblk, 1, STATE_COLS), a2i.reshape(nblk, 1, STATE_COLS))


def _s5(u_tm, win, wout, wt, a2r, a2i, d_skip, *, bsz, n_pairs, blocks_per_step):
    seqlen = u_tm.shape[0]
    width = u_tm.shape[1] // bsz
    assert bsz == SUBLANES and seqlen % (2 * n_pairs) == 0 and width % LANES == 0
    nblk = width // LANES
    assert nblk % blocks_per_step == 0 and n_pairs % 2 == 0
    nb = blocks_per_step
    rows = n_pairs * SUBLANES
    u3 = u_tm.reshape(seqlen, bsz, width)
    blk = pl.BlockSpec((2 * n_pairs, bsz, nb * LANES), lambda j, c: (c, 0, j))
    per_blk = lambda shape: pl.BlockSpec((nb,) + shape, lambda j, c: (j, 0, 0))
    out = pl.pallas_call(
        functools.partial(_s5_kernel, n_pairs=n_pairs),
        grid=(nblk // nb, seqlen // (2 * n_pairs)),
        in_specs=[blk,
                  per_blk((2 * LANES, 2 * STATE_COLS)),
                  per_blk((2 * STATE_COLS, 2 * LANES)),
                  per_blk((2 * LANES, 2 * LANES)),
                  per_blk((1, STATE_COLS)),
                  per_blk((1, STATE_COLS)),
                  pl.BlockSpec((1, nb * LANES), lambda j, c: (0, j))],
        out_specs=blk,
        out_shape=jax.ShapeDtypeStruct((seqlen, bsz, width), F32),
        scratch_shapes=[pltpu.VMEM((nb, rows, 2 * STATE_COLS), F32),
                        pltpu.VMEM((nb, rows, 2 * STATE_COLS), BF16),
                        pltpu.VMEM((nb, 2, SUBLANES, STATE_COLS), F32)],
        compiler_params=_params(("arbitrary", "arbitrary")),
        name="s5_scan",
    )(u3, win, wout, wt, a2r, a2i, d_skip.reshape(1, width).astype(F32))
    return out.reshape(seqlen, bsz * width)


def _post(o, g_ref, h_ref, out_ref):
    ms = jnp.mean(o * o, axis=-1, keepdims=True)
    out_ref[...] = h_ref[...] + o * lax.rsqrt(ms + EPS) * g_ref[...]


def _glu_out_kernel(yg_ref, z_ref, memo_ref, h_ref, wglu_ref, bglu_ref,
                    wmain_ref, wmem_ref, g_ref, gkv_ref, gnext_ref, wf_ref, bf_ref,
                    out_ref, xkv_ref, xnext_ref, fp_ref, carry_ref, *, heads, sub):
    @pl.when(pl.program_id(1) == 0)
    def _():
        carry_ref[...] = jnp.zeros_like(carry_ref)

    for s in range(out_ref.shape[0] // sub):
        rows = slice(s * sub, (s + 1) * sub)
        yg = yg_ref[rows, :]
        t = jnp.dot(yg.astype(BF16), wglu_ref[...], preferred_element_type=F32) + bglu_ref[...]
        z = z_ref[rows, :].astype(F32)
        main = yg * jax.nn.sigmoid(t) * (z * jax.nn.sigmoid(z))
        o = jnp.dot(main.astype(BF16), wmain_ref[...], preferred_element_type=F32)
        o = o + jnp.dot(memo_ref[rows, :], wmem_ref[...], preferred_element_type=F32)
        ms = jnp.mean(o * o, axis=-1, keepdims=True)
        h1 = h_ref[rows, :] + o * lax.rsqrt(ms + EPS) * g_ref[...]
        out_ref[rows, :] = h1
        r = h1 * lax.rsqrt(jnp.mean(h1 * h1, axis=-1, keepdims=True) + EPS)
        xnext_ref[rows, :] = (r * gnext_ref[...]).astype(BF16)
        xkv = r * gkv_ref[...]
        xkv_ref[rows, :] = xkv.astype(BF16)
        fp_ref[rows, :] = _fgate_block(xkv, wf_ref, bf_ref, carry_ref, heads)


def _out_kernel(main_ref, memo_ref, h_ref, wmain_ref, wmem_ref, g_ref, out_ref):
    main = jnp.concatenate([main_ref[s] for s in range(main_ref.shape[0])], axis=1)
    o = jnp.dot(main, wmain_ref[...], preferred_element_type=F32)
    o = o + jnp.dot(memo_ref[...], wmem_ref[...], preferred_element_type=F32)
    _post(o, g_ref, h_ref, out_ref)


def _resident(shape):
    return pl.BlockSpec(shape, lambda *_: (0,) * len(shape), pipeline_mode=pl.Buffered(1))


def _glu_out(yg_tm, rest, memo, h2d, w_glu, b_glu, w_out, g, g_kv, g_next, w_fgate, b_fgate,
             *, bsz, tl, sub):
    assert tl % sub == 0
    m, d = h2d.shape
    seqlen = m // bsz
    main_w = w_glu.shape[0]
    mem_w = w_out.shape[0] - main_w
    heads = w_fgate.shape[1]
    assert 3 * heads <= LANES and seqlen % tl == 0
    wpad = jnp.zeros((d, LANES), F32).at[:, :heads].set(w_fgate.astype(F32))
    wcat = jnp.concatenate(_split2(wpad), axis=1)
    bpad = jnp.zeros((1, LANES), F32).at[0, :heads].set(b_fgate.astype(F32))
    per_b = seqlen // tl
    row = lambda b, i: (b * per_b + i, 0)
    vec = lambda v: v.reshape(1, -1).astype(F32)
    return pl.pallas_call(
        functools.partial(_glu_out_kernel, heads=heads, sub=sub),
        grid=(bsz, per_b),
        in_specs=[pl.BlockSpec((tl, main_w), lambda b, i: (i, b)),
                  pl.BlockSpec((tl, main_w), row),
                  pl.BlockSpec((tl, mem_w), row),
                  pl.BlockSpec((tl, d), row),
                  _resident((main_w, main_w)),
                  _resident((1, main_w)),
                  _resident((main_w, d)),
                  _resident((mem_w, d)),
                  _resident((1, d)),
                  _resident((1, d)),
                  _resident((1, d)),
                  _resident((d, 2 * LANES)),
                  _resident((1, LANES))],
        out_specs=[pl.BlockSpec((tl, d), row),
                   pl.BlockSpec((tl, d), row),
                   pl.BlockSpec((tl, d), row),
                   pl.BlockSpec((tl, LANES), row)],
        out_shape=[jax.ShapeDtypeStruct((m, d), F32),
                   jax.ShapeDtypeStruct((m, d), BF16),
                   jax.ShapeDtypeStruct((m, d), BF16),
                   jax.ShapeDtypeStruct((m, LANES), BF16)],
        scratch_shapes=[pltpu.VMEM((1, LANES), F32)],
        compiler_params=_params(("arbitrary", "arbitrary")),
        name="glu_out_proj",
    )(yg_tm, rest, memo, h2d, w_glu.astype(BF16), vec(b_glu),
      w_out[:main_w].astype(BF16), w_out[main_w:].astype(BF16), vec(g), vec(g_kv),
      vec(g_next), wcat, bpad)


def _out_proj(main, memo, h2d, w_out, g, *, tl):
    m, d = h2d.shape
    heads, _, dh = main.shape
    main_w = heads * dh
    mem_w = memo.shape[1]
    row = lambda i: (i, 0)
    return pl.pallas_call(
        _out_kernel,
        grid=(m // tl,),
        in_specs=[pl.BlockSpec((heads, tl, dh), lambda i: (0, i, 0)),
                  pl.BlockSpec((tl, mem_w), row),
                  pl.BlockSpec((tl, d), row),
                  _resident((main_w, d)),
                  _resident((mem_w, d)),
                  _resident((1, d))],
        out_specs=pl.BlockSpec((tl, d), row),
        out_shape=jax.ShapeDtypeStruct((m, d), F32),
        compiler_params=_params(("parallel",)),
        name="out_proj",
    )(main, memo, h2d, w_out[:main_w].astype(BF16), w_out[main_w:].astype(BF16),
      g.reshape(1, d).astype(F32))


FOX_CHUNK = 64


def _fox_kernel(q_ref, k_ref, v_ref, fp_ref, sel_ref, cst_ref, z_ref, o_ref,
                ka_ref, qa_ref, vt_ref, st_ref, pt_ref, m_ref, l_ref, acc_ref, *, tq):
    nh, seqlen, _ = q_ref.shape
    ncg = tq // LANES
    nch = tq // FOX_CHUNK
    for hh in range(nh):
        ext = jnp.dot(fp_ref[...], sel_ref[hh], preferred_element_type=F32)
        ka_ref[hh, :, :HEAD_DIM] = k_ref[hh]
        ka_ref[hh, :, HEAD_DIM:] = (ext[:, :HEAD_DIM] + cst_ref[0:1, :]).astype(BF16)
        qa_ref[hh, :, :HEAD_DIM] = q_ref[hh]
        qa_ref[hh, :, HEAD_DIM:] = (ext[:, HEAD_DIM:] + cst_ref[1:2, :]).astype(BF16)
        vt_ref[hh] = v_ref[hh].astype(F32).T.astype(BF16)
        for ci in range(nch):
            for g in range(ncg):
                if g * LANES + LANES - 1 < ci * FOX_CHUNK:
                    pt_ref[hh, 1, ci * FOX_CHUNK:(ci + 1) * FOX_CHUNK,
                           g * LANES:(g + 1) * LANES] = jnp.zeros((FOX_CHUNK, LANES), BF16)

    def scores(hh, qi, kj, slot):
        st_ref[hh, slot] = lax.dot_general(ka_ref[hh, kj * tq:(kj + 1) * tq, :],
                                           qa_ref[hh, qi * tq:(qi + 1) * tq, :],
                                           (((1,), (1,)), ((), ())),
                                           preferred_element_type=F32)

    def piece(hh, slot, ci, g, masked):
        r0, c0 = ci * FOX_CHUNK, g * LANES
        if masked and c0 + LANES - 1 < r0:
            return None
        x = st_ref[hh, slot, r0:r0 + FOX_CHUNK, c0:c0 + LANES]
        if masked and c0 < r0 + FOX_CHUNK - 1:
            key = r0 + lax.broadcasted_iota(jnp.int32, (FOX_CHUNK, LANES), 0)
            qry = c0 + lax.broadcasted_iota(jnp.int32, (FOX_CHUNK, LANES), 1)
            x = jnp.where(key <= qry, x, -jnp.inf)
        return x

    def fold(x, op):
        return op(x.reshape(FOX_CHUNK // SUBLANES, SUBLANES, LANES), axis=0)

    def softmax_pv(hh, kj, slot, masked, first):
        mx = [None] * ncg
        for ci in range(nch):
            for g in range(ncg):
                x = piece(hh, slot, ci, g, masked)
                if x is not None:
                    f = fold(x, jnp.max)
                    mx[g] = f if mx[g] is None else jnp.maximum(mx[g], f)
        m_new = jnp.concatenate([jnp.max(v, axis=0, keepdims=True) for v in mx], axis=1)
        if not first:
            m_old = m_ref[hh]
            m_new = jnp.maximum(m_old, m_new)
            alpha = jnp.exp2(m_old - m_new)
        m_ref[hh] = m_new
        ls = [None] * ncg
        for ci in range(nch):
            for g in range(ncg):
                r0, c0 = ci * FOX_CHUNK, g * LANES
                x = piece(hh, slot, ci, g, masked)
                if x is None:
                    continue
                p = jnp.exp2(x - m_new[:, c0:c0 + LANES])
                f = fold(p, jnp.sum)
                ls[g] = f if ls[g] is None else ls[g] + f
                pt_ref[hh, int(masked), r0:r0 + FOX_CHUNK, c0:c0 + LANES] = p.astype(BF16)
        l_new = jnp.concatenate([jnp.sum(v, axis=0, keepdims=True) for v in ls], axis=1)
        pv = jnp.dot(vt_ref[hh, :, kj * tq:(kj + 1) * tq], pt_ref[hh, int(masked)],
                     preferred_element_type=F32)
        if first:
            l_ref[hh] = l_new
            acc_ref[hh] = pv
        else:
            l_ref[hh] = alpha * l_ref[hh] + l_new
            acc_ref[hh] = alpha * acc_ref[hh] + pv

    for qi in range(seqlen // tq):
        rows = slice(qi * tq, (qi + 1) * tq)
        for hh in range(nh):
            scores(hh, qi, 0, 0)
        for kj in range(qi + 1):
            for hh in range(nh):
                if kj < qi:
                    scores(hh, qi, kj + 1, (kj + 1) % 2)
                softmax_pv(hh, kj, kj % 2, masked=(kj == qi), first=(kj == 0))
        for hh in range(nh):
            z = z_ref[hh, rows, :].astype(F32)
            o = (acc_ref[hh] / l_ref[hh]).T
            o_ref[hh, rows, :] = (o * (z * jax.nn.sigmoid(z))).astype(o_ref.dtype)


def _fox_attn(q, k, v, z, fparts, *, bsz, tq, heads_per_step):
    heads, m, _ = q.shape
    seqlen = m // bsz
    nterms = 3
    assert seqlen % tq == 0 and tq % FOX_CHUNK == 0 and nterms * heads <= LANES
    h_idx = jnp.arange(heads)[:, None, None]
    r_idx = jnp.arange(LANES)[None, :, None]
    c_idx = jnp.arange(2 * HEAD_DIM)[None, None, :]
    key_side = (c_idx < nterms) & (r_idx == c_idx * heads + h_idx)
    qc = c_idx - HEAD_DIM - nterms
    qry_side = (qc >= 0) & (qc < nterms) & (r_idx == qc * heads + h_idx)
    sel = (qry_side.astype(F32) - key_side.astype(F32)).astype(BF16)
    col = jnp.arange(HEAD_DIM)
    cst = jnp.zeros((SUBLANES, HEAD_DIM), F32)
    cst = cst.at[0].set(((col >= nterms) & (col < 2 * nterms)).astype(F32))
    cst = cst.at[1].set((col < nterms).astype(F32))
    nh = heads_per_step
    assert heads % nh == 0
    head_seq = pl.BlockSpec((nh, seqlen, HEAD_DIM), lambda b, h: (h, b, 0))
    return pl.pallas_call(
        functools.partial(_fox_kernel, tq=tq),
        grid=(bsz, heads // nh),
        in_specs=[head_seq, head_seq, head_seq,
                  pl.BlockSpec((seqlen, LANES), lambda b, h: (b, 0)),
                  pl.BlockSpec((nh, LANES, 2 * HEAD_DIM), lambda b, h: (h, 0, 0)),
                  pl.BlockSpec((SUBLANES, HEAD_DIM), lambda b, h: (0, 0)),
                  head_seq],
        out_specs=head_seq,
        out_shape=jax.ShapeDtypeStruct((heads, m, HEAD_DIM), BF16),
        scratch_shapes=[pltpu.VMEM((nh, seqlen, 2 * HEAD_DIM), BF16),
                        pltpu.VMEM((nh, seqlen, 2 * HEAD_DIM), BF16),
                        pltpu.VMEM((nh, HEAD_DIM, seqlen), BF16),
                        pltpu.VMEM((nh, 2, tq, tq), F32),
                        pltpu.VMEM((nh, 2, tq, tq), BF16),
                        pltpu.VMEM((nh, 1, tq), F32),
                        pltpu.VMEM((nh, 1, tq), F32),
                        pltpu.VMEM((nh, HEAD_DIM, tq), F32)],
        compiler_params=_params(("arbitrary", "arbitrary")),
        name="fox_attn",
    )(q, k, v, fparts, sel, cst, z)


def kernel(x, mem, pre_norm_g, post_norm_g, w_in_a, lam_re, lam_im, log_step, b_re, b_im,
           c_re, c_im, d_skip, w_glu, b_glu, kv_norm_g, w_kv, w_fgate, b_fgate, w_in_b,
           mem_norm_g, w_mem_kv, w_out):
    bsz, seqlen, d = x.shape
    n_mem = mem.shape[1]
    main_w = w_glu.shape[1]
    mem_w = w_out.shape[1] - main_w
    heads = main_w // HEAD_DIM
    scale = HEAD_DIM ** -0.5
    x2d = x.reshape(bsz * seqlen, d)
    mem2d = mem.reshape(bsz * n_mem, d)

    ones = jnp.ones((main_w,), F32)
    mem_scale = jnp.full((mem_w,), scale * LOG2E, F32)
    cs_a = jnp.concatenate([ones, ones, mem_scale, ones[:mem_w]])
    cs_b = jnp.concatenate([ones * (scale * LOG2E), ones, mem_scale, ones[:mem_w]])
    xa = _prenorm(x2d, pre_norm_g[0], tm=512)
    proj = functools.partial(_matmul, out_dtype=BF16, tm=1024)
    u_tm = _matmul(xa, w_in_a[0], cs_a, col0=0, n=main_w, out_dtype=F32, tm=1024, tn=main_w,
                   time_major_batches=bsz, name="in_proj_a_u")
    z_a = proj(xa, w_in_a[0], cs_a, col0=main_w, n=main_w, tn=main_w, name="in_proj_a_z")
    qz_a = proj(xa, w_in_a[0], cs_a, col0=2 * main_w, n=2 * mem_w, tn=2 * mem_w,
                slab_width=HEAD_DIM, name="in_proj_a_mem")
    kvm0 = _norm_matmul(mem2d, mem_norm_g[0], w_mem_kv[0], out_dtype=BF16,
                        tm=512, tn=512, name="mem_kv0")
    memo0 = _mem_attn(qz_a, kvm0, bsz=bsz, tq=1024, name="mem_attn0")
    s5_ops = _s5_discretise(lam_re[0], lam_im[0], log_step[0], b_re[0], b_im[0],
                            c_re[0], c_im[0])
    yg_tm = _s5(u_tm, *s5_ops, d_skip[0], bsz=bsz, n_pairs=128, blocks_per_step=2)
    h1, xkv, xb, fparts = _glu_out(
        yg_tm, z_a, memo0, x2d, w_glu[0], b_glu[0], w_out[0], post_norm_g[0],
        kv_norm_g, pre_norm_g[1], w_fgate, b_fgate, bsz=bsz, tl=512, sub=256)

    cs_kv = jnp.ones((w_kv.shape[1],), F32)
    k_sh = proj(xkv, w_kv, cs_kv, col0=0, n=main_w, tn=main_w, slab_width=HEAD_DIM,
                name="k_proj")
    v_sh = proj(xkv, w_kv, cs_kv, col0=main_w, n=main_w, tn=main_w, slab_width=HEAD_DIM,
                name="v_proj")

    q_b = proj(xb, w_in_b[0], cs_b, col0=0, n=main_w, tn=main_w, slab_width=HEAD_DIM,
               name="in_proj_b_q")
    z_b = proj(xb, w_in_b[0], cs_b, col0=main_w, n=main_w, tn=main_w, slab_width=HEAD_DIM,
               name="in_proj_b_z")
    qz_b = proj(xb, w_in_b[0], cs_b, col0=2 * main_w, n=2 * mem_w, tn=2 * mem_w,
                slab_width=HEAD_DIM, name="in_proj_b_mem")
    kvm1 = _norm_matmul(mem2d, mem_norm_g[1], w_mem_kv[1], out_dtype=BF16,
                        tm=512, tn=512, name="mem_kv1")
    memo1 = _mem_attn(qz_b, kvm1, bsz=bsz, tq=1024, name="mem_attn1")
    att = _fox_attn(q_b, k_sh, v_sh, z_b, fparts, bsz=bsz, tq=512, heads_per_step=2)
    out = _out_proj(att, memo1, h1, w_out[1], post_norm_g[1], tl=512)
    return out.reshape(bsz, seqlen, d)
```

```python
import functools
import math

import jax
import jax.numpy as jnp
from jax import lax
from jax.experimental import pallas as pl
from jax.experimental.pallas import tpu as pltpu

F32 = jnp.float32
BF16 = jnp.bfloat16

EPS = 1e-6
LOG2E = 1.4426950408889634
HEAD_DIM = 128
SSM_GROUP = 16
SSM_STATE = 64
MEM_HEADS = 4
LANES = 128
SUBLANES = 8
GROUPS_PER_BLOCK = LANES // SSM_GROUP
STATE_COLS = GROUPS_PER_BLOCK * SSM_STATE
VMEM_LIMIT = 56 * 1024 * 1024


def _params(sem, vmem=VMEM_LIMIT):
    return pltpu.CompilerParams(dimension_semantics=sem, vmem_limit_bytes=vmem)


def _norm_matmul_kernel(x_ref, g_ref, w_ref, o_ref, xn_ref):
    @pl.when(pl.program_id(1) == 0)
    def _():
        x = x_ref[...]
        ms = jnp.mean(x * x, axis=-1, keepdims=True)
        xn_ref[...] = (x * lax.rsqrt(ms + EPS) * g_ref[...]).astype(BF16)

    o_ref[...] = jnp.dot(xn_ref[...], w_ref[...].astype(BF16),
                         preferred_element_type=F32).astype(o_ref.dtype)


def _norm_matmul(x2d, g, w, *, out_dtype, tm, tn, time_major_batches=None, name):
    m, d = x2d.shape
    n = w.shape[1]
    assert m % tm == 0 and n % tn == 0
    nj = n // tn
    if time_major_batches is None:
        out_shape = jax.ShapeDtypeStruct((m, n), out_dtype)
        out_spec = pl.BlockSpec((tm, tn), lambda i, j: (i, j))
    else:
        bsz = time_major_batches
        seqlen = m // bsz
        assert seqlen % tm == 0
        per_b = seqlen // tm
        out_shape = jax.ShapeDtypeStruct((seqlen, bsz * n), out_dtype)
        out_spec = pl.BlockSpec((tm, tn), lambda i, j: (i % per_b, (i // per_b) * nj + j))
    return pl.pallas_call(
        _norm_matmul_kernel,
        grid=(m // tm, nj),
        in_specs=[pl.BlockSpec((tm, d), lambda i, j: (i, 0)),
                  pl.BlockSpec((1, d), lambda i, j: (0, 0)),
                  pl.BlockSpec((d, tn), lambda i, j: (0, j))],
        out_specs=out_spec,
        out_shape=out_shape,
        scratch_shapes=[pltpu.VMEM((tm, d), BF16)],
        compiler_params=_params(("parallel", "arbitrary")),
        name=name,
    )(x2d, g.reshape(1, d), w)


def _prenorm_kernel(x_ref, g_ref, o_ref):
    x = x_ref[...]
    ms = jnp.mean(x * x, axis=-1, keepdims=True)
    o_ref[...] = (x * lax.rsqrt(ms + EPS) * g_ref[...]).astype(o_ref.dtype)


def _prenorm(x2d, g, *, tm):
    m, d = x2d.shape
    return pl.pallas_call(
        _prenorm_kernel,
        grid=(m // tm,),
        in_specs=[pl.BlockSpec((tm, d), lambda i: (i, 0)),
                  pl.BlockSpec((1, d), lambda i: (0, 0))],
        out_specs=pl.BlockSpec((tm, d), lambda i: (i, 0)),
        out_shape=jax.ShapeDtypeStruct((m, d), BF16),
        compiler_params=_params(("parallel",)),
        name="prenorm",
    )(x2d, g.reshape(1, d).astype(F32))


def _matmul_kernel(x_ref, w_ref, cs_ref, o_ref, wb_ref):
    @pl.when(pl.program_id(1) == 0)
    def _():
        wb_ref[...] = (w_ref[...] * cs_ref[...]).astype(BF16)

    res = jnp.dot(x_ref[...], wb_ref[...], preferred_element_type=F32).astype(o_ref.dtype)
    if len(o_ref.shape) == 3:
        sw = o_ref.shape[2]
        for s in range(o_ref.shape[0]):
            o_ref[s] = res[:, s * sw:(s + 1) * sw]
    else:
        o_ref[...] = res


def _matmul(xn, w, colscale, *, col0, n, out_dtype, tm, tn, time_major_batches=None,
            slab_width=None, name):
    m, d = xn.shape
    assert m % tm == 0 and n % tn == 0 and col0 % tn == 0
    nj = n // tn
    j0 = col0 // tn
    if slab_width is not None:
        assert time_major_batches is None and tn % slab_width == 0
        per_tile = tn // slab_width
        out_shape = jax.ShapeDtypeStruct((n // slab_width, m, slab_width), out_dtype)
        out_spec = pl.BlockSpec((per_tile, tm, slab_width), lambda j, i: (j, i, 0))
    elif time_major_batches is None:
        out_shape = jax.ShapeDtypeStruct((m, n), out_dtype)
        out_spec = pl.BlockSpec((tm, tn), lambda j, i: (i, j))
    else:
        bsz = time_major_batches
        seqlen = m // bsz
        assert seqlen % tm == 0
        per_b = seqlen // tm
        out_shape = jax.ShapeDtypeStruct((seqlen, bsz * n), out_dtype)
        out_spec = pl.BlockSpec((tm, tn), lambda j, i: (i % per_b, (i // per_b) * nj + j))
    return pl.pallas_call(
        _matmul_kernel,
        grid=(nj, m // tm),
        in_specs=[pl.BlockSpec((tm, d), lambda j, i: (i, 0)),
                  pl.BlockSpec((d, tn), lambda j, i: (0, j0 + j), pipeline_mode=pl.Buffered(1)),
                  pl.BlockSpec((1, tn), lambda j, i: (0, j0 + j))],
        out_specs=out_spec,
        out_shape=out_shape,
        scratch_shapes=[pltpu.VMEM((d, tn), BF16)],
        compiler_params=_params(("arbitrary", "arbitrary")),
        name=name,
    )(xn, w, colscale.reshape(1, -1).astype(F32))


def _split2(x):
    hi = x.astype(BF16)
    return hi, (x - hi.astype(F32)).astype(BF16)


def _split3(x):
    hi = x.astype(BF16)
    r = x - hi.astype(F32)
    mid = r.astype(BF16)
    return hi, mid, (r - mid.astype(F32)).astype(BF16)


def _fgate_block(xn, wcat_ref, b_ref, carry_ref, heads):
    tl = xn.shape[0]
    xh, xl = _split2(xn)
    both = jnp.dot(xh, wcat_ref[...], preferred_element_type=F32)
    logit = (both[:, :LANES] + both[:, LANES:]
             + jnp.dot(xl, wcat_ref[:, :LANES], preferred_element_type=F32)) + b_ref[...]
    logf = jnp.minimum(logit, 0.0) - jnp.log(1.0 + jnp.exp(-jnp.abs(logit)))
    t_idx = lax.broadcasted_iota(jnp.int32, (tl, tl), 0)
    s_idx = lax.broadcasted_iota(jnp.int32, (tl, tl), 1)
    tril = (s_idx <= t_idx).astype(BF16)
    csum = carry_ref[...]
    for part in _split3(logf):
        csum = csum + jnp.dot(tril, part, preferred_element_type=F32)
    carry_ref[...] = csum[tl - 1:tl, :]
    lane = lax.broadcasted_iota(jnp.int32, (tl, LANES), 1)
    packed = jnp.zeros((tl, LANES), F32)
    for n, part in reversed(list(enumerate(_split3(csum * LOG2E)))):
        shifted = part.astype(F32) if n == 0 else pltpu.roll(part.astype(F32), n * heads, 1)
        packed = jnp.where(lane < (n + 1) * heads, shifted, packed)
    return packed.astype(BF16)


def _mem_attn_kernel(q_ref, zm_ref, kv_ref, o_ref):
    width = MEM_HEADS * HEAD_DIM
    heads = range(MEM_HEADS)
    cols = [slice(h * HEAD_DIM, (h + 1) * HEAD_DIM) for h in heads]
    st = [lax.dot_general(kv_ref[:, cols[h]], q_ref[h], (((1,), (1,)), ((), ())),
                          preferred_element_type=F32) for h in heads]
    vt = [kv_ref[:, width + h * HEAD_DIM:width + (h + 1) * HEAD_DIM].astype(F32).T.astype(BF16)
          for h in heads]
    p = [jnp.exp2(st[h] - jnp.max(st[h], axis=0, keepdims=True)) for h in heads]
    inv = [1.0 / jnp.sum(p[h], axis=0, keepdims=True) for h in heads]
    ot = [jnp.dot(vt[h], p[h].astype(BF16), preferred_element_type=F32) * inv[h] for h in heads]
    for h in heads:
        zm = zm_ref[h].astype(F32)
        o_ref[:, cols[h]] = (ot[h].T * (zm * jax.nn.sigmoid(zm))).astype(o_ref.dtype)


def _mem_attn(qz, kvm, *, bsz, tq, name):
    width = MEM_HEADS * HEAD_DIM
    m = qz.shape[1]
    seqlen = m // bsz
    n_mem = kvm.shape[0] // bsz
    per_b = seqlen // tq
    spec = lambda blk: pl.BlockSpec((MEM_HEADS, tq, HEAD_DIM),
                                    lambda b, i: (blk, b * per_b + i, 0))
    return pl.pallas_call(
        _mem_attn_kernel,
        grid=(bsz, per_b),
        in_specs=[spec(0), spec(1),
                  pl.BlockSpec((n_mem, 2 * width), lambda b, i: (b, 0))],
        out_specs=pl.BlockSpec((tq, width), lambda b, i: (b * per_b + i, 0)),
        out_shape=jax.ShapeDtypeStruct((m, width), BF16),
        compiler_params=_params(("parallel", "arbitrary")),
        name=name,
    )(qz, qz, kvm)


def _s5_kernel(u_ref, winc_ref, woutc_ref, wtc_ref, ar_ref, ai_ref, d_ref, y_ref,
               win_ref, wout_ref, wt_ref, bu_ref, xs_ref, st_ref, *, n_pairs):
    nblk = win_ref.shape[0]

    @pl.when(pl.program_id(1) == 0)
    def _():
        st_ref[...] = jnp.zeros_like(st_ref)
        for k in range(nblk):
            win_ref[k] = _s5_expand(winc_ref[k], SSM_STATE)
            wout_ref[k] = _s5_expand(woutc_ref[k], SSM_STATE)
            wt_ref[k] = _s5_expand(wtc_ref[k], SSM_GROUP)

    rows = n_pairs * SUBLANES
    u0, u1, ucat, a2, state = [], [], [], [], []
    for k in range(nblk):
        upair = u_ref[:, :, k * LANES:(k + 1) * LANES].reshape(n_pairs, 2, SUBLANES, LANES)
        u0.append(upair[:, 0].reshape(rows, LANES))
        u1.append(upair[:, 1].reshape(rows, LANES))
        ucat.append(jnp.concatenate([u0[k], u1[k]], axis=1).astype(BF16))
        bu_ref[k] = jnp.dot(ucat[k], win_ref[k], preferred_element_type=F32)
        a2.append((jnp.broadcast_to(ar_ref[k], (SUBLANES, STATE_COLS)),
                   jnp.broadcast_to(ai_ref[k], (SUBLANES, STATE_COLS))))
        state.append((st_ref[k, 0], st_ref[k, 1]))

    held = [None] * nblk
    for c in range(n_pairs):
        for k in range(nblk):
            xr, xi = state[k]
            cur = jnp.concatenate([xr, xi], axis=1)
            if c % 2 == 1:
                xs_ref[k, (c - 1) * SUBLANES:(c + 1) * SUBLANES, :] = jnp.concatenate(
                    [held[k], cur], axis=0).astype(BF16)
            held[k] = cur
            bur = bu_ref[k, c * SUBLANES:(c + 1) * SUBLANES, 0:STATE_COLS]
            bui = bu_ref[k, c * SUBLANES:(c + 1) * SUBLANES, STATE_COLS:2 * STATE_COLS]
            ar, ai = a2[k]
            state[k] = (ar * xr - ai * xi + bur, ar * xi + ai * xr + bui)

    for k in range(nblk):
        st_ref[k, 0] = state[k][0]
        st_ref[k, 1] = state[k][1]
        y = (lax.dot_general(xs_ref[k], wout_ref[k], (((1,), (1,)), ((), ())),
                             preferred_element_type=F32)
             + jnp.dot(ucat[k], wt_ref[k], preferred_element_type=F32))
        d = d_ref[:, k * LANES:(k + 1) * LANES]
        y0 = jax.nn.gelu(y[:, :LANES] + d * u0[k]).reshape(n_pairs, SUBLANES, LANES)
        y1 = jax.nn.gelu(y[:, LANES:] + d * u1[k]).reshape(n_pairs, SUBLANES, LANES)
        y_ref[:, :, k * LANES:(k + 1) * LANES] = jnp.stack([y0, y1], axis=1).reshape(
            2 * n_pairs, SUBLANES, LANES)


def _s5_discretise(lam_re, lam_im, log_step, b_re, b_im, c_re, c_im):
    groups = lam_re.shape[0]
    nblk = groups // GROUPS_PER_BLOCK
    lr = lam_re.astype(F32)
    li = lam_im.astype(F32)
    dt = jnp.exp(log_step.astype(F32))[:, None]
    mag = jnp.exp(lr * dt)
    ar = mag * jnp.cos(li * dt)
    ai = mag * jnp.sin(li * dt)
    den = lr * lr + li * li
    cr = ((ar - 1.0) * lr + ai * li) / den
    ci = (ai * lr - (ar - 1.0) * li) / den
    br = b_re.astype(F32)
    bi = b_im.astype(F32)
    bbar_re = cr[..., None] * br - ci[..., None] * bi
    bbar_im = cr[..., None] * bi + ci[..., None] * br
    a2r = ar * ar - ai * ai
    a2i = 2.0 * ar * ai
    ab_re = ar[..., None] * bbar_re - ai[..., None] * bbar_im
    ab_im = ar[..., None] * bbar_im + ai[..., None] * bbar_re
    cre = c_re.astype(F32)
    cim = c_im.astype(F32)
    ca_re = cre * ar[:, None, :] - cim * ai[:, None, :]
    ca_im = cre * ai[:, None, :] + cim * ar[:, None, :]
    ca2_re = cre * a2r[:, None, :] - cim * a2i[:, None, :]
    ca2_im = cre * a2i[:, None, :] + cim * a2r[:, None, :]
    k0 = jnp.einsum('ghp,gpk->ghk', cre, bbar_re) - jnp.einsum('ghp,gpk->ghk', cim, bbar_im)
    k1 = jnp.einsum('ghp,gpk->ghk', cre, ab_re) - jnp.einsum('ghp,gpk->ghk', cim, ab_im)

    def per_block(x):
        return x.reshape((nblk, GROUPS_PER_BLOCK) + x.shape[1:])

    b_in = jnp.stack([jnp.stack([per_block(ab_re), per_block(ab_im)]),
                      jnp.stack([per_block(bbar_re), per_block(bbar_im)])])
    win_c = b_in.transpose(2, 0, 5, 1, 3, 4).reshape(nblk, 2 * SSM_GROUP, 2 * STATE_COLS)
    c_out = jnp.stack([jnp.stack([per_block(ca_re), per_block(ca2_re)]),
                       jnp.stack([-per_block(ca_im), -per_block(ca2_im)])])
    wout_c = c_out.transpose(2, 1, 4, 0, 3, 5).reshape(nblk, 2 * SSM_GROUP, 2 * STATE_COLS)
    k0b, k1b = per_block(k0), per_block(k1)
    k_mix = jnp.stack([jnp.stack([k0b, k1b]),
                       jnp.stack([jnp.zeros_like(k0b), k0b])])
    wt_c = k_mix.transpose(2, 0, 5, 1, 3, 4).reshape(nblk, 2 * SSM_GROUP, 2 * LANES)
    return (win_c, wout_c, wt_c,
            a2r.reshape(nblk, 1, STATE_COLS), a2i.reshape(nblk, 1, STATE_COLS))


def _s5_expand(compact, cols_per_group):
    width = compact.shape[1]
    col_group = (lax.broadcasted_iota(jnp.int32, (SSM_GROUP, width), 1)
                 // cols_per_group) % GROUPS_PER_BLOCK
    pieces = []
    for i in range(2):
        rows = compact[i * SSM_GROUP:(i + 1) * SSM_GROUP, :]
        for g in range(GROUPS_PER_BLOCK):
            pieces.append(jnp.where(col_group == g, rows, 0.0).astype(BF16))
    return jnp.concatenate(pieces, axis=0)


def _s5(u_tm, win, wout, wt, a2r, a2i, d_skip, *, bsz, n_pairs, blocks_per_step):
    seqlen = u_tm.shape[0]
    width = u_tm.shape[1] // bsz
    assert bsz == SUBLANES and seqlen % (2 * n_pairs) == 0 and width % LANES == 0
    nblk = width // LANES
    assert nblk % blocks_per_step == 0 and n_pairs % 2 == 0
    nb = blocks_per_step
    rows = n_pairs * SUBLANES
    u3 = u_tm.reshape(seqlen, bsz, width)
    blk = pl.BlockSpec((2 * n_pairs, bsz, nb * LANES), lambda j, c: (c, 0, j))
    per_blk = lambda shape: pl.BlockSpec((nb,) + shape, lambda j, c: (j, 0, 0))
    out = pl.pallas_call(
        functools.partial(_s5_kernel, n_pairs=n_pairs),
        grid=(nblk // nb, seqlen // (2 * n_pairs)),
        in_specs=[blk,
                  per_blk((2 * SSM_GROUP, 2 * STATE_COLS)),
                  per_blk((2 * SSM_GROUP, 2 * STATE_COLS)),
                  per_blk((2 * SSM_GROUP, 2 * LANES)),
                  per_blk((1, STATE_COLS)),
                  per_blk((1, STATE_COLS)),
                  pl.BlockSpec((1, nb * LANES), lambda j, c: (0, j))],
        out_specs=blk,
        out_shape=jax.ShapeDtypeStruct((seqlen, bsz, width), F32),
        scratch_shapes=[pltpu.VMEM((nb, 2 * LANES, 2 * STATE_COLS), BF16),
                        pltpu.VMEM((nb, 2 * LANES, 2 * STATE_COLS), BF16),
                        pltpu.VMEM((nb, 2 * LANES, 2 * LANES), BF16),
                        pltpu.VMEM((nb, rows, 2 * STATE_COLS), F32),
                        pltpu.VMEM((nb, rows, 2 * STATE_COLS), BF16),
                        pltpu.VMEM((nb, 2, SUBLANES, STATE_COLS), F32)],
        compiler_params=_params(("arbitrary", "arbitrary")),
        name="s5_scan",
    )(u3, win, wout, wt, a2r, a2i, d_skip.reshape(1, width).astype(F32))
    return out.reshape(seqlen, bsz * width)


def _post(o, g_ref, h_ref, out_ref):
    ms = jnp.mean(o * o, axis=-1, keepdims=True)
    out_ref[...] = h_ref[...] + o * lax.rsqrt(ms + EPS) * g_ref[...]


def _glu_out_kernel(yg_ref, z_ref, memo_ref, h_ref, wglu_ref, bglu_ref,
                    wmain_ref, wmem_ref, g_ref, gkv_ref, gnext_ref, wf_ref, bf_ref,
                    out_ref, xkv_ref, xnext_ref, fp_ref, carry_ref, *, heads, sub):
    @pl.when(pl.program_id(1) == 0)
    def _():
        carry_ref[...] = jnp.zeros_like(carry_ref)

    for s in range(out_ref.shape[0] // sub):
        rows = slice(s * sub, (s + 1) * sub)
        yg = yg_ref[rows, :]
        t = jnp.dot(yg.astype(BF16), wglu_ref[...], preferred_element_type=F32) + bglu_ref[...]
        z = z_ref[rows, :].astype(F32)
        main = yg * jax.nn.sigmoid(t) * (z * jax.nn.sigmoid(z))
        o = jnp.dot(main.astype(BF16), wmain_ref[...], preferred_element_type=F32)
        o = o + jnp.dot(memo_ref[rows, :], wmem_ref[...], preferred_element_type=F32)
        ms = jnp.mean(o * o, axis=-1, keepdims=True)
        h1 = h_ref[rows, :] + o * lax.rsqrt(ms + EPS) * g_ref[...]
        out_ref[rows, :] = h1
        r = h1 * lax.rsqrt(jnp.mean(h1 * h1, axis=-1, keepdims=True) + EPS)
        xnext_ref[rows, :] = (r * gnext_ref[...]).astype(BF16)
        xkv = r * gkv_ref[...]
        xkv_ref[rows, :] = xkv.astype(BF16)
        fp_ref[rows, :] = _fgate_block(xkv, wf_ref, bf_ref, carry_ref, heads)


def _out_kernel(main_ref, memo_ref, h_ref, wmain_ref, wmem_ref, g_ref, out_ref):
    main = jnp.concatenate([main_ref[s] for s in range(main_ref.shape[0])], axis=1)
    o = jnp.dot(main, wmain_ref[...], preferred_element_type=F32)
    o = o + jnp.dot(memo_ref[...], wmem_ref[...], preferred_element_type=F32)
    _post(o, g_ref, h_ref, out_ref)


def _resident(shape):
    return pl.BlockSpec(shape, lambda *_: (0,) * len(shape), pipeline_mode=pl.Buffered(1))


def _glu_out(yg_tm, rest, memo, h2d, w_glu, b_glu, w_out, g, g_kv, g_next, w_fgate, b_fgate,
             *, bsz, tl, sub):
    assert tl % sub == 0
    m, d = h2d.shape
    seqlen = m // bsz
    main_w = w_glu.shape[0]
    mem_w = w_out.shape[0] - main_w
    heads = w_fgate.shape[1]
    assert 3 * heads <= LANES and seqlen % tl == 0
    wpad = jnp.zeros((d, LANES), F32).at[:, :heads].set(w_fgate.astype(F32))
    wcat = jnp.concatenate(_split2(wpad), axis=1)
    bpad = jnp.zeros((1, LANES), F32).at[0, :heads].set(b_fgate.astype(F32))
    per_b = seqlen // tl
    row = lambda b, i: (b * per_b + i, 0)
    vec = lambda v: v.reshape(1, -1).astype(F32)
    return pl.pallas_call(
        functools.partial(_glu_out_kernel, heads=heads, sub=sub),
        grid=(bsz, per_b),
        in_specs=[pl.BlockSpec((tl, main_w), lambda b, i: (i, b)),
                  pl.BlockSpec((tl, main_w), row),
                  pl.BlockSpec((tl, mem_w), row),
                  pl.BlockSpec((tl, d), row),
                  _resident((main_w, main_w)),
                  _resident((1, main_w)),
                  _resident((main_w, d)),
                  _resident((mem_w, d)),
                  _resident((1, d)),
                  _resident((1, d)),
                  _resident((1, d)),
                  _resident((d, 2 * LANES)),
                  _resident((1, LANES))],
        out_specs=[pl.BlockSpec((tl, d), row),
                   pl.BlockSpec((tl, d), row),
                   pl.BlockSpec((tl, d), row),
                   pl.BlockSpec((tl, LANES), row)],
        out_shape=[jax.ShapeDtypeStruct((m, d), F32),
                   jax.ShapeDtypeStruct((m, d), BF16),
                   jax.ShapeDtypeStruct((m, d), BF16),
                   jax.ShapeDtypeStruct((m, LANES), BF16)],
        scratch_shapes=[pltpu.VMEM((1, LANES), F32)],
        compiler_params=_params(("arbitrary", "arbitrary")),
        name="glu_out_proj",
    )(yg_tm, rest, memo, h2d, w_glu.astype(BF16), vec(b_glu),
      w_out[:main_w].astype(BF16), w_out[main_w:].astype(BF16), vec(g), vec(g_kv),
      vec(g_next), wcat, bpad)


def _out_proj(main, memo, h2d, w_out, g, *, tl):
    m, d = h2d.shape
    heads, _, dh = main.shape
    main_w = heads * dh
    mem_w = memo.shape[1]
    row = lambda i: (i, 0)
    return pl.pallas_call(
        _out_kernel,
        grid=(m // tl,),
        in_specs=[pl.BlockSpec((heads, tl, dh), lambda i: (0, i, 0)),
                  pl.BlockSpec((tl, mem_w), row),
                  pl.BlockSpec((tl, d), row),
                  _resident((main_w, d)),
                  _resident((mem_w, d)),
                  _resident((1, d))],
        out_specs=pl.BlockSpec((tl, d), row),
        out_shape=jax.ShapeDtypeStruct((m, d), F32),
        compiler_params=_params(("parallel",)),
        name="out_proj",
    )(main, memo, h2d, w_out[:main_w].astype(BF16), w_out[main_w:].astype(BF16),
      g.reshape(1, d).astype(F32))


FOX_CHUNK = 64


def _fox_kernel(q_ref, k_ref, v_ref, fp_ref, sel_ref, cst_ref, z_ref, o_ref,
                ka_ref, qa_ref, vt_ref, st_ref, pt_ref, m_ref, l_ref, acc_ref, *, tq):
    nh, seqlen, _ = q_ref.shape
    ncg = tq // LANES
    nch = tq // FOX_CHUNK
    for hh in range(nh):
        ext = jnp.dot(fp_ref[...], sel_ref[hh], preferred_element_type=F32)
        ka_ref[hh, :, :HEAD_DIM] = k_ref[hh]
        ka_ref[hh, :, HEAD_DIM:] = (ext[:, :HEAD_DIM] + cst_ref[0:1, :]).astype(BF16)
        qa_ref[hh, :, :HEAD_DIM] = q_ref[hh]
        qa_ref[hh, :, HEAD_DIM:] = (ext[:, HEAD_DIM:] + cst_ref[1:2, :]).astype(BF16)
        vt_ref[hh] = v_ref[hh].astype(F32).T.astype(BF16)
        for ci in range(nch):
            for g in range(ncg):
                if g * LANES + LANES - 1 < ci * FOX_CHUNK:
                    pt_ref[hh, 1, ci * FOX_CHUNK:(ci + 1) * FOX_CHUNK,
                           g * LANES:(g + 1) * LANES] = jnp.zeros((FOX_CHUNK, LANES), BF16)

    def scores(hh, qi, kj, slot):
        st_ref[hh, slot] = lax.dot_general(ka_ref[hh, kj * tq:(kj + 1) * tq, :],
                                           qa_ref[hh, qi * tq:(qi + 1) * tq, :],
                                           (((1,), (1,)), ((), ())),
                                           preferred_element_type=F32)

    def piece(hh, slot, ci, g, masked):
        r0, c0 = ci * FOX_CHUNK, g * LANES
        if masked and c0 + LANES - 1 < r0:
            return None
        x = st_ref[hh, slot, r0:r0 + FOX_CHUNK, c0:c0 + LANES]
        if masked and c0 < r0 + FOX_CHUNK - 1:
            key = r0 + lax.broadcasted_iota(jnp.int32, (FOX_CHUNK, LANES), 0)
            qry = c0 + lax.broadcasted_iota(jnp.int32, (FOX_CHUNK, LANES), 1)
            x = jnp.where(key <= qry, x, -jnp.inf)
        return x

    def fold(x, op):
        return op(x.reshape(FOX_CHUNK // SUBLANES, SUBLANES, LANES), axis=0)

    def softmax_pv(hh, kj, slot, masked, first):
        mx = [None] * ncg
        for ci in range(nch):
            for g in range(ncg):
                x = piece(hh, slot, ci, g, masked)
                if x is not None:
                    f = fold(x, jnp.max)
                    mx[g] = f if mx[g] is None else jnp.maximum(mx[g], f)
        m_new = jnp.concatenate([jnp.max(v, axis=0, keepdims=True) for v in mx], axis=1)
        if not first:
            m_old = m_ref[hh]
            m_new = jnp.maximum(m_old, m_new)
            alpha = jnp.exp2(m_old - m_new)
        m_ref[hh] = m_new
        ls = [None] * ncg
        for ci in range(nch):
            for g in range(ncg):
                r0, c0 = ci * FOX_CHUNK, g * LANES
                x = piece(hh, slot, ci, g, masked)
                if x is None:
                    continue
                p = jnp.exp2(x - m_new[:, c0:c0 + LANES])
                f = fold(p, jnp.sum)
                ls[g] = f if ls[g] is None else ls[g] + f
                pt_ref[hh, int(masked), r0:r0 + FOX_CHUNK, c0:c0 + LANES] = p.astype(BF16)
        l_new = jnp.concatenate([jnp.sum(v, axis=0, keepdims=True) for v in ls], axis=1)
        pv = jnp.dot(vt_ref[hh, :, kj * tq:(kj + 1) * tq], pt_ref[hh, int(masked)],
                     preferred_element_type=F32)
        if first:
            l_ref[hh] = l_new
            acc_ref[hh] = pv
        else:
            l_ref[hh] = alpha * l_ref[hh] + l_new
            acc_ref[hh] = alpha * acc_ref[hh] + pv

    for qi in range(seqlen // tq):
        rows = slice(qi * tq, (qi + 1) * tq)
        for hh in range(nh):
            scores(hh, qi, 0, 0)
        for kj in range(qi + 1):
            for hh in range(nh):
                if kj < qi:
                    scores(hh, qi, kj + 1, (kj + 1) % 2)
                softmax_pv(hh, kj, kj % 2, masked=(kj == qi), first=(kj == 0))
        for hh in range(nh):
            z = z_ref[hh, rows, :].astype(F32)
            o = (acc_ref[hh] / l_ref[hh]).T
            o_ref[hh, rows, :] = (o * (z * jax.nn.sigmoid(z))).astype(o_ref.dtype)


def _fox_attn(q, k, v, z, fparts, *, bsz, tq, heads_per_step):
    heads, m, _ = q.shape
    seqlen = m // bsz
    nterms = 3
    assert seqlen % tq == 0 and tq % FOX_CHUNK == 0 and nterms * heads <= LANES
    h_idx = jnp.arange(heads)[:, None, None]
    r_idx = jnp.arange(LANES)[None, :, None]
    c_idx = jnp.arange(2 * HEAD_DIM)[None, None, :]
    key_side = (c_idx < nterms) & (r_idx == c_idx * heads + h_idx)
    qc = c_idx - HEAD_DIM - nterms
    qry_side = (qc >= 0) & (qc < nterms) & (r_idx == qc * heads + h_idx)
    sel = (qry_side.astype(F32) - key_side.astype(F32)).astype(BF16)
    col = jnp.arange(HEAD_DIM)
    cst = jnp.zeros((SUBLANES, HEAD_DIM), F32)
    cst = cst.at[0].set(((col >= nterms) & (col < 2 * nterms)).astype(F32))
    cst = cst.at[1].set((col < nterms).astype(F32))
    nh = heads_per_step
    assert heads % nh == 0
    head_seq = pl.BlockSpec((nh, seqlen, HEAD_DIM), lambda b, h: (h, b, 0))
    return pl.pallas_call(
        functools.partial(_fox_kernel, tq=tq),
        grid=(bsz, heads // nh),
        in_specs=[head_seq, head_seq, head_seq,
                  pl.BlockSpec((seqlen, LANES), lambda b, h: (b, 0)),
                  pl.BlockSpec((nh, LANES, 2 * HEAD_DIM), lambda b, h: (h, 0, 0)),
                  pl.BlockSpec((SUBLANES, HEAD_DIM), lambda b, h: (0, 0)),
                  head_seq],
        out_specs=head_seq,
        out_shape=jax.ShapeDtypeStruct((heads, m, HEAD_DIM), BF16),
        scratch_shapes=[pltpu.VMEM((nh, seqlen, 2 * HEAD_DIM), BF16),
                        pltpu.VMEM((nh, seqlen, 2 * HEAD_DIM), BF16),
                        pltpu.VMEM((nh, HEAD_DIM, seqlen), BF16),
                        pltpu.VMEM((nh, 2, tq, tq), F32),
                        pltpu.VMEM((nh, 2, tq, tq), BF16),
                        pltpu.VMEM((nh, 1, tq), F32),
                        pltpu.VMEM((nh, 1, tq), F32),
                        pltpu.VMEM((nh, HEAD_DIM, tq), F32)],
        compiler_params=_params(("arbitrary", "arbitrary")),
        name="fox_attn",
    )(q, k, v, fparts, sel, cst, z)


def kernel(x, mem, pre_norm_g, post_norm_g, w_in_a, lam_re, lam_im, log_step, b_re, b_im,
           c_re, c_im, d_skip, w_glu, b_glu, kv_norm_g, w_kv, w_fgate, b_fgate, w_in_b,
           mem_norm_g, w_mem_kv, w_out):
    bsz, seqlen, d = x.shape
    n_mem = mem.shape[1]
    main_w = w_glu.shape[1]
    mem_w = w_out.shape[1] - main_w
    heads = main_w // HEAD_DIM
    scale = HEAD_DIM ** -0.5
    x2d = x.reshape(bsz * seqlen, d)
    mem2d = mem.reshape(bsz * n_mem, d)

    ones = jnp.ones((main_w,), F32)
    mem_scale = jnp.full((mem_w,), scale * LOG2E, F32)
    cs_a = jnp.concatenate([ones, ones, mem_scale, ones[:mem_w]])
    cs_b = jnp.concatenate([ones * (scale * LOG2E), ones, mem_scale, ones[:mem_w]])
    xa = _prenorm(x2d, pre_norm_g[0], tm=512)
    proj = functools.partial(_matmul, out_dtype=BF16, tm=1024)
    u_tm = _matmul(xa, w_in_a[0], cs_a, col0=0, n=main_w, out_dtype=F32, tm=1024, tn=main_w,
                   time_major_batches=bsz, name="in_proj_a_u")
    z_a = proj(xa, w_in_a[0], cs_a, col0=main_w, n=main_w, tn=main_w, name="in_proj_a_z")
    qz_a = proj(xa, w_in_a[0], cs_a, col0=2 * main_w, n=2 * mem_w, tn=2 * mem_w,
                slab_width=HEAD_DIM, name="in_proj_a_mem")
    kvm0 = _norm_matmul(mem2d, mem_norm_g[0], w_mem_kv[0], out_dtype=BF16,
                        tm=512, tn=512, name="mem_kv0")
    memo0 = _mem_attn(qz_a, kvm0, bsz=bsz, tq=1024, name="mem_attn0")
    s5_ops = _s5_discretise(lam_re[0], lam_im[0], log_step[0], b_re[0], b_im[0],
                            c_re[0], c_im[0])
    yg_tm = _s5(u_tm, *s5_ops, d_skip[0], bsz=bsz, n_pairs=128, blocks_per_step=2)
    h1, xkv, xb, fparts = _glu_out(
        yg_tm, z_a, memo0, x2d, w_glu[0], b_glu[0], w_out[0], post_norm_g[0],
        kv_norm_g, pre_norm_g[1], w_fgate, b_fgate, bsz=bsz, tl=512, sub=256)

    cs_kv = jnp.ones((w_kv.shape[1],), F32)
    k_sh = proj(xkv, w_kv, cs_kv, col0=0, n=main_w, tn=main_w, slab_width=HEAD_DIM,
                name="k_proj")
    v_sh = proj(xkv, w_kv, cs_kv, col0=main_w, n=main_w, tn=main_w, slab_width=HEAD_DIM,
                name="v_proj")

    q_b = proj(xb, w_in_b[0], cs_b, col0=0, n=main_w, tn=main_w, slab_width=HEAD_DIM,
               name="in_proj_b_q")
    z_b = proj(xb, w_in_b[0], cs_b, col0=main_w, n=main_w, tn=main_w, slab_width=HEAD_DIM,
               name="in_proj_b_z")
    qz_b = proj(xb, w_in_b[0], cs_b, col0=2 * main_w, n=2 * mem_w, tn=2 * mem_w,
                slab_width=HEAD_DIM, name="in_proj_b_mem")
    kvm1 = _norm_matmul(mem2d, mem_norm_g[1], w_mem_kv[1], out_dtype=BF16,
                        tm=512, tn=512, name="mem_kv1")
    memo1 = _mem_attn(qz_b, kvm1, bsz=bsz, tq=1024, name="mem_attn1")
    att = _fox_attn(q_b, k_sh, v_sh, z_b, fparts, bsz=bsz, tq=512, heads_per_step=3)
    out = _out_proj(att, memo1, h1, w_out[1], post_norm_g[1], tl=512)
    return out.reshape(bsz, seqlen, d)
```

```python
import functools
import math

import jax
import jax.numpy as jnp
from jax import lax
from jax.experimental import pallas as pl
from jax.experimental.pallas import tpu as pltpu

F32 = jnp.float32
BF16 = jnp.bfloat16

EPS = 1e-6
LOG2E = 1.4426950408889634
HEAD_DIM = 128
SSM_GROUP = 16
SSM_STATE = 64
MEM_HEADS = 4
LANES = 128
SUBLANES = 8
GROUPS_PER_BLOCK = LANES // SSM_GROUP
STATE_COLS = GROUPS_PER_BLOCK * SSM_STATE
VMEM_LIMIT = 56 * 1024 * 1024


def _params(sem, vmem=VMEM_LIMIT):
    return pltpu.CompilerParams(dimension_semantics=sem, vmem_limit_bytes=vmem)


def _norm_matmul_kernel(x_ref, g_ref, w_ref, o_ref, xn_ref):
    @pl.when(pl.program_id(1) == 0)
    def _():
        x = x_ref[...]
        ms = jnp.mean(x * x, axis=-1, keepdims=True)
        xn_ref[...] = (x * lax.rsqrt(ms + EPS) * g_ref[...]).astype(BF16)

    o_ref[...] = jnp.dot(xn_ref[...], w_ref[...].astype(BF16),
                         preferred_element_type=F32).astype(o_ref.dtype)


def _norm_matmul(x2d, g, w, *, out_dtype, tm, tn, time_major_batches=None, name):
    m, d = x2d.shape
    n = w.shape[1]
    assert m % tm == 0 and n % tn == 0
    nj = n // tn
    if time_major_batches is None:
        out_shape = jax.ShapeDtypeStruct((m, n), out_dtype)
        out_spec = pl.BlockSpec((tm, tn), lambda i, j: (i, j))
    else:
        bsz = time_major_batches
        seqlen = m // bsz
        assert seqlen % tm == 0
        per_b = seqlen // tm
        out_shape = jax.ShapeDtypeStruct((seqlen, bsz * n), out_dtype)
        out_spec = pl.BlockSpec((tm, tn), lambda i, j: (i % per_b, (i // per_b) * nj + j))
    return pl.pallas_call(
        _norm_matmul_kernel,
        grid=(m // tm, nj),
        in_specs=[pl.BlockSpec((tm, d), lambda i, j: (i, 0)),
                  pl.BlockSpec((1, d), lambda i, j: (0, 0)),
                  pl.BlockSpec((d, tn), lambda i, j: (0, j))],
        out_specs=out_spec,
        out_shape=out_shape,
        scratch_shapes=[pltpu.VMEM((tm, d), BF16)],
        compiler_params=_params(("parallel", "arbitrary")),
        name=name,
    )(x2d, g.reshape(1, d), w)


def _prenorm_kernel(x_ref, g_ref, o_ref):
    x = x_ref[...]
    ms = jnp.mean(x * x, axis=-1, keepdims=True)
    o_ref[...] = (x * lax.rsqrt(ms + EPS) * g_ref[...]).astype(o_ref.dtype)


def _prenorm(x2d, g, *, tm):
    m, d = x2d.shape
    return pl.pallas_call(
        _prenorm_kernel,
        grid=(m // tm,),
        in_specs=[pl.BlockSpec((tm, d), lambda i: (i, 0)),
                  pl.BlockSpec((1, d), lambda i: (0, 0))],
        out_specs=pl.BlockSpec((tm, d), lambda i: (i, 0)),
        out_shape=jax.ShapeDtypeStruct((m, d), BF16),
        compiler_params=_params(("parallel",)),
        name="prenorm",
    )(x2d, g.reshape(1, d).astype(F32))


def _store_product(res, o_ref):
    if len(o_ref.shape) == 3:
        sw = o_ref.shape[2]
        for s in range(o_ref.shape[0]):
            o_ref[s] = res[:, s * sw:(s + 1) * sw]
    else:
        o_ref[...] = res


def _matmul_kernel(x_ref, w_ref, cs_ref, o_ref, wb_ref):
    @pl.when(pl.program_id(1) == 0)
    def _():
        wb_ref[...] = (w_ref[...] * cs_ref[...]).astype(BF16)

    res = jnp.dot(x_ref[...], wb_ref[...], preferred_element_type=F32).astype(o_ref.dtype)
    _store_product(res, o_ref)


def _norm_then_matmul_kernel(x_ref, g_ref, w_ref, cs_ref, o_ref, xn_ref, wb_ref):
    @pl.when(pl.program_id(1) == 0)
    def _():
        wb_ref[...] = (w_ref[...] * cs_ref[...]).astype(BF16)

    x = x_ref[...]
    ms = jnp.mean(x * x, axis=-1, keepdims=True)
    xn = (x * lax.rsqrt(ms + EPS) * g_ref[...]).astype(BF16)
    xn_ref[...] = xn
    res = jnp.dot(xn, wb_ref[...], preferred_element_type=F32).astype(o_ref.dtype)
    _store_product(res, o_ref)


def _matmul(xn, w, colscale, *, col0, n, out_dtype, tm, tn, time_major_batches=None,
            slab_width=None, norm_gain=None, name):
    m, d = xn.shape
    assert m % tm == 0 and n % tn == 0 and col0 % tn == 0
    nj = n // tn
    j0 = col0 // tn
    if slab_width is not None:
        assert time_major_batches is None and tn % slab_width == 0
        per_tile = tn // slab_width
        out_shape = jax.ShapeDtypeStruct((n // slab_width, m, slab_width), out_dtype)
        out_spec = pl.BlockSpec((per_tile, tm, slab_width), lambda j, i: (j, i, 0))
    elif time_major_batches is None:
        out_shape = jax.ShapeDtypeStruct((m, n), out_dtype)
        out_spec = pl.BlockSpec((tm, tn), lambda j, i: (i, j))
    else:
        bsz = time_major_batches
        seqlen = m // bsz
        assert seqlen % tm == 0
        per_b = seqlen // tm
        out_shape = jax.ShapeDtypeStruct((seqlen, bsz * n), out_dtype)
        out_spec = pl.BlockSpec((tm, tn), lambda j, i: (i % per_b, (i // per_b) * nj + j))
    x_spec = pl.BlockSpec((tm, d), lambda j, i: (i, 0))
    w_specs = [pl.BlockSpec((d, tn), lambda j, i: (0, j0 + j), pipeline_mode=pl.Buffered(1)),
               pl.BlockSpec((1, tn), lambda j, i: (0, j0 + j))]
    common = dict(grid=(nj, m // tm), scratch_shapes=[pltpu.VMEM((d, tn), BF16)],
                  compiler_params=_params(("arbitrary", "arbitrary")), name=name)
    cs = colscale.reshape(1, -1).astype(F32)
    if norm_gain is None:
        return pl.pallas_call(
            _matmul_kernel, in_specs=[x_spec] + w_specs, out_specs=out_spec,
            out_shape=out_shape, **common)(xn, w, cs)
    assert nj == 1
    return pl.pallas_call(
        _norm_then_matmul_kernel,
        in_specs=[x_spec, pl.BlockSpec((1, d), lambda j, i: (0, 0))] + w_specs,
        out_specs=[out_spec, x_spec],
        out_shape=[out_shape, jax.ShapeDtypeStruct((m, d), BF16)],
        **common)(xn, norm_gain.reshape(1, d).astype(F32), w, cs)


def _split2(x):
    hi = x.astype(BF16)
    return hi, (x - hi.astype(F32)).astype(BF16)


def _split3(x):
    hi = x.astype(BF16)
    r = x - hi.astype(F32)
    mid = r.astype(BF16)
    return hi, mid, (r - mid.astype(F32)).astype(BF16)


def _fgate_block(xn, wcat_ref, b_ref, carry_ref, heads):
    tl = xn.shape[0]
    xh, xl = _split2(xn)
    both = jnp.dot(xh, wcat_ref[...], preferred_element_type=F32)
    logit = (both[:, :LANES] + both[:, LANES:]
             + jnp.dot(xl, wcat_ref[:, :LANES], preferred_element_type=F32)) + b_ref[...]
    logf = jnp.minimum(logit, 0.0) - jnp.log(1.0 + jnp.exp(-jnp.abs(logit)))
    t_idx = lax.broadcasted_iota(jnp.int32, (tl, tl), 0)
    s_idx = lax.broadcasted_iota(jnp.int32, (tl, tl), 1)
    tril = (s_idx <= t_idx).astype(BF16)
    csum = carry_ref[...]
    for part in _split3(logf):
        csum = csum + jnp.dot(tril, part, preferred_element_type=F32)
    carry_ref[...] = csum[tl - 1:tl, :]
    lane = lax.broadcasted_iota(jnp.int32, (tl, LANES), 1)
    packed = jnp.zeros((tl, LANES), F32)
    for n, part in reversed(list(enumerate(_split3(csum * LOG2E)))):
        shifted = part.astype(F32) if n == 0 else pltpu.roll(part.astype(F32), n * heads, 1)
        packed = jnp.where(lane < (n + 1) * heads, shifted, packed)
    return packed.astype(BF16)


def _mem_attn_kernel(q_ref, zm_ref, kv_ref, o_ref):
    width = MEM_HEADS * HEAD_DIM
    heads = range(MEM_HEADS)
    cols = [slice(h * HEAD_DIM, (h + 1) * HEAD_DIM) for h in heads]
    st = [lax.dot_general(kv_ref[:, cols[h]], q_ref[h], (((1,), (1,)), ((), ())),
                          preferred_element_type=F32) for h in heads]
    vt = [kv_ref[:, width + h * HEAD_DIM:width + (h + 1) * HEAD_DIM].astype(F32).T.astype(BF16)
          for h in heads]
    p = [jnp.exp2(st[h] - jnp.max(st[h], axis=0, keepdims=True)) for h in heads]
    inv = [1.0 / jnp.sum(p[h], axis=0, keepdims=True) for h in heads]
    ot = [jnp.dot(vt[h], p[h].astype(BF16), preferred_element_type=F32) * inv[h] for h in heads]
    for h in heads:
        zm = zm_ref[h].astype(F32)
        o_ref[:, cols[h]] = (ot[h].T * (zm * jax.nn.sigmoid(zm))).astype(o_ref.dtype)


def _mem_attn(qz, kvm, *, bsz, tq, name):
    width = MEM_HEADS * HEAD_DIM
    m = qz.shape[1]
    seqlen = m // bsz
    n_mem = kvm.shape[0] // bsz
    per_b = seqlen // tq
    spec = lambda blk: pl.BlockSpec((MEM_HEADS, tq, HEAD_DIM),
                                    lambda b, i: (blk, b * per_b + i, 0))
    return pl.pallas_call(
        _mem_attn_kernel,
        grid=(bsz, per_b),
        in_specs=[spec(0), spec(1),
                  pl.BlockSpec((n_mem, 2 * width), lambda b, i: (b, 0))],
        out_specs=pl.BlockSpec((tq, width), lambda b, i: (b * per_b + i, 0)),
        out_shape=jax.ShapeDtypeStruct((m, width), BF16),
        compiler_params=_params(("parallel", "arbitrary")),
        name=name,
    )(qz, qz, kvm)


def _s5_kernel(u_ref, winc_ref, woutc_ref, wtc_ref, ar_ref, ai_ref, d_ref, y_ref,
               win_ref, wout_ref, wt_ref, bu_ref, xs_ref, st_ref, *, n_pairs):
    nblk = win_ref.shape[0]

    @pl.when(pl.program_id(1) == 0)
    def _():
        st_ref[...] = jnp.zeros_like(st_ref)
        for k in range(nblk):
            win_ref[k] = _s5_expand(winc_ref[k], SSM_STATE)
            wout_ref[k] = _s5_expand(woutc_ref[k], SSM_STATE)
            wt_ref[k] = _s5_expand(wtc_ref[k], SSM_GROUP)

    rows = n_pairs * SUBLANES
    u0, u1, ucat, a2, state = [], [], [], [], []
    for k in range(nblk):
        upair = u_ref[:, :, k * LANES:(k + 1) * LANES].reshape(n_pairs, 2, SUBLANES, LANES)
        u0.append(upair[:, 0].reshape(rows, LANES))
        u1.append(upair[:, 1].reshape(rows, LANES))
        ucat.append(jnp.concatenate([u0[k], u1[k]], axis=1).astype(BF16))
        bu_ref[k] = jnp.dot(ucat[k], win_ref[k], preferred_element_type=F32)
        a2.append((jnp.broadcast_to(ar_ref[k], (SUBLANES, STATE_COLS)),
                   jnp.broadcast_to(ai_ref[k], (SUBLANES, STATE_COLS))))
        state.append((st_ref[k, 0], st_ref[k, 1]))

    held = [None] * nblk
    for c in range(n_pairs):
        for k in range(nblk):
            xr, xi = state[k]
            cur = jnp.concatenate([xr, xi], axis=1)
            if c % 2 == 1:
                xs_ref[k, (c - 1) * SUBLANES:(c + 1) * SUBLANES, :] = jnp.concatenate(
                    [held[k], cur], axis=0).astype(BF16)
            held[k] = cur
            bur = bu_ref[k, c * SUBLANES:(c + 1) * SUBLANES, 0:STATE_COLS]
            bui = bu_ref[k, c * SUBLANES:(c + 1) * SUBLANES, STATE_COLS:2 * STATE_COLS]
            ar, ai = a2[k]
            state[k] = (ar * xr - ai * xi + bur, ar * xi + ai * xr + bui)

    for k in range(nblk):
        st_ref[k, 0] = state[k][0]
        st_ref[k, 1] = state[k][1]
        y = (lax.dot_general(xs_ref[k], wout_ref[k], (((1,), (1,)), ((), ())),
                             preferred_element_type=F32)
             + jnp.dot(ucat[k], wt_ref[k], preferred_element_type=F32))
        d = d_ref[:, k * LANES:(k + 1) * LANES]
        y0 = jax.nn.gelu(y[:, :LANES] + d * u0[k]).reshape(n_pairs, SUBLANES, LANES)
        y1 = jax.nn.gelu(y[:, LANES:] + d * u1[k]).reshape(n_pairs, SUBLANES, LANES)
        y_ref[:, :, k * LANES:(k + 1) * LANES] = jnp.stack([y0, y1], axis=1).reshape(
            2 * n_pairs, SUBLANES, LANES)


def _s5_discretise(lam_re, lam_im, log_step, b_re, b_im, c_re, c_im):
    groups = lam_re.shape[0]
    nblk = groups // GROUPS_PER_BLOCK
    lr = lam_re.astype(F32)
    li = lam_im.astype(F32)
    dt = jnp.exp(log_step.astype(F32))[:, None]
    mag = jnp.exp(lr * dt)
    ar = mag * jnp.cos(li * dt)
    ai = mag * jnp.sin(li * dt)
    den = lr * lr + li * li
    cr = ((ar - 1.0) * lr + ai * li) / den
    ci = (ai * lr - (ar - 1.0) * li) / den
    br = b_re.astype(F32)
    bi = b_im.astype(F32)
    bbar_re = cr[..., None] * br - ci[..., None] * bi
    bbar_im = cr[..., None] * bi + ci[..., None] * br
    a2r = ar * ar - ai * ai
    a2i = 2.0 * ar * ai
    ab_re = ar[..., None] * bbar_re - ai[..., None] * bbar_im
    ab_im = ar[..., None] * bbar_im + ai[..., None] * bbar_re
    cre = c_re.astype(F32)
    cim = c_im.astype(F32)
    ca_re = cre * ar[:, None, :] - cim * ai[:, None, :]
    ca_im = cre * ai[:, None, :] + cim * ar[:, None, :]
    ca2_re = cre * a2r[:, None, :] - cim * a2i[:, None, :]
    ca2_im = cre * a2i[:, None, :] + cim * a2r[:, None, :]
    k0 = jnp.einsum('ghp,gpk->ghk', cre, bbar_re) - jnp.einsum('ghp,gpk->ghk', cim, bbar_im)
    k1 = jnp.einsum('ghp,gpk->ghk', cre, ab_re) - jnp.einsum('ghp,gpk->ghk', cim, ab_im)

    def per_block(x):
        return x.reshape((nblk, GROUPS_PER_BLOCK) + x.shape[1:])

    b_in = jnp.stack([jnp.stack([per_block(ab_re), per_block(ab_im)]),
                      jnp.stack([per_block(bbar_re), per_block(bbar_im)])])
    win_c = b_in.transpose(2, 0, 5, 1, 3, 4).reshape(nblk, 2 * SSM_GROUP, 2 * STATE_COLS)
    c_out = jnp.stack([jnp.stack([per_block(ca_re), per_block(ca2_re)]),
                       jnp.stack([-per_block(ca_im), -per_block(ca2_im)])])
    wout_c = c_out.transpose(2, 1, 4, 0, 3, 5).reshape(nblk, 2 * SSM_GROUP, 2 * STATE_COLS)
    k0b, k1b = per_block(k0), per_block(k1)
    k_mix = jnp.stack([jnp.stack([k0b, k1b]),
                       jnp.stack([jnp.zeros_like(k0b), k0b])])
    wt_c = k_mix.transpose(2, 0, 5, 1, 3, 4).reshape(nblk, 2 * SSM_GROUP, 2 * LANES)
    return (win_c, wout_c, wt_c,
            a2r.reshape(nblk, 1, STATE_COLS), a2i.reshape(nblk, 1, STATE_COLS))


def _s5_expand(compact, cols_per_group):
    width = compact.shape[1]
    col_group = (lax.broadcasted_iota(jnp.int32, (SSM_GROUP, width), 1)
                 // cols_per_group) % GROUPS_PER_BLOCK
    pieces = []
    for i in range(2):
        rows = compact[i * SSM_GROUP:(i + 1) * SSM_GROUP, :]
        for g in range(GROUPS_PER_BLOCK):
            pieces.append(jnp.where(col_group == g, rows, 0.0).astype(BF16))
    return jnp.concatenate(pieces, axis=0)


def _s5(u_tm, win, wout, wt, a2r, a2i, d_skip, *, bsz, n_pairs, blocks_per_step):
    seqlen = u_tm.shape[0]
    width = u_tm.shape[1] // bsz
    assert bsz == SUBLANES and seqlen % (2 * n_pairs) == 0 and width % LANES == 0
    nblk = width // LANES
    assert nblk % blocks_per_step == 0 and n_pairs % 2 == 0
    nb = blocks_per_step
    rows = n_pairs * SUBLANES
    u3 = u_tm.reshape(seqlen, bsz, width)
    blk = pl.BlockSpec((2 * n_pairs, bsz, nb * LANES), lambda j, c: (c, 0, j))
    per_blk = lambda shape: pl.BlockSpec((nb,) + shape, lambda j, c: (j, 0, 0))
    out = pl.pallas_call(
        functools.partial(_s5_kernel, n_pairs=n_pairs),
        grid=(nblk // nb, seqlen // (2 * n_pairs)),
        in_specs=[blk,
                  per_blk((2 * SSM_GROUP, 2 * STATE_COLS)),
                  per_blk((2 * SSM_GROUP, 2 * STATE_COLS)),
                  per_blk((2 * SSM_GROUP, 2 * LANES)),
                  per_blk((1, STATE_COLS)),
                  per_blk((1, STATE_COLS)),
                  pl.BlockSpec((1, nb * LANES), lambda j, c: (0, j))],
        out_specs=blk,
        out_shape=jax.ShapeDtypeStruct((seqlen, bsz, width), F32),
        scratch_shapes=[pltpu.VMEM((nb, 2 * LANES, 2 * STATE_COLS), BF16),
                        pltpu.VMEM((nb, 2 * LANES, 2 * STATE_COLS), BF16),
                        pltpu.VMEM((nb, 2 * LANES, 2 * LANES), BF16),
                        pltpu.VMEM((nb, rows, 2 * STATE_COLS), F32),
                        pltpu.VMEM((nb, rows, 2 * STATE_COLS), BF16),
                        pltpu.VMEM((nb, 2, SUBLANES, STATE_COLS), F32)],
        compiler_params=_params(("arbitrary", "arbitrary")),
        name="s5_scan",
    )(u3, win, wout, wt, a2r, a2i, d_skip.reshape(1, width).astype(F32))
    return out.reshape(seqlen, bsz * width)


def _post(o, g_ref, h_ref, out_ref):
    ms = jnp.mean(o * o, axis=-1, keepdims=True)
    out_ref[...] = h_ref[...] + o * lax.rsqrt(ms + EPS) * g_ref[...]


def _glu_out_kernel(yg_ref, z_ref, memo_ref, h_ref, wglu_ref, bglu_ref,
                    wmain_ref, wmem_ref, g_ref, gkv_ref, gnext_ref, wf_ref, bf_ref,
                    out_ref, xkv_ref, xnext_ref, fp_ref, carry_ref, *, heads, sub):
    @pl.when(pl.program_id(1) == 0)
    def _():
        carry_ref[...] = jnp.zeros_like(carry_ref)

    for s in range(out_ref.shape[0] // sub):
        rows = slice(s * sub, (s + 1) * sub)
        yg = yg_ref[rows, :]
        t = jnp.dot(yg.astype(BF16), wglu_ref[...], preferred_element_type=F32) + bglu_ref[...]
        z = z_ref[rows, :].astype(F32)
        main = yg * jax.nn.sigmoid(t) * (z * jax.nn.sigmoid(z))
        o = jnp.dot(main.astype(BF16), wmain_ref[...], preferred_element_type=F32)
        o = o + jnp.dot(memo_ref[rows, :], wmem_ref[...], preferred_element_type=F32)
        ms = jnp.mean(o * o, axis=-1, keepdims=True)
        h1 = h_ref[rows, :] + o * lax.rsqrt(ms + EPS) * g_ref[...]
        out_ref[rows, :] = h1
        r = h1 * lax.rsqrt(jnp.mean(h1 * h1, axis=-1, keepdims=True) + EPS)
        xnext_ref[rows, :] = (r * gnext_ref[...]).astype(BF16)
        xkv = r * gkv_ref[...]
        xkv_ref[rows, :] = xkv.astype(BF16)
        fp_ref[rows, :] = _fgate_block(xkv, wf_ref, bf_ref, carry_ref, heads)


def _out_kernel(main_ref, memo_ref, h_ref, wmain_ref, wmem_ref, g_ref, out_ref):
    main = jnp.concatenate([main_ref[s] for s in range(main_ref.shape[0])], axis=1)
    o = jnp.dot(main, wmain_ref[...], preferred_element_type=F32)
    o = o + jnp.dot(memo_ref[...], wmem_ref[...], preferred_element_type=F32)
    _post(o, g_ref, h_ref, out_ref)


def _resident(shape):
    return pl.BlockSpec(shape, lambda *_: (0,) * len(shape), pipeline_mode=pl.Buffered(1))


def _glu_out(yg_tm, rest, memo, h2d, w_glu, b_glu, w_out, g, g_kv, g_next, w_fgate, b_fgate,
             *, bsz, tl, sub):
    assert tl % sub == 0
    m, d = h2d.shape
    seqlen = m // bsz
    main_w = w_glu.shape[0]
    mem_w = w_out.shape[0] - main_w
    heads = w_fgate.shape[1]
    assert 3 * heads <= LANES and seqlen % tl == 0
    wpad = jnp.zeros((d, LANES), F32).at[:, :heads].set(w_fgate.astype(F32))
    wcat = jnp.concatenate(_split2(wpad), axis=1)
    bpad = jnp.zeros((1, LANES), F32).at[0, :heads].set(b_fgate.astype(F32))
    per_b = seqlen // tl
    row = lambda b, i: (b * per_b + i, 0)
    vec = lambda v: v.reshape(1, -1).astype(F32)
    return pl.pallas_call(
        functools.partial(_glu_out_kernel, heads=heads, sub=sub),
        grid=(bsz, per_b),
        in_specs=[pl.BlockSpec((tl, main_w), lambda b, i: (i, b)),
                  pl.BlockSpec((tl, main_w), row),
                  pl.BlockSpec((tl, mem_w), row),
                  pl.BlockSpec((tl, d), row),
                  _resident((main_w, main_w)),
                  _resident((1, main_w)),
                  _resident((main_w, d)),
                  _resident((mem_w, d)),
                  _resident((1, d)),
                  _resident((1, d)),
                  _resident((1, d)),
                  _resident((d, 2 * LANES)),
                  _resident((1, LANES))],
        out_specs=[pl.BlockSpec((tl, d), row),
                   pl.BlockSpec((tl, d), row),
                   pl.BlockSpec((tl, d), row),
                   pl.BlockSpec((tl, LANES), row)],
        out_shape=[jax.ShapeDtypeStruct((m, d), F32),
                   jax.ShapeDtypeStruct((m, d), BF16),
                   jax.ShapeDtypeStruct((m, d), BF16),
                   jax.ShapeDtypeStruct((m, LANES), BF16)],
        scratch_shapes=[pltpu.VMEM((1, LANES), F32)],
        compiler_params=_params(("arbitrary", "arbitrary")),
        name="glu_out_proj",
    )(yg_tm, rest, memo, h2d, w_glu.astype(BF16), vec(b_glu),
      w_out[:main_w].astype(BF16), w_out[main_w:].astype(BF16), vec(g), vec(g_kv),
      vec(g_next), wcat, bpad)


def _out_proj(main, memo, h2d, w_out, g, *, tl):
    m, d = h2d.shape
    heads, _, dh = main.shape
    main_w = heads * dh
    mem_w = memo.shape[1]
    row = lambda i: (i, 0)
    return pl.pallas_call(
        _out_kernel,
        grid=(m // tl,),
        in_specs=[pl.BlockSpec((heads, tl, dh), lambda i: (0, i, 0)),
                  pl.BlockSpec((tl, mem_w), row),
                  pl.BlockSpec((tl, d), row),
                  _resident((main_w, d)),
                  _resident((mem_w, d)),
                  _resident((1, d))],
        out_specs=pl.BlockSpec((tl, d), row),
        out_shape=jax.ShapeDtypeStruct((m, d), F32),
        compiler_params=_params(("parallel",)),
        name="out_proj",
    )(main, memo, h2d, w_out[:main_w].astype(BF16), w_out[main_w:].astype(BF16),
      g.reshape(1, d).astype(F32))


FOX_CHUNK = 64


def _fox_kernel(q_ref, k_ref, v_ref, fp_ref, sel_ref, cst_ref, z_ref, o_ref,
                ka_ref, qa_ref, vt_ref, st_ref, pt_ref, m_ref, l_ref, acc_ref, *, tq):
    nh, seqlen, _ = q_ref.shape
    ncg = tq // LANES
    nch = tq // FOX_CHUNK
    for hh in range(nh):
        ext = jnp.dot(fp_ref[...], sel_ref[hh], preferred_element_type=F32)
        ka_ref[hh, :, :HEAD_DIM] = k_ref[hh]
        ka_ref[hh, :, HEAD_DIM:] = (ext[:, :HEAD_DIM] + cst_ref[0:1, :]).astype(BF16)
        qa_ref[hh, :, :HEAD_DIM] = q_ref[hh]
        qa_ref[hh, :, HEAD_DIM:] = (ext[:, HEAD_DIM:] + cst_ref[1:2, :]).astype(BF16)
        vt_ref[hh] = v_ref[hh].astype(F32).T.astype(BF16)
        for ci in range(nch):
            for g in range(ncg):
                if g * LANES + LANES - 1 < ci * FOX_CHUNK:
                    pt_ref[hh, 1, ci * FOX_CHUNK:(ci + 1) * FOX_CHUNK,
                           g * LANES:(g + 1) * LANES] = jnp.zeros((FOX_CHUNK, LANES), BF16)

    def scores(hh, qi, kj, slot):
        st_ref[hh, slot] = lax.dot_general(ka_ref[hh, kj * tq:(kj + 1) * tq, :],
                                           qa_ref[hh, qi * tq:(qi + 1) * tq, :],
                                           (((1,), (1,)), ((), ())),
                                           preferred_element_type=F32)

    def piece(hh, slot, ci, g, masked):
        r0, c0 = ci * FOX_CHUNK, g * LANES
        if masked and c0 + LANES - 1 < r0:
            return None
        x = st_ref[hh, slot, r0:r0 + FOX_CHUNK, c0:c0 + LANES]
        if masked and c0 < r0 + FOX_CHUNK - 1:
            key = r0 + lax.broadcasted_iota(jnp.int32, (FOX_CHUNK, LANES), 0)
            qry = c0 + lax.broadcasted_iota(jnp.int32, (FOX_CHUNK, LANES), 1)
            x = jnp.where(key <= qry, x, -jnp.inf)
        return x

    def fold(x, op):
        return op(x.reshape(FOX_CHUNK // SUBLANES, SUBLANES, LANES), axis=0)

    def softmax_pv(hh, kj, slot, masked, first):
        mx = [None] * ncg
        for ci in range(nch):
            for g in range(ncg):
                x = piece(hh, slot, ci, g, masked)
                if x is not None:
                    f = fold(x, jnp.max)
                    mx[g] = f if mx[g] is None else jnp.maximum(mx[g], f)
        m_new = jnp.concatenate([jnp.max(v, axis=0, keepdims=True) for v in mx], axis=1)
        if not first:
            m_old = m_ref[hh]
            m_new = jnp.maximum(m_old, m_new)
            alpha = jnp.exp2(m_old - m_new)
        m_ref[hh] = m_new
        ls = [None] * ncg
        for ci in range(nch):
            for g in range(ncg):
                r0, c0 = ci * FOX_CHUNK, g * LANES
                x = piece(hh, slot, ci, g, masked)
                if x is None:
                    continue
                p = jnp.exp2(x - m_new[:, c0:c0 + LANES])
                f = fold(p, jnp.sum)
                ls[g] = f if ls[g] is None else ls[g] + f
                pt_ref[hh, int(masked), r0:r0 + FOX_CHUNK, c0:c0 + LANES] = p.astype(BF16)
        l_new = jnp.concatenate([jnp.sum(v, axis=0, keepdims=True) for v in ls], axis=1)
        pv = jnp.dot(vt_ref[hh, :, kj * tq:(kj + 1) * tq], pt_ref[hh, int(masked)],
                     preferred_element_type=F32)
        if first:
            l_ref[hh] = l_new
            acc_ref[hh] = pv
        else:
            l_ref[hh] = alpha * l_ref[hh] + l_new
            acc_ref[hh] = alpha * acc_ref[hh] + pv

    for qi in range(seqlen // tq):
        rows = slice(qi * tq, (qi + 1) * tq)
        for hh in range(nh):
            scores(hh, qi, 0, 0)
        for kj in range(qi + 1):
            for hh in range(nh):
                if kj < qi:
                    scores(hh, qi, kj + 1, (kj + 1) % 2)
                softmax_pv(hh, kj, kj % 2, masked=(kj == qi), first=(kj == 0))
        for hh in range(nh):
            z = z_ref[hh, rows, :].astype(F32)
            o = (acc_ref[hh] / l_ref[hh]).T
            o_ref[hh, rows, :] = (o * (z * jax.nn.sigmoid(z))).astype(o_ref.dtype)


def _fox_attn(q, k, v, z, fparts, *, bsz, tq, heads_per_step):
    heads, m, _ = q.shape
    seqlen = m // bsz
    nterms = 3
    assert seqlen % tq == 0 and tq % FOX_CHUNK == 0 and nterms * heads <= LANES
    h_idx = jnp.arange(heads)[:, None, None]
    r_idx = jnp.arange(LANES)[None, :, None]
    c_idx = jnp.arange(2 * HEAD_DIM)[None, None, :]
    key_side = (c_idx < nterms) & (r_idx == c_idx * heads + h_idx)
    qc = c_idx - HEAD_DIM - nterms
    qry_side = (qc >= 0) & (qc < nterms) & (r_idx == qc * heads + h_idx)
    sel = (qry_side.astype(F32) - key_side.astype(F32)).astype(BF16)
    col = jnp.arange(HEAD_DIM)
    cst = jnp.zeros((SUBLANES, HEAD_DIM), F32)
    cst = cst.at[0].set(((col >= nterms) & (col < 2 * nterms)).astype(F32))
    cst = cst.at[1].set((col < nterms).astype(F32))
    nh = heads_per_step
    assert heads % nh == 0
    head_seq = pl.BlockSpec((nh, seqlen, HEAD_DIM), lambda b, h: (h, b, 0))
    return pl.pallas_call(
        functools.partial(_fox_kernel, tq=tq),
        grid=(bsz, heads // nh),
        in_specs=[head_seq, head_seq, head_seq,
                  pl.BlockSpec((seqlen, LANES), lambda b, h: (b, 0)),
                  pl.BlockSpec((nh, LANES, 2 * HEAD_DIM), lambda b, h: (h, 0, 0)),
                  pl.BlockSpec((SUBLANES, HEAD_DIM), lambda b, h: (0, 0)),
                  head_seq],
        out_specs=head_seq,
        out_shape=jax.ShapeDtypeStruct((heads, m, HEAD_DIM), BF16),
        scratch_shapes=[pltpu.VMEM((nh, seqlen, 2 * HEAD_DIM), BF16),
                        pltpu.VMEM((nh, seqlen, 2 * HEAD_DIM), BF16),
                        pltpu.VMEM((nh, HEAD_DIM, seqlen), BF16),
                        pltpu.VMEM((nh, 2, tq, tq), F32),
                        pltpu.VMEM((nh, 2, tq, tq), BF16),
                        pltpu.VMEM((nh, 1, tq), F32),
                        pltpu.VMEM((nh, 1, tq), F32),
                        pltpu.VMEM((nh, HEAD_DIM, tq), F32)],
        compiler_params=_params(("arbitrary", "arbitrary")),
        name="fox_attn",
    )(q, k, v, fparts, sel, cst, z)


def kernel(x, mem, pre_norm_g, post_norm_g, w_in_a, lam_re, lam_im, log_step, b_re, b_im,
           c_re, c_im, d_skip, w_glu, b_glu, kv_norm_g, w_kv, w_fgate, b_fgate, w_in_b,
           mem_norm_g, w_mem_kv, w_out):
    bsz, seqlen, d = x.shape
    n_mem = mem.shape[1]
    main_w = w_glu.shape[1]
    mem_w = w_out.shape[1] - main_w
    heads = main_w // HEAD_DIM
    scale = HEAD_DIM ** -0.5
    x2d = x.reshape(bsz * seqlen, d)
    mem2d = mem.reshape(bsz * n_mem, d)

    ones = jnp.ones((main_w,), F32)
    mem_scale = jnp.full((mem_w,), scale * LOG2E, F32)
    cs_a = jnp.concatenate([ones, ones, mem_scale, ones[:mem_w]])
    cs_b = jnp.concatenate([ones * (scale * LOG2E), ones, mem_scale, ones[:mem_w]])
    proj = functools.partial(_matmul, out_dtype=BF16, tm=1024)
    u_tm, xa = _matmul(x2d, w_in_a[0], cs_a, col0=0, n=main_w, out_dtype=F32, tm=512,
                       tn=main_w, time_major_batches=bsz, norm_gain=pre_norm_g[0],
                       name="in_proj_a_u")
    z_a = proj(xa, w_in_a[0], cs_a, col0=main_w, n=main_w, tn=main_w, name="in_proj_a_z")
    qz_a = proj(xa, w_in_a[0], cs_a, col0=2 * main_w, n=2 * mem_w, tn=2 * mem_w,
                slab_width=HEAD_DIM, name="in_proj_a_mem")
    kvm0 = _norm_matmul(mem2d, mem_norm_g[0], w_mem_kv[0], out_dtype=BF16,
                        tm=512, tn=512, name="mem_kv0")
    memo0 = _mem_attn(qz_a, kvm0, bsz=bsz, tq=1024, name="mem_attn0")
    s5_ops = _s5_discretise(lam_re[0], lam_im[0], log_step[0], b_re[0], b_im[0],
                            c_re[0], c_im[0])
    yg_tm = _s5(u_tm, *s5_ops, d_skip[0], bsz=bsz, n_pairs=128, blocks_per_step=2)
    h1, xkv, xb, fparts = _glu_out(
        yg_tm, z_a, memo0, x2d, w_glu[0], b_glu[0], w_out[0], post_norm_g[0],
        kv_norm_g, pre_norm_g[1], w_fgate, b_fgate, bsz=bsz, tl=512, sub=256)

    cs_kv = jnp.ones((w_kv.shape[1],), F32)
    k_sh = proj(xkv, w_kv, cs_kv, col0=0, n=main_w, tn=main_w, slab_width=HEAD_DIM,
                name="k_proj")
    v_sh = proj(xkv, w_kv, cs_kv, col0=main_w, n=main_w, tn=main_w, slab_width=HEAD_DIM,
                name="v_proj")

    q_b = proj(xb, w_in_b[0], cs_b, col0=0, n=main_w, tn=main_w, slab_width=HEAD_DIM,
               name="in_proj_b_q")
    z_b = proj(xb, w_in_b[0], cs_b, col0=main_w, n=main_w, tn=main_w, slab_width=HEAD_DIM,
               name="in_proj_b_z")
    qz_b = proj(xb, w_in_b[0], cs_b, col0=2 * main_w, n=2 * mem_w, tn=2 * mem_w,
                slab_width=HEAD_DIM, name="in_proj_b_mem")
    kvm1 = _norm_matmul(mem2d, mem_norm_g[1], w_mem_kv[1], out_dtype=BF16,
                        tm=512, tn=512, name="mem_kv1")
    memo1 = _mem_attn(qz_b, kvm1, bsz=bsz, tq=1024, name="mem_attn1")
    att = _fox_attn(q_b, k_sh, v_sh, z_b, fparts, bsz=bsz, tq=512, heads_per_step=3)
    out = _out_proj(att, memo1, h1, w_out[1], post_norm_g[1], tl=512)
    return out.reshape(bsz, seqlen, d)
```

```python
import functools

import jax
import jax.numpy as jnp
from jax import lax
from jax.experimental import pallas as pl
from jax.experimental.pallas import tpu as pltpu

F32 = jnp.float32
BF16 = jnp.bfloat16

EPS = 1e-6
LOG2E = 1.4426950408889634
HEAD_DIM = 128
SSM_GROUP = 16
SSM_STATE = 64
MEM_HEADS = 4
LANES = 128
SUBLANES = 8
GROUPS_PER_BLOCK = LANES // SSM_GROUP
STATE_COLS = GROUPS_PER_BLOCK * SSM_STATE
VMEM_LIMIT = 56 * 1024 * 1024


def _params(sem, vmem=VMEM_LIMIT):
    return pltpu.CompilerParams(dimension_semantics=sem, vmem_limit_bytes=vmem)


def _norm_matmul_kernel(x_ref, g_ref, w_ref, o_ref, xn_ref):
    @pl.when(pl.program_id(1) == 0)
    def _():
        x = x_ref[...]
        ms = jnp.mean(x * x, axis=-1, keepdims=True)
        xn_ref[...] = (x * lax.rsqrt(ms + EPS) * g_ref[...]).astype(BF16)

    o_ref[...] = jnp.dot(xn_ref[...], w_ref[...].astype(BF16),
                         preferred_element_type=F32).astype(o_ref.dtype)


def _norm_matmul(x2d, g, w, *, out_dtype, tm, tn, name):
    m, d = x2d.shape
    n = w.shape[1]
    assert m % tm == 0 and n % tn == 0
    return pl.pallas_call(
        _norm_matmul_kernel,
        grid=(m // tm, n // tn),
        in_specs=[pl.BlockSpec((tm, d), lambda i, j: (i, 0)),
                  pl.BlockSpec((1, d), lambda i, j: (0, 0)),
                  pl.BlockSpec((d, tn), lambda i, j: (0, j))],
        out_specs=pl.BlockSpec((tm, tn), lambda i, j: (i, j)),
        out_shape=jax.ShapeDtypeStruct((m, n), out_dtype),
        scratch_shapes=[pltpu.VMEM((tm, d), BF16)],
        compiler_params=_params(("parallel", "arbitrary")),
        name=name,
    )(x2d, g.reshape(1, d).astype(F32), w)


def _store_product(res, o_ref):
    if len(o_ref.shape) == 3:
        sw = o_ref.shape[2]
        for s in range(o_ref.shape[0]):
            o_ref[s] = res[:, s * sw:(s + 1) * sw]
    else:
        o_ref[...] = res


def _matmul_kernel(x_ref, w_ref, cs_ref, o_ref, wb_ref):
    @pl.when(pl.program_id(1) == 0)
    def _():
        wb_ref[...] = (w_ref[...] * cs_ref[...]).astype(BF16)

    res = jnp.dot(x_ref[...], wb_ref[...], preferred_element_type=F32).astype(o_ref.dtype)
    _store_product(res, o_ref)


def _norm_then_matmul_kernel(x_ref, g_ref, w_ref, cs_ref, o_ref, xn_ref, wb_ref):
    @pl.when(pl.program_id(1) == 0)
    def _():
        wb_ref[...] = (w_ref[...] * cs_ref[...]).astype(BF16)

    x = x_ref[...]
    ms = jnp.mean(x * x, axis=-1, keepdims=True)
    xn = (x * lax.rsqrt(ms + EPS) * g_ref[...]).astype(BF16)
    xn_ref[...] = xn
    res = jnp.dot(xn, wb_ref[...], preferred_element_type=F32).astype(o_ref.dtype)
    _store_product(res, o_ref)


def _matmul(xn, w, colscale, *, col0, n, out_dtype, tm, tn, time_major_batches=None,
            slab_width=None, norm_gain=None, name):
    m, d = xn.shape
    assert m % tm == 0 and n % tn == 0 and col0 % tn == 0
    nj = n // tn
    j0 = col0 // tn
    if slab_width is not None:
        assert time_major_batches is None and tn % slab_width == 0
        per_tile = tn // slab_width
        out_shape = jax.ShapeDtypeStruct((n // slab_width, m, slab_width), out_dtype)
        out_spec = pl.BlockSpec((per_tile, tm, slab_width), lambda j, i: (j, i, 0))
    elif time_major_batches is None:
        out_shape = jax.ShapeDtypeStruct((m, n), out_dtype)
        out_spec = pl.BlockSpec((tm, tn), lambda j, i: (i, j))
    else:
        bsz = time_major_batches
        seqlen = m // bsz
        assert seqlen % tm == 0
        per_b = seqlen // tm
        out_shape = jax.ShapeDtypeStruct((seqlen, bsz * n), out_dtype)
        out_spec = pl.BlockSpec((tm, tn), lambda j, i: (i % per_b, (i // per_b) * nj + j))
    x_spec = pl.BlockSpec((tm, d), lambda j, i: (i, 0))
    w_specs = [pl.BlockSpec((d, tn), lambda j, i: (0, j0 + j), pipeline_mode=pl.Buffered(1)),
               pl.BlockSpec((1, tn), lambda j, i: (0, j0 + j))]
    common = dict(grid=(nj, m // tm), scratch_shapes=[pltpu.VMEM((d, tn), BF16)],
                  compiler_params=_params(("arbitrary", "arbitrary")), name=name)
    cs = colscale.reshape(1, -1).astype(F32)
    if norm_gain is None:
        return pl.pallas_call(
            _matmul_kernel, in_specs=[x_spec] + w_specs, out_specs=out_spec,
            out_shape=out_shape, **common)(xn, w, cs)
    assert nj == 1
    return pl.pallas_call(
        _norm_then_matmul_kernel,
        in_specs=[x_spec, pl.BlockSpec((1, d), lambda j, i: (0, 0))] + w_specs,
        out_specs=[out_spec, x_spec],
        out_shape=[out_shape, jax.ShapeDtypeStruct((m, d), BF16)],
        **common)(xn, norm_gain.reshape(1, d).astype(F32), w, cs)


def _split2(x):
    hi = x.astype(BF16)
    return hi, (x - hi.astype(F32)).astype(BF16)


def _split3(x):
    hi = x.astype(BF16)
    r = x - hi.astype(F32)
    mid = r.astype(BF16)
    return hi, mid, (r - mid.astype(F32)).astype(BF16)


def _fgate_logits(xh, xl, wcat_ref, b_ref):
    both = jnp.dot(xh, wcat_ref[...], preferred_element_type=F32)
    return (both[:, :LANES] + both[:, LANES:]
            + jnp.dot(xl, wcat_ref[:, :LANES], preferred_element_type=F32)) + b_ref[...]


def _fgate_cumsum(logit, carry, heads):
    tl = logit.shape[0]
    logf = jnp.minimum(logit, 0.0) - jnp.log(1.0 + jnp.exp(-jnp.abs(logit)))
    t_idx = lax.broadcasted_iota(jnp.int32, (tl, tl), 0)
    s_idx = lax.broadcasted_iota(jnp.int32, (tl, tl), 1)
    tril = (s_idx <= t_idx).astype(BF16)
    csum = carry
    for part in _split3(logf):
        csum = csum + jnp.dot(tril, part, preferred_element_type=F32)
    lane = lax.broadcasted_iota(jnp.int32, (tl, LANES), 1)
    packed = jnp.zeros((tl, LANES), F32)
    for n, part in reversed(list(enumerate(_split3(csum * LOG2E)))):
        shifted = part.astype(F32) if n == 0 else pltpu.roll(part.astype(F32), n * heads, 1)
        packed = jnp.where(lane < (n + 1) * heads, shifted, packed)
    return packed.astype(BF16), csum[tl - 1:tl, :]


def _mem_attn_kernel(q_ref, zm_ref, kv_ref, o_ref):
    width = MEM_HEADS * HEAD_DIM
    heads = range(MEM_HEADS)
    cols = [slice(h * HEAD_DIM, (h + 1) * HEAD_DIM) for h in heads]
    st = [lax.dot_general(kv_ref[:, cols[h]], q_ref[h], (((1,), (1,)), ((), ())),
                          preferred_element_type=F32) for h in heads]
    vt = [kv_ref[:, width + h * HEAD_DIM:width + (h + 1) * HEAD_DIM].astype(F32).T.astype(BF16)
          for h in heads]
    p = [jnp.exp2(st[h] - jnp.max(st[h], axis=0, keepdims=True)) for h in heads]
    inv = [1.0 / jnp.sum(p[h], axis=0, keepdims=True) for h in heads]
    ot = [jnp.dot(vt[h], p[h].astype(BF16), preferred_element_type=F32) * inv[h] for h in heads]
    for h in heads:
        zm = zm_ref[h].astype(F32)
        o_ref[:, cols[h]] = (ot[h].T * (zm * jax.nn.sigmoid(zm))).astype(o_ref.dtype)


def _mem_attn(qz, kvm, *, bsz, tq, name):
    width = MEM_HEADS * HEAD_DIM
    m = qz.shape[1]
    seqlen = m // bsz
    n_mem = kvm.shape[0] // bsz
    per_b = seqlen // tq
    spec = lambda blk: pl.BlockSpec((MEM_HEADS, tq, HEAD_DIM),
                                    lambda b, i: (blk, b * per_b + i, 0))
    return pl.pallas_call(
        _mem_attn_kernel,
        grid=(bsz, per_b),
        in_specs=[spec(0), spec(1),
                  pl.BlockSpec((n_mem, 2 * width), lambda b, i: (b, 0))],
        out_specs=pl.BlockSpec((tq, width), lambda b, i: (b * per_b + i, 0)),
        out_shape=jax.ShapeDtypeStruct((m, width), BF16),
        compiler_params=_params(("parallel", "arbitrary")),
        name=name,
    )(qz, qz, kvm)


def _s5_kernel(u_ref, winc_ref, woutc_ref, wtc_ref, ar_ref, ai_ref, d_ref, y_ref,
               win_ref, wout_ref, wt_ref, bu_ref, xs_ref, st_ref, *, n_pairs):
    nblk = win_ref.shape[0]

    @pl.when(pl.program_id(1) == 0)
    def _():
        st_ref[...] = jnp.zeros_like(st_ref)
        for k in range(nblk):
            win_ref[k] = _s5_expand(winc_ref[k], SSM_STATE)
            wout_ref[k] = _s5_expand(woutc_ref[k], SSM_STATE)
            wt_ref[k] = _s5_expand(wtc_ref[k], SSM_GROUP)

    rows = n_pairs * SUBLANES
    u0, u1, ucat, a2, state = [], [], [], [], []
    for k in range(nblk):
        upair = u_ref[:, :, k * LANES:(k + 1) * LANES].reshape(n_pairs, 2, SUBLANES, LANES)
        u0.append(upair[:, 0].reshape(rows, LANES))
        u1.append(upair[:, 1].reshape(rows, LANES))
        ucat.append(jnp.concatenate([u0[k], u1[k]], axis=1).astype(BF16))
        bu_ref[k] = jnp.dot(ucat[k], win_ref[k], preferred_element_type=F32)
        a2.append((jnp.broadcast_to(ar_ref[k], (SUBLANES, STATE_COLS)),
                   jnp.broadcast_to(ai_ref[k], (SUBLANES, STATE_COLS))))
        state.append((st_ref[k, 0], st_ref[k, 1]))

    held = [None] * nblk
    for c in range(n_pairs):
        for k in range(nblk):
            xr, xi = state[k]
            cur = jnp.concatenate([xr, xi], axis=1)
            if c % 2 == 1:
                xs_ref[k, (c - 1) * SUBLANES:(c + 1) * SUBLANES, :] = jnp.concatenate(
                    [held[k], cur], axis=0).astype(BF16)
            held[k] = cur
            bur = bu_ref[k, c * SUBLANES:(c + 1) * SUBLANES, 0:STATE_COLS]
            bui = bu_ref[k, c * SUBLANES:(c + 1) * SUBLANES, STATE_COLS:2 * STATE_COLS]
            ar, ai = a2[k]
            state[k] = (ar * xr - ai * xi + bur, ar * xi + ai * xr + bui)

    for k in range(nblk):
        st_ref[k, 0] = state[k][0]
        st_ref[k, 1] = state[k][1]
        y = (lax.dot_general(xs_ref[k], wout_ref[k], (((1,), (1,)), ((), ())),
                             preferred_element_type=F32)
             + jnp.dot(ucat[k], wt_ref[k], preferred_element_type=F32))
        d = d_ref[:, k * LANES:(k + 1) * LANES]
        y0 = jax.nn.gelu(y[:, :LANES] + d * u0[k]).reshape(n_pairs, SUBLANES, LANES)
        y1 = jax.nn.gelu(y[:, LANES:] + d * u1[k]).reshape(n_pairs, SUBLANES, LANES)
        y_ref[:, :, k * LANES:(k + 1) * LANES] = jnp.stack([y0, y1], axis=1).reshape(
            2 * n_pairs, SUBLANES, LANES)


def _s5_discretise(lam_re, lam_im, log_step, b_re, b_im, c_re, c_im):
    groups = lam_re.shape[0]
    nblk = groups // GROUPS_PER_BLOCK
    lr = lam_re.astype(F32)
    li = lam_im.astype(F32)
    dt = jnp.exp(log_step.astype(F32))[:, None]
    mag = jnp.exp(lr * dt)
    ar = mag * jnp.cos(li * dt)
    ai = mag * jnp.sin(li * dt)
    den = lr * lr + li * li
    cr = ((ar - 1.0) * lr + ai * li) / den
    ci = (ai * lr - (ar - 1.0) * li) / den
    br = b_re.astype(F32)
    bi = b_im.astype(F32)
    bbar_re = cr[..., None] * br - ci[..., None] * bi
    bbar_im = cr[..., None] * bi + ci[..., None] * br
    a2r = ar * ar - ai * ai
    a2i = 2.0 * ar * ai
    ab_re = ar[..., None] * bbar_re - ai[..., None] * bbar_im
    ab_im = ar[..., None] * bbar_im + ai[..., None] * bbar_re
    cre = c_re.astype(F32)
    cim = c_im.astype(F32)
    ca_re = cre * ar[:, None, :] - cim * ai[:, None, :]
    ca_im = cre * ai[:, None, :] + cim * ar[:, None, :]
    ca2_re = cre * a2r[:, None, :] - cim * a2i[:, None, :]
    ca2_im = cre * a2i[:, None, :] + cim * a2r[:, None, :]
    k0 = jnp.einsum('ghp,gpk->ghk', cre, bbar_re) - jnp.einsum('ghp,gpk->ghk', cim, bbar_im)
    k1 = jnp.einsum('ghp,gpk->ghk', cre, ab_re) - jnp.einsum('ghp,gpk->ghk', cim, ab_im)

    def per_block(x):
        return x.reshape((nblk, GROUPS_PER_BLOCK) + x.shape[1:])

    b_in = jnp.stack([jnp.stack([per_block(ab_re), per_block(ab_im)]),
                      jnp.stack([per_block(bbar_re), per_block(bbar_im)])])
    win_c = b_in.transpose(2, 0, 5, 1, 3, 4).reshape(nblk, 2 * SSM_GROUP, 2 * STATE_COLS)
    c_out = jnp.stack([jnp.stack([per_block(ca_re), per_block(ca2_re)]),
                       jnp.stack([-per_block(ca_im), -per_block(ca2_im)])])
    wout_c = c_out.transpose(2, 1, 4, 0, 3, 5).reshape(nblk, 2 * SSM_GROUP, 2 * STATE_COLS)
    k0b, k1b = per_block(k0), per_block(k1)
    k_mix = jnp.stack([jnp.stack([k0b, k1b]),
                       jnp.stack([jnp.zeros_like(k0b), k0b])])
    wt_c = k_mix.transpose(2, 0, 5, 1, 3, 4).reshape(nblk, 2 * SSM_GROUP, 2 * LANES)
    return (win_c, wout_c, wt_c,
            a2r.reshape(nblk, 1, STATE_COLS), a2i.reshape(nblk, 1, STATE_COLS))


def _s5_expand(compact, cols_per_group):
    width = compact.shape[1]
    col_group = (lax.broadcasted_iota(jnp.int32, (SSM_GROUP, width), 1)
                 // cols_per_group) % GROUPS_PER_BLOCK
    pieces = []
    for i in range(2):
        rows = compact[i * SSM_GROUP:(i + 1) * SSM_GROUP, :]
        for g in range(GROUPS_PER_BLOCK):
            pieces.append(jnp.where(col_group == g, rows, 0.0).astype(BF16))
    return jnp.concatenate(pieces, axis=0)


def _s5(u_tm, win, wout, wt, a2r, a2i, d_skip, *, bsz, n_pairs, blocks_per_step):
    seqlen = u_tm.shape[0]
    width = u_tm.shape[1] // bsz
    assert bsz == SUBLANES and seqlen % (2 * n_pairs) == 0 and width % LANES == 0
    nblk = width // LANES
    assert nblk % blocks_per_step == 0 and n_pairs % 2 == 0
    nb = blocks_per_step
    rows = n_pairs * SUBLANES
    u3 = u_tm.reshape(seqlen, bsz, width)
    blk = pl.BlockSpec((2 * n_pairs, bsz, nb * LANES), lambda j, c: (c, 0, j))
    per_blk = lambda shape: pl.BlockSpec((nb,) + shape, lambda j, c: (j, 0, 0))
    out = pl.pallas_call(
        functools.partial(_s5_kernel, n_pairs=n_pairs),
        grid=(nblk // nb, seqlen // (2 * n_pairs)),
        in_specs=[blk,
                  per_blk((2 * SSM_GROUP, 2 * STATE_COLS)),
                  per_blk((2 * SSM_GROUP, 2 * STATE_COLS)),
                  per_blk((2 * SSM_GROUP, 2 * LANES)),
                  per_blk((1, STATE_COLS)),
                  per_blk((1, STATE_COLS)),
                  pl.BlockSpec((1, nb * LANES), lambda j, c: (0, j))],
        out_specs=blk,
        out_shape=jax.ShapeDtypeStruct((seqlen, bsz, width), F32),
        scratch_shapes=[pltpu.VMEM((nb, 2 * LANES, 2 * STATE_COLS), BF16),
                        pltpu.VMEM((nb, 2 * LANES, 2 * STATE_COLS), BF16),
                        pltpu.VMEM((nb, 2 * LANES, 2 * LANES), BF16),
                        pltpu.VMEM((nb, rows, 2 * STATE_COLS), F32),
                        pltpu.VMEM((nb, rows, 2 * STATE_COLS), BF16),
                        pltpu.VMEM((nb, 2, SUBLANES, STATE_COLS), F32)],
        compiler_params=_params(("arbitrary", "arbitrary")),
        name="s5_scan",
    )(u3, win, wout, wt, a2r, a2i, d_skip.reshape(1, width).astype(F32))
    return out.reshape(seqlen, bsz * width)


def _post(o, g_ref, h_ref, out_ref):
    ms = jnp.mean(o * o, axis=-1, keepdims=True)
    out_ref[...] = h_ref[...] + o * lax.rsqrt(ms + EPS) * g_ref[...]


def _glu_out_kernel(yg_ref, z_ref, memo_ref, h_ref, wglu_ref, bglu_ref,
                    wmain_ref, wmem_ref, g_ref, gkv_ref, gnext_ref, wf_ref, bf_ref,
                    out_ref, xkv_ref, xnext_ref, fp_ref, carry_ref, *, heads, sub):
    @pl.when(pl.program_id(1) == 0)
    def _():
        carry_ref[...] = jnp.zeros_like(carry_ref)

    for s in range(out_ref.shape[0] // sub):
        rows = slice(s * sub, (s + 1) * sub)
        yg = yg_ref[rows, :]
        t = jnp.dot(yg.astype(BF16), wglu_ref[...], preferred_element_type=F32) + bglu_ref[...]
        z = z_ref[rows, :].astype(F32)
        main = yg * jax.nn.sigmoid(t) * (z * jax.nn.sigmoid(z))
        o = jnp.dot(main.astype(BF16), wmain_ref[...], preferred_element_type=F32)
        o = o + jnp.dot(memo_ref[rows, :], wmem_ref[...], preferred_element_type=F32)
        ms = jnp.mean(o * o, axis=-1, keepdims=True)
        h1 = h_ref[rows, :] + o * lax.rsqrt(ms + EPS) * g_ref[...]
        out_ref[rows, :] = h1
        r = h1 * lax.rsqrt(jnp.mean(h1 * h1, axis=-1, keepdims=True) + EPS)
        xnext_ref[rows, :] = (r * gnext_ref[...]).astype(BF16)
        xh, xl = _split2(r * gkv_ref[...])
        xkv_ref[rows, :] = xh
        logit = _fgate_logits(xh, xl, wf_ref, bf_ref)
        fp_ref[rows, :], carry_ref[...] = _fgate_cumsum(logit, carry_ref[...], heads)


def _out_kernel(main_ref, memo_ref, h_ref, wmain_ref, wmem_ref, g_ref, out_ref):
    main = jnp.concatenate([main_ref[s] for s in range(main_ref.shape[0])], axis=1)
    o = jnp.dot(main, wmain_ref[...], preferred_element_type=F32)
    o = o + jnp.dot(memo_ref[...], wmem_ref[...], preferred_element_type=F32)
    _post(o, g_ref, h_ref, out_ref)


def _resident(shape):
    return pl.BlockSpec(shape, lambda *_: (0,) * len(shape), pipeline_mode=pl.Buffered(1))


def _w_out_specs(layer, main_w, mem_w, d):
    assert main_w % mem_w == 0
    return [pl.BlockSpec((None, main_w, d), lambda *_: (layer, 0, 0),
                         pipeline_mode=pl.Buffered(1)),
            pl.BlockSpec((None, mem_w, d), lambda *_: (layer, main_w // mem_w, 0),
                         pipeline_mode=pl.Buffered(1))]


def _glu_out(yg_tm, rest, memo, h2d, w_glu, b_glu, w_out_bf, layer, g, g_kv, g_next,
             w_fgate, b_fgate, *, bsz, tl, sub):
    assert tl % sub == 0
    m, d = h2d.shape
    seqlen = m // bsz
    main_w = w_glu.shape[0]
    mem_w = w_out_bf.shape[1] - main_w
    heads = w_fgate.shape[1]
    assert 3 * heads <= LANES and seqlen % tl == 0
    wpad = jnp.zeros((d, LANES), F32).at[:, :heads].set(w_fgate.astype(F32))
    wcat = jnp.concatenate(_split2(wpad), axis=1)
    bpad = jnp.zeros((1, LANES), F32).at[0, :heads].set(b_fgate.astype(F32))
    per_b = seqlen // tl
    row = lambda b, i: (b * per_b + i, 0)
    vec = lambda v: v.reshape(1, -1).astype(F32)
    return pl.pallas_call(
        functools.partial(_glu_out_kernel, heads=heads, sub=sub),
        grid=(bsz, per_b),
        in_specs=[pl.BlockSpec((tl, main_w), lambda b, i: (i, b)),
                  pl.BlockSpec((tl, main_w), row),
                  pl.BlockSpec((tl, mem_w), row),
                  pl.BlockSpec((tl, d), row),
                  _resident((main_w, main_w)),
                  _resident((1, main_w))]
                 + _w_out_specs(layer, main_w, mem_w, d)
                 + [_resident((1, d)),
                    _resident((1, d)),
                    _resident((1, d)),
                    _resident((d, 2 * LANES)),
                    _resident((1, LANES))],
        out_specs=[pl.BlockSpec((tl, d), row),
                   pl.BlockSpec((tl, d), row),
                   pl.BlockSpec((tl, d), row),
                   pl.BlockSpec((tl, LANES), row)],
        out_shape=[jax.ShapeDtypeStruct((m, d), F32),
                   jax.ShapeDtypeStruct((m, d), BF16),
                   jax.ShapeDtypeStruct((m, d), BF16),
                   jax.ShapeDtypeStruct((m, LANES), BF16)],
        scratch_shapes=[pltpu.VMEM((1, LANES), F32)],
        compiler_params=_params(("arbitrary", "arbitrary")),
        name="glu_out_proj",
    )(yg_tm, rest, memo, h2d, w_glu.astype(BF16), vec(b_glu), w_out_bf, w_out_bf,
      vec(g), vec(g_kv), vec(g_next), wcat, bpad)


def _out_proj(main, memo, h2d, w_out_bf, layer, g, *, tl):
    m, d = h2d.shape
    heads, _, dh = main.shape
    main_w = heads * dh
    mem_w = memo.shape[1]
    row = lambda i: (i, 0)
    return pl.pallas_call(
        _out_kernel,
        grid=(m // tl,),
        in_specs=[pl.BlockSpec((heads, tl, dh), lambda i: (0, i, 0)),
                  pl.BlockSpec((tl, mem_w), row),
                  pl.BlockSpec((tl, d), row)]
                 + _w_out_specs(layer, main_w, mem_w, d)
                 + [_resident((1, d))],
        out_specs=pl.BlockSpec((tl, d), row),
        out_shape=jax.ShapeDtypeStruct((m, d), F32),
        compiler_params=_params(("parallel",)),
        name="out_proj",
    )(main, memo, h2d, w_out_bf, w_out_bf, g.reshape(1, d).astype(F32))


FOX_CHUNK = 64


def _fox_kernel(q_ref, k_ref, v_ref, fp_ref, sel_ref, cst_ref, z_ref, o_ref,
                ka_ref, qa_ref, vt_ref, st_ref, pt_ref, m_ref, l_ref, acc_ref, *, tq):
    nh, seqlen, _ = q_ref.shape
    ncg = tq // LANES
    nch = tq // FOX_CHUNK
    for hh in range(nh):
        ext = jnp.dot(fp_ref[...], sel_ref[hh], preferred_element_type=F32)
        ka_ref[hh, :, :HEAD_DIM] = k_ref[hh]
        ka_ref[hh, :, HEAD_DIM:] = (ext[:, :HEAD_DIM] + cst_ref[0:1, :]).astype(BF16)
        qa_ref[hh, :, :HEAD_DIM] = q_ref[hh]
        qa_ref[hh, :, HEAD_DIM:] = (ext[:, HEAD_DIM:] + cst_ref[1:2, :]).astype(BF16)
        vt_ref[hh] = v_ref[hh].astype(F32).T.astype(BF16)
        for ci in range(nch):
            for g in range(ncg):
                if g * LANES + LANES - 1 < ci * FOX_CHUNK:
                    pt_ref[hh, 1, ci * FOX_CHUNK:(ci + 1) * FOX_CHUNK,
                           g * LANES:(g + 1) * LANES] = jnp.zeros((FOX_CHUNK, LANES), BF16)

    def scores(hh, qi, kj, slot):
        st_ref[hh, slot] = lax.dot_general(ka_ref[hh, kj * tq:(kj + 1) * tq, :],
                                           qa_ref[hh, qi * tq:(qi + 1) * tq, :],
                                           (((1,), (1,)), ((), ())),
                                           preferred_element_type=F32)

    def piece(hh, slot, ci, g, masked):
        r0, c0 = ci * FOX_CHUNK, g * LANES
        if masked and c0 + LANES - 1 < r0:
            return None
        x = st_ref[hh, slot, r0:r0 + FOX_CHUNK, c0:c0 + LANES]
        if masked and c0 < r0 + FOX_CHUNK - 1:
            key = r0 + lax.broadcasted_iota(jnp.int32, (FOX_CHUNK, LANES), 0)
            qry = c0 + lax.broadcasted_iota(jnp.int32, (FOX_CHUNK, LANES), 1)
            x = jnp.where(key <= qry, x, -jnp.inf)
        return x

    def fold(x, op):
        return op(x.reshape(FOX_CHUNK // SUBLANES, SUBLANES, LANES), axis=0)

    def softmax_pv(hh, kj, slot, masked, first):
        mx = [None] * ncg
        for ci in range(nch):
            for g in range(ncg):
                x = piece(hh, slot, ci, g, masked)
                if x is not None:
                    f = fold(x, jnp.max)
                    mx[g] = f if mx[g] is None else jnp.maximum(mx[g], f)
        m_new = jnp.concatenate([jnp.max(v, axis=0, keepdims=True) for v in mx], axis=1)
        if not first:
            m_old = m_ref[hh]
            m_new = jnp.maximum(m_old, m_new)
            alpha = jnp.exp2(m_old - m_new)
        m_ref[hh] = m_new
        ls = [None] * ncg
        for ci in range(nch):
            for g in range(ncg):
                r0, c0 = ci * FOX_CHUNK, g * LANES
                x = piece(hh, slot, ci, g, masked)
                if x is None:
                    continue
                p = jnp.exp2(x - m_new[:, c0:c0 + LANES])
                f = fold(p, jnp.sum)
                ls[g] = f if ls[g] is None else ls[g] + f
                pt_ref[hh, int(masked), r0:r0 + FOX_CHUNK, c0:c0 + LANES] = p.astype(BF16)
        l_new = jnp.concatenate([jnp.sum(v, axis=0, keepdims=True) for v in ls], axis=1)
        pv = jnp.dot(vt_ref[hh, :, kj * tq:(kj + 1) * tq], pt_ref[hh, int(masked)],
                     preferred_element_type=F32)
        if first:
            l_ref[hh] = l_new
            acc_ref[hh] = pv
        else:
            l_ref[hh] = alpha * l_ref[hh] + l_new
            acc_ref[hh] = alpha * acc_ref[hh] + pv

    for qi in range(seqlen // tq):
        rows = slice(qi * tq, (qi + 1) * tq)
        for hh in range(nh):
            scores(hh, qi, 0, 0)
        for kj in range(qi + 1):
            for hh in range(nh):
                if kj < qi:
                    scores(hh, qi, kj + 1, (kj + 1) % 2)
                softmax_pv(hh, kj, kj % 2, masked=(kj == qi), first=(kj == 0))
        for hh in range(nh):
            z = z_ref[hh, rows, :].astype(F32)
            o = (acc_ref[hh] / l_ref[hh]).T
            o_ref[hh, rows, :] = (o * (z * jax.nn.sigmoid(z))).astype(o_ref.dtype)


def _fox_attn(q, k, v, z, fparts, *, bsz, tq, heads_per_step):
    heads, m, _ = q.shape
    seqlen = m // bsz
    nterms = 3
    assert seqlen % tq == 0 and tq % FOX_CHUNK == 0 and nterms * heads <= LANES
    h_idx = jnp.arange(heads)[:, None, None]
    r_idx = jnp.arange(LANES)[None, :, None]
    c_idx = jnp.arange(2 * HEAD_DIM)[None, None, :]
    key_side = (c_idx < nterms) & (r_idx == c_idx * heads + h_idx)
    qc = c_idx - HEAD_DIM - nterms
    qry_side = (qc >= 0) & (qc < nterms) & (r_idx == qc * heads + h_idx)
    sel = (qry_side.astype(F32) - key_side.astype(F32)).astype(BF16)
    col = jnp.arange(HEAD_DIM)
    cst = jnp.zeros((SUBLANES, HEAD_DIM), F32)
    cst = cst.at[0].set(((col >= nterms) & (col < 2 * nterms)).astype(F32))
    cst = cst.at[1].set((col < nterms).astype(F32))
    nh = heads_per_step
    assert heads % nh == 0
    head_seq = pl.BlockSpec((nh, seqlen, HEAD_DIM), lambda b, h: (h, b, 0))
    return pl.pallas_call(
        functools.partial(_fox_kernel, tq=tq),
        grid=(bsz, heads // nh),
        in_specs=[head_seq, head_seq, head_seq,
                  pl.BlockSpec((seqlen, LANES), lambda b, h: (b, 0)),
                  pl.BlockSpec((nh, LANES, 2 * HEAD_DIM), lambda b, h: (h, 0, 0)),
                  pl.BlockSpec((SUBLANES, HEAD_DIM), lambda b, h: (0, 0)),
                  head_seq],
        out_specs=head_seq,
        out_shape=jax.ShapeDtypeStruct((heads, m, HEAD_DIM), BF16),
        scratch_shapes=[pltpu.VMEM((nh, seqlen, 2 * HEAD_DIM), BF16),
                        pltpu.VMEM((nh, seqlen, 2 * HEAD_DIM), BF16),
                        pltpu.VMEM((nh, HEAD_DIM, seqlen), BF16),
                        pltpu.VMEM((nh, 2, tq, tq), F32),
                        pltpu.VMEM((nh, 2, tq, tq), BF16),
                        pltpu.VMEM((nh, 1, tq), F32),
                        pltpu.VMEM((nh, 1, tq), F32),
                        pltpu.VMEM((nh, HEAD_DIM, tq), F32)],
        compiler_params=_params(("arbitrary", "arbitrary")),
        name="fox_attn",
    )(q, k, v, fparts, sel, cst, z)


def kernel(x, mem, pre_norm_g, post_norm_g, w_in_a, lam_re, lam_im, log_step, b_re, b_im,
           c_re, c_im, d_skip, w_glu, b_glu, kv_norm_g, w_kv, w_fgate, b_fgate, w_in_b,
           mem_norm_g, w_mem_kv, w_out):
    bsz, seqlen, d = x.shape
    n_mem = mem.shape[1]
    main_w = w_glu.shape[1]
    mem_w = w_out.shape[1] - main_w
    scale = HEAD_DIM ** -0.5
    x2d = x.reshape(bsz * seqlen, d)
    mem2d = mem.reshape(bsz * n_mem, d)

    ones = jnp.ones((main_w,), F32)
    mem_scale = jnp.full((mem_w,), scale * LOG2E, F32)
    cs_a = jnp.concatenate([ones, ones, mem_scale, ones[:mem_w]])
    cs_b = jnp.concatenate([ones * (scale * LOG2E), ones, mem_scale, ones[:mem_w]])
    proj = functools.partial(_matmul, out_dtype=BF16, tm=1024)
    u_tm, xa = _matmul(x2d, w_in_a[0], cs_a, col0=0, n=main_w, out_dtype=F32, tm=512,
                       tn=main_w, time_major_batches=bsz, norm_gain=pre_norm_g[0],
                       name="in_proj_a_u")
    z_a = proj(xa, w_in_a[0], cs_a, col0=main_w, n=main_w, tn=main_w, name="in_proj_a_z")
    qz_a = proj(xa, w_in_a[0], cs_a, col0=2 * main_w, n=2 * mem_w, tn=2 * mem_w,
                slab_width=HEAD_DIM, name="in_proj_a_mem")
    kvm0 = _norm_matmul(mem2d, mem_norm_g[0], w_mem_kv[0], out_dtype=BF16,
                        tm=512, tn=512, name="mem_kv0")
    memo0 = _mem_attn(qz_a, kvm0, bsz=bsz, tq=1024, name="mem_attn0")
    s5_ops = _s5_discretise(lam_re[0], lam_im[0], log_step[0], b_re[0], b_im[0],
                            c_re[0], c_im[0])
    yg_tm = _s5(u_tm, *s5_ops, d_skip[0], bsz=bsz, n_pairs=128, blocks_per_step=2)
    w_out_bf = w_out.astype(BF16)
    h1, xkv, xb, fparts = _glu_out(
        yg_tm, z_a, memo0, x2d, w_glu[0], b_glu[0], w_out_bf, 0, post_norm_g[0],
        kv_norm_g, pre_norm_g[1], w_fgate, b_fgate, bsz=bsz, tl=512, sub=256)

    cs_kv = jnp.ones((w_kv.shape[1],), F32)
    k_sh = proj(xkv, w_kv, cs_kv, col0=0, n=main_w, tn=main_w, slab_width=HEAD_DIM,
                name="k_proj")
    v_sh = proj(xkv, w_kv, cs_kv, col0=main_w, n=main_w, tn=main_w, slab_width=HEAD_DIM,
                name="v_proj")

    q_b = proj(xb, w_in_b[0], cs_b, col0=0, n=main_w, tn=main_w, slab_width=HEAD_DIM,
               name="in_proj_b_q")
    z_b = proj(xb, w_in_b[0], cs_b, col0=main_w, n=main_w, tn=main_w, slab_width=HEAD_DIM,
               name="in_proj_b_z")
    qz_b = proj(xb, w_in_b[0], cs_b, col0=2 * main_w, n=2 * mem_w, tn=2 * mem_w,
                slab_width=HEAD_DIM, name="in_proj_b_mem")
    kvm1 = _norm_matmul(mem2d, mem_norm_g[1], w_mem_kv[1], out_dtype=BF16,
                        tm=512, tn=512, name="mem_kv1")
    memo1 = _mem_attn(qz_b, kvm1, bsz=bsz, tq=1024, name="mem_attn1")
    att = _fox_attn(q_b, k_sh, v_sh, z_b, fparts, bsz=bsz, tq=512, heads_per_step=3)
    out = _out_proj(att, memo1, h1, w_out_bf, 1, post_norm_g[1], tl=512)
    return out.reshape(bsz, seqlen, d)
```

```python
import functools

import jax
import jax.numpy as jnp
from jax import lax
from jax.experimental import pallas as pl
from jax.experimental.pallas import tpu as pltpu

F32 = jnp.float32
BF16 = jnp.bfloat16

EPS = 1e-6
LOG2E = 1.4426950408889634
HEAD_DIM = 128
SSM_GROUP = 16
SSM_STATE = 64
MEM_HEADS = 4
LANES = 128
SUBLANES = 8
GROUPS_PER_BLOCK = LANES // SSM_GROUP
STATE_COLS = GROUPS_PER_BLOCK * SSM_STATE
VMEM_LIMIT = 56 * 1024 * 1024


def _params(sem, vmem=VMEM_LIMIT):
    return pltpu.CompilerParams(dimension_semantics=sem, vmem_limit_bytes=vmem)


def _norm_matmul_kernel(x_ref, g_ref, w_ref, o_ref, xn_ref):
    @pl.when(pl.program_id(1) == 0)
    def _():
        x = x_ref[...]
        ms = jnp.mean(x * x, axis=-1, keepdims=True)
        xn_ref[...] = (x * lax.rsqrt(ms + EPS) * g_ref[...]).astype(BF16)

    o_ref[...] = jnp.dot(xn_ref[...], w_ref[...].astype(BF16),
                         preferred_element_type=F32).astype(o_ref.dtype)


def _norm_matmul(x2d, g, w, *, out_dtype, tm, tn, name):
    m, d = x2d.shape
    n = w.shape[1]
    assert m % tm == 0 and n % tn == 0
    return pl.pallas_call(
        _norm_matmul_kernel,
        grid=(m // tm, n // tn),
        in_specs=[pl.BlockSpec((tm, d), lambda i, j: (i, 0)),
                  pl.BlockSpec((1, d), lambda i, j: (0, 0)),
                  pl.BlockSpec((d, tn), lambda i, j: (0, j))],
        out_specs=pl.BlockSpec((tm, tn), lambda i, j: (i, j)),
        out_shape=jax.ShapeDtypeStruct((m, n), out_dtype),
        scratch_shapes=[pltpu.VMEM((tm, d), BF16)],
        compiler_params=_params(("parallel", "arbitrary")),
        name=name,
    )(x2d, g.reshape(1, d).astype(F32), w)


def _store_product(res, o_ref):
    if len(o_ref.shape) == 3:
        sw = o_ref.shape[2]
        for s in range(o_ref.shape[0]):
            o_ref[s] = res[:, s * sw:(s + 1) * sw]
    else:
        o_ref[...] = res


def _matmul_kernel(x_ref, w_ref, cs_ref, o_ref, wb_ref):
    @pl.when(pl.program_id(1) == 0)
    def _():
        wb_ref[...] = (w_ref[...] * cs_ref[...]).astype(BF16)

    res = jnp.dot(x_ref[...], wb_ref[...], preferred_element_type=F32).astype(o_ref.dtype)
    _store_product(res, o_ref)


def _norm_then_matmul_kernel(x_ref, g_ref, w_ref, cs_ref, o_ref, xn_ref, wb_ref):
    @pl.when(pl.program_id(1) == 0)
    def _():
        wb_ref[...] = (w_ref[...] * cs_ref[...]).astype(BF16)

    x = x_ref[...]
    ms = jnp.mean(x * x, axis=-1, keepdims=True)
    xn = (x * lax.rsqrt(ms + EPS) * g_ref[...]).astype(BF16)
    xn_ref[...] = xn
    res = jnp.dot(xn, wb_ref[...], preferred_element_type=F32).astype(o_ref.dtype)
    _store_product(res, o_ref)


def _matmul(xn, w, colscale, *, col0, n, out_dtype, tm, tn, time_major_batches=None,
            slab_width=None, norm_gain=None, name):
    m, d = xn.shape
    assert m % tm == 0 and n % tn == 0 and col0 % tn == 0
    nj = n // tn
    j0 = col0 // tn
    if slab_width is not None:
        assert time_major_batches is None and tn % slab_width == 0
        per_tile = tn // slab_width
        out_shape = jax.ShapeDtypeStruct((n // slab_width, m, slab_width), out_dtype)
        out_spec = pl.BlockSpec((per_tile, tm, slab_width), lambda j, i: (j, i, 0))
    elif time_major_batches is None:
        out_shape = jax.ShapeDtypeStruct((m, n), out_dtype)
        out_spec = pl.BlockSpec((tm, tn), lambda j, i: (i, j))
    else:
        bsz = time_major_batches
        seqlen = m // bsz
        assert seqlen % tm == 0
        per_b = seqlen // tm
        out_shape = jax.ShapeDtypeStruct((seqlen, bsz * n), out_dtype)
        out_spec = pl.BlockSpec((tm, tn), lambda j, i: (i % per_b, (i // per_b) * nj + j))
    x_spec = pl.BlockSpec((tm, d), lambda j, i: (i, 0))
    w_specs = [pl.BlockSpec((d, tn), lambda j, i: (0, j0 + j), pipeline_mode=pl.Buffered(1)),
               pl.BlockSpec((1, tn), lambda j, i: (0, j0 + j))]
    common = dict(grid=(nj, m // tm), scratch_shapes=[pltpu.VMEM((d, tn), BF16)],
                  compiler_params=_params(("arbitrary", "arbitrary")), name=name)
    cs = colscale.reshape(1, -1).astype(F32)
    if norm_gain is None:
        return pl.pallas_call(
            _matmul_kernel, in_specs=[x_spec] + w_specs, out_specs=out_spec,
            out_shape=out_shape, **common)(xn, w, cs)
    assert nj == 1
    return pl.pallas_call(
        _norm_then_matmul_kernel,
        in_specs=[x_spec, pl.BlockSpec((1, d), lambda j, i: (0, 0))] + w_specs,
        out_specs=[out_spec, x_spec],
        out_shape=[out_shape, jax.ShapeDtypeStruct((m, d), BF16)],
        **common)(xn, norm_gain.reshape(1, d).astype(F32), w, cs)


def _split2(x):
    hi = x.astype(BF16)
    return hi, (x - hi.astype(F32)).astype(BF16)


def _split3(x):
    hi = x.astype(BF16)
    r = x - hi.astype(F32)
    mid = r.astype(BF16)
    return hi, mid, (r - mid.astype(F32)).astype(BF16)


def _fgate_logits(xh, xl, wcat_ref, b_ref):
    both = jnp.dot(xh, wcat_ref[...], preferred_element_type=F32)
    return (both[:, :LANES] + both[:, LANES:]
            + jnp.dot(xl, wcat_ref[:, :LANES], preferred_element_type=F32)) + b_ref[...]


def _fgate_cumsum(logit, carry, heads):
    tl = logit.shape[0]
    logf = jnp.minimum(logit, 0.0) - jnp.log(1.0 + jnp.exp(-jnp.abs(logit)))
    t_idx = lax.broadcasted_iota(jnp.int32, (tl, tl), 0)
    s_idx = lax.broadcasted_iota(jnp.int32, (tl, tl), 1)
    tril = (s_idx <= t_idx).astype(BF16)
    csum = carry
    for part in _split3(logf):
        csum = csum + jnp.dot(tril, part, preferred_element_type=F32)
    lane = lax.broadcasted_iota(jnp.int32, (tl, LANES), 1)
    packed = jnp.zeros((tl, LANES), F32)
    for n, part in reversed(list(enumerate(_split3(csum * LOG2E)))):
        shifted = part.astype(F32) if n == 0 else pltpu.roll(part.astype(F32), n * heads, 1)
        packed = jnp.where(lane < (n + 1) * heads, shifted, packed)
    return packed.astype(BF16), csum[tl - 1:tl, :]


def _mem_attn_kernel(q_ref, zm_ref, kv_ref, o_ref):
    width = MEM_HEADS * HEAD_DIM
    heads = range(MEM_HEADS)
    cols = [slice(h * HEAD_DIM, (h + 1) * HEAD_DIM) for h in heads]
    st = [lax.dot_general(kv_ref[:, cols[h]], q_ref[h], (((1,), (1,)), ((), ())),
                          preferred_element_type=F32) for h in heads]
    vt = [kv_ref[:, width + h * HEAD_DIM:width + (h + 1) * HEAD_DIM].astype(F32).T.astype(BF16)
          for h in heads]
    p = [jnp.exp2(st[h] - jnp.max(st[h], axis=0, keepdims=True)) for h in heads]
    inv = [1.0 / jnp.sum(p[h], axis=0, keepdims=True) for h in heads]
    ot = [jnp.dot(vt[h], p[h].astype(BF16), preferred_element_type=F32) * inv[h] for h in heads]
    for h in heads:
        zm = zm_ref[h].astype(F32)
        o_ref[:, cols[h]] = (ot[h].T * (zm * jax.nn.sigmoid(zm))).astype(o_ref.dtype)


def _mem_attn(qz, kvm, *, bsz, tq, name):
    width = MEM_HEADS * HEAD_DIM
    m = qz.shape[1]
    seqlen = m // bsz
    n_mem = kvm.shape[0] // bsz
    per_b = seqlen // tq
    spec = lambda blk: pl.BlockSpec((MEM_HEADS, tq, HEAD_DIM),
                                    lambda b, i: (blk, b * per_b + i, 0))
    return pl.pallas_call(
        _mem_attn_kernel,
        grid=(bsz, per_b),
        in_specs=[spec(0), spec(1),
                  pl.BlockSpec((n_mem, 2 * width), lambda b, i: (b, 0))],
        out_specs=pl.BlockSpec((tq, width), lambda b, i: (b * per_b + i, 0)),
        out_shape=jax.ShapeDtypeStruct((m, width), BF16),
        compiler_params=_params(("parallel", "arbitrary")),
        name=name,
    )(qz, qz, kvm)


def _s5_kernel(u_ref, winc_ref, woutc_ref, wtc_ref, ar_ref, ai_ref, d_ref, y_ref,
               win_ref, wout_ref, wt_ref, bu_ref, xs_ref, st_ref, *, n_pairs):
    nblk = win_ref.shape[0]

    @pl.when(pl.program_id(1) == 0)
    def _():
        st_ref[...] = jnp.zeros_like(st_ref)
        for k in range(nblk):
            win_ref[k] = _s5_expand(winc_ref[k], SSM_STATE)
            wout_ref[k] = _s5_expand(woutc_ref[k], SSM_STATE)
            wt_ref[k] = _s5_expand(wtc_ref[k], SSM_GROUP)

    rows = n_pairs * SUBLANES
    u0, u1, ucat, a2, state = [], [], [], [], []
    for k in range(nblk):
        upair = u_ref[:, :, k * LANES:(k + 1) * LANES].reshape(n_pairs, 2, SUBLANES, LANES)
        u0.append(upair[:, 0].reshape(rows, LANES))
        u1.append(upair[:, 1].reshape(rows, LANES))
        ucat.append(jnp.concatenate([u0[k], u1[k]], axis=1).astype(BF16))
        bu_ref[k] = jnp.dot(ucat[k], win_ref[k], preferred_element_type=F32)
        a2.append((jnp.broadcast_to(ar_ref[k], (SUBLANES, STATE_COLS)),
                   jnp.broadcast_to(ai_ref[k], (SUBLANES, STATE_COLS))))
        state.append((st_ref[k, 0], st_ref[k, 1]))

    held = [None] * nblk
    for c in range(n_pairs):
        for k in range(nblk):
            xr, xi = state[k]
            cur = jnp.concatenate([xr, xi], axis=1)
            if c % 2 == 1:
                xs_ref[k, (c - 1) * SUBLANES:(c + 1) * SUBLANES, :] = jnp.concatenate(
                    [held[k], cur], axis=0).astype(BF16)
            held[k] = cur
            bur = bu_ref[k, c * SUBLANES:(c + 1) * SUBLANES, 0:STATE_COLS]
            bui = bu_ref[k, c * SUBLANES:(c + 1) * SUBLANES, STATE_COLS:2 * STATE_COLS]
            ar, ai = a2[k]
            state[k] = (ar * xr - ai * xi + bur, ar * xi + ai * xr + bui)

    for k in range(nblk):
        st_ref[k, 0] = state[k][0]
        st_ref[k, 1] = state[k][1]
        y = (lax.dot_general(xs_ref[k], wout_ref[k], (((1,), (1,)), ((), ())),
                             preferred_element_type=F32)
             + jnp.dot(ucat[k], wt_ref[k], preferred_element_type=F32))
        d = d_ref[:, k * LANES:(k + 1) * LANES]
        y0 = jax.nn.gelu(y[:, :LANES] + d * u0[k]).reshape(n_pairs, SUBLANES, LANES)
        y1 = jax.nn.gelu(y[:, LANES:] + d * u1[k]).reshape(n_pairs, SUBLANES, LANES)
        y_ref[:, :, k * LANES:(k + 1) * LANES] = jnp.stack([y0, y1], axis=1).reshape(
            2 * n_pairs, SUBLANES, LANES)


def _s5_discretise(lam_re, lam_im, log_step, b_re, b_im, c_re, c_im):
    groups = lam_re.shape[0]
    nblk = groups // GROUPS_PER_BLOCK
    lr = lam_re.astype(F32)
    li = lam_im.astype(F32)
    dt = jnp.exp(log_step.astype(F32))[:, None]
    mag = jnp.exp(lr * dt)
    ar = mag * jnp.cos(li * dt)
    ai = mag * jnp.sin(li * dt)
    den = lr * lr + li * li
    cr = ((ar - 1.0) * lr + ai * li) / den
    ci = (ai * lr - (ar - 1.0) * li) / den
    br = b_re.astype(F32)
    bi = b_im.astype(F32)
    bbar_re = cr[..., None] * br - ci[..., None] * bi
    bbar_im = cr[..., None] * bi + ci[..., None] * br
    a2r = ar * ar - ai * ai
    a2i = 2.0 * ar * ai
    ab_re = ar[..., None] * bbar_re - ai[..., None] * bbar_im
    ab_im = ar[..., None] * bbar_im + ai[..., None] * bbar_re
    cre = c_re.astype(F32)
    cim = c_im.astype(F32)
    ca_re = cre * ar[:, None, :] - cim * ai[:, None, :]
    ca_im = cre * ai[:, None, :] + cim * ar[:, None, :]
    ca2_re = cre * a2r[:, None, :] - cim * a2i[:, None, :]
    ca2_im = cre * a2i[:, None, :] + cim * a2r[:, None, :]
    k0 = jnp.einsum('ghp,gpk->ghk', cre, bbar_re) - jnp.einsum('ghp,gpk->ghk', cim, bbar_im)
    k1 = jnp.einsum('ghp,gpk->ghk', cre, ab_re) - jnp.einsum('ghp,gpk->ghk', cim, ab_im)

    def per_block(x):
        return x.reshape((nblk, GROUPS_PER_BLOCK) + x.shape[1:])

    b_in = jnp.stack([jnp.stack([per_block(ab_re), per_block(ab_im)]),
                      jnp.stack([per_block(bbar_re), per_block(bbar_im)])])
    win_c = b_in.transpose(2, 0, 5, 1, 3, 4).reshape(nblk, 2 * SSM_GROUP, 2 * STATE_COLS)
    c_out = jnp.stack([jnp.stack([per_block(ca_re), per_block(ca2_re)]),
                       jnp.stack([-per_block(ca_im), -per_block(ca2_im)])])
    wout_c = c_out.transpose(2, 1, 4, 0, 3, 5).reshape(nblk, 2 * SSM_GROUP, 2 * STATE_COLS)
    k0b, k1b = per_block(k0), per_block(k1)
    k_mix = jnp.stack([jnp.stack([k0b, k1b]),
                       jnp.stack([jnp.zeros_like(k0b), k0b])])
    wt_c = k_mix.transpose(2, 0, 5, 1, 3, 4).reshape(nblk, 2 * SSM_GROUP, 2 * LANES)
    return (win_c, wout_c, wt_c,
            a2r.reshape(nblk, 1, STATE_COLS), a2i.reshape(nblk, 1, STATE_COLS))


def _s5_expand(compact, cols_per_group):
    width = compact.shape[1]
    col_group = (lax.broadcasted_iota(jnp.int32, (SSM_GROUP, width), 1)
                 // cols_per_group) % GROUPS_PER_BLOCK
    pieces = []
    for i in range(2):
        rows = compact[i * SSM_GROUP:(i + 1) * SSM_GROUP, :]
        for g in range(GROUPS_PER_BLOCK):
            pieces.append(jnp.where(col_group == g, rows, 0.0).astype(BF16))
    return jnp.concatenate(pieces, axis=0)


def _s5(u_tm, win, wout, wt, a2r, a2i, d_skip, *, bsz, n_pairs, blocks_per_step):
    seqlen = u_tm.shape[0]
    width = u_tm.shape[1] // bsz
    assert bsz == SUBLANES and seqlen % (2 * n_pairs) == 0 and width % LANES == 0
    nblk = width // LANES
    assert nblk % blocks_per_step == 0 and n_pairs % 2 == 0
    nb = blocks_per_step
    rows = n_pairs * SUBLANES
    u3 = u_tm.reshape(seqlen, bsz, width)
    blk = pl.BlockSpec((2 * n_pairs, bsz, nb * LANES), lambda j, c: (c, 0, j))
    per_blk = lambda shape: pl.BlockSpec((nb,) + shape, lambda j, c: (j, 0, 0))
    out = pl.pallas_call(
        functools.partial(_s5_kernel, n_pairs=n_pairs),
        grid=(nblk // nb, seqlen // (2 * n_pairs)),
        in_specs=[blk,
                  per_blk((2 * SSM_GROUP, 2 * STATE_COLS)),
                  per_blk((2 * SSM_GROUP, 2 * STATE_COLS)),
                  per_blk((2 * SSM_GROUP, 2 * LANES)),
                  per_blk((1, STATE_COLS)),
                  per_blk((1, STATE_COLS)),
                  pl.BlockSpec((1, nb * LANES), lambda j, c: (0, j))],
        out_specs=blk,
        out_shape=jax.ShapeDtypeStruct((seqlen, bsz, width), F32),
        scratch_shapes=[pltpu.VMEM((nb, 2 * LANES, 2 * STATE_COLS), BF16),
                        pltpu.VMEM((nb, 2 * LANES, 2 * STATE_COLS), BF16),
                        pltpu.VMEM((nb, 2 * LANES, 2 * LANES), BF16),
                        pltpu.VMEM((nb, rows, 2 * STATE_COLS), F32),
                        pltpu.VMEM((nb, rows, 2 * STATE_COLS), BF16),
                        pltpu.VMEM((nb, 2, SUBLANES, STATE_COLS), F32)],
        compiler_params=_params(("arbitrary", "arbitrary")),
        name="s5_scan",
    )(u3, win, wout, wt, a2r, a2i, d_skip.reshape(1, width).astype(F32))
    return out.reshape(seqlen, bsz * width)


def _glu_out_kernel(yg_ref, z_ref, memo_ref, h_ref, wglu_ref, bglu_ref,
                    wmain_ref, wmem_ref, g_ref, gkv_ref, gnext_ref, wf_ref, bf_ref,
                    out_ref, xkv_ref, xnext_ref, fp_ref, carry_ref, *, heads, sub):
    @pl.when(pl.program_id(1) == 0)
    def _():
        carry_ref[...] = jnp.zeros_like(carry_ref)

    for s in range(out_ref.shape[0] // sub):
        rows = slice(s * sub, (s + 1) * sub)
        yg = yg_ref[rows, :]
        t = jnp.dot(yg.astype(BF16), wglu_ref[...], preferred_element_type=F32) + bglu_ref[...]
        z = z_ref[rows, :].astype(F32)
        main = yg * jax.nn.sigmoid(t) * (z * jax.nn.sigmoid(z))
        o = jnp.dot(main.astype(BF16), wmain_ref[...], preferred_element_type=F32)
        o = o + jnp.dot(memo_ref[rows, :], wmem_ref[...], preferred_element_type=F32)
        ms = jnp.mean(o * o, axis=-1, keepdims=True)
        h1 = h_ref[rows, :] + o * lax.rsqrt(ms + EPS) * g_ref[...]
        out_ref[rows, :] = h1
        r = h1 * lax.rsqrt(jnp.mean(h1 * h1, axis=-1, keepdims=True) + EPS)
        xnext_ref[rows, :] = (r * gnext_ref[...]).astype(BF16)
        xh, xl = _split2(r * gkv_ref[...])
        xkv_ref[rows, :] = xh
        logit = _fgate_logits(xh, xl, wf_ref, bf_ref)
        fp_ref[rows, :], carry_ref[...] = _fgate_cumsum(logit, carry_ref[...], heads)


def _out_kernel(main_ref, memo_ref, h_ref, wmain_ref, wmem_ref, g_ref, out_ref, *, sub):
    for s in range(out_ref.shape[0] // sub):
        rows = slice(s * sub, (s + 1) * sub)
        main = jnp.concatenate([main_ref[hd, rows, :] for hd in range(main_ref.shape[0])],
                               axis=1)
        o = jnp.dot(main, wmain_ref[...], preferred_element_type=F32)
        o = o + jnp.dot(memo_ref[rows, :], wmem_ref[...], preferred_element_type=F32)
        ms = jnp.mean(o * o, axis=-1, keepdims=True)
        out_ref[rows, :] = h_ref[rows, :] + o * lax.rsqrt(ms + EPS) * g_ref[...]


def _resident(shape):
    return pl.BlockSpec(shape, lambda *_: (0,) * len(shape), pipeline_mode=pl.Buffered(1))


def _w_out_specs(layer, main_w, mem_w, d):
    assert main_w % mem_w == 0
    return [pl.BlockSpec((None, main_w, d), lambda *_: (layer, 0, 0),
                         pipeline_mode=pl.Buffered(1)),
            pl.BlockSpec((None, mem_w, d), lambda *_: (layer, main_w // mem_w, 0),
                         pipeline_mode=pl.Buffered(1))]


def _glu_out(yg_tm, rest, memo, h2d, w_glu, b_glu, w_out_bf, layer, g, g_kv, g_next,
             w_fgate, b_fgate, *, bsz, tl, sub):
    assert tl % sub == 0
    m, d = h2d.shape
    seqlen = m // bsz
    main_w = w_glu.shape[0]
    mem_w = w_out_bf.shape[1] - main_w
    heads = w_fgate.shape[1]
    assert 3 * heads <= LANES and seqlen % tl == 0
    wpad = jnp.zeros((d, LANES), F32).at[:, :heads].set(w_fgate.astype(F32))
    wcat = jnp.concatenate(_split2(wpad), axis=1)
    bpad = jnp.zeros((1, LANES), F32).at[0, :heads].set(b_fgate.astype(F32))
    per_b = seqlen // tl
    row = lambda b, i: (b * per_b + i, 0)
    vec = lambda v: v.reshape(1, -1).astype(F32)
    return pl.pallas_call(
        functools.partial(_glu_out_kernel, heads=heads, sub=sub),
        grid=(bsz, per_b),
        in_specs=[pl.BlockSpec((tl, main_w), lambda b, i: (i, b)),
                  pl.BlockSpec((tl, main_w), row),
                  pl.BlockSpec((tl, mem_w), row),
                  pl.BlockSpec((tl, d), row),
                  _resident((main_w, main_w)),
                  _resident((1, main_w))]
                 + _w_out_specs(layer, main_w, mem_w, d)
                 + [_resident((1, d)),
                    _resident((1, d)),
                    _resident((1, d)),
                    _resident((d, 2 * LANES)),
                    _resident((1, LANES))],
        out_specs=[pl.BlockSpec((tl, d), row),
                   pl.BlockSpec((tl, d), row),
                   pl.BlockSpec((tl, d), row),
                   pl.BlockSpec((tl, LANES), row)],
        out_shape=[jax.ShapeDtypeStruct((m, d), F32),
                   jax.ShapeDtypeStruct((m, d), BF16),
                   jax.ShapeDtypeStruct((m, d), BF16),
                   jax.ShapeDtypeStruct((m, LANES), BF16)],
        scratch_shapes=[pltpu.VMEM((1, LANES), F32)],
        compiler_params=_params(("arbitrary", "arbitrary")),
        name="glu_out_proj",
    )(yg_tm, rest, memo, h2d, w_glu.astype(BF16), vec(b_glu), w_out_bf, w_out_bf,
      vec(g), vec(g_kv), vec(g_next), wcat, bpad)


def _out_proj(main, memo, h2d, w_out_bf, layer, g, *, tl, sub):
    m, d = h2d.shape
    heads, _, dh = main.shape
    main_w = heads * dh
    mem_w = memo.shape[1]
    row = lambda i: (i, 0)
    return pl.pallas_call(
        functools.partial(_out_kernel, sub=sub),
        grid=(m // tl,),
        in_specs=[pl.BlockSpec((heads, tl, dh), lambda i: (0, i, 0)),
                  pl.BlockSpec((tl, mem_w), row),
                  pl.BlockSpec((tl, d), row)]
                 + _w_out_specs(layer, main_w, mem_w, d)
                 + [_resident((1, d))],
        out_specs=pl.BlockSpec((tl, d), row),
        out_shape=jax.ShapeDtypeStruct((m, d), F32),
        compiler_params=_params(("parallel",)),
        name="out_proj",
    )(main, memo, h2d, w_out_bf, w_out_bf, g.reshape(1, d).astype(F32))


FOX_CHUNK = 64


def _fox_kernel(q_ref, k_ref, v_ref, fp_ref, sel_ref, cst_ref, z_ref, o_ref,
                ka_ref, qa_ref, vt_ref, st_ref, pt_ref, m_ref, l_ref, acc_ref, *, tq):
    nh, seqlen, _ = q_ref.shape
    ncg = tq // LANES
    nch = tq // FOX_CHUNK
    for hh in range(nh):
        ext = jnp.dot(fp_ref[...], sel_ref[hh], preferred_element_type=F32)
        ka_ref[hh, :, :HEAD_DIM] = k_ref[hh]
        ka_ref[hh, :, HEAD_DIM:] = (ext[:, :HEAD_DIM] + cst_ref[0:1, :]).astype(BF16)
        qa_ref[hh, :, :HEAD_DIM] = q_ref[hh]
        qa_ref[hh, :, HEAD_DIM:] = (ext[:, HEAD_DIM:] + cst_ref[1:2, :]).astype(BF16)
        vt_ref[hh] = v_ref[hh].astype(F32).T.astype(BF16)
        for ci in range(nch):
            for g in range(ncg):
                if g * LANES + LANES - 1 < ci * FOX_CHUNK:
                    pt_ref[hh, 1, ci * FOX_CHUNK:(ci + 1) * FOX_CHUNK,
                           g * LANES:(g + 1) * LANES] = jnp.zeros((FOX_CHUNK, LANES), BF16)

    half = tq // 2

    def scores(hh, qi, kj, slot):
        nt = (((1,), (1,)), ((), ()))
        k0, q0 = kj * tq, qi * tq
        if kj < qi:
            st_ref[hh, slot] = lax.dot_general(ka_ref[hh, k0:k0 + tq, :], qa_ref[hh, q0:q0 + tq, :],
                                               nt, preferred_element_type=F32)
        else:
            st_ref[hh, slot, :half, :] = lax.dot_general(
                ka_ref[hh, k0:k0 + half, :], qa_ref[hh, q0:q0 + tq, :], nt,
                preferred_element_type=F32)
            st_ref[hh, slot, half:, half:] = lax.dot_general(
                ka_ref[hh, k0 + half:k0 + tq, :], qa_ref[hh, q0 + half:q0 + tq, :], nt,
                preferred_element_type=F32)

    def piece(hh, slot, ci, g, masked):
        r0, c0 = ci * FOX_CHUNK, g * LANES
        if masked and c0 + LANES - 1 < r0:
            return None
        x = st_ref[hh, slot, r0:r0 + FOX_CHUNK, c0:c0 + LANES]
        if masked and c0 < r0 + FOX_CHUNK - 1:
            key = r0 + lax.broadcasted_iota(jnp.int32, (FOX_CHUNK, LANES), 0)
            qry = c0 + lax.broadcasted_iota(jnp.int32, (FOX_CHUNK, LANES), 1)
            x = jnp.where(key <= qry, x, -jnp.inf)
        return x

    def fold(x, op):
        return op(x.reshape(FOX_CHUNK // SUBLANES, SUBLANES, LANES), axis=0)

    def softmax_pv(hh, kj, slot, masked, first):
        mx = [None] * ncg
        for ci in range(nch):
            for g in range(ncg):
                x = piece(hh, slot, ci, g, masked)
                if x is not None:
                    f = fold(x, jnp.max)
                    mx[g] = f if mx[g] is None else jnp.maximum(mx[g], f)
        m_new = jnp.concatenate([jnp.max(v, axis=0, keepdims=True) for v in mx], axis=1)
        if not first:
            m_old = m_ref[hh]
            m_new = jnp.maximum(m_old, m_new)
            alpha = jnp.exp2(m_old - m_new)
        m_ref[hh] = m_new
        ls = [None] * ncg
        for ci in range(nch):
            for g in range(ncg):
                r0, c0 = ci * FOX_CHUNK, g * LANES
                x = piece(hh, slot, ci, g, masked)
                if x is None:
                    continue
                p = jnp.exp2(x - m_new[:, c0:c0 + LANES])
                f = fold(p, jnp.sum)
                ls[g] = f if ls[g] is None else ls[g] + f
                pt_ref[hh, int(masked), r0:r0 + FOX_CHUNK, c0:c0 + LANES] = p.astype(BF16)
        l_new = jnp.concatenate([jnp.sum(v, axis=0, keepdims=True) for v in ls], axis=1)
        k0 = kj * tq
        if masked:
            pv = jnp.dot(vt_ref[hh, :, k0:k0 + half], pt_ref[hh, 1, :half, :],
                         preferred_element_type=F32)
            late = jnp.dot(vt_ref[hh, :, k0 + half:k0 + tq], pt_ref[hh, 1, half:, half:],
                           preferred_element_type=F32)
            pv = jnp.concatenate([pv[:, :half], pv[:, half:] + late], axis=1)
        else:
            pv = jnp.dot(vt_ref[hh, :, k0:k0 + tq], pt_ref[hh, 0], preferred_element_type=F32)
        if first:
            l_ref[hh] = l_new
            acc_ref[hh] = pv
        else:
            l_ref[hh] = alpha * l_ref[hh] + l_new
            acc_ref[hh] = alpha * acc_ref[hh] + pv

    for qi in range(seqlen // tq):
        rows = slice(qi * tq, (qi + 1) * tq)
        for hh in range(nh):
            scores(hh, qi, 0, 0)
        for kj in range(qi + 1):
            for hh in range(nh):
                if kj < qi:
                    scores(hh, qi, kj + 1, (kj + 1) % 2)
                softmax_pv(hh, kj, kj % 2, masked=(kj == qi), first=(kj == 0))
        for hh in range(nh):
            z = z_ref[hh, rows, :].astype(F32)
            o = (acc_ref[hh] / l_ref[hh]).T
            o_ref[hh, rows, :] = (o * (z * jax.nn.sigmoid(z))).astype(o_ref.dtype)


def _fox_attn(q, k, v, z, fparts, *, bsz, tq, heads_per_step):
    heads, m, _ = q.shape
    seqlen = m // bsz
    nterms = 3
    assert seqlen % tq == 0 and tq % FOX_CHUNK == 0 and nterms * heads <= LANES
    h_idx = jnp.arange(heads)[:, None, None]
    r_idx = jnp.arange(LANES)[None, :, None]
    c_idx = jnp.arange(2 * HEAD_DIM)[None, None, :]
    key_side = (c_idx < nterms) & (r_idx == c_idx * heads + h_idx)
    qc = c_idx - HEAD_DIM - nterms
    qry_side = (qc >= 0) & (qc < nterms) & (r_idx == qc * heads + h_idx)
    sel = (qry_side.astype(F32) - key_side.astype(F32)).astype(BF16)
    col = jnp.arange(HEAD_DIM)
    cst = jnp.zeros((SUBLANES, HEAD_DIM), F32)
    cst = cst.at[0].set(((col >= nterms) & (col < 2 * nterms)).astype(F32))
    cst = cst.at[1].set((col < nterms).astype(F32))
    nh = heads_per_step
    assert heads % nh == 0
    head_seq = pl.BlockSpec((nh, seqlen, HEAD_DIM), lambda b, h: (h, b, 0))
    return pl.pallas_call(
        functools.partial(_fox_kernel, tq=tq),
        grid=(bsz, heads // nh),
        in_specs=[head_seq, head_seq, head_seq,
                  pl.BlockSpec((seqlen, LANES), lambda b, h: (b, 0)),
                  pl.BlockSpec((nh, LANES, 2 * HEAD_DIM), lambda b, h: (h, 0, 0)),
                  pl.BlockSpec((SUBLANES, HEAD_DIM), lambda b, h: (0, 0)),
                  head_seq],
        out_specs=head_seq,
        out_shape=jax.ShapeDtypeStruct((heads, m, HEAD_DIM), BF16),
        scratch_shapes=[pltpu.VMEM((nh, seqlen, 2 * HEAD_DIM), BF16),
                        pltpu.VMEM((nh, seqlen, 2 * HEAD_DIM), BF16),
                        pltpu.VMEM((nh, HEAD_DIM, seqlen), BF16),
                        pltpu.VMEM((nh, 2, tq, tq), F32),
                        pltpu.VMEM((nh, 2, tq, tq), BF16),
                        pltpu.VMEM((nh, 1, tq), F32),
                        pltpu.VMEM((nh, 1, tq), F32),
                        pltpu.VMEM((nh, HEAD_DIM, tq), F32)],
        compiler_params=_params(("arbitrary", "arbitrary")),
        name="fox_attn",
    )(q, k, v, fparts, sel, cst, z)


def kernel(x, mem, pre_norm_g, post_norm_g, w_in_a, lam_re, lam_im, log_step, b_re, b_im,
           c_re, c_im, d_skip, w_glu, b_glu, kv_norm_g, w_kv, w_fgate, b_fgate, w_in_b,
           mem_norm_g, w_mem_kv, w_out):
    bsz, seqlen, d = x.shape
    n_mem = mem.shape[1]
    main_w = w_glu.shape[1]
    mem_w = w_out.shape[1] - main_w
    scale = HEAD_DIM ** -0.5
    x2d = x.reshape(bsz * seqlen, d)
    mem2d = mem.reshape(bsz * n_mem, d)

    ones = jnp.ones((main_w,), F32)
    mem_scale = jnp.full((mem_w,), scale * LOG2E, F32)
    cs_a = jnp.concatenate([ones, ones, mem_scale, ones[:mem_w]])
    cs_b = jnp.concatenate([ones * (scale * LOG2E), ones, mem_scale, ones[:mem_w]])
    proj = functools.partial(_matmul, out_dtype=BF16, tm=1024)
    u_tm, xa = _matmul(x2d, w_in_a[0], cs_a, col0=0, n=main_w, out_dtype=F32, tm=512,
                       tn=main_w, time_major_batches=bsz, norm_gain=pre_norm_g[0],
                       name="in_proj_a_u")
    z_a = proj(xa, w_in_a[0], cs_a, col0=main_w, n=main_w, tn=main_w, name="in_proj_a_z")
    qz_a = proj(xa, w_in_a[0], cs_a, col0=2 * main_w, n=2 * mem_w, tn=2 * mem_w,
                slab_width=HEAD_DIM, name="in_proj_a_mem")
    kvm0 = _norm_matmul(mem2d, mem_norm_g[0], w_mem_kv[0], out_dtype=BF16,
                        tm=512, tn=512, name="mem_kv0")
    memo0 = _mem_attn(qz_a, kvm0, bsz=bsz, tq=1024, name="mem_attn0")
    s5_ops = _s5_discretise(lam_re[0], lam_im[0], log_step[0], b_re[0], b_im[0],
                            c_re[0], c_im[0])
    yg_tm = _s5(u_tm, *s5_ops, d_skip[0], bsz=bsz, n_pairs=128, blocks_per_step=3)
    w_out_bf = w_out.astype(BF16)
    h1, xkv, xb, fparts = _glu_out(
        yg_tm, z_a, memo0, x2d, w_glu[0], b_glu[0], w_out_bf, 0, post_norm_g[0],
        kv_norm_g, pre_norm_g[1], w_fgate, b_fgate, bsz=bsz, tl=512, sub=256)

    cs_kv = jnp.ones((w_kv.shape[1],), F32)
    k_sh = proj(xkv, w_kv, cs_kv, col0=0, n=main_w, tn=main_w, slab_width=HEAD_DIM,
                name="k_proj")
    v_sh = proj(xkv, w_kv, cs_kv, col0=main_w, n=main_w, tn=main_w, slab_width=HEAD_DIM,
                name="v_proj")

    q_b = proj(xb, w_in_b[0], cs_b, col0=0, n=main_w, tn=main_w, slab_width=HEAD_DIM,
               name="in_proj_b_q")
    z_b = proj(xb, w_in_b[0], cs_b, col0=main_w, n=main_w, tn=main_w, slab_width=HEAD_DIM,
               name="in_proj_b_z")
    qz_b = proj(xb, w_in_b[0], cs_b, col0=2 * main_w, n=2 * mem_w, tn=2 * mem_w,
                slab_width=HEAD_DIM, name="in_proj_b_mem")
    kvm1 = _norm_matmul(mem2d, mem_norm_g[1], w_mem_kv[1], out_dtype=BF16,
                        tm=512, tn=512, name="mem_kv1")
    memo1 = _mem_attn(qz_b, kvm1, bsz=bsz, tq=1024, name="mem_attn1")
    att = _fox_attn(q_b, k_sh, v_sh, z_b, fparts, bsz=bsz, tq=512, heads_per_step=3)
    out = _out_proj(att, memo1, h1, w_out_bf, 1, post_norm_g[1], tl=512, sub=256)
    return out.reshape(bsz, seqlen, d)
```

```python
import functools

import jax
import jax.numpy as jnp
from jax import lax
from jax.experimental import pallas as pl
from jax.experimental.pallas import tpu as pltpu

F32 = jnp.float32
BF16 = jnp.bfloat16

EPS = 1e-6
LOG2E = 1.4426950408889634
HEAD_DIM = 128
SSM_GROUP = 16
SSM_STATE = 64
MEM_HEADS = 4
LANES = 128
SUBLANES = 8
GROUPS_PER_BLOCK = LANES // SSM_GROUP
STATE_COLS = GROUPS_PER_BLOCK * SSM_STATE
VMEM_LIMIT = 56 * 1024 * 1024


def _params(sem, vmem=VMEM_LIMIT):
    return pltpu.CompilerParams(dimension_semantics=sem, vmem_limit_bytes=vmem)


def _norm_matmul_kernel(x_ref, g_ref, w_ref, o_ref, xn_ref):
    @pl.when(pl.program_id(1) == 0)
    def _():
        x = x_ref[...]
        ms = jnp.mean(x * x, axis=-1, keepdims=True)
        xn_ref[...] = (x * lax.rsqrt(ms + EPS) * g_ref[...]).astype(BF16)

    o_ref[...] = jnp.dot(xn_ref[...], w_ref[...].astype(BF16),
                         preferred_element_type=F32).astype(o_ref.dtype)


def _norm_matmul(x2d, g, w, *, out_dtype, tm, tn, name):
    m, d = x2d.shape
    n = w.shape[1]
    assert m % tm == 0 and n % tn == 0
    return pl.pallas_call(
        _norm_matmul_kernel,
        grid=(m // tm, n // tn),
        in_specs=[pl.BlockSpec((tm, d), lambda i, j: (i, 0)),
                  pl.BlockSpec((1, d), lambda i, j: (0, 0)),
                  pl.BlockSpec((d, tn), lambda i, j: (0, j))],
        out_specs=pl.BlockSpec((tm, tn), lambda i, j: (i, j)),
        out_shape=jax.ShapeDtypeStruct((m, n), out_dtype),
        scratch_shapes=[pltpu.VMEM((tm, d), BF16)],
        compiler_params=_params(("parallel", "arbitrary")),
        name=name,
    )(x2d, g.reshape(1, d).astype(F32), w)


def _store_product(res, o_ref):
    if len(o_ref.shape) == 3:
        sw = o_ref.shape[2]
        for s in range(o_ref.shape[0]):
            o_ref[s] = res[:, s * sw:(s + 1) * sw]
    else:
        o_ref[...] = res


def _matmul_kernel(x_ref, w_ref, cs_ref, o_ref, wb_ref):
    @pl.when(pl.program_id(1) == 0)
    def _():
        wb_ref[...] = (w_ref[...] * cs_ref[...]).astype(BF16)

    res = jnp.dot(x_ref[...], wb_ref[...], preferred_element_type=F32).astype(o_ref.dtype)
    _store_product(res, o_ref)


def _norm_then_matmul_kernel(x_ref, g_ref, w_ref, cs_ref, o_ref, xn_ref, wb_ref):
    @pl.when(pl.program_id(1) == 0)
    def _():
        wb_ref[...] = (w_ref[...] * cs_ref[...]).astype(BF16)

    x = x_ref[...]
    ms = jnp.mean(x * x, axis=-1, keepdims=True)
    xn = (x * lax.rsqrt(ms + EPS) * g_ref[...]).astype(BF16)
    xn_ref[...] = xn
    res = jnp.dot(xn, wb_ref[...], preferred_element_type=F32).astype(o_ref.dtype)
    _store_product(res, o_ref)


def _matmul(xn, w, colscale, *, col0, n, out_dtype, tm, tn, time_major_batches=None,
            slab_width=None, norm_gain=None, name):
    m, d = xn.shape
    assert m % tm == 0 and n % tn == 0 and col0 % tn == 0
    nj = n // tn
    j0 = col0 // tn
    if slab_width is not None:
        assert time_major_batches is None and tn % slab_width == 0
        per_tile = tn // slab_width
        out_shape = jax.ShapeDtypeStruct((n // slab_width, m, slab_width), out_dtype)
        out_spec = pl.BlockSpec((per_tile, tm, slab_width), lambda j, i: (j, i, 0))
    elif time_major_batches is None:
        out_shape = jax.ShapeDtypeStruct((m, n), out_dtype)
        out_spec = pl.BlockSpec((tm, tn), lambda j, i: (i, j))
    else:
        bsz = time_major_batches
        seqlen = m // bsz
        assert seqlen % tm == 0
        per_b = seqlen // tm
        out_shape = jax.ShapeDtypeStruct((seqlen, bsz * n), out_dtype)
        out_spec = pl.BlockSpec((tm, tn), lambda j, i: (i % per_b, (i // per_b) * nj + j))
    x_spec = pl.BlockSpec((tm, d), lambda j, i: (i, 0))
    w_specs = [pl.BlockSpec((d, tn), lambda j, i: (0, j0 + j), pipeline_mode=pl.Buffered(1)),
               pl.BlockSpec((1, tn), lambda j, i: (0, j0 + j))]
    common = dict(grid=(nj, m // tm), scratch_shapes=[pltpu.VMEM((d, tn), BF16)],
                  compiler_params=_params(("arbitrary", "arbitrary")), name=name)
    cs = colscale.reshape(1, -1).astype(F32)
    if norm_gain is None:
        return pl.pallas_call(
            _matmul_kernel, in_specs=[x_spec] + w_specs, out_specs=out_spec,
            out_shape=out_shape, **common)(xn, w, cs)
    assert nj == 1
    return pl.pallas_call(
        _norm_then_matmul_kernel,
        in_specs=[x_spec, pl.BlockSpec((1, d), lambda j, i: (0, 0))] + w_specs,
        out_specs=[out_spec, x_spec],
        out_shape=[out_shape, jax.ShapeDtypeStruct((m, d), BF16)],
        **common)(xn, norm_gain.reshape(1, d).astype(F32), w, cs)


def _split2(x):
    hi = x.astype(BF16)
    return hi, (x - hi.astype(F32)).astype(BF16)


def _split3(x):
    hi = x.astype(BF16)
    r = x - hi.astype(F32)
    mid = r.astype(BF16)
    return hi, mid, (r - mid.astype(F32)).astype(BF16)


def _fgate_logits(xh, xl, wcat_ref, b_ref):
    both = jnp.dot(xh, wcat_ref[...], preferred_element_type=F32)
    return (both[:, :LANES] + both[:, LANES:]
            + jnp.dot(xl, wcat_ref[:, :LANES], preferred_element_type=F32)) + b_ref[...]


def _fgate_cumsum(logit, carry, heads):
    tl = logit.shape[0]
    logf = jnp.minimum(logit, 0.0) - jnp.log(1.0 + jnp.exp(-jnp.abs(logit)))
    t_idx = lax.broadcasted_iota(jnp.int32, (tl, tl), 0)
    s_idx = lax.broadcasted_iota(jnp.int32, (tl, tl), 1)
    tril = (s_idx <= t_idx).astype(BF16)
    csum = carry
    for part in _split3(logf):
        csum = csum + jnp.dot(tril, part, preferred_element_type=F32)
    lane = lax.broadcasted_iota(jnp.int32, (tl, LANES), 1)
    packed = jnp.zeros((tl, LANES), F32)
    for n, part in reversed(list(enumerate(_split3(csum * LOG2E)))):
        shifted = part.astype(F32) if n == 0 else pltpu.roll(part.astype(F32), n * heads, 1)
        packed = jnp.where(lane < (n + 1) * heads, shifted, packed)
    return packed.astype(BF16), csum[tl - 1:tl, :]


def _mem_attn_kernel(q_ref, zm_ref, kv_ref, o_ref):
    width = MEM_HEADS * HEAD_DIM
    heads = range(MEM_HEADS)
    cols = [slice(h * HEAD_DIM, (h + 1) * HEAD_DIM) for h in heads]
    st = [lax.dot_general(kv_ref[:, cols[h]], q_ref[h], (((1,), (1,)), ((), ())),
                          preferred_element_type=F32) for h in heads]
    vt = [kv_ref[:, width + h * HEAD_DIM:width + (h + 1) * HEAD_DIM].astype(F32).T.astype(BF16)
          for h in heads]
    p = [jnp.exp2(st[h] - jnp.max(st[h], axis=0, keepdims=True)) for h in heads]
    inv = [1.0 / jnp.sum(p[h], axis=0, keepdims=True) for h in heads]
    ot = [jnp.dot(vt[h], p[h].astype(BF16), preferred_element_type=F32) * inv[h] for h in heads]
    for h in heads:
        zm = zm_ref[h].astype(F32)
        o_ref[:, cols[h]] = (ot[h].T * (zm * jax.nn.sigmoid(zm))).astype(o_ref.dtype)


def _mem_attn(qz, kvm, *, bsz, tq, name):
    width = MEM_HEADS * HEAD_DIM
    m = qz.shape[1]
    seqlen = m // bsz
    n_mem = kvm.shape[0] // bsz
    per_b = seqlen // tq
    spec = lambda blk: pl.BlockSpec((MEM_HEADS, tq, HEAD_DIM),
                                    lambda b, i: (blk, b * per_b + i, 0))
    return pl.pallas_call(
        _mem_attn_kernel,
        grid=(bsz, per_b),
        in_specs=[spec(0), spec(1),
                  pl.BlockSpec((n_mem, 2 * width), lambda b, i: (b, 0))],
        out_specs=pl.BlockSpec((tq, width), lambda b, i: (b * per_b + i, 0)),
        out_shape=jax.ShapeDtypeStruct((m, width), BF16),
        compiler_params=_params(("parallel", "arbitrary")),
        name=name,
    )(qz, qz, kvm)


def _s5_kernel(u_ref, winc_ref, woutc_ref, wtc_ref, ar_ref, ai_ref, d_ref, y_ref,
               win_ref, wout_ref, wt_ref, bu_ref, xs_ref, st_ref, *, n_pairs):
    nblk = win_ref.shape[0]

    @pl.when(pl.program_id(1) == 0)
    def _():
        st_ref[...] = jnp.zeros_like(st_ref)
        for k in range(nblk):
            win_ref[k] = _s5_expand(winc_ref[k], SSM_STATE)
            wout_ref[k] = _s5_expand(woutc_ref[k], SSM_STATE)
            wt_ref[k] = _s5_expand(wtc_ref[k], SSM_GROUP)

    rows = n_pairs * SUBLANES
    u0, u1, ucat, a2, state = [], [], [], [], []
    for k in range(nblk):
        upair = u_ref[:, :, k * LANES:(k + 1) * LANES].reshape(n_pairs, 2, SUBLANES, LANES)
        u0.append(upair[:, 0].reshape(rows, LANES))
        u1.append(upair[:, 1].reshape(rows, LANES))
        ucat.append(jnp.concatenate([u0[k], u1[k]], axis=1).astype(BF16))
        bu_ref[k] = jnp.dot(ucat[k], win_ref[k], preferred_element_type=F32)
        a2.append((jnp.broadcast_to(ar_ref[k], (SUBLANES, STATE_COLS)),
                   jnp.broadcast_to(ai_ref[k], (SUBLANES, STATE_COLS))))
        state.append((st_ref[k, 0], st_ref[k, 1]))

    held = [None] * nblk
    for c in range(n_pairs):
        for k in range(nblk):
            xr, xi = state[k]
            cur = jnp.concatenate([xr, xi], axis=1)
            if c % 2 == 1:
                xs_ref[k, (c - 1) * SUBLANES:(c + 1) * SUBLANES, :] = jnp.concatenate(
                    [held[k], cur], axis=0).astype(BF16)
            held[k] = cur
            bur = bu_ref[k, c * SUBLANES:(c + 1) * SUBLANES, 0:STATE_COLS]
            bui = bu_ref[k, c * SUBLANES:(c + 1) * SUBLANES, STATE_COLS:2 * STATE_COLS]
            ar, ai = a2[k]
            state[k] = (ar * xr - ai * xi + bur, ar * xi + ai * xr + bui)

    for k in range(nblk):
        st_ref[k, 0] = state[k][0]
        st_ref[k, 1] = state[k][1]
        y = (lax.dot_general(xs_ref[k], wout_ref[k], (((1,), (1,)), ((), ())),
                             preferred_element_type=F32)
             + jnp.dot(ucat[k], wt_ref[k], preferred_element_type=F32))
        d = d_ref[:, k * LANES:(k + 1) * LANES]
        y0 = jax.nn.gelu(y[:, :LANES] + d * u0[k]).reshape(n_pairs, SUBLANES, LANES)
        y1 = jax.nn.gelu(y[:, LANES:] + d * u1[k]).reshape(n_pairs, SUBLANES, LANES)
        y_ref[:, :, k * LANES:(k + 1) * LANES] = jnp.stack([y0, y1], axis=1).reshape(
            2 * n_pairs, SUBLANES, LANES)


def _s5_discretise(lam_re, lam_im, log_step, b_re, b_im, c_re, c_im):
    groups = lam_re.shape[0]
    nblk = groups // GROUPS_PER_BLOCK
    lr = lam_re.astype(F32)
    li = lam_im.astype(F32)
    dt = jnp.exp(log_step.astype(F32))[:, None]
    mag = jnp.exp(lr * dt)
    ar = mag * jnp.cos(li * dt)
    ai = mag * jnp.sin(li * dt)
    den = lr * lr + li * li
    cr = ((ar - 1.0) * lr + ai * li) / den
    ci = (ai * lr - (ar - 1.0) * li) / den
    br = b_re.astype(F32)
    bi = b_im.astype(F32)
    bbar_re = cr[..., None] * br - ci[..., None] * bi
    bbar_im = cr[..., None] * bi + ci[..., None] * br
    a2r = ar * ar - ai * ai
    a2i = 2.0 * ar * ai
    ab_re = ar[..., None] * bbar_re - ai[..., None] * bbar_im
    ab_im = ar[..., None] * bbar_im + ai[..., None] * bbar_re
    cre = c_re.astype(F32)
    cim = c_im.astype(F32)
    ca_re = cre * ar[:, None, :] - cim * ai[:, None, :]
    ca_im = cre * ai[:, None, :] + cim * ar[:, None, :]
    ca2_re = cre * a2r[:, None, :] - cim * a2i[:, None, :]
    ca2_im = cre * a2i[:, None, :] + cim * a2r[:, None, :]
    k0 = jnp.einsum('ghp,gpk->ghk', cre, bbar_re) - jnp.einsum('ghp,gpk->ghk', cim, bbar_im)
    k1 = jnp.einsum('ghp,gpk->ghk', cre, ab_re) - jnp.einsum('ghp,gpk->ghk', cim, ab_im)

    def per_block(x):
        return x.reshape((nblk, GROUPS_PER_BLOCK) + x.shape[1:])

    b_in = jnp.stack([jnp.stack([per_block(ab_re), per_block(ab_im)]),
                      jnp.stack([per_block(bbar_re), per_block(bbar_im)])])
    win_c = b_in.transpose(2, 0, 5, 1, 3, 4).reshape(nblk, 2 * SSM_GROUP, 2 * STATE_COLS)
    c_out = jnp.stack([jnp.stack([per_block(ca_re), per_block(ca2_re)]),
                       jnp.stack([-per_block(ca_im), -per_block(ca2_im)])])
    wout_c = c_out.transpose(2, 1, 4, 0, 3, 5).reshape(nblk, 2 * SSM_GROUP, 2 * STATE_COLS)
    k0b, k1b = per_block(k0), per_block(k1)
    k_mix = jnp.stack([jnp.stack([k0b, k1b]),
                       jnp.stack([jnp.zeros_like(k0b), k0b])])
    wt_c = k_mix.transpose(2, 0, 5, 1, 3, 4).reshape(nblk, 2 * SSM_GROUP, 2 * LANES)
    return (win_c, wout_c, wt_c,
            a2r.reshape(nblk, 1, STATE_COLS), a2i.reshape(nblk, 1, STATE_COLS))


def _s5_expand(compact, cols_per_group):
    width = compact.shape[1]
    col_group = (lax.broadcasted_iota(jnp.int32, (SSM_GROUP, width), 1)
                 // cols_per_group) % GROUPS_PER_BLOCK
    pieces = []
    for i in range(2):
        rows = compact[i * SSM_GROUP:(i + 1) * SSM_GROUP, :]
        for g in range(GROUPS_PER_BLOCK):
            pieces.append(jnp.where(col_group == g, rows, 0.0).astype(BF16))
    return jnp.concatenate(pieces, axis=0)


def _s5(u_tm, win, wout, wt, a2r, a2i, d_skip, *, bsz, n_pairs, blocks_per_step):
    seqlen = u_tm.shape[0]
    width = u_tm.shape[1] // bsz
    assert bsz == SUBLANES and seqlen % (2 * n_pairs) == 0 and width % LANES == 0
    nblk = width // LANES
    assert nblk % blocks_per_step == 0 and n_pairs % 2 == 0
    nb = blocks_per_step
    rows = n_pairs * SUBLANES
    u3 = u_tm.reshape(seqlen, bsz, width)
    blk = pl.BlockSpec((2 * n_pairs, bsz, nb * LANES), lambda j, c: (c, 0, j))
    per_blk = lambda shape: pl.BlockSpec((nb,) + shape, lambda j, c: (j, 0, 0))
    out = pl.pallas_call(
        functools.partial(_s5_kernel, n_pairs=n_pairs),
        grid=(nblk // nb, seqlen // (2 * n_pairs)),
        in_specs=[blk,
                  per_blk((2 * SSM_GROUP, 2 * STATE_COLS)),
                  per_blk((2 * SSM_GROUP, 2 * STATE_COLS)),
                  per_blk((2 * SSM_GROUP, 2 * LANES)),
                  per_blk((1, STATE_COLS)),
                  per_blk((1, STATE_COLS)),
                  pl.BlockSpec((1, nb * LANES), lambda j, c: (0, j))],
        out_specs=blk,
        out_shape=jax.ShapeDtypeStruct((seqlen, bsz, width), F32),
        scratch_shapes=[pltpu.VMEM((nb, 2 * LANES, 2 * STATE_COLS), BF16),
                        pltpu.VMEM((nb, 2 * LANES, 2 * STATE_COLS), BF16),
                        pltpu.VMEM((nb, 2 * LANES, 2 * LANES), BF16),
                        pltpu.VMEM((nb, rows, 2 * STATE_COLS), F32),
                        pltpu.VMEM((nb, rows, 2 * STATE_COLS), BF16),
                        pltpu.VMEM((nb, 2, SUBLANES, STATE_COLS), F32)],
        compiler_params=_params(("arbitrary", "arbitrary")),
        name="s5_scan",
    )(u3, win, wout, wt, a2r, a2i, d_skip.reshape(1, width).astype(F32))
    return out.reshape(seqlen, bsz * width)


def _glu_out_kernel(yg_ref, z_ref, memo_ref, h_ref, wglu_ref, bglu_ref,
                    wmain_ref, wmem_ref, g_ref, gkv_ref, gnext_ref, wf_ref, bf_ref,
                    out_ref, xkv_ref, xnext_ref, fp_ref, carry_ref, *, heads, sub):
    @pl.when(pl.program_id(1) == 0)
    def _():
        carry_ref[...] = jnp.zeros_like(carry_ref)

    subs = [slice(s * sub, (s + 1) * sub) for s in range(out_ref.shape[0] // sub)]
    t = [jnp.dot(yg_ref[rows, :].astype(BF16), wglu_ref[...], preferred_element_type=F32)
         + bglu_ref[...] for rows in subs]
    for rows, tt in zip(subs, t):
        z = z_ref[rows, :].astype(F32)
        main = yg_ref[rows, :] * jax.nn.sigmoid(tt) * (z * jax.nn.sigmoid(z))
        o = jnp.dot(main.astype(BF16), wmain_ref[...], preferred_element_type=F32)
        o = o + jnp.dot(memo_ref[rows, :], wmem_ref[...], preferred_element_type=F32)
        ms = jnp.mean(o * o, axis=-1, keepdims=True)
        out_ref[rows, :] = h_ref[rows, :] + o * lax.rsqrt(ms + EPS) * g_ref[...]
    split = []
    for rows in subs:
        h1 = out_ref[rows, :]
        r = h1 * lax.rsqrt(jnp.mean(h1 * h1, axis=-1, keepdims=True) + EPS)
        xnext_ref[rows, :] = (r * gnext_ref[...]).astype(BF16)
        split.append(_split2(r * gkv_ref[...]))
        xkv_ref[rows, :] = split[-1][0]
    logit = [_fgate_logits(xh, xl, wf_ref, bf_ref) for xh, xl in split]
    for rows, lg in zip(subs, logit):
        fp_ref[rows, :], carry_ref[...] = _fgate_cumsum(lg, carry_ref[...], heads)


def _out_kernel(main_ref, memo_ref, h_ref, wmain_ref, wmem_ref, g_ref, out_ref, *, sub):
    subs = [slice(s * sub, (s + 1) * sub) for s in range(out_ref.shape[0] // sub)]
    o = []
    for rows in subs:
        main = jnp.concatenate([main_ref[hd, rows, :] for hd in range(main_ref.shape[0])],
                               axis=1)
        o.append(jnp.dot(main, wmain_ref[...], preferred_element_type=F32)
                 + jnp.dot(memo_ref[rows, :], wmem_ref[...], preferred_element_type=F32))
    for rows, oo in zip(subs, o):
        ms = jnp.mean(oo * oo, axis=-1, keepdims=True)
        out_ref[rows, :] = h_ref[rows, :] + oo * lax.rsqrt(ms + EPS) * g_ref[...]


def _resident(shape):
    return pl.BlockSpec(shape, lambda *_: (0,) * len(shape), pipeline_mode=pl.Buffered(1))


def _w_out_specs(layer, main_w, mem_w, d):
    assert main_w % mem_w == 0
    return [pl.BlockSpec((None, main_w, d), lambda *_: (layer, 0, 0),
                         pipeline_mode=pl.Buffered(1)),
            pl.BlockSpec((None, mem_w, d), lambda *_: (layer, main_w // mem_w, 0),
                         pipeline_mode=pl.Buffered(1))]


def _glu_out(yg_tm, rest, memo, h2d, w_glu, b_glu, w_out_bf, layer, g, g_kv, g_next,
             w_fgate, b_fgate, *, bsz, tl, sub):
    assert tl % sub == 0
    m, d = h2d.shape
    seqlen = m // bsz
    main_w = w_glu.shape[0]
    mem_w = w_out_bf.shape[1] - main_w
    heads = w_fgate.shape[1]
    assert 3 * heads <= LANES and seqlen % tl == 0
    wpad = jnp.zeros((d, LANES), F32).at[:, :heads].set(w_fgate.astype(F32))
    wcat = jnp.concatenate(_split2(wpad), axis=1)
    bpad = jnp.zeros((1, LANES), F32).at[0, :heads].set(b_fgate.astype(F32))
    per_b = seqlen // tl
    row = lambda b, i: (b * per_b + i, 0)
    vec = lambda v: v.reshape(1, -1).astype(F32)
    return pl.pallas_call(
        functools.partial(_glu_out_kernel, heads=heads, sub=sub),
        grid=(bsz, per_b),
        in_specs=[pl.BlockSpec((tl, main_w), lambda b, i: (i, b)),
                  pl.BlockSpec((tl, main_w), row),
                  pl.BlockSpec((tl, mem_w), row),
                  pl.BlockSpec((tl, d), row),
                  _resident((main_w, main_w)),
                  _resident((1, main_w))]
                 + _w_out_specs(layer, main_w, mem_w, d)
                 + [_resident((1, d)),
                    _resident((1, d)),
                    _resident((1, d)),
                    _resident((d, 2 * LANES)),
                    _resident((1, LANES))],
        out_specs=[pl.BlockSpec((tl, d), row),
                   pl.BlockSpec((tl, d), row),
                   pl.BlockSpec((tl, d), row),
                   pl.BlockSpec((tl, LANES), row)],
        out_shape=[jax.ShapeDtypeStruct((m, d), F32),
                   jax.ShapeDtypeStruct((m, d), BF16),
                   jax.ShapeDtypeStruct((m, d), BF16),
                   jax.ShapeDtypeStruct((m, LANES), BF16)],
        scratch_shapes=[pltpu.VMEM((1, LANES), F32)],
        compiler_params=_params(("arbitrary", "arbitrary")),
        name="glu_out_proj",
    )(yg_tm, rest, memo, h2d, w_glu.astype(BF16), vec(b_glu), w_out_bf, w_out_bf,
      vec(g), vec(g_kv), vec(g_next), wcat, bpad)


def _out_proj(main, memo, h2d, w_out_bf, layer, g, *, tl, sub):
    m, d = h2d.shape
    heads, _, dh = main.shape
    main_w = heads * dh
    mem_w = memo.shape[1]
    row = lambda i: (i, 0)
    return pl.pallas_call(
        functools.partial(_out_kernel, sub=sub),
        grid=(m // tl,),
        in_specs=[pl.BlockSpec((heads, tl, dh), lambda i: (0, i, 0)),
                  pl.BlockSpec((tl, mem_w), row),
                  pl.BlockSpec((tl, d), row)]
                 + _w_out_specs(layer, main_w, mem_w, d)
                 + [_resident((1, d))],
        out_specs=pl.BlockSpec((tl, d), row),
        out_shape=jax.ShapeDtypeStruct((m, d), F32),
        compiler_params=_params(("parallel",)),
        name="out_proj",
    )(main, memo, h2d, w_out_bf, w_out_bf, g.reshape(1, d).astype(F32))


FOX_CHUNK = 64


def _fox_kernel(q_ref, k_ref, v_ref, fp_ref, sel_ref, cst_ref, z_ref, o_ref,
                ka_ref, qa_ref, vt_ref, st_ref, pt_ref, m_ref, l_ref, acc_ref, *, tq):
    nh, seqlen, _ = q_ref.shape
    ncg = tq // LANES
    nch = tq // FOX_CHUNK
    for hh in range(nh):
        ext = jnp.dot(fp_ref[...], sel_ref[hh], preferred_element_type=F32)
        ka_ref[hh, :, :HEAD_DIM] = k_ref[hh]
        ka_ref[hh, :, HEAD_DIM:] = (ext[:, :HEAD_DIM] + cst_ref[0:1, :]).astype(BF16)
        qa_ref[hh, :, :HEAD_DIM] = q_ref[hh]
        qa_ref[hh, :, HEAD_DIM:] = (ext[:, HEAD_DIM:] + cst_ref[1:2, :]).astype(BF16)
        vt_ref[hh] = v_ref[hh].astype(F32).T.astype(BF16)
        for ci in range(nch):
            for g in range(ncg):
                if g * LANES + LANES - 1 < ci * FOX_CHUNK:
                    pt_ref[hh, 1, ci * FOX_CHUNK:(ci + 1) * FOX_CHUNK,
                           g * LANES:(g + 1) * LANES] = jnp.zeros((FOX_CHUNK, LANES), BF16)

    half = tq // 2

    def scores(hh, qi, kj, slot):
        nt = (((1,), (1,)), ((), ()))
        k0, q0 = kj * tq, qi * tq
        if kj < qi:
            st_ref[hh, slot] = lax.dot_general(ka_ref[hh, k0:k0 + tq, :], qa_ref[hh, q0:q0 + tq, :],
                                               nt, preferred_element_type=F32)
        else:
            st_ref[hh, slot, :half, :] = lax.dot_general(
                ka_ref[hh, k0:k0 + half, :], qa_ref[hh, q0:q0 + tq, :], nt,
                preferred_element_type=F32)
            st_ref[hh, slot, half:, half:] = lax.dot_general(
                ka_ref[hh, k0 + half:k0 + tq, :], qa_ref[hh, q0 + half:q0 + tq, :], nt,
                preferred_element_type=F32)

    def piece(hh, slot, ci, g, masked):
        r0, c0 = ci * FOX_CHUNK, g * LANES
        if masked and c0 + LANES - 1 < r0:
            return None
        x = st_ref[hh, slot, r0:r0 + FOX_CHUNK, c0:c0 + LANES]
        if masked and c0 < r0 + FOX_CHUNK - 1:
            key = r0 + lax.broadcasted_iota(jnp.int32, (FOX_CHUNK, LANES), 0)
            qry = c0 + lax.broadcasted_iota(jnp.int32, (FOX_CHUNK, LANES), 1)
            x = jnp.where(key <= qry, x, -jnp.inf)
        return x

    def fold(x, op):
        return op(x.reshape(FOX_CHUNK // SUBLANES, SUBLANES, LANES), axis=0)

    def softmax_pv(hh, kj, slot, masked, first):
        mx = [None] * ncg
        for ci in range(nch):
            for g in range(ncg):
                x = piece(hh, slot, ci, g, masked)
                if x is not None:
                    f = fold(x, jnp.max)
                    mx[g] = f if mx[g] is None else jnp.maximum(mx[g], f)
        m_new = jnp.concatenate([jnp.max(v, axis=0, keepdims=True) for v in mx], axis=1)
        if not first:
            m_old = m_ref[hh]
            m_new = jnp.maximum(m_old, m_new)
            alpha = jnp.exp2(m_old - m_new)
        m_ref[hh] = m_new
        ls = [None] * ncg
        for ci in range(nch):
            for g in range(ncg):
                r0, c0 = ci * FOX_CHUNK, g * LANES
                x = piece(hh, slot, ci, g, masked)
                if x is None:
                    continue
                p = jnp.exp2(x - m_new[:, c0:c0 + LANES])
                f = fold(p, jnp.sum)
                ls[g] = f if ls[g] is None else ls[g] + f
                pt_ref[hh, int(masked), r0:r0 + FOX_CHUNK, c0:c0 + LANES] = p.astype(BF16)
        l_new = jnp.concatenate([jnp.sum(v, axis=0, keepdims=True) for v in ls], axis=1)
        k0 = kj * tq
        if masked:
            pv = jnp.dot(vt_ref[hh, :, k0:k0 + half], pt_ref[hh, 1, :half, :],
                         preferred_element_type=F32)
            late = jnp.dot(vt_ref[hh, :, k0 + half:k0 + tq], pt_ref[hh, 1, half:, half:],
                           preferred_element_type=F32)
            pv = jnp.concatenate([pv[:, :half], pv[:, half:] + late], axis=1)
        else:
            pv = jnp.dot(vt_ref[hh, :, k0:k0 + tq], pt_ref[hh, 0], preferred_element_type=F32)
        if first:
            l_ref[hh] = l_new
            acc_ref[hh] = pv
        else:
            l_ref[hh] = alpha * l_ref[hh] + l_new
            acc_ref[hh] = alpha * acc_ref[hh] + pv

    for qi in range(seqlen // tq):
        rows = slice(qi * tq, (qi + 1) * tq)
        for hh in range(nh):
            scores(hh, qi, 0, 0)
        for kj in range(qi + 1):
            for hh in range(nh):
                if kj < qi:
                    scores(hh, qi, kj + 1, (kj + 1) % 2)
                softmax_pv(hh, kj, kj % 2, masked=(kj == qi), first=(kj == 0))
        for hh in range(nh):
            z = z_ref[hh, rows, :].astype(F32)
            o = (acc_ref[hh] / l_ref[hh]).T
            o_ref[hh, rows, :] = (o * (z * jax.nn.sigmoid(z))).astype(o_ref.dtype)


def _fox_attn(q, k, v, z, fparts, *, bsz, tq, heads_per_step):
    heads, m, _ = q.shape
    seqlen = m // bsz
    nterms = 3
    assert seqlen % tq == 0 and tq % FOX_CHUNK == 0 and nterms * heads <= LANES
    h_idx = jnp.arange(heads)[:, None, None]
    r_idx = jnp.arange(LANES)[None, :, None]
    c_idx = jnp.arange(2 * HEAD_DIM)[None, None, :]
    key_side = (c_idx < nterms) & (r_idx == c_idx * heads + h_idx)
    qc = c_idx - HEAD_DIM - nterms
    qry_side = (qc >= 0) & (qc < nterms) & (r_idx == qc * heads + h_idx)
    sel = (qry_side.astype(F32) - key_side.astype(F32)).astype(BF16)
    col = jnp.arange(HEAD_DIM)
    cst = jnp.zeros((SUBLANES, HEAD_DIM), F32)
    cst = cst.at[0].set(((col >= nterms) & (col < 2 * nterms)).astype(F32))
    cst = cst.at[1].set((col < nterms).astype(F32))
    nh = heads_per_step
    assert heads % nh == 0
    head_seq = pl.BlockSpec((nh, seqlen, HEAD_DIM), lambda b, h: (h, b, 0))
    return pl.pallas_call(
        functools.partial(_fox_kernel, tq=tq),
        grid=(bsz, heads // nh),
        in_specs=[head_seq, head_seq, head_seq,
                  pl.BlockSpec((seqlen, LANES), lambda b, h: (b, 0)),
                  pl.BlockSpec((nh, LANES, 2 * HEAD_DIM), lambda b, h: (h, 0, 0)),
                  pl.BlockSpec((SUBLANES, HEAD_DIM), lambda b, h: (0, 0)),
                  head_seq],
        out_specs=head_seq,
        out_shape=jax.ShapeDtypeStruct((heads, m, HEAD_DIM), BF16),
        scratch_shapes=[pltpu.VMEM((nh, seqlen, 2 * HEAD_DIM), BF16),
                        pltpu.VMEM((nh, seqlen, 2 * HEAD_DIM), BF16),
                        pltpu.VMEM((nh, HEAD_DIM, seqlen), BF16),
                        pltpu.VMEM((nh, 2, tq, tq), F32),
                        pltpu.VMEM((nh, 2, tq, tq), BF16),
                        pltpu.VMEM((nh, 1, tq), F32),
                        pltpu.VMEM((nh, 1, tq), F32),
                        pltpu.VMEM((nh, HEAD_DIM, tq), F32)],
        compiler_params=_params(("arbitrary", "arbitrary")),
        name="fox_attn",
    )(q, k, v, fparts, sel, cst, z)


def kernel(x, mem, pre_norm_g, post_norm_g, w_in_a, lam_re, lam_im, log_step, b_re, b_im,
           c_re, c_im, d_skip, w_glu, b_glu, kv_norm_g, w_kv, w_fgate, b_fgate, w_in_b,
           mem_norm_g, w_mem_kv, w_out):
    bsz, seqlen, d = x.shape
    n_mem = mem.shape[1]
    main_w = w_glu.shape[1]
    mem_w = w_out.shape[1] - main_w
    scale = HEAD_DIM ** -0.5
    x2d = x.reshape(bsz * seqlen, d)
    mem2d = mem.reshape(bsz * n_mem, d)

    ones = jnp.ones((main_w,), F32)
    mem_scale = jnp.full((mem_w,), scale * LOG2E, F32)
    cs_a = jnp.concatenate([ones, ones, mem_scale, ones[:mem_w]])
    cs_b = jnp.concatenate([ones * (scale * LOG2E), ones, mem_scale, ones[:mem_w]])
    proj = functools.partial(_matmul, out_dtype=BF16, tm=1024)
    u_tm, xa = _matmul(x2d, w_in_a[0], cs_a, col0=0, n=main_w, out_dtype=F32, tm=512,
                       tn=main_w, time_major_batches=bsz, norm_gain=pre_norm_g[0],
                       name="in_proj_a_u")
    z_a = proj(xa, w_in_a[0], cs_a, col0=main_w, n=main_w, tn=main_w, name="in_proj_a_z")
    qz_a = proj(xa, w_in_a[0], cs_a, col0=2 * main_w, n=2 * mem_w, tn=2 * mem_w,
                slab_width=HEAD_DIM, name="in_proj_a_mem")
    kvm0 = _norm_matmul(mem2d, mem_norm_g[0], w_mem_kv[0], out_dtype=BF16,
                        tm=512, tn=512, name="mem_kv0")
    memo0 = _mem_attn(qz_a, kvm0, bsz=bsz, tq=1024, name="mem_attn0")
    s5_ops = _s5_discretise(lam_re[0], lam_im[0], log_step[0], b_re[0], b_im[0],
                            c_re[0], c_im[0])
    yg_tm = _s5(u_tm, *s5_ops, d_skip[0], bsz=bsz, n_pairs=128, blocks_per_step=3)
    w_out_bf = w_out.astype(BF16)
    h1, xkv, xb, fparts = _glu_out(
        yg_tm, z_a, memo0, x2d, w_glu[0], b_glu[0], w_out_bf, 0, post_norm_g[0],
        kv_norm_g, pre_norm_g[1], w_fgate, b_fgate, bsz=bsz, tl=512, sub=256)

    cs_kv = jnp.ones((w_kv.shape[1],), F32)
    k_sh = proj(xkv, w_kv, cs_kv, col0=0, n=main_w, tn=main_w, slab_width=HEAD_DIM,
                name="k_proj")
    v_sh = proj(xkv, w_kv, cs_kv, col0=main_w, n=main_w, tn=main_w, slab_width=HEAD_DIM,
                name="v_proj")

    q_b = proj(xb, w_in_b[0], cs_b, col0=0, n=main_w, tn=main_w, slab_width=HEAD_DIM,
               name="in_proj_b_q")
    z_b = proj(xb, w_in_b[0], cs_b, col0=main_w, n=main_w, tn=main_w, slab_width=HEAD_DIM,
               name="in_proj_b_z")
    qz_b = proj(xb, w_in_b[0], cs_b, col0=2 * main_w, n=2 * mem_w, tn=2 * mem_w,
                slab_width=HEAD_DIM, name="in_proj_b_mem")
    kvm1 = _norm_matmul(mem2d, mem_norm_g[1], w_mem_kv[1], out_dtype=BF16,
                        tm=512, tn=512, name="mem_kv1")
    memo1 = _mem_attn(qz_b, kvm1, bsz=bsz, tq=1024, name="mem_attn1")
    att = _fox_attn(q_b, k_sh, v_sh, z_b, fparts, bsz=bsz, tq=512, heads_per_step=3)
    out = _out_proj(att, memo1, h1, w_out_bf, 1, post_norm_g[1], tl=512, sub=256)
    return out.reshape(bsz, seqlen, d)
```

```python
import functools

import jax
import jax.numpy as jnp
from jax import lax
from jax.experimental import pallas as pl
from jax.experimental.pallas import tpu as pltpu

F32 = jnp.float32
BF16 = jnp.bfloat16

EPS = 1e-6
LOG2E = 1.4426950408889634
HEAD_DIM = 128
SSM_GROUP = 16
SSM_STATE = 64
MEM_HEADS = 4
LANES = 128
SUBLANES = 8
GROUPS_PER_BLOCK = LANES // SSM_GROUP
STATE_COLS = GROUPS_PER_BLOCK * SSM_STATE
VMEM_LIMIT = 56 * 1024 * 1024


def _params(sem, vmem=VMEM_LIMIT):
    return pltpu.CompilerParams(dimension_semantics=sem, vmem_limit_bytes=vmem)


def _norm_matmul_kernel(x_ref, g_ref, w_ref, o_ref, xn_ref):
    @pl.when(pl.program_id(1) == 0)
    def _():
        x = x_ref[...]
        ms = jnp.mean(x * x, axis=-1, keepdims=True)
        xn_ref[...] = (x * lax.rsqrt(ms + EPS) * g_ref[...]).astype(BF16)

    o_ref[...] = jnp.dot(xn_ref[...], w_ref[...].astype(BF16),
                         preferred_element_type=F32).astype(o_ref.dtype)


def _norm_matmul(x2d, g, w, *, out_dtype, tm, tn, name):
    m, d = x2d.shape
    n = w.shape[1]
    assert m % tm == 0 and n % tn == 0
    return pl.pallas_call(
        _norm_matmul_kernel,
        grid=(m // tm, n // tn),
        in_specs=[pl.BlockSpec((tm, d), lambda i, j: (i, 0)),
                  pl.BlockSpec((1, d), lambda i, j: (0, 0)),
                  pl.BlockSpec((d, tn), lambda i, j: (0, j))],
        out_specs=pl.BlockSpec((tm, tn), lambda i, j: (i, j)),
        out_shape=jax.ShapeDtypeStruct((m, n), out_dtype),
        scratch_shapes=[pltpu.VMEM((tm, d), BF16)],
        compiler_params=_params(("parallel", "arbitrary")),
        name=name,
    )(x2d, g.reshape(1, d).astype(F32), w)


def _store_product(res, o_ref):
    if len(o_ref.shape) == 3:
        sw = o_ref.shape[2]
        for s in range(o_ref.shape[0]):
            o_ref[s] = res[:, s * sw:(s + 1) * sw]
    else:
        o_ref[...] = res


def _matmul_kernel(x_ref, w_ref, cs_ref, o_ref, wb_ref):
    @pl.when(pl.program_id(1) == 0)
    def _():
        wb_ref[...] = (w_ref[...] * cs_ref[...]).astype(BF16)

    res = jnp.dot(x_ref[...], wb_ref[...], preferred_element_type=F32).astype(o_ref.dtype)
    _store_product(res, o_ref)


def _norm_then_matmul_kernel(x_ref, g_ref, w_ref, cs_ref, o_ref, xn_ref, wb_ref):
    @pl.when(pl.program_id(1) == 0)
    def _():
        wb_ref[...] = (w_ref[...] * cs_ref[...]).astype(BF16)

    half = x_ref.shape[0] // 2
    xn = []
    for rows in (slice(0, half), slice(half, 2 * half)):
        x = x_ref[rows, :]
        ms = jnp.mean(x * x, axis=-1, keepdims=True)
        xn.append((x * lax.rsqrt(ms + EPS) * g_ref[...]).astype(BF16))
        xn_ref[rows, :] = xn[-1]
    res = jnp.concatenate(
        [jnp.dot(v, wb_ref[...], preferred_element_type=F32).astype(o_ref.dtype) for v in xn],
        axis=0)
    _store_product(res, o_ref)


def _matmul(xn, w, colscale, *, col0, n, out_dtype, tm, tn, time_major_batches=None,
            slab_width=None, norm_gain=None, name):
    m, d = xn.shape
    assert m % tm == 0 and n % tn == 0 and col0 % tn == 0
    nj = n // tn
    j0 = col0 // tn
    if slab_width is not None:
        assert time_major_batches is None and tn % slab_width == 0
        per_tile = tn // slab_width
        out_shape = jax.ShapeDtypeStruct((n // slab_width, m, slab_width), out_dtype)
        out_spec = pl.BlockSpec((per_tile, tm, slab_width), lambda j, i: (j, i, 0))
    elif time_major_batches is None:
        out_shape = jax.ShapeDtypeStruct((m, n), out_dtype)
        out_spec = pl.BlockSpec((tm, tn), lambda j, i: (i, j))
    else:
        bsz = time_major_batches
        seqlen = m // bsz
        assert seqlen % tm == 0
        per_b = seqlen // tm
        out_shape = jax.ShapeDtypeStruct((seqlen, bsz * n), out_dtype)
        out_spec = pl.BlockSpec((tm, tn), lambda j, i: (i % per_b, (i // per_b) * nj + j))
    x_spec = pl.BlockSpec((tm, d), lambda j, i: (i, 0))
    w_specs = [pl.BlockSpec((d, tn), lambda j, i: (0, j0 + j), pipeline_mode=pl.Buffered(1)),
               pl.BlockSpec((1, tn), lambda j, i: (0, j0 + j))]
    common = dict(grid=(nj, m // tm), scratch_shapes=[pltpu.VMEM((d, tn), BF16)],
                  compiler_params=_params(("arbitrary", "arbitrary")), name=name)
    cs = colscale.reshape(1, -1).astype(F32)
    if norm_gain is None:
        return pl.pallas_call(
            _matmul_kernel, in_specs=[x_spec] + w_specs, out_specs=out_spec,
            out_shape=out_shape, **common)(xn, w, cs)
    assert nj == 1
    return pl.pallas_call(
        _norm_then_matmul_kernel,
        in_specs=[x_spec, pl.BlockSpec((1, d), lambda j, i: (0, 0))] + w_specs,
        out_specs=[out_spec, x_spec],
        out_shape=[out_shape, jax.ShapeDtypeStruct((m, d), BF16)],
        **common)(xn, norm_gain.reshape(1, d).astype(F32), w, cs)


def _split2(x):
    hi = x.astype(BF16)
    return hi, (x - hi.astype(F32)).astype(BF16)


def _split3(x):
    hi = x.astype(BF16)
    r = x - hi.astype(F32)
    mid = r.astype(BF16)
    return hi, mid, (r - mid.astype(F32)).astype(BF16)


def _fgate_logits(xh, xl, wcat_ref, b_ref):
    both = jnp.dot(xh, wcat_ref[...], preferred_element_type=F32)
    return (both[:, :LANES] + both[:, LANES:]
            + jnp.dot(xl, wcat_ref[:, :LANES], preferred_element_type=F32)) + b_ref[...]


def _fgate_cumsum(logit, carry, heads):
    tl = logit.shape[0]
    logf = jnp.minimum(logit, 0.0) - jnp.log(1.0 + jnp.exp(-jnp.abs(logit)))
    t_idx = lax.broadcasted_iota(jnp.int32, (tl, tl), 0)
    s_idx = lax.broadcasted_iota(jnp.int32, (tl, tl), 1)
    tril = (s_idx <= t_idx).astype(BF16)
    csum = carry
    for part in _split3(logf):
        csum = csum + jnp.dot(tril, part, preferred_element_type=F32)
    lane = lax.broadcasted_iota(jnp.int32, (tl, LANES), 1)
    packed = jnp.zeros((tl, LANES), F32)
    for n, part in reversed(list(enumerate(_split3(csum * LOG2E)))):
        shifted = part.astype(F32) if n == 0 else pltpu.roll(part.astype(F32), n * heads, 1)
        packed = jnp.where(lane < (n + 1) * heads, shifted, packed)
    return packed.astype(BF16), csum[tl - 1:tl, :]


def _mem_attn_kernel(q_ref, zm_ref, kv_ref, o_ref):
    width = MEM_HEADS * HEAD_DIM
    heads = range(MEM_HEADS)
    cols = [slice(h * HEAD_DIM, (h + 1) * HEAD_DIM) for h in heads]
    st = [lax.dot_general(kv_ref[:, cols[h]], q_ref[h], (((1,), (1,)), ((), ())),
                          preferred_element_type=F32) for h in heads]
    vt = [kv_ref[:, width + h * HEAD_DIM:width + (h + 1) * HEAD_DIM].astype(F32).T.astype(BF16)
          for h in heads]
    p = [jnp.exp2(st[h] - jnp.max(st[h], axis=0, keepdims=True)) for h in heads]
    inv = [1.0 / jnp.sum(p[h], axis=0, keepdims=True) for h in heads]
    ot = [jnp.dot(vt[h], p[h].astype(BF16), preferred_element_type=F32) * inv[h] for h in heads]
    for h in heads:
        zm = zm_ref[h].astype(F32)
        o_ref[:, cols[h]] = (ot[h].T * (zm * jax.nn.sigmoid(zm))).astype(o_ref.dtype)


def _mem_attn(qz, kvm, *, bsz, tq, name):
    width = MEM_HEADS * HEAD_DIM
    m = qz.shape[1]
    seqlen = m // bsz
    n_mem = kvm.shape[0] // bsz
    per_b = seqlen // tq
    spec = lambda blk: pl.BlockSpec((MEM_HEADS, tq, HEAD_DIM),
                                    lambda b, i: (blk, b * per_b + i, 0))
    return pl.pallas_call(
        _mem_attn_kernel,
        grid=(bsz, per_b),
        in_specs=[spec(0), spec(1),
                  pl.BlockSpec((n_mem, 2 * width), lambda b, i: (b, 0))],
        out_specs=pl.BlockSpec((tq, width), lambda b, i: (b * per_b + i, 0)),
        out_shape=jax.ShapeDtypeStruct((m, width), BF16),
        compiler_params=_params(("parallel", "arbitrary")),
        name=name,
    )(qz, qz, kvm)


def _s5_kernel(u_ref, winc_ref, woutc_ref, wtc_ref, ar_ref, ai_ref, d_ref, y_ref,
               win_ref, wout_ref, wt_ref, bu_ref, xs_ref, st_ref, *, n_pairs):
    nblk = win_ref.shape[0]

    @pl.when(pl.program_id(1) == 0)
    def _():
        st_ref[...] = jnp.zeros_like(st_ref)
        for k in range(nblk):
            win_ref[k] = _s5_expand(winc_ref[k], SSM_STATE)
            wout_ref[k] = _s5_expand(woutc_ref[k], SSM_STATE)
            wt_ref[k] = _s5_expand(wtc_ref[k], SSM_GROUP)

    rows = n_pairs * SUBLANES
    u0, u1, ucat, a2, state = [], [], [], [], []
    for k in range(nblk):
        upair = u_ref[:, :, k * LANES:(k + 1) * LANES].reshape(n_pairs, 2, SUBLANES, LANES)
        u0.append(upair[:, 0].reshape(rows, LANES))
        u1.append(upair[:, 1].reshape(rows, LANES))
        ucat.append(jnp.concatenate([u0[k], u1[k]], axis=1).astype(BF16))
        bu_ref[k] = jnp.dot(ucat[k], win_ref[k], preferred_element_type=F32)
        a2.append((jnp.broadcast_to(ar_ref[k], (SUBLANES, STATE_COLS)),
                   jnp.broadcast_to(ai_ref[k], (SUBLANES, STATE_COLS))))
        state.append((st_ref[k, 0], st_ref[k, 1]))

    held = [None] * nblk
    for c in range(n_pairs):
        for k in range(nblk):
            xr, xi = state[k]
            cur = jnp.concatenate([xr, xi], axis=1)
            if c % 2 == 1:
                xs_ref[k, (c - 1) * SUBLANES:(c + 1) * SUBLANES, :] = jnp.concatenate(
                    [held[k], cur], axis=0).astype(BF16)
            held[k] = cur
            bur = bu_ref[k, c * SUBLANES:(c + 1) * SUBLANES, 0:STATE_COLS]
            bui = bu_ref[k, c * SUBLANES:(c + 1) * SUBLANES, STATE_COLS:2 * STATE_COLS]
            ar, ai = a2[k]
            state[k] = (ar * xr - ai * xi + bur, ar * xi + ai * xr + bui)

    for k in range(nblk):
        st_ref[k, 0] = state[k][0]
        st_ref[k, 1] = state[k][1]
    ys = [lax.dot_general(xs_ref[k], wout_ref[k], (((1,), (1,)), ((), ())),
                          preferred_element_type=F32)
          + jnp.dot(ucat[k], wt_ref[k], preferred_element_type=F32) for k in range(nblk)]
    for k in range(nblk):
        y = ys[k]
        d = d_ref[:, k * LANES:(k + 1) * LANES]
        y0 = jax.nn.gelu(y[:, :LANES] + d * u0[k]).reshape(n_pairs, SUBLANES, LANES)
        y1 = jax.nn.gelu(y[:, LANES:] + d * u1[k]).reshape(n_pairs, SUBLANES, LANES)
        y_ref[:, :, k * LANES:(k + 1) * LANES] = jnp.stack([y0, y1], axis=1).reshape(
            2 * n_pairs, SUBLANES, LANES)


def _s5_discretise(lam_re, lam_im, log_step, b_re, b_im, c_re, c_im):
    groups = lam_re.shape[0]
    nblk = groups // GROUPS_PER_BLOCK
    lr = lam_re.astype(F32)
    li = lam_im.astype(F32)
    dt = jnp.exp(log_step.astype(F32))[:, None]
    mag = jnp.exp(lr * dt)
    ar = mag * jnp.cos(li * dt)
    ai = mag * jnp.sin(li * dt)
    den = lr * lr + li * li
    cr = ((ar - 1.0) * lr + ai * li) / den
    ci = (ai * lr - (ar - 1.0) * li) / den
    br = b_re.astype(F32)
    bi = b_im.astype(F32)
    bbar_re = cr[..., None] * br - ci[..., None] * bi
    bbar_im = cr[..., None] * bi + ci[..., None] * br
    a2r = ar * ar - ai * ai
    a2i = 2.0 * ar * ai
    ab_re = ar[..., None] * bbar_re - ai[..., None] * bbar_im
    ab_im = ar[..., None] * bbar_im + ai[..., None] * bbar_re
    cre = c_re.astype(F32)
    cim = c_im.astype(F32)
    ca_re = cre * ar[:, None, :] - cim * ai[:, None, :]
    ca_im = cre * ai[:, None, :] + cim * ar[:, None, :]
    ca2_re = cre * a2r[:, None, :] - cim * a2i[:, None, :]
    ca2_im = cre * a2i[:, None, :] + cim * a2r[:, None, :]
    k0 = jnp.einsum('ghp,gpk->ghk', cre, bbar_re) - jnp.einsum('ghp,gpk->ghk', cim, bbar_im)
    k1 = jnp.einsum('ghp,gpk->ghk', cre, ab_re) - jnp.einsum('ghp,gpk->ghk', cim, ab_im)

    def per_block(x):
        return x.reshape((nblk, GROUPS_PER_BLOCK) + x.shape[1:])

    b_in = jnp.stack([jnp.stack([per_block(ab_re), per_block(ab_im)]),
                      jnp.stack([per_block(bbar_re), per_block(bbar_im)])])
    win_c = b_in.transpose(2, 0, 5, 1, 3, 4).reshape(nblk, 2 * SSM_GROUP, 2 * STATE_COLS)
    c_out = jnp.stack([jnp.stack([per_block(ca_re), per_block(ca2_re)]),
                       jnp.stack([-per_block(ca_im), -per_block(ca2_im)])])
    wout_c = c_out.transpose(2, 1, 4, 0, 3, 5).reshape(nblk, 2 * SSM_GROUP, 2 * STATE_COLS)
    k0b, k1b = per_block(k0), per_block(k1)
    k_mix = jnp.stack([jnp.stack([k0b, k1b]),
                       jnp.stack([jnp.zeros_like(k0b), k0b])])
    wt_c = k_mix.transpose(2, 0, 5, 1, 3, 4).reshape(nblk, 2 * SSM_GROUP, 2 * LANES)
    return (win_c, wout_c, wt_c,
            a2r.reshape(nblk, 1, STATE_COLS), a2i.reshape(nblk, 1, STATE_COLS))


def _s5_expand(compact, cols_per_group):
    width = compact.shape[1]
    col_group = (lax.broadcasted_iota(jnp.int32, (SSM_GROUP, width), 1)
                 // cols_per_group) % GROUPS_PER_BLOCK
    pieces = []
    for i in range(2):
        rows = compact[i * SSM_GROUP:(i + 1) * SSM_GROUP, :]
        for g in range(GROUPS_PER_BLOCK):
            pieces.append(jnp.where(col_group == g, rows, 0.0).astype(BF16))
    return jnp.concatenate(pieces, axis=0)


def _s5(u_tm, win, wout, wt, a2r, a2i, d_skip, *, bsz, n_pairs, blocks_per_step):
    seqlen = u_tm.shape[0]
    width = u_tm.shape[1] // bsz
    assert bsz == SUBLANES and seqlen % (2 * n_pairs) == 0 and width % LANES == 0
    nblk = width // LANES
    assert nblk % blocks_per_step == 0 and n_pairs % 2 == 0
    nb = blocks_per_step
    rows = n_pairs * SUBLANES
    u3 = u_tm.reshape(seqlen, bsz, width)
    blk = pl.BlockSpec((2 * n_pairs, bsz, nb * LANES), lambda j, c: (c, 0, j))
    per_blk = lambda shape: pl.BlockSpec((nb,) + shape, lambda j, c: (j, 0, 0))
    out = pl.pallas_call(
        functools.partial(_s5_kernel, n_pairs=n_pairs),
        grid=(nblk // nb, seqlen // (2 * n_pairs)),
        in_specs=[blk,
                  per_blk((2 * SSM_GROUP, 2 * STATE_COLS)),
                  per_blk((2 * SSM_GROUP, 2 * STATE_COLS)),
                  per_blk((2 * SSM_GROUP, 2 * LANES)),
                  per_blk((1, STATE_COLS)),
                  per_blk((1, STATE_COLS)),
                  pl.BlockSpec((1, nb * LANES), lambda j, c: (0, j))],
        out_specs=blk,
        out_shape=jax.ShapeDtypeStruct((seqlen, bsz, width), F32),
        scratch_shapes=[pltpu.VMEM((nb, 2 * LANES, 2 * STATE_COLS), BF16),
                        pltpu.VMEM((nb, 2 * LANES, 2 * STATE_COLS), BF16),
                        pltpu.VMEM((nb, 2 * LANES, 2 * LANES), BF16),
                        pltpu.VMEM((nb, rows, 2 * STATE_COLS), F32),
                        pltpu.VMEM((nb, rows, 2 * STATE_COLS), BF16),
                        pltpu.VMEM((nb, 2, SUBLANES, STATE_COLS), F32)],
        compiler_params=_params(("arbitrary", "arbitrary")),
        name="s5_scan",
    )(u3, win, wout, wt, a2r, a2i, d_skip.reshape(1, width).astype(F32))
    return out.reshape(seqlen, bsz * width)


def _glu_out_kernel(yg_ref, z_ref, memo_ref, h_ref, wglu_ref, bglu_ref,
                    wmain_ref, wmem_ref, g_ref, gkv_ref, gnext_ref, wf_ref, bf_ref,
                    out_ref, xkv_ref, xnext_ref, fp_ref, carry_ref, *, heads, sub):
    @pl.when(pl.program_id(1) == 0)
    def _():
        carry_ref[...] = jnp.zeros_like(carry_ref)

    subs = [slice(s * sub, (s + 1) * sub) for s in range(out_ref.shape[0] // sub)]
    t = [jnp.dot(yg_ref[rows, :].astype(BF16), wglu_ref[...], preferred_element_type=F32)
         + bglu_ref[...] for rows in subs]
    for rows, tt in zip(subs, t):
        z = z_ref[rows, :].astype(F32)
        main = yg_ref[rows, :] * jax.nn.sigmoid(tt) * (z * jax.nn.sigmoid(z))
        o = jnp.dot(main.astype(BF16), wmain_ref[...], preferred_element_type=F32)
        o = o + jnp.dot(memo_ref[rows, :], wmem_ref[...], preferred_element_type=F32)
        ms = jnp.mean(o * o, axis=-1, keepdims=True)
        out_ref[rows, :] = h_ref[rows, :] + o * lax.rsqrt(ms + EPS) * g_ref[...]
    split = []
    for rows in subs:
        h1 = out_ref[rows, :]
        r = h1 * lax.rsqrt(jnp.mean(h1 * h1, axis=-1, keepdims=True) + EPS)
        xnext_ref[rows, :] = (r * gnext_ref[...]).astype(BF16)
        split.append(_split2(r * gkv_ref[...]))
        xkv_ref[rows, :] = split[-1][0]
    logit = [_fgate_logits(xh, xl, wf_ref, bf_ref) for xh, xl in split]
    for rows, lg in zip(subs, logit):
        fp_ref[rows, :], carry_ref[...] = _fgate_cumsum(lg, carry_ref[...], heads)


def _out_kernel(main_ref, memo_ref, h_ref, wmain_ref, wmem_ref, g_ref, out_ref, *, sub):
    subs = [slice(s * sub, (s + 1) * sub) for s in range(out_ref.shape[0] // sub)]
    o = []
    for rows in subs:
        main = jnp.concatenate([main_ref[hd, rows, :] for hd in range(main_ref.shape[0])],
                               axis=1)
        o.append(jnp.dot(main, wmain_ref[...], preferred_element_type=F32)
                 + jnp.dot(memo_ref[rows, :], wmem_ref[...], preferred_element_type=F32))
    for rows, oo in zip(subs, o):
        ms = jnp.mean(oo * oo, axis=-1, keepdims=True)
        out_ref[rows, :] = h_ref[rows, :] + oo * lax.rsqrt(ms + EPS) * g_ref[...]


def _resident(shape):
    return pl.BlockSpec(shape, lambda *_: (0,) * len(shape), pipeline_mode=pl.Buffered(1))


def _w_out_specs(layer, main_w, mem_w, d):
    assert main_w % mem_w == 0
    return [pl.BlockSpec((None, main_w, d), lambda *_: (layer, 0, 0),
                         pipeline_mode=pl.Buffered(1)),
            pl.BlockSpec((None, mem_w, d), lambda *_: (layer, main_w // mem_w, 0),
                         pipeline_mode=pl.Buffered(1))]


def _glu_out(yg_tm, rest, memo, h2d, w_glu, b_glu, w_out_bf, layer, g, g_kv, g_next,
             w_fgate, b_fgate, *, bsz, tl, sub):
    assert tl % sub == 0
    m, d = h2d.shape
    seqlen = m // bsz
    main_w = w_glu.shape[0]
    mem_w = w_out_bf.shape[1] - main_w
    heads = w_fgate.shape[1]
    assert 3 * heads <= LANES and seqlen % tl == 0
    wpad = jnp.zeros((d, LANES), F32).at[:, :heads].set(w_fgate.astype(F32))
    wcat = jnp.concatenate(_split2(wpad), axis=1)
    bpad = jnp.zeros((1, LANES), F32).at[0, :heads].set(b_fgate.astype(F32))
    per_b = seqlen // tl
    row = lambda b, i: (b * per_b + i, 0)
    vec = lambda v: v.reshape(1, -1).astype(F32)
    return pl.pallas_call(
        functools.partial(_glu_out_kernel, heads=heads, sub=sub),
        grid=(bsz, per_b),
        in_specs=[pl.BlockSpec((tl, main_w), lambda b, i: (i, b)),
                  pl.BlockSpec((tl, main_w), row),
                  pl.BlockSpec((tl, mem_w), row),
                  pl.BlockSpec((tl, d), row),
                  _resident((main_w, main_w)),
                  _resident((1, main_w))]
                 + _w_out_specs(layer, main_w, mem_w, d)
                 + [_resident((1, d)),
                    _resident((1, d)),
                    _resident((1, d)),
                    _resident((d, 2 * LANES)),
                    _resident((1, LANES))],
        out_specs=[pl.BlockSpec((tl, d), row),
                   pl.BlockSpec((tl, d), row),
                   pl.BlockSpec((tl, d), row),
                   pl.BlockSpec((tl, LANES), row)],
        out_shape=[jax.ShapeDtypeStruct((m, d), F32),
                   jax.ShapeDtypeStruct((m, d), BF16),
                   jax.ShapeDtypeStruct((m, d), BF16),
                   jax.ShapeDtypeStruct((m, LANES), BF16)],
        scratch_shapes=[pltpu.VMEM((1, LANES), F32)],
        compiler_params=_params(("arbitrary", "arbitrary")),
        name="glu_out_proj",
    )(yg_tm, rest, memo, h2d, w_glu.astype(BF16), vec(b_glu), w_out_bf, w_out_bf,
      vec(g), vec(g_kv), vec(g_next), wcat, bpad)


def _out_proj(main, memo, h2d, w_out_bf, layer, g, *, tl, sub):
    m, d = h2d.shape
    heads, _, dh = main.shape
    main_w = heads * dh
    mem_w = memo.shape[1]
    row = lambda i: (i, 0)
    return pl.pallas_call(
        functools.partial(_out_kernel, sub=sub),
        grid=(m // tl,),
        in_specs=[pl.BlockSpec((heads, tl, dh), lambda i: (0, i, 0)),
                  pl.BlockSpec((tl, mem_w), row),
                  pl.BlockSpec((tl, d), row)]
                 + _w_out_specs(layer, main_w, mem_w, d)
                 + [_resident((1, d))],
        out_specs=pl.BlockSpec((tl, d), row),
        out_shape=jax.ShapeDtypeStruct((m, d), F32),
        compiler_params=_params(("parallel",)),
        name="out_proj",
    )(main, memo, h2d, w_out_bf, w_out_bf, g.reshape(1, d).astype(F32))


FOX_CHUNK = 64


def _fox_kernel(q_ref, k_ref, v_ref, fp_ref, sel_ref, cst_ref, z_ref, o_ref,
                ka_ref, qa_ref, vt_ref, st_ref, pt_ref, m_ref, l_ref, acc_ref, *, tq):
    nh, seqlen, _ = q_ref.shape
    ncg = tq // LANES
    nch = tq // FOX_CHUNK
    for hh in range(nh):
        ext = jnp.dot(fp_ref[...], sel_ref[hh], preferred_element_type=F32)
        ka_ref[hh, :, :HEAD_DIM] = k_ref[hh]
        ka_ref[hh, :, HEAD_DIM:] = (ext[:, :HEAD_DIM] + cst_ref[0:1, :]).astype(BF16)
        qa_ref[hh, :, :HEAD_DIM] = q_ref[hh]
        qa_ref[hh, :, HEAD_DIM:] = (ext[:, HEAD_DIM:] + cst_ref[1:2, :]).astype(BF16)
        vt_ref[hh] = v_ref[hh].astype(F32).T.astype(BF16)
        for ci in range(nch):
            for g in range(ncg):
                if g * LANES + LANES - 1 < ci * FOX_CHUNK:
                    pt_ref[hh, 1, ci * FOX_CHUNK:(ci + 1) * FOX_CHUNK,
                           g * LANES:(g + 1) * LANES] = jnp.zeros((FOX_CHUNK, LANES), BF16)

    half = tq // 2

    def scores(hh, qi, kj, slot):
        nt = (((1,), (1,)), ((), ()))
        k0, q0 = kj * tq, qi * tq
        if kj < qi:
            st_ref[hh, slot] = lax.dot_general(ka_ref[hh, k0:k0 + tq, :], qa_ref[hh, q0:q0 + tq, :],
                                               nt, preferred_element_type=F32)
        else:
            st_ref[hh, slot, :half, :] = lax.dot_general(
                ka_ref[hh, k0:k0 + half, :], qa_ref[hh, q0:q0 + tq, :], nt,
                preferred_element_type=F32)
            st_ref[hh, slot, half:, half:] = lax.dot_general(
                ka_ref[hh, k0 + half:k0 + tq, :], qa_ref[hh, q0 + half:q0 + tq, :], nt,
                preferred_element_type=F32)

    def piece(hh, slot, ci, g, masked):
        r0, c0 = ci * FOX_CHUNK, g * LANES
        if masked and c0 + LANES - 1 < r0:
            return None
        x = st_ref[hh, slot, r0:r0 + FOX_CHUNK, c0:c0 + LANES]
        if masked and c0 < r0 + FOX_CHUNK - 1:
            key = r0 + lax.broadcasted_iota(jnp.int32, (FOX_CHUNK, LANES), 0)
            qry = c0 + lax.broadcasted_iota(jnp.int32, (FOX_CHUNK, LANES), 1)
            x = jnp.where(key <= qry, x, -jnp.inf)
        return x

    def fold(x, op):
        return op(x.reshape(FOX_CHUNK // SUBLANES, SUBLANES, LANES), axis=0)

    def tile_stats(hh, slot, masked, first):
        mx = [None] * ncg
        for ci in range(nch):
            for g in range(ncg):
                x = piece(hh, slot, ci, g, masked)
                if x is not None:
                    f = fold(x, jnp.max)
                    mx[g] = f if mx[g] is None else jnp.maximum(mx[g], f)
        m_new = jnp.concatenate([jnp.max(v, axis=0, keepdims=True) for v in mx], axis=1)
        alpha = None
        if not first:
            m_old = m_ref[hh]
            m_new = jnp.maximum(m_old, m_new)
            alpha = jnp.exp2(m_old - m_new)
        m_ref[hh] = m_new
        return m_new, alpha

    def tile_probs(hh, slot, masked, m_new):
        ls = [None] * ncg
        for ci in range(nch):
            for g in range(ncg):
                r0, c0 = ci * FOX_CHUNK, g * LANES
                x = piece(hh, slot, ci, g, masked)
                if x is None:
                    continue
                p = jnp.exp2(x - m_new[:, c0:c0 + LANES])
                f = fold(p, jnp.sum)
                ls[g] = f if ls[g] is None else ls[g] + f
                pt_ref[hh, int(masked), r0:r0 + FOX_CHUNK, c0:c0 + LANES] = p.astype(BF16)
        return jnp.concatenate([jnp.sum(v, axis=0, keepdims=True) for v in ls], axis=1)

    def tile_update(hh, kj, masked, alpha, l_new):
        k0 = kj * tq
        if masked:
            pv = jnp.dot(vt_ref[hh, :, k0:k0 + half], pt_ref[hh, 1, :half, :],
                         preferred_element_type=F32)
            late = jnp.dot(vt_ref[hh, :, k0 + half:k0 + tq], pt_ref[hh, 1, half:, half:],
                           preferred_element_type=F32)
            pv = jnp.concatenate([pv[:, :half], pv[:, half:] + late], axis=1)
        else:
            pv = jnp.dot(vt_ref[hh, :, k0:k0 + tq], pt_ref[hh, 0], preferred_element_type=F32)
        if alpha is None:
            l_ref[hh] = l_new
            acc_ref[hh] = pv
        else:
            l_ref[hh] = alpha * l_ref[hh] + l_new
            acc_ref[hh] = alpha * acc_ref[hh] + pv

    heads = range(nh)
    for qi in range(seqlen // tq):
        rows = slice(qi * tq, (qi + 1) * tq)
        for hh in heads:
            scores(hh, qi, 0, 0)
        for kj in range(qi + 1):
            slot, masked = kj % 2, kj == qi
            if kj < qi:
                for hh in heads:
                    scores(hh, qi, kj + 1, (kj + 1) % 2)
            stats = [tile_stats(hh, slot, masked, kj == 0) for hh in heads]
            sums = [tile_probs(hh, slot, masked, stats[hh][0]) for hh in heads]
            for hh in heads:
                tile_update(hh, kj, masked, stats[hh][1], sums[hh])
        for hh in range(nh):
            z = z_ref[hh, rows, :].astype(F32)
            o = (acc_ref[hh] / l_ref[hh]).T
            o_ref[hh, rows, :] = (o * (z * jax.nn.sigmoid(z))).astype(o_ref.dtype)


def _fox_attn(q, k, v, z, fparts, *, bsz, tq, heads_per_step):
    heads, m, _ = q.shape
    seqlen = m // bsz
    nterms = 3
    assert seqlen % tq == 0 and tq % FOX_CHUNK == 0 and nterms * heads <= LANES
    h_idx = jnp.arange(heads)[:, None, None]
    r_idx = jnp.arange(LANES)[None, :, None]
    c_idx = jnp.arange(2 * HEAD_DIM)[None, None, :]
    key_side = (c_idx < nterms) & (r_idx == c_idx * heads + h_idx)
    qc = c_idx - HEAD_DIM - nterms
    qry_side = (qc >= 0) & (qc < nterms) & (r_idx == qc * heads + h_idx)
    sel = (qry_side.astype(F32) - key_side.astype(F32)).astype(BF16)
    col = jnp.arange(HEAD_DIM)
    cst = jnp.zeros((SUBLANES, HEAD_DIM), F32)
    cst = cst.at[0].set(((col >= nterms) & (col < 2 * nterms)).astype(F32))
    cst = cst.at[1].set((col < nterms).astype(F32))
    nh = heads_per_step
    assert heads % nh == 0
    head_seq = pl.BlockSpec((nh, seqlen, HEAD_DIM), lambda b, h: (h, b, 0))
    return pl.pallas_call(
        functools.partial(_fox_kernel, tq=tq),
        grid=(bsz, heads // nh),
        in_specs=[head_seq, head_seq, head_seq,
                  pl.BlockSpec((seqlen, LANES), lambda b, h: (b, 0)),
                  pl.BlockSpec((nh, LANES, 2 * HEAD_DIM), lambda b, h: (h, 0, 0)),
                  pl.BlockSpec((SUBLANES, HEAD_DIM), lambda b, h: (0, 0)),
                  head_seq],
        out_specs=head_seq,
        out_shape=jax.ShapeDtypeStruct((heads, m, HEAD_DIM), BF16),
        scratch_shapes=[pltpu.VMEM((nh, seqlen, 2 * HEAD_DIM), BF16),
                        pltpu.VMEM((nh, seqlen, 2 * HEAD_DIM), BF16),
                        pltpu.VMEM((nh, HEAD_DIM, seqlen), BF16),
                        pltpu.VMEM((nh, 2, tq, tq), F32),
                        pltpu.VMEM((nh, 2, tq, tq), BF16),
                        pltpu.VMEM((nh, 1, tq), F32),
                        pltpu.VMEM((nh, 1, tq), F32),
                        pltpu.VMEM((nh, HEAD_DIM, tq), F32)],
        compiler_params=_params(("arbitrary", "arbitrary")),
        name="fox_attn",
    )(q, k, v, fparts, sel, cst, z)


def kernel(x, mem, pre_norm_g, post_norm_g, w_in_a, lam_re, lam_im, log_step, b_re, b_im,
           c_re, c_im, d_skip, w_glu, b_glu, kv_norm_g, w_kv, w_fgate, b_fgate, w_in_b,
           mem_norm_g, w_mem_kv, w_out):
    bsz, seqlen, d = x.shape
    n_mem = mem.shape[1]
    main_w = w_glu.shape[1]
    mem_w = w_out.shape[1] - main_w
    scale = HEAD_DIM ** -0.5
    x2d = x.reshape(bsz * seqlen, d)
    mem2d = mem.reshape(bsz * n_mem, d)

    ones = jnp.ones((main_w,), F32)
    mem_scale = jnp.full((mem_w,), scale * LOG2E, F32)
    cs_a = jnp.concatenate([ones, ones, mem_scale, ones[:mem_w]])
    cs_b = jnp.concatenate([ones * (scale * LOG2E), ones, mem_scale, ones[:mem_w]])
    proj = functools.partial(_matmul, out_dtype=BF16, tm=1024)
    u_tm, xa = _matmul(x2d, w_in_a[0], cs_a, col0=0, n=main_w, out_dtype=F32, tm=512,
                       tn=main_w, time_major_batches=bsz, norm_gain=pre_norm_g[0],
                       name="in_proj_a_u")
    z_a = proj(xa, w_in_a[0], cs_a, col0=main_w, n=main_w, tn=main_w, name="in_proj_a_z")
    qz_a = proj(xa, w_in_a[0], cs_a, col0=2 * main_w, n=2 * mem_w, tn=2 * mem_w,
                slab_width=HEAD_DIM, name="in_proj_a_mem")
    kvm0 = _norm_matmul(mem2d, mem_norm_g[0], w_mem_kv[0], out_dtype=BF16,
                        tm=512, tn=512, name="mem_kv0")
    memo0 = _mem_attn(qz_a, kvm0, bsz=bsz, tq=1024, name="mem_attn0")
    s5_ops = _s5_discretise(lam_re[0], lam_im[0], log_step[0], b_re[0], b_im[0],
                            c_re[0], c_im[0])
    yg_tm = _s5(u_tm, *s5_ops, d_skip[0], bsz=bsz, n_pairs=128, blocks_per_step=3)
    w_out_bf = w_out.astype(BF16)
    h1, xkv, xb, fparts = _glu_out(
        yg_tm, z_a, memo0, x2d, w_glu[0], b_glu[0], w_out_bf, 0, post_norm_g[0],
        kv_norm_g, pre_norm_g[1], w_fgate, b_fgate, bsz=bsz, tl=512, sub=256)

    cs_kv = jnp.ones((w_kv.shape[1],), F32)
    k_sh = proj(xkv, w_kv, cs_kv, col0=0, n=main_w, tn=main_w, slab_width=HEAD_DIM,
                name="k_proj")
    v_sh = proj(xkv, w_kv, cs_kv, col0=main_w, n=main_w, tn=main_w, slab_width=HEAD_DIM,
                name="v_proj")

    q_b = proj(xb, w_in_b[0], cs_b, col0=0, n=main_w, tn=main_w, slab_width=HEAD_DIM,
               name="in_proj_b_q")
    z_b = proj(xb, w_in_b[0], cs_b, col0=main_w, n=main_w, tn=main_w, slab_width=HEAD_DIM,
               name="in_proj_b_z")
    qz_b = proj(xb, w_in_b[0], cs_b, col0=2 * main_w, n=2 * mem_w, tn=2 * mem_w,
                slab_width=HEAD_DIM, name="in_proj_b_mem")
    kvm1 = _norm_matmul(mem2d, mem_norm_g[1], w_mem_kv[1], out_dtype=BF16,
                        tm=512, tn=512, name="mem_kv1")
    memo1 = _mem_attn(qz_b, kvm1, bsz=bsz, tq=1024, name="mem_attn1")
    att = _fox_attn(q_b, k_sh, v_sh, z_b, fparts, bsz=bsz, tq=512, heads_per_step=3)
    out = _out_proj(att, memo1, h1, w_out_bf, 1, post_norm_g[1], tl=512, sub=256)
    return out.reshape(bsz, seqlen, d)
```

```python
import functools

import jax
import jax.numpy as jnp
from jax import lax
from jax.experimental import pallas as pl
from jax.experimental.pallas import tpu as pltpu

F32 = jnp.float32
BF16 = jnp.bfloat16

EPS = 1e-6
LOG2E = 1.4426950408889634
HEAD_DIM = 128
SSM_GROUP = 16
SSM_STATE = 64
MEM_HEADS = 4
LANES = 128
SUBLANES = 8
GROUPS_PER_BLOCK = LANES // SSM_GROUP
STATE_COLS = GROUPS_PER_BLOCK * SSM_STATE
VMEM_LIMIT = 56 * 1024 * 1024


def _params(sem, vmem=VMEM_LIMIT):
    return pltpu.CompilerParams(dimension_semantics=sem, vmem_limit_bytes=vmem)


def _norm_matmul_kernel(x_ref, g_ref, w_ref, o_ref, r_ref, xn_ref):
    layer, j = pl.program_id(1), pl.program_id(2)

    @pl.when((layer == 0) & (j == 0))
    def _():
        x = x_ref[...]
        r_ref[...] = x * lax.rsqrt(jnp.mean(x * x, axis=-1, keepdims=True) + EPS)

    @pl.when(j == 0)
    def _():
        xn_ref[...] = (r_ref[...] * g_ref[...]).astype(BF16)

    o_ref[...] = jnp.dot(xn_ref[...], w_ref[...].astype(BF16),
                         preferred_element_type=F32).astype(o_ref.dtype)


def _norm_matmul(x2d, g, w, *, out_dtype, tm, tn, name):
    m, d = x2d.shape
    layers, _, n = w.shape
    assert m % tm == 0 and n % tn == 0
    return pl.pallas_call(
        _norm_matmul_kernel,
        grid=(m // tm, layers, n // tn),
        in_specs=[pl.BlockSpec((tm, d), lambda i, l, j: (i, 0)),
                  pl.BlockSpec((None, 1, d), lambda i, l, j: (l, 0, 0)),
                  pl.BlockSpec((None, d, tn), lambda i, l, j: (l, 0, j))],
        out_specs=pl.BlockSpec((None, tm, tn), lambda i, l, j: (l, i, j)),
        out_shape=jax.ShapeDtypeStruct((layers, m, n), out_dtype),
        scratch_shapes=[pltpu.VMEM((tm, d), F32), pltpu.VMEM((tm, d), BF16)],
        compiler_params=_params(("arbitrary", "arbitrary", "arbitrary")),
        name=name,
    )(x2d, g.reshape(layers, 1, d).astype(F32), w)


def _store_product(res, o_ref):
    if len(o_ref.shape) == 3:
        sw = o_ref.shape[2]
        for s in range(o_ref.shape[0]):
            o_ref[s] = res[:, s * sw:(s + 1) * sw]
    else:
        o_ref[...] = res


def _matmul_kernel(x_ref, w_ref, cs_ref, o_ref, wb_ref):
    @pl.when(pl.program_id(1) == 0)
    def _():
        wb_ref[...] = (w_ref[...] * cs_ref[...]).astype(BF16)

    res = jnp.dot(x_ref[...], wb_ref[...], preferred_element_type=F32).astype(o_ref.dtype)
    _store_product(res, o_ref)


def _norm_then_matmul_kernel(x_ref, g_ref, w_ref, cs_ref, o_ref, xn_ref, wb_ref):
    @pl.when(pl.program_id(1) == 0)
    def _():
        wb_ref[...] = (w_ref[...] * cs_ref[...]).astype(BF16)

    half = x_ref.shape[0] // 2
    xn = []
    for rows in (slice(0, half), slice(half, 2 * half)):
        x = x_ref[rows, :]
        ms = jnp.mean(x * x, axis=-1, keepdims=True)
        xn.append((x * lax.rsqrt(ms + EPS) * g_ref[...]).astype(BF16))
        xn_ref[rows, :] = xn[-1]
    res = jnp.concatenate(
        [jnp.dot(v, wb_ref[...], preferred_element_type=F32).astype(o_ref.dtype) for v in xn],
        axis=0)
    _store_product(res, o_ref)


def _matmul(xn, w, colscale, *, col0, n, out_dtype, tm, tn, time_major_batches=None,
            slab_width=None, norm_gain=None, name):
    m, d = xn.shape
    assert m % tm == 0 and n % tn == 0 and col0 % tn == 0
    nj = n // tn
    j0 = col0 // tn
    if slab_width is not None:
        assert time_major_batches is None and tn % slab_width == 0
        per_tile = tn // slab_width
        out_shape = jax.ShapeDtypeStruct((n // slab_width, m, slab_width), out_dtype)
        out_spec = pl.BlockSpec((per_tile, tm, slab_width), lambda j, i: (j, i, 0))
    elif time_major_batches is None:
        out_shape = jax.ShapeDtypeStruct((m, n), out_dtype)
        out_spec = pl.BlockSpec((tm, tn), lambda j, i: (i, j))
    else:
        bsz = time_major_batches
        seqlen = m // bsz
        assert seqlen % tm == 0
        per_b = seqlen // tm
        out_shape = jax.ShapeDtypeStruct((seqlen, bsz * n), out_dtype)
        out_spec = pl.BlockSpec((tm, tn), lambda j, i: (i % per_b, (i // per_b) * nj + j))
    x_spec = pl.BlockSpec((tm, d), lambda j, i: (i, 0))
    w_specs = [pl.BlockSpec((d, tn), lambda j, i: (0, j0 + j), pipeline_mode=pl.Buffered(1)),
               pl.BlockSpec((1, tn), lambda j, i: (0, j0 + j))]
    common = dict(grid=(nj, m // tm), scratch_shapes=[pltpu.VMEM((d, tn), BF16)],
                  compiler_params=_params(("arbitrary", "arbitrary")), name=name)
    cs = colscale.reshape(1, -1).astype(F32)
    if norm_gain is None:
        return pl.pallas_call(
            _matmul_kernel, in_specs=[x_spec] + w_specs, out_specs=out_spec,
            out_shape=out_shape, **common)(xn, w, cs)
    assert nj == 1
    return pl.pallas_call(
        _norm_then_matmul_kernel,
        in_specs=[x_spec, pl.BlockSpec((1, d), lambda j, i: (0, 0))] + w_specs,
        out_specs=[out_spec, x_spec],
        out_shape=[out_shape, jax.ShapeDtypeStruct((m, d), BF16)],
        **common)(xn, norm_gain.reshape(1, d).astype(F32), w, cs)


def _split2(x):
    hi = x.astype(BF16)
    return hi, (x - hi.astype(F32)).astype(BF16)


def _split3(x):
    hi = x.astype(BF16)
    r = x - hi.astype(F32)
    mid = r.astype(BF16)
    return hi, mid, (r - mid.astype(F32)).astype(BF16)


def _fgate_logits(xh, xl, wcat_ref, b_ref):
    both = jnp.dot(xh, wcat_ref[...], preferred_element_type=F32)
    return (both[:, :LANES] + both[:, LANES:]
            + jnp.dot(xl, wcat_ref[:, :LANES], preferred_element_type=F32)) + b_ref[...]


def _fgate_cumsum(logit, carry, heads):
    tl = logit.shape[0]
    logf = jnp.minimum(logit, 0.0) - jnp.log(1.0 + jnp.exp(-jnp.abs(logit)))
    t_idx = lax.broadcasted_iota(jnp.int32, (tl, tl), 0)
    s_idx = lax.broadcasted_iota(jnp.int32, (tl, tl), 1)
    tril = (s_idx <= t_idx).astype(BF16)
    csum = carry
    for part in _split3(logf):
        csum = csum + jnp.dot(tril, part, preferred_element_type=F32)
    lane = lax.broadcasted_iota(jnp.int32, (tl, LANES), 1)
    packed = jnp.zeros((tl, LANES), F32)
    for n, part in reversed(list(enumerate(_split3(csum * LOG2E)))):
        shifted = part.astype(F32) if n == 0 else pltpu.roll(part.astype(F32), n * heads, 1)
        packed = jnp.where(lane < (n + 1) * heads, shifted, packed)
    return packed.astype(BF16), csum[tl - 1:tl, :]


def _mem_attn_kernel(q_ref, zm_ref, kv_ref, o_ref):
    width = MEM_HEADS * HEAD_DIM
    heads = range(MEM_HEADS)
    cols = [slice(h * HEAD_DIM, (h + 1) * HEAD_DIM) for h in heads]
    st = [lax.dot_general(kv_ref[:, cols[h]], q_ref[h], (((1,), (1,)), ((), ())),
                          preferred_element_type=F32) for h in heads]
    vt = [kv_ref[:, width + h * HEAD_DIM:width + (h + 1) * HEAD_DIM].astype(F32).T.astype(BF16)
          for h in heads]
    p = [jnp.exp2(st[h] - jnp.max(st[h], axis=0, keepdims=True)) for h in heads]
    inv = [1.0 / jnp.sum(p[h], axis=0, keepdims=True) for h in heads]
    ot = [jnp.dot(vt[h], p[h].astype(BF16), preferred_element_type=F32) * inv[h] for h in heads]
    for h in heads:
        zm = zm_ref[h].astype(F32)
        o_ref[:, cols[h]] = (ot[h].T * (zm * jax.nn.sigmoid(zm))).astype(o_ref.dtype)


def _mem_attn(qz, kvm, layer, *, bsz, tq, name):
    width = MEM_HEADS * HEAD_DIM
    m = qz.shape[1]
    seqlen = m // bsz
    n_mem = kvm.shape[1] // bsz
    per_b = seqlen // tq
    spec = lambda blk: pl.BlockSpec((MEM_HEADS, tq, HEAD_DIM),
                                    lambda b, i: (blk, b * per_b + i, 0))
    return pl.pallas_call(
        _mem_attn_kernel,
        grid=(bsz, per_b),
        in_specs=[spec(0), spec(1),
                  pl.BlockSpec((None, n_mem, 2 * width), lambda b, i: (layer, b, 0))],
        out_specs=pl.BlockSpec((tq, width), lambda b, i: (b * per_b + i, 0)),
        out_shape=jax.ShapeDtypeStruct((m, width), BF16),
        compiler_params=_params(("parallel", "arbitrary")),
        name=name,
    )(qz, qz, kvm)


def _s5_kernel(u_ref, winc_ref, woutc_ref, wtc_ref, ar_ref, ai_ref, d_ref, y_ref,
               win_ref, wout_ref, wt_ref, bu_ref, xs_ref, st_ref, *, n_pairs):
    nblk = win_ref.shape[0]

    @pl.when(pl.program_id(1) == 0)
    def _():
        st_ref[...] = jnp.zeros_like(st_ref)
        for k in range(nblk):
            win_ref[k] = _s5_expand(winc_ref[k], SSM_STATE)
            wout_ref[k] = _s5_expand(woutc_ref[k], SSM_STATE)
            wt_ref[k] = _s5_expand(wtc_ref[k], SSM_GROUP)

    rows = n_pairs * SUBLANES
    u0, u1, ucat, a2, state = [], [], [], [], []
    for k in range(nblk):
        upair = u_ref[:, :, k * LANES:(k + 1) * LANES].reshape(n_pairs, 2, SUBLANES, LANES)
        u0.append(upair[:, 0].reshape(rows, LANES))
        u1.append(upair[:, 1].reshape(rows, LANES))
        ucat.append(jnp.concatenate([u0[k], u1[k]], axis=1).astype(BF16))
        bu_ref[k] = jnp.dot(ucat[k], win_ref[k], preferred_element_type=F32)
        a2.append((jnp.broadcast_to(ar_ref[k], (SUBLANES, STATE_COLS)),
                   jnp.broadcast_to(ai_ref[k], (SUBLANES, STATE_COLS))))
        state.append((st_ref[k, 0], st_ref[k, 1]))

    held = [None] * nblk
    for c in range(n_pairs):
        for k in range(nblk):
            xr, xi = state[k]
            cur = jnp.concatenate([xr, xi], axis=1)
            if c % 2 == 1:
                xs_ref[k, (c - 1) * SUBLANES:(c + 1) * SUBLANES, :] = jnp.concatenate(
                    [held[k], cur], axis=0).astype(BF16)
            held[k] = cur
            bur = bu_ref[k, c * SUBLANES:(c + 1) * SUBLANES, 0:STATE_COLS]
            bui = bu_ref[k, c * SUBLANES:(c + 1) * SUBLANES, STATE_COLS:2 * STATE_COLS]
            ar, ai = a2[k]
            state[k] = (ar * xr - ai * xi + bur, ar * xi + ai * xr + bui)

    for k in range(nblk):
        st_ref[k, 0] = state[k][0]
        st_ref[k, 1] = state[k][1]
    ys = [lax.dot_general(xs_ref[k], wout_ref[k], (((1,), (1,)), ((), ())),
                          preferred_element_type=F32)
          + jnp.dot(ucat[k], wt_ref[k], preferred_element_type=F32) for k in range(nblk)]
    for k in range(nblk):
        y = ys[k]
        d = d_ref[:, k * LANES:(k + 1) * LANES]
        y0 = jax.nn.gelu(y[:, :LANES] + d * u0[k]).reshape(n_pairs, SUBLANES, LANES)
        y1 = jax.nn.gelu(y[:, LANES:] + d * u1[k]).reshape(n_pairs, SUBLANES, LANES)
        y_ref[:, :, k * LANES:(k + 1) * LANES] = jnp.stack([y0, y1], axis=1).reshape(
            2 * n_pairs, SUBLANES, LANES)


def _s5_discretise(lam_re, lam_im, log_step, b_re, b_im, c_re, c_im):
    groups = lam_re.shape[0]
    nblk = groups // GROUPS_PER_BLOCK
    lr = lam_re.astype(F32)
    li = lam_im.astype(F32)
    dt = jnp.exp(log_step.astype(F32))[:, None]
    mag = jnp.exp(lr * dt)
    ar = mag * jnp.cos(li * dt)
    ai = mag * jnp.sin(li * dt)
    den = lr * lr + li * li
    cr = ((ar - 1.0) * lr + ai * li) / den
    ci = (ai * lr - (ar - 1.0) * li) / den
    br = b_re.astype(F32)
    bi = b_im.astype(F32)
    bbar_re = cr[..., None] * br - ci[..., None] * bi
    bbar_im = cr[..., None] * bi + ci[..., None] * br
    a2r = ar * ar - ai * ai
    a2i = 2.0 * ar * ai
    ab_re = ar[..., None] * bbar_re - ai[..., None] * bbar_im
    ab_im = ar[..., None] * bbar_im + ai[..., None] * bbar_re
    cre = c_re.astype(F32)
    cim = c_im.astype(F32)
    ca_re = cre * ar[:, None, :] - cim * ai[:, None, :]
    ca_im = cre * ai[:, None, :] + cim * ar[:, None, :]
    ca2_re = cre * a2r[:, None, :] - cim * a2i[:, None, :]
    ca2_im = cre * a2i[:, None, :] + cim * a2r[:, None, :]
    k0 = jnp.einsum('ghp,gpk->ghk', cre, bbar_re) - jnp.einsum('ghp,gpk->ghk', cim, bbar_im)
    k1 = jnp.einsum('ghp,gpk->ghk', cre, ab_re) - jnp.einsum('ghp,gpk->ghk', cim, ab_im)

    def per_block(x):
        return x.reshape((nblk, GROUPS_PER_BLOCK) + x.shape[1:])

    b_in = jnp.stack([jnp.stack([per_block(ab_re), per_block(ab_im)]),
                      jnp.stack([per_block(bbar_re), per_block(bbar_im)])])
    win_c = b_in.transpose(2, 0, 5, 1, 3, 4).reshape(nblk, 2 * SSM_GROUP, 2 * STATE_COLS)
    c_out = jnp.stack([jnp.stack([per_block(ca_re), per_block(ca2_re)]),
                       jnp.stack([-per_block(ca_im), -per_block(ca2_im)])])
    wout_c = c_out.transpose(2, 1, 4, 0, 3, 5).reshape(nblk, 2 * SSM_GROUP, 2 * STATE_COLS)
    k0b, k1b = per_block(k0), per_block(k1)
    k_mix = jnp.stack([jnp.stack([k0b, k1b]),
                       jnp.stack([jnp.zeros_like(k0b), k0b])])
    wt_c = k_mix.transpose(2, 0, 5, 1, 3, 4).reshape(nblk, 2 * SSM_GROUP, 2 * LANES)
    return (win_c, wout_c, wt_c,
            a2r.reshape(nblk, 1, STATE_COLS), a2i.reshape(nblk, 1, STATE_COLS))


def _s5_expand(compact, cols_per_group):
    width = compact.shape[1]
    col_group = (lax.broadcasted_iota(jnp.int32, (SSM_GROUP, width), 1)
                 // cols_per_group) % GROUPS_PER_BLOCK
    pieces = []
    for i in range(2):
        rows = compact[i * SSM_GROUP:(i + 1) * SSM_GROUP, :]
        for g in range(GROUPS_PER_BLOCK):
            pieces.append(jnp.where(col_group == g, rows, 0.0).astype(BF16))
    return jnp.concatenate(pieces, axis=0)


def _s5(u_tm, win, wout, wt, a2r, a2i, d_skip, *, bsz, n_pairs, blocks_per_step):
    seqlen = u_tm.shape[0]
    width = u_tm.shape[1] // bsz
    assert bsz == SUBLANES and seqlen % (2 * n_pairs) == 0 and width % LANES == 0
    nblk = width // LANES
    assert nblk % blocks_per_step == 0 and n_pairs % 2 == 0
    nb = blocks_per_step
    rows = n_pairs * SUBLANES
    u3 = u_tm.reshape(seqlen, bsz, width)
    blk = pl.BlockSpec((2 * n_pairs, bsz, nb * LANES), lambda j, c: (c, 0, j))
    per_blk = lambda shape: pl.BlockSpec((nb,) + shape, lambda j, c: (j, 0, 0))
    out = pl.pallas_call(
        functools.partial(_s5_kernel, n_pairs=n_pairs),
        grid=(nblk // nb, seqlen // (2 * n_pairs)),
        in_specs=[blk,
                  per_blk((2 * SSM_GROUP, 2 * STATE_COLS)),
                  per_blk((2 * SSM_GROUP, 2 * STATE_COLS)),
                  per_blk((2 * SSM_GROUP, 2 * LANES)),
                  per_blk((1, STATE_COLS)),
                  per_blk((1, STATE_COLS)),
                  pl.BlockSpec((1, nb * LANES), lambda j, c: (0, j))],
        out_specs=blk,
        out_shape=jax.ShapeDtypeStruct((seqlen, bsz, width), F32),
        scratch_shapes=[pltpu.VMEM((nb, 2 * LANES, 2 * STATE_COLS), BF16),
                        pltpu.VMEM((nb, 2 * LANES, 2 * STATE_COLS), BF16),
                        pltpu.VMEM((nb, 2 * LANES, 2 * LANES), BF16),
                        pltpu.VMEM((nb, rows, 2 * STATE_COLS), F32),
                        pltpu.VMEM((nb, rows, 2 * STATE_COLS), BF16),
                        pltpu.VMEM((nb, 2, SUBLANES, STATE_COLS), F32)],
        compiler_params=_params(("arbitrary", "arbitrary")),
        name="s5_scan",
    )(u3, win, wout, wt, a2r, a2i, d_skip.reshape(1, width).astype(F32))
    return out.reshape(seqlen, bsz * width)


def _glu_out_kernel(yg_ref, z_ref, memo_ref, h_ref, wglu_ref, bglu_ref,
                    wmain_ref, wmem_ref, g_ref, gkv_ref, gnext_ref, wf_ref, bf_ref,
                    out_ref, xkv_ref, xnext_ref, fp_ref, carry_ref, *, heads, sub):
    @pl.when(pl.program_id(1) == 0)
    def _():
        carry_ref[...] = jnp.zeros_like(carry_ref)

    subs = [slice(s * sub, (s + 1) * sub) for s in range(out_ref.shape[0] // sub)]
    t = [jnp.dot(yg_ref[rows, :].astype(BF16), wglu_ref[...], preferred_element_type=F32)
         + bglu_ref[...] for rows in subs]
    for rows, tt in zip(subs, t):
        z = z_ref[rows, :].astype(F32)
        main = yg_ref[rows, :] * jax.nn.sigmoid(tt) * (z * jax.nn.sigmoid(z))
        o = jnp.dot(main.astype(BF16), wmain_ref[...], preferred_element_type=F32)
        o = o + jnp.dot(memo_ref[rows, :], wmem_ref[...], preferred_element_type=F32)
        ms = jnp.mean(o * o, axis=-1, keepdims=True)
        out_ref[rows, :] = h_ref[rows, :] + o * lax.rsqrt(ms + EPS) * g_ref[...]
    split = []
    for rows in subs:
        h1 = out_ref[rows, :]
        r = h1 * lax.rsqrt(jnp.mean(h1 * h1, axis=-1, keepdims=True) + EPS)
        xnext_ref[rows, :] = (r * gnext_ref[...]).astype(BF16)
        split.append(_split2(r * gkv_ref[...]))
        xkv_ref[rows, :] = split[-1][0]
    logit = [_fgate_logits(xh, xl, wf_ref, bf_ref) for xh, xl in split]
    for rows, lg in zip(subs, logit):
        fp_ref[rows, :], carry_ref[...] = _fgate_cumsum(lg, carry_ref[...], heads)


def _out_kernel(main_ref, memo_ref, h_ref, wmain_ref, wmem_ref, g_ref, out_ref, *, sub):
    subs = [slice(s * sub, (s + 1) * sub) for s in range(out_ref.shape[0] // sub)]
    o = []
    for rows in subs:
        main = jnp.concatenate([main_ref[hd, rows, :] for hd in range(main_ref.shape[0])],
                               axis=1)
        o.append(jnp.dot(main, wmain_ref[...], preferred_element_type=F32)
                 + jnp.dot(memo_ref[rows, :], wmem_ref[...], preferred_element_type=F32))
    for rows, oo in zip(subs, o):
        ms = jnp.mean(oo * oo, axis=-1, keepdims=True)
        out_ref[rows, :] = h_ref[rows, :] + oo * lax.rsqrt(ms + EPS) * g_ref[...]


def _resident(shape):
    return pl.BlockSpec(shape, lambda *_: (0,) * len(shape), pipeline_mode=pl.Buffered(1))


def _w_out_specs(layer, main_w, mem_w, d):
    assert main_w % mem_w == 0
    return [pl.BlockSpec((None, main_w, d), lambda *_: (layer, 0, 0),
                         pipeline_mode=pl.Buffered(1)),
            pl.BlockSpec((None, mem_w, d), lambda *_: (layer, main_w // mem_w, 0),
                         pipeline_mode=pl.Buffered(1))]


def _glu_out(yg_tm, rest, memo, h2d, w_glu, b_glu, w_out_bf, layer, g, g_kv, g_next,
             w_fgate, b_fgate, *, bsz, tl, sub):
    assert tl % sub == 0
    m, d = h2d.shape
    seqlen = m // bsz
    main_w = w_glu.shape[0]
    mem_w = w_out_bf.shape[1] - main_w
    heads = w_fgate.shape[1]
    assert 3 * heads <= LANES and seqlen % tl == 0
    wpad = jnp.zeros((d, LANES), F32).at[:, :heads].set(w_fgate.astype(F32))
    wcat = jnp.concatenate(_split2(wpad), axis=1)
    bpad = jnp.zeros((1, LANES), F32).at[0, :heads].set(b_fgate.astype(F32))
    per_b = seqlen // tl
    row = lambda b, i: (b * per_b + i, 0)
    vec = lambda v: v.reshape(1, -1).astype(F32)
    return pl.pallas_call(
        functools.partial(_glu_out_kernel, heads=heads, sub=sub),
        grid=(bsz, per_b),
        in_specs=[pl.BlockSpec((tl, main_w), lambda b, i: (i, b)),
                  pl.BlockSpec((tl, main_w), row),
                  pl.BlockSpec((tl, mem_w), row),
                  pl.BlockSpec((tl, d), row),
                  _resident((main_w, main_w)),
                  _resident((1, main_w))]
                 + _w_out_specs(layer, main_w, mem_w, d)
                 + [_resident((1, d)),
                    _resident((1, d)),
                    _resident((1, d)),
                    _resident((d, 2 * LANES)),
                    _resident((1, LANES))],
        out_specs=[pl.BlockSpec((tl, d), row),
                   pl.BlockSpec((tl, d), row),
                   pl.BlockSpec((tl, d), row),
                   pl.BlockSpec((tl, LANES), row)],
        out_shape=[jax.ShapeDtypeStruct((m, d), F32),
                   jax.ShapeDtypeStruct((m, d), BF16),
                   jax.ShapeDtypeStruct((m, d), BF16),
                   jax.ShapeDtypeStruct((m, LANES), BF16)],
        scratch_shapes=[pltpu.VMEM((1, LANES), F32)],
        compiler_params=_params(("arbitrary", "arbitrary")),
        name="glu_out_proj",
    )(yg_tm, rest, memo, h2d, w_glu.astype(BF16), vec(b_glu), w_out_bf, w_out_bf,
      vec(g), vec(g_kv), vec(g_next), wcat, bpad)


def _out_proj(main, memo, h2d, w_out_bf, layer, g, *, tl, sub):
    m, d = h2d.shape
    heads, _, dh = main.shape
    main_w = heads * dh
    mem_w = memo.shape[1]
    row = lambda i: (i, 0)
    return pl.pallas_call(
        functools.partial(_out_kernel, sub=sub),
        grid=(m // tl,),
        in_specs=[pl.BlockSpec((heads, tl, dh), lambda i: (0, i, 0)),
                  pl.BlockSpec((tl, mem_w), row),
                  pl.BlockSpec((tl, d), row)]
                 + _w_out_specs(layer, main_w, mem_w, d)
                 + [_resident((1, d))],
        out_specs=pl.BlockSpec((tl, d), row),
        out_shape=jax.ShapeDtypeStruct((m, d), F32),
        compiler_params=_params(("parallel",)),
        name="out_proj",
    )(main, memo, h2d, w_out_bf, w_out_bf, g.reshape(1, d).astype(F32))


FOX_CHUNK = 64


def _fox_kernel(q_ref, k_ref, v_ref, fp_ref, sel_ref, cst_ref, z_ref, o_ref,
                ka_ref, qa_ref, vt_ref, st_ref, pt_ref, m_ref, l_ref, acc_ref, *, tq):
    nh, seqlen, _ = q_ref.shape
    ncg = tq // LANES
    nch = tq // FOX_CHUNK
    for hh in range(nh):
        ext = jnp.dot(fp_ref[...], sel_ref[hh], preferred_element_type=F32)
        ka_ref[hh, :, :HEAD_DIM] = k_ref[hh]
        ka_ref[hh, :, HEAD_DIM:] = (ext[:, :HEAD_DIM] + cst_ref[0:1, :]).astype(BF16)
        qa_ref[hh, :, :HEAD_DIM] = q_ref[hh]
        qa_ref[hh, :, HEAD_DIM:] = (ext[:, HEAD_DIM:] + cst_ref[1:2, :]).astype(BF16)
        vt_ref[hh] = v_ref[hh].astype(F32).T.astype(BF16)
        for ci in range(nch):
            for g in range(ncg):
                if g * LANES + LANES - 1 < ci * FOX_CHUNK:
                    pt_ref[hh, 1, ci * FOX_CHUNK:(ci + 1) * FOX_CHUNK,
                           g * LANES:(g + 1) * LANES] = jnp.zeros((FOX_CHUNK, LANES), BF16)

    half = tq // 2

    def scores(hh, qi, kj, slot):
        nt = (((1,), (1,)), ((), ()))
        k0, q0 = kj * tq, qi * tq
        if kj < qi:
            st_ref[hh, slot] = lax.dot_general(ka_ref[hh, k0:k0 + tq, :], qa_ref[hh, q0:q0 + tq, :],
                                               nt, preferred_element_type=F32)
        else:
            st_ref[hh, slot, :half, :] = lax.dot_general(
                ka_ref[hh, k0:k0 + half, :], qa_ref[hh, q0:q0 + tq, :], nt,
                preferred_element_type=F32)
            st_ref[hh, slot, half:, half:] = lax.dot_general(
                ka_ref[hh, k0 + half:k0 + tq, :], qa_ref[hh, q0 + half:q0 + tq, :], nt,
                preferred_element_type=F32)

    def piece(hh, slot, ci, g, masked):
        r0, c0 = ci * FOX_CHUNK, g * LANES
        if masked and c0 + LANES - 1 < r0:
            return None
        x = st_ref[hh, slot, r0:r0 + FOX_CHUNK, c0:c0 + LANES]
        if masked and c0 < r0 + FOX_CHUNK - 1:
            key = r0 + lax.broadcasted_iota(jnp.int32, (FOX_CHUNK, LANES), 0)
            qry = c0 + lax.broadcasted_iota(jnp.int32, (FOX_CHUNK, LANES), 1)
            x = jnp.where(key <= qry, x, -jnp.inf)
        return x

    def fold(x, op):
        return op(x.reshape(FOX_CHUNK // SUBLANES, SUBLANES, LANES), axis=0)

    def tile_stats(hh, slot, masked, first):
        mx = [None] * ncg
        for ci in range(nch):
            for g in range(ncg):
                x = piece(hh, slot, ci, g, masked)
                if x is not None:
                    f = fold(x, jnp.max)
                    mx[g] = f if mx[g] is None else jnp.maximum(mx[g], f)
        m_new = jnp.concatenate([jnp.max(v, axis=0, keepdims=True) for v in mx], axis=1)
        alpha = None
        if not first:
            m_old = m_ref[hh]
            m_new = jnp.maximum(m_old, m_new)
            alpha = jnp.exp2(m_old - m_new)
        m_ref[hh] = m_new
        return m_new, alpha

    def tile_probs(hh, slot, masked, m_new):
        ls = [None] * ncg
        for ci in range(nch):
            for g in range(ncg):
                r0, c0 = ci * FOX_CHUNK, g * LANES
                x = piece(hh, slot, ci, g, masked)
                if x is None:
                    continue
                p = jnp.exp2(x - m_new[:, c0:c0 + LANES])
                f = fold(p, jnp.sum)
                ls[g] = f if ls[g] is None else ls[g] + f
                pt_ref[hh, int(masked), r0:r0 + FOX_CHUNK, c0:c0 + LANES] = p.astype(BF16)
        return jnp.concatenate([jnp.sum(v, axis=0, keepdims=True) for v in ls], axis=1)

    def tile_update(hh, kj, masked, alpha, l_new):
        k0 = kj * tq
        if masked:
            pv = jnp.dot(vt_ref[hh, :, k0:k0 + half], pt_ref[hh, 1, :half, :],
                         preferred_element_type=F32)
            late = jnp.dot(vt_ref[hh, :, k0 + half:k0 + tq], pt_ref[hh, 1, half:, half:],
                           preferred_element_type=F32)
            pv = jnp.concatenate([pv[:, :half], pv[:, half:] + late], axis=1)
        else:
            pv = jnp.dot(vt_ref[hh, :, k0:k0 + tq], pt_ref[hh, 0], preferred_element_type=F32)
        if alpha is None:
            l_ref[hh] = l_new
            acc_ref[hh] = pv
        else:
            l_ref[hh] = alpha * l_ref[hh] + l_new
            acc_ref[hh] = alpha * acc_ref[hh] + pv

    heads = range(nh)
    for qi in range(seqlen // tq):
        rows = slice(qi * tq, (qi + 1) * tq)
        for hh in heads:
            scores(hh, qi, 0, 0)
        for kj in range(qi + 1):
            slot, masked = kj % 2, kj == qi
            if kj < qi:
                for hh in heads:
                    scores(hh, qi, kj + 1, (kj + 1) % 2)
            stats = [tile_stats(hh, slot, masked, kj == 0) for hh in heads]
            sums = [tile_probs(hh, slot, masked, stats[hh][0]) for hh in heads]
            for hh in heads:
                tile_update(hh, kj, masked, stats[hh][1], sums[hh])
        for hh in range(nh):
            z = z_ref[hh, rows, :].astype(F32)
            o = (acc_ref[hh] / l_ref[hh]).T
            o_ref[hh, rows, :] = (o * (z * jax.nn.sigmoid(z))).astype(o_ref.dtype)


def _fox_attn(q, k, v, z, fparts, *, bsz, tq, heads_per_step):
    heads, m, _ = q.shape
    seqlen = m // bsz
    nterms = 3
    assert seqlen % tq == 0 and tq % FOX_CHUNK == 0 and nterms * heads <= LANES
    h_idx = jnp.arange(heads)[:, None, None]
    r_idx = jnp.arange(LANES)[None, :, None]
    c_idx = jnp.arange(2 * HEAD_DIM)[None, None, :]
    key_side = (c_idx < nterms) & (r_idx == c_idx * heads + h_idx)
    qc = c_idx - HEAD_DIM - nterms
    qry_side = (qc >= 0) & (qc < nterms) & (r_idx == qc * heads + h_idx)
    sel = (qry_side.astype(F32) - key_side.astype(F32)).astype(BF16)
    col = jnp.arange(HEAD_DIM)
    cst = jnp.zeros((SUBLANES, HEAD_DIM), F32)
    cst = cst.at[0].set(((col >= nterms) & (col < 2 * nterms)).astype(F32))
    cst = cst.at[1].set((col < nterms).astype(F32))
    nh = heads_per_step
    assert heads % nh == 0
    head_seq = pl.BlockSpec((nh, seqlen, HEAD_DIM), lambda b, h: (h, b, 0))
    return pl.pallas_call(
        functools.partial(_fox_kernel, tq=tq),
        grid=(bsz, heads // nh),
        in_specs=[head_seq, head_seq, head_seq,
                  pl.BlockSpec((seqlen, LANES), lambda b, h: (b, 0)),
                  pl.BlockSpec((nh, LANES, 2 * HEAD_DIM), lambda b, h: (h, 0, 0)),
                  pl.BlockSpec((SUBLANES, HEAD_DIM), lambda b, h: (0, 0)),
                  head_seq],
        out_specs=head_seq,
        out_shape=jax.ShapeDtypeStruct((heads, m, HEAD_DIM), BF16),
        scratch_shapes=[pltpu.VMEM((nh, seqlen, 2 * HEAD_DIM), BF16),
                        pltpu.VMEM((nh, seqlen, 2 * HEAD_DIM), BF16),
                        pltpu.VMEM((nh, HEAD_DIM, seqlen), BF16),
                        pltpu.VMEM((nh, 2, tq, tq), F32),
                        pltpu.VMEM((nh, 2, tq, tq), BF16),
                        pltpu.VMEM((nh, 1, tq), F32),
                        pltpu.VMEM((nh, 1, tq), F32),
                        pltpu.VMEM((nh, HEAD_DIM, tq), F32)],
        compiler_params=_params(("arbitrary", "arbitrary")),
        name="fox_attn",
    )(q, k, v, fparts, sel, cst, z)


def kernel(x, mem, pre_norm_g, post_norm_g, w_in_a, lam_re, lam_im, log_step, b_re, b_im,
           c_re, c_im, d_skip, w_glu, b_glu, kv_norm_g, w_kv, w_fgate, b_fgate, w_in_b,
           mem_norm_g, w_mem_kv, w_out):
    bsz, seqlen, d = x.shape
    n_mem = mem.shape[1]
    main_w = w_glu.shape[1]
    mem_w = w_out.shape[1] - main_w
    scale = HEAD_DIM ** -0.5
    x2d = x.reshape(bsz * seqlen, d)
    mem2d = mem.reshape(bsz * n_mem, d)

    ones = jnp.ones((main_w,), F32)
    mem_scale = jnp.full((mem_w,), scale * LOG2E, F32)
    cs_a = jnp.concatenate([ones, ones, mem_scale, ones[:mem_w]])
    cs_b = jnp.concatenate([ones * (scale * LOG2E), ones, mem_scale, ones[:mem_w]])
    proj = functools.partial(_matmul, out_dtype=BF16, tm=1024)
    u_tm, xa = _matmul(x2d, w_in_a[0], cs_a, col0=0, n=main_w, out_dtype=F32, tm=512,
                       tn=main_w, time_major_batches=bsz, norm_gain=pre_norm_g[0],
                       name="in_proj_a_u")
    z_a = proj(xa, w_in_a[0], cs_a, col0=main_w, n=main_w, tn=main_w, name="in_proj_a_z")
    qz_a = proj(xa, w_in_a[0], cs_a, col0=2 * main_w, n=2 * mem_w, tn=2 * mem_w,
                slab_width=HEAD_DIM, name="in_proj_a_mem")
    kvm = _norm_matmul(mem2d, mem_norm_g, w_mem_kv, out_dtype=BF16, tm=512, tn=512,
                       name="mem_kv")
    memo0 = _mem_attn(qz_a, kvm, 0, bsz=bsz, tq=1024, name="mem_attn0")
    s5_ops = _s5_discretise(lam_re[0], lam_im[0], log_step[0], b_re[0], b_im[0],
                            c_re[0], c_im[0])
    yg_tm = _s5(u_tm, *s5_ops, d_skip[0], bsz=bsz, n_pairs=128, blocks_per_step=3)
    w_out_bf = w_out.astype(BF16)
    h1, xkv, xb, fparts = _glu_out(
        yg_tm, z_a, memo0, x2d, w_glu[0], b_glu[0], w_out_bf, 0, post_norm_g[0],
        kv_norm_g, pre_norm_g[1], w_fgate, b_fgate, bsz=bsz, tl=512, sub=256)

    cs_kv = jnp.ones((w_kv.shape[1],), F32)
    k_sh = proj(xkv, w_kv, cs_kv, col0=0, n=main_w, tn=main_w, slab_width=HEAD_DIM,
                name="k_proj")
    v_sh = proj(xkv, w_kv, cs_kv, col0=main_w, n=main_w, tn=main_w, slab_width=HEAD_DIM,
                name="v_proj")

    q_b = proj(xb, w_in_b[0], cs_b, col0=0, n=main_w, tn=main_w, slab_width=HEAD_DIM,
               name="in_proj_b_q")
    z_b = proj(xb, w_in_b[0], cs_b, col0=main_w, n=main_w, tn=main_w, slab_width=HEAD_DIM,
               name="in_proj_b_z")
    qz_b = proj(xb, w_in_b[0], cs_b, col0=2 * main_w, n=2 * mem_w, tn=2 * mem_w,
                slab_width=HEAD_DIM, name="in_proj_b_mem")
    memo1 = _mem_attn(qz_b, kvm, 1, bsz=bsz, tq=1024, name="mem_attn1")
    att = _fox_attn(q_b, k_sh, v_sh, z_b, fparts, bsz=bsz, tq=512, heads_per_step=3)
    out = _out_proj(att, memo1, h1, w_out_bf, 1, post_norm_g[1], tl=512, sub=256)
    return out.reshape(bsz, seqlen, d)
```

```python
import functools

import jax
import jax.numpy as jnp
from jax import lax
from jax.experimental import pallas as pl
from jax.experimental.pallas import tpu as pltpu

F32 = jnp.float32
BF16 = jnp.bfloat16

EPS = 1e-6
LOG2E = 1.4426950408889634
HEAD_DIM = 128
SSM_GROUP = 16
SSM_STATE = 64
MEM_HEADS = 4
LANES = 128
SUBLANES = 8
GROUPS_PER_BLOCK = LANES // SSM_GROUP
STATE_COLS = GROUPS_PER_BLOCK * SSM_STATE
VMEM_LIMIT = 56 * 1024 * 1024


def _params(sem, vmem=VMEM_LIMIT):
    return pltpu.CompilerParams(dimension_semantics=sem, vmem_limit_bytes=vmem)


def _norm_matmul_kernel(x_ref, g_ref, w_ref, o_ref, r_ref, xn_ref):
    layer, j = pl.program_id(1), pl.program_id(2)

    @pl.when((layer == 0) & (j == 0))
    def _():
        x = x_ref[...]
        r_ref[...] = x * lax.rsqrt(jnp.mean(x * x, axis=-1, keepdims=True) + EPS)

    @pl.when(j == 0)
    def _():
        xn_ref[...] = (r_ref[...] * g_ref[...]).astype(BF16)

    o_ref[...] = jnp.dot(xn_ref[...], w_ref[...].astype(BF16),
                         preferred_element_type=F32).astype(o_ref.dtype)


def _norm_matmul(x2d, g, w, *, out_dtype, tm, tn, name):
    m, d = x2d.shape
    layers, _, n = w.shape
    assert m % tm == 0 and n % tn == 0
    return pl.pallas_call(
        _norm_matmul_kernel,
        grid=(m // tm, layers, n // tn),
        in_specs=[pl.BlockSpec((tm, d), lambda i, l, j: (i, 0)),
                  pl.BlockSpec((None, 1, d), lambda i, l, j: (l, 0, 0)),
                  pl.BlockSpec((None, d, tn), lambda i, l, j: (l, 0, j))],
        out_specs=pl.BlockSpec((None, tm, tn), lambda i, l, j: (l, i, j)),
        out_shape=jax.ShapeDtypeStruct((layers, m, n), out_dtype),
        scratch_shapes=[pltpu.VMEM((tm, d), F32), pltpu.VMEM((tm, d), BF16)],
        compiler_params=_params(("arbitrary", "arbitrary", "arbitrary")),
        name=name,
    )(x2d, g.reshape(layers, 1, d).astype(F32), w)


def _store_product(res, o_ref):
    if len(o_ref.shape) == 3:
        sw = o_ref.shape[2]
        for s in range(o_ref.shape[0]):
            o_ref[s] = res[:, s * sw:(s + 1) * sw]
    else:
        o_ref[...] = res


def _matmul_kernel(x_ref, w_ref, cs_ref, o_ref, wb_ref):
    @pl.when(pl.program_id(1) == 0)
    def _():
        wb_ref[...] = (w_ref[...] * cs_ref[...]).astype(BF16)

    res = jnp.dot(x_ref[...], wb_ref[...], preferred_element_type=F32).astype(o_ref.dtype)
    _store_product(res, o_ref)


def _norm_then_matmul_kernel(x_ref, g_ref, w_ref, cs_ref, o_ref, xn_ref, wb_ref):
    @pl.when(pl.program_id(1) == 0)
    def _():
        wb_ref[...] = (w_ref[...] * cs_ref[...]).astype(BF16)

    half = x_ref.shape[0] // 2
    xn = []
    for rows in (slice(0, half), slice(half, 2 * half)):
        x = x_ref[rows, :]
        ms = jnp.mean(x * x, axis=-1, keepdims=True)
        xn.append((x * lax.rsqrt(ms + EPS) * g_ref[...]).astype(BF16))
        xn_ref[rows, :] = xn[-1]
    res = jnp.concatenate(
        [jnp.dot(v, wb_ref[...], preferred_element_type=F32).astype(o_ref.dtype) for v in xn],
        axis=0)
    _store_product(res, o_ref)


def _matmul(xn, w, colscale, *, col0, n, out_dtype, tm, tn, time_major_batches=None,
            slab_width=None, norm_gain=None, name):
    m, d = xn.shape
    assert m % tm == 0 and n % tn == 0 and col0 % tn == 0
    nj = n // tn
    j0 = col0 // tn
    if slab_width is not None:
        assert time_major_batches is None and tn % slab_width == 0
        per_tile = tn // slab_width
        out_shape = jax.ShapeDtypeStruct((n // slab_width, m, slab_width), out_dtype)
        out_spec = pl.BlockSpec((per_tile, tm, slab_width), lambda j, i: (j, i, 0))
    elif time_major_batches is None:
        out_shape = jax.ShapeDtypeStruct((m, n), out_dtype)
        out_spec = pl.BlockSpec((tm, tn), lambda j, i: (i, j))
    else:
        bsz = time_major_batches
        seqlen = m // bsz
        assert seqlen % tm == 0
        per_b = seqlen // tm
        out_shape = jax.ShapeDtypeStruct((seqlen, bsz * n), out_dtype)
        out_spec = pl.BlockSpec((tm, tn), lambda j, i: (i % per_b, (i // per_b) * nj + j))
    x_spec = pl.BlockSpec((tm, d), lambda j, i: (i, 0))
    w_specs = [pl.BlockSpec((d, tn), lambda j, i: (0, j0 + j), pipeline_mode=pl.Buffered(1)),
               pl.BlockSpec((1, tn), lambda j, i: (0, j0 + j))]
    common = dict(grid=(nj, m // tm), scratch_shapes=[pltpu.VMEM((d, tn), BF16)],
                  compiler_params=_params(("arbitrary", "arbitrary")), name=name)
    cs = colscale.reshape(1, -1).astype(F32)
    if norm_gain is None:
        return pl.pallas_call(
            _matmul_kernel, in_specs=[x_spec] + w_specs, out_specs=out_spec,
            out_shape=out_shape, **common)(xn, w, cs)
    assert nj == 1
    return pl.pallas_call(
        _norm_then_matmul_kernel,
        in_specs=[x_spec, pl.BlockSpec((1, d), lambda j, i: (0, 0))] + w_specs,
        out_specs=[out_spec, x_spec],
        out_shape=[out_shape, jax.ShapeDtypeStruct((m, d), BF16)],
        **common)(xn, norm_gain.reshape(1, d).astype(F32), w, cs)


def _split2(x):
    hi = x.astype(BF16)
    return hi, (x - hi.astype(F32)).astype(BF16)


def _split3(x):
    hi = x.astype(BF16)
    r = x - hi.astype(F32)
    mid = r.astype(BF16)
    return hi, mid, (r - mid.astype(F32)).astype(BF16)


def _fgate_logits(xh, xl, wcat_ref, b_ref):
    both = jnp.dot(xh, wcat_ref[...], preferred_element_type=F32)
    return (both[:, :LANES] + both[:, LANES:]
            + jnp.dot(xl, wcat_ref[:, :LANES], preferred_element_type=F32)) + b_ref[...]


def _fgate_cumsum(logit, carry, heads):
    tl = logit.shape[0]
    logf = jnp.minimum(logit, 0.0) - jnp.log(1.0 + jnp.exp(-jnp.abs(logit)))
    t_idx = lax.broadcasted_iota(jnp.int32, (tl, tl), 0)
    s_idx = lax.broadcasted_iota(jnp.int32, (tl, tl), 1)
    tril = (s_idx <= t_idx).astype(BF16)
    csum = carry
    for part in _split3(logf):
        csum = csum + jnp.dot(tril, part, preferred_element_type=F32)
    lane = lax.broadcasted_iota(jnp.int32, (tl, LANES), 1)
    packed = jnp.zeros((tl, LANES), F32)
    for n, part in reversed(list(enumerate(_split3(csum * LOG2E)))):
        shifted = part.astype(F32) if n == 0 else pltpu.roll(part.astype(F32), n * heads, 1)
        packed = jnp.where(lane < (n + 1) * heads, shifted, packed)
    return packed.astype(BF16), csum[tl - 1:tl, :]


def _mem_attn_kernel(q_ref, zm_ref, kv_ref, o_ref):
    width = MEM_HEADS * HEAD_DIM
    heads = range(MEM_HEADS)
    cols = [slice(h * HEAD_DIM, (h + 1) * HEAD_DIM) for h in heads]
    st = [lax.dot_general(kv_ref[:, cols[h]], q_ref[h], (((1,), (1,)), ((), ())),
                          preferred_element_type=F32) for h in heads]
    vt = [kv_ref[:, width + h * HEAD_DIM:width + (h + 1) * HEAD_DIM].astype(F32).T.astype(BF16)
          for h in heads]
    p = [jnp.exp2(st[h] - jnp.max(st[h], axis=0, keepdims=True)) for h in heads]
    inv = [1.0 / jnp.sum(p[h], axis=0, keepdims=True) for h in heads]
    ot = [jnp.dot(vt[h], p[h].astype(BF16), preferred_element_type=F32) * inv[h] for h in heads]
    for h in heads:
        zm = zm_ref[h].astype(F32)
        o_ref[:, cols[h]] = (ot[h].T * (zm * jax.nn.sigmoid(zm))).astype(o_ref.dtype)


def _mem_attn(qz, kvm, layer, *, bsz, tq, name):
    width = MEM_HEADS * HEAD_DIM
    m = qz.shape[1]
    seqlen = m // bsz
    n_mem = kvm.shape[1] // bsz
    per_b = seqlen // tq
    spec = lambda blk: pl.BlockSpec((MEM_HEADS, tq, HEAD_DIM),
                                    lambda b, i: (blk, b * per_b + i, 0))
    return pl.pallas_call(
        _mem_attn_kernel,
        grid=(bsz, per_b),
        in_specs=[spec(0), spec(1),
                  pl.BlockSpec((None, n_mem, 2 * width), lambda b, i: (layer, b, 0))],
        out_specs=pl.BlockSpec((tq, width), lambda b, i: (b * per_b + i, 0)),
        out_shape=jax.ShapeDtypeStruct((m, width), BF16),
        compiler_params=_params(("parallel", "arbitrary")),
        name=name,
    )(qz, qz, kvm)


def _s5_kernel(u_ref, winc_ref, woutc_ref, wtc_ref, ar_ref, ai_ref, d_ref, y_ref,
               win_ref, wout_ref, wt_ref, bu_ref, xs_ref, st_ref, *, n_pairs):
    nblk = win_ref.shape[0]

    @pl.when(pl.program_id(1) == 0)
    def _():
        st_ref[...] = jnp.zeros_like(st_ref)
        for k in range(nblk):
            win_ref[k] = _s5_expand(winc_ref[k], SSM_STATE)
            wout_ref[k] = _s5_expand(woutc_ref[k], SSM_STATE)
            wt_ref[k] = _s5_expand(wtc_ref[k], SSM_GROUP)

    rows = n_pairs * SUBLANES
    u0, u1, ucat, a2, state = [], [], [], [], []
    for k in range(nblk):
        upair = u_ref[:, :, k * LANES:(k + 1) * LANES].reshape(n_pairs, 2, SUBLANES, LANES)
        u0.append(upair[:, 0].reshape(rows, LANES))
        u1.append(upair[:, 1].reshape(rows, LANES))
        ucat.append(jnp.concatenate([u0[k], u1[k]], axis=1).astype(BF16))
        bu_ref[k] = jnp.dot(ucat[k], win_ref[k], preferred_element_type=F32)
        a2.append((jnp.broadcast_to(ar_ref[k], (SUBLANES, STATE_COLS)),
                   jnp.broadcast_to(ai_ref[k], (SUBLANES, STATE_COLS))))
        state.append((st_ref[k, 0], st_ref[k, 1]))

    held = [None] * nblk
    for c in range(n_pairs):
        for k in range(nblk):
            xr, xi = state[k]
            cur = jnp.concatenate([xr, xi], axis=1)
            if c % 2 == 1:
                xs_ref[k, (c - 1) * SUBLANES:(c + 1) * SUBLANES, :] = jnp.concatenate(
                    [held[k], cur], axis=0).astype(BF16)
            held[k] = cur
            bur = bu_ref[k, c * SUBLANES:(c + 1) * SUBLANES, 0:STATE_COLS]
            bui = bu_ref[k, c * SUBLANES:(c + 1) * SUBLANES, STATE_COLS:2 * STATE_COLS]
            ar, ai = a2[k]
            state[k] = (ar * xr - ai * xi + bur, ar * xi + ai * xr + bui)

    for k in range(nblk):
        st_ref[k, 0] = state[k][0]
        st_ref[k, 1] = state[k][1]
    ys = [lax.dot_general(xs_ref[k], wout_ref[k], (((1,), (1,)), ((), ())),
                          preferred_element_type=F32)
          + jnp.dot(ucat[k], wt_ref[k], preferred_element_type=F32) for k in range(nblk)]
    for k in range(nblk):
        y = ys[k]
        d = d_ref[:, k * LANES:(k + 1) * LANES]
        y0 = jax.nn.gelu(y[:, :LANES] + d * u0[k]).reshape(n_pairs, SUBLANES, LANES)
        y1 = jax.nn.gelu(y[:, LANES:] + d * u1[k]).reshape(n_pairs, SUBLANES, LANES)
        y_ref[:, :, k * LANES:(k + 1) * LANES] = jnp.stack([y0, y1], axis=1).reshape(
            2 * n_pairs, SUBLANES, LANES)


def _s5_discretise(lam_re, lam_im, log_step, b_re, b_im, c_re, c_im):
    groups = lam_re.shape[0]
    nblk = groups // GROUPS_PER_BLOCK
    lr = lam_re.astype(F32)
    li = lam_im.astype(F32)
    dt = jnp.exp(log_step.astype(F32))[:, None]
    mag = jnp.exp(lr * dt)
    ar = mag * jnp.cos(li * dt)
    ai = mag * jnp.sin(li * dt)
    den = lr * lr + li * li
    cr = ((ar - 1.0) * lr + ai * li) / den
    ci = (ai * lr - (ar - 1.0) * li) / den
    br = b_re.astype(F32)
    bi = b_im.astype(F32)
    bbar_re = cr[..., None] * br - ci[..., None] * bi
    bbar_im = cr[..., None] * bi + ci[..., None] * br
    a2r = ar * ar - ai * ai
    a2i = 2.0 * ar * ai
    ab_re = ar[..., None] * bbar_re - ai[..., None] * bbar_im
    ab_im = ar[..., None] * bbar_im + ai[..., None] * bbar_re
    cre = c_re.astype(F32)
    cim = c_im.astype(F32)
    ca_re = cre * ar[:, None, :] - cim * ai[:, None, :]
    ca_im = cre * ai[:, None, :] + cim * ar[:, None, :]
    ca2_re = cre * a2r[:, None, :] - cim * a2i[:, None, :]
    ca2_im = cre * a2i[:, None, :] + cim * a2r[:, None, :]
    k0 = jnp.einsum('ghp,gpk->ghk', cre, bbar_re) - jnp.einsum('ghp,gpk->ghk', cim, bbar_im)
    k1 = jnp.einsum('ghp,gpk->ghk', cre, ab_re) - jnp.einsum('ghp,gpk->ghk', cim, ab_im)

    def per_block(x):
        return x.reshape((nblk, GROUPS_PER_BLOCK) + x.shape[1:])

    b_in = jnp.stack([jnp.stack([per_block(ab_re), per_block(ab_im)]),
                      jnp.stack([per_block(bbar_re), per_block(bbar_im)])])
    win_c = b_in.transpose(2, 0, 5, 1, 3, 4).reshape(nblk, 2 * SSM_GROUP, 2 * STATE_COLS)
    c_out = jnp.stack([jnp.stack([per_block(ca_re), per_block(ca2_re)]),
                       jnp.stack([-per_block(ca_im), -per_block(ca2_im)])])
    wout_c = c_out.transpose(2, 1, 4, 0, 3, 5).reshape(nblk, 2 * SSM_GROUP, 2 * STATE_COLS)
    k0b, k1b = per_block(k0), per_block(k1)
    k_mix = jnp.stack([jnp.stack([k0b, k1b]),
                       jnp.stack([jnp.zeros_like(k0b), k0b])])
    wt_c = k_mix.transpose(2, 0, 5, 1, 3, 4).reshape(nblk, 2 * SSM_GROUP, 2 * LANES)
    return (win_c, wout_c, wt_c,
            a2r.reshape(nblk, 1, STATE_COLS), a2i.reshape(nblk, 1, STATE_COLS))


def _s5_expand(compact, cols_per_group):
    width = compact.shape[1]
    col_group = (lax.broadcasted_iota(jnp.int32, (SSM_GROUP, width), 1)
                 // cols_per_group) % GROUPS_PER_BLOCK
    pieces = []
    for i in range(2):
        rows = compact[i * SSM_GROUP:(i + 1) * SSM_GROUP, :]
        for g in range(GROUPS_PER_BLOCK):
            pieces.append(jnp.where(col_group == g, rows, 0.0).astype(BF16))
    return jnp.concatenate(pieces, axis=0)


def _s5(u3, win, wout, wt, a2r, a2i, d_skip, *, n_pairs, blocks_per_step):
    seqlen, bsz, width = u3.shape
    assert bsz == SUBLANES and seqlen % (2 * n_pairs) == 0 and width % LANES == 0
    nblk = width // LANES
    assert nblk % blocks_per_step == 0 and n_pairs % 2 == 0
    nb = blocks_per_step
    rows = n_pairs * SUBLANES
    blk = pl.BlockSpec((2 * n_pairs, bsz, nb * LANES), lambda j, c: (c, 0, j))
    per_blk = lambda shape: pl.BlockSpec((nb,) + shape, lambda j, c: (j, 0, 0))
    return pl.pallas_call(
        functools.partial(_s5_kernel, n_pairs=n_pairs),
        grid=(nblk // nb, seqlen // (2 * n_pairs)),
        in_specs=[blk,
                  per_blk((2 * SSM_GROUP, 2 * STATE_COLS)),
                  per_blk((2 * SSM_GROUP, 2 * STATE_COLS)),
                  per_blk((2 * SSM_GROUP, 2 * LANES)),
                  per_blk((1, STATE_COLS)),
                  per_blk((1, STATE_COLS)),
                  pl.BlockSpec((1, nb * LANES), lambda j, c: (0, j))],
        out_specs=blk,
        out_shape=jax.ShapeDtypeStruct((seqlen, bsz, width), F32),
        scratch_shapes=[pltpu.VMEM((nb, 2 * LANES, 2 * STATE_COLS), BF16),
                        pltpu.VMEM((nb, 2 * LANES, 2 * STATE_COLS), BF16),
                        pltpu.VMEM((nb, 2 * LANES, 2 * LANES), BF16),
                        pltpu.VMEM((nb, rows, 2 * STATE_COLS), F32),
                        pltpu.VMEM((nb, rows, 2 * STATE_COLS), BF16),
                        pltpu.VMEM((nb, 2, SUBLANES, STATE_COLS), F32)],
        compiler_params=_params(("arbitrary", "arbitrary")),
        name="s5_scan",
    )(u3, win, wout, wt, a2r, a2i, d_skip.reshape(1, width).astype(F32))


def _glu_out_kernel(yg_ref, z_ref, memo_ref, h_ref, wglu_ref, bglu_ref,
                    wmain_ref, wmem_ref, g_ref, gkv_ref, gnext_ref, wf_ref, bf_ref,
                    out_ref, xkv_ref, xnext_ref, fp_ref, carry_ref, *, heads, sub):
    @pl.when(pl.program_id(1) == 0)
    def _():
        carry_ref[...] = jnp.zeros_like(carry_ref)

    subs = [slice(s * sub, (s + 1) * sub) for s in range(out_ref.shape[0] // sub)]
    t = [jnp.dot(yg_ref[rows, :].astype(BF16), wglu_ref[...], preferred_element_type=F32)
         + bglu_ref[...] for rows in subs]
    for rows, tt in zip(subs, t):
        z = z_ref[rows, :].astype(F32)
        main = yg_ref[rows, :] * jax.nn.sigmoid(tt) * (z * jax.nn.sigmoid(z))
        o = jnp.dot(main.astype(BF16), wmain_ref[...], preferred_element_type=F32)
        o = o + jnp.dot(memo_ref[rows, :], wmem_ref[...], preferred_element_type=F32)
        ms = jnp.mean(o * o, axis=-1, keepdims=True)
        out_ref[rows, :] = h_ref[rows, :] + o * lax.rsqrt(ms + EPS) * g_ref[...]
    split = []
    for rows in subs:
        h1 = out_ref[rows, :]
        r = h1 * lax.rsqrt(jnp.mean(h1 * h1, axis=-1, keepdims=True) + EPS)
        xnext_ref[rows, :] = (r * gnext_ref[...]).astype(BF16)
        split.append(_split2(r * gkv_ref[...]))
        xkv_ref[rows, :] = split[-1][0]
    logit = [_fgate_logits(xh, xl, wf_ref, bf_ref) for xh, xl in split]
    for rows, lg in zip(subs, logit):
        fp_ref[rows, :], carry_ref[...] = _fgate_cumsum(lg, carry_ref[...], heads)


def _out_kernel(main_ref, memo_ref, h_ref, wmain_ref, wmem_ref, g_ref, out_ref, *, sub):
    subs = [slice(s * sub, (s + 1) * sub) for s in range(out_ref.shape[0] // sub)]
    o = []
    for rows in subs:
        main = jnp.concatenate([main_ref[hd, rows, :] for hd in range(main_ref.shape[0])],
                               axis=1)
        o.append(jnp.dot(main, wmain_ref[...], preferred_element_type=F32)
                 + jnp.dot(memo_ref[rows, :], wmem_ref[...], preferred_element_type=F32))
    for rows, oo in zip(subs, o):
        ms = jnp.mean(oo * oo, axis=-1, keepdims=True)
        out_ref[rows, :] = h_ref[rows, :] + oo * lax.rsqrt(ms + EPS) * g_ref[...]


def _resident(shape):
    return pl.BlockSpec(shape, lambda *_: (0,) * len(shape), pipeline_mode=pl.Buffered(1))


def _w_out_specs(layer, main_w, mem_w, d):
    assert main_w % mem_w == 0
    return [pl.BlockSpec((None, main_w, d), lambda *_: (layer, 0, 0),
                         pipeline_mode=pl.Buffered(1)),
            pl.BlockSpec((None, mem_w, d), lambda *_: (layer, main_w // mem_w, 0),
                         pipeline_mode=pl.Buffered(1))]


def _glu_out(yg_tm, rest, memo, h2d, w_glu, b_glu, w_out_bf, layer, g, g_kv, g_next,
             w_fgate, b_fgate, *, bsz, tl, sub):
    assert tl % sub == 0
    m, d = h2d.shape
    seqlen = m // bsz
    main_w = w_glu.shape[0]
    mem_w = w_out_bf.shape[1] - main_w
    heads = w_fgate.shape[1]
    assert 3 * heads <= LANES and seqlen % tl == 0
    wpad = jnp.zeros((d, LANES), F32).at[:, :heads].set(w_fgate.astype(F32))
    wcat = jnp.concatenate(_split2(wpad), axis=1)
    bpad = jnp.zeros((1, LANES), F32).at[0, :heads].set(b_fgate.astype(F32))
    per_b = seqlen // tl
    row = lambda b, i: (b * per_b + i, 0)
    vec = lambda v: v.reshape(1, -1).astype(F32)
    return pl.pallas_call(
        functools.partial(_glu_out_kernel, heads=heads, sub=sub),
        grid=(bsz, per_b),
        in_specs=[pl.BlockSpec((tl, main_w), lambda b, i: (i, b)),
                  pl.BlockSpec((tl, main_w), row),
                  pl.BlockSpec((tl, mem_w), row),
                  pl.BlockSpec((tl, d), row),
                  _resident((main_w, main_w)),
                  _resident((1, main_w))]
                 + _w_out_specs(layer, main_w, mem_w, d)
                 + [_resident((1, d)),
                    _resident((1, d)),
                    _resident((1, d)),
                    _resident((d, 2 * LANES)),
                    _resident((1, LANES))],
        out_specs=[pl.BlockSpec((tl, d), row),
                   pl.BlockSpec((tl, d), row),
                   pl.BlockSpec((tl, d), row),
                   pl.BlockSpec((tl, LANES), row)],
        out_shape=[jax.ShapeDtypeStruct((m, d), F32),
                   jax.ShapeDtypeStruct((m, d), BF16),
                   jax.ShapeDtypeStruct((m, d), BF16),
                   jax.ShapeDtypeStruct((m, LANES), BF16)],
        scratch_shapes=[pltpu.VMEM((1, LANES), F32)],
        compiler_params=_params(("arbitrary", "arbitrary")),
        name="glu_out_proj",
    )(yg_tm, rest, memo, h2d, w_glu.astype(BF16), vec(b_glu), w_out_bf, w_out_bf,
      vec(g), vec(g_kv), vec(g_next), wcat, bpad)


def _out_proj(main, memo, h2d, w_out_bf, layer, g, *, tl, sub):
    m, d = h2d.shape
    heads, _, dh = main.shape
    main_w = heads * dh
    mem_w = memo.shape[1]
    row = lambda i: (i, 0)
    return pl.pallas_call(
        functools.partial(_out_kernel, sub=sub),
        grid=(m // tl,),
        in_specs=[pl.BlockSpec((heads, tl, dh), lambda i: (0, i, 0)),
                  pl.BlockSpec((tl, mem_w), row),
                  pl.BlockSpec((tl, d), row)]
                 + _w_out_specs(layer, main_w, mem_w, d)
                 + [_resident((1, d))],
        out_specs=pl.BlockSpec((tl, d), row),
        out_shape=jax.ShapeDtypeStruct((m, d), F32),
        compiler_params=_params(("parallel",)),
        name="out_proj",
    )(main, memo, h2d, w_out_bf, w_out_bf, g.reshape(1, d).astype(F32))


FOX_CHUNK = 64


def _fox_kernel(q_ref, k_ref, v_ref, fp_ref, sel_ref, cst_ref, z_ref, o_ref,
                ka_ref, qa_ref, vt_ref, st_ref, pt_ref, m_ref, l_ref, acc_ref, *, tq):
    nh, seqlen, _ = q_ref.shape
    ncg = tq // LANES
    nch = tq // FOX_CHUNK
    for hh in range(nh):
        ext = jnp.dot(fp_ref[...], sel_ref[hh], preferred_element_type=F32)
        ka_ref[hh, :, :HEAD_DIM] = k_ref[hh]
        ka_ref[hh, :, HEAD_DIM:] = (ext[:, :HEAD_DIM] + cst_ref[0:1, :]).astype(BF16)
        qa_ref[hh, :, :HEAD_DIM] = q_ref[hh]
        qa_ref[hh, :, HEAD_DIM:] = (ext[:, HEAD_DIM:] + cst_ref[1:2, :]).astype(BF16)
        vt_ref[hh] = v_ref[hh].astype(F32).T.astype(BF16)
        for ci in range(nch):
            for g in range(ncg):
                if g * LANES + LANES - 1 < ci * FOX_CHUNK:
                    pt_ref[hh, 1, ci * FOX_CHUNK:(ci + 1) * FOX_CHUNK,
                           g * LANES:(g + 1) * LANES] = jnp.zeros((FOX_CHUNK, LANES), BF16)

    half = tq // 2

    def scores(hh, qi, kj, slot):
        nt = (((1,), (1,)), ((), ()))
        k0, q0 = kj * tq, qi * tq
        if kj < qi:
            st_ref[hh, slot] = lax.dot_general(ka_ref[hh, k0:k0 + tq, :], qa_ref[hh, q0:q0 + tq, :],
                                               nt, preferred_element_type=F32)
        else:
            st_ref[hh, slot, :half, :] = lax.dot_general(
                ka_ref[hh, k0:k0 + half, :], qa_ref[hh, q0:q0 + tq, :], nt,
                preferred_element_type=F32)
            st_ref[hh, slot, half:, half:] = lax.dot_general(
                ka_ref[hh, k0 + half:k0 + tq, :], qa_ref[hh, q0 + half:q0 + tq, :], nt,
                preferred_element_type=F32)

    def piece(hh, slot, ci, g, masked):
        r0, c0 = ci * FOX_CHUNK, g * LANES
        if masked and c0 + LANES - 1 < r0:
            return None
        x = st_ref[hh, slot, r0:r0 + FOX_CHUNK, c0:c0 + LANES]
        if masked and c0 < r0 + FOX_CHUNK - 1:
            key = r0 + lax.broadcasted_iota(jnp.int32, (FOX_CHUNK, LANES), 0)
            qry = c0 + lax.broadcasted_iota(jnp.int32, (FOX_CHUNK, LANES), 1)
            x = jnp.where(key <= qry, x, -jnp.inf)
        return x

    def fold(x, op):
        return op(x.reshape(FOX_CHUNK // SUBLANES, SUBLANES, LANES), axis=0)

    def tile_stats(hh, slot, masked, first):
        mx = [None] * ncg
        for ci in range(nch):
            for g in range(ncg):
                x = piece(hh, slot, ci, g, masked)
                if x is not None:
                    f = fold(x, jnp.max)
                    mx[g] = f if mx[g] is None else jnp.maximum(mx[g], f)
        m_new = jnp.concatenate([jnp.max(v, axis=0, keepdims=True) for v in mx], axis=1)
        alpha = None
        if not first:
            m_old = m_ref[hh]
            m_new = jnp.maximum(m_old, m_new)
            alpha = jnp.exp2(m_old - m_new)
        m_ref[hh] = m_new
        return m_new, alpha

    def tile_probs(hh, slot, masked, m_new):
        ls = [None] * ncg
        for ci in range(nch):
            for g in range(ncg):
                r0, c0 = ci * FOX_CHUNK, g * LANES
                x = piece(hh, slot, ci, g, masked)
                if x is None:
                    continue
                p = jnp.exp2(x - m_new[:, c0:c0 + LANES])
                f = fold(p, jnp.sum)
                ls[g] = f if ls[g] is None else ls[g] + f
                pt_ref[hh, int(masked), r0:r0 + FOX_CHUNK, c0:c0 + LANES] = p.astype(BF16)
        return jnp.concatenate([jnp.sum(v, axis=0, keepdims=True) for v in ls], axis=1)

    def tile_update(hh, kj, masked, alpha, l_new):
        k0 = kj * tq
        if masked:
            pv = jnp.dot(vt_ref[hh, :, k0:k0 + half], pt_ref[hh, 1, :half, :],
                         preferred_element_type=F32)
            late = jnp.dot(vt_ref[hh, :, k0 + half:k0 + tq], pt_ref[hh, 1, half:, half:],
                           preferred_element_type=F32)
            pv = jnp.concatenate([pv[:, :half], pv[:, half:] + late], axis=1)
        else:
            pv = jnp.dot(vt_ref[hh, :, k0:k0 + tq], pt_ref[hh, 0], preferred_element_type=F32)
        if alpha is None:
            l_ref[hh] = l_new
            acc_ref[hh] = pv
        else:
            l_ref[hh] = alpha * l_ref[hh] + l_new
            acc_ref[hh] = alpha * acc_ref[hh] + pv

    heads = range(nh)
    for qi in range(seqlen // tq):
        rows = slice(qi * tq, (qi + 1) * tq)
        for hh in heads:
            scores(hh, qi, 0, 0)
        for kj in range(qi + 1):
            slot, masked = kj % 2, kj == qi
            if kj < qi:
                for hh in heads:
                    scores(hh, qi, kj + 1, (kj + 1) % 2)
            stats = [tile_stats(hh, slot, masked, kj == 0) for hh in heads]
            sums = [tile_probs(hh, slot, masked, stats[hh][0]) for hh in heads]
            for hh in heads:
                tile_update(hh, kj, masked, stats[hh][1], sums[hh])
        for hh in range(nh):
            z = z_ref[hh, rows, :].astype(F32)
            o = (acc_ref[hh] / l_ref[hh]).T
            o_ref[hh, rows, :] = (o * (z * jax.nn.sigmoid(z))).astype(o_ref.dtype)


def _fox_attn(q, k, v, z, fparts, *, bsz, tq, heads_per_step):
    heads, m, _ = q.shape
    seqlen = m // bsz
    nterms = 3
    assert seqlen % tq == 0 and tq % FOX_CHUNK == 0 and nterms * heads <= LANES
    h_idx = jnp.arange(heads)[:, None, None]
    r_idx = jnp.arange(LANES)[None, :, None]
    c_idx = jnp.arange(2 * HEAD_DIM)[None, None, :]
    key_side = (c_idx < nterms) & (r_idx == c_idx * heads + h_idx)
    qc = c_idx - HEAD_DIM - nterms
    qry_side = (qc >= 0) & (qc < nterms) & (r_idx == qc * heads + h_idx)
    sel = (qry_side.astype(F32) - key_side.astype(F32)).astype(BF16)
    col = jnp.arange(HEAD_DIM)
    cst = jnp.zeros((SUBLANES, HEAD_DIM), F32)
    cst = cst.at[0].set(((col >= nterms) & (col < 2 * nterms)).astype(F32))
    cst = cst.at[1].set((col < nterms).astype(F32))
    nh = heads_per_step
    assert heads % nh == 0
    head_seq = pl.BlockSpec((nh, seqlen, HEAD_DIM), lambda b, h: (h, b, 0))
    return pl.pallas_call(
        functools.partial(_fox_kernel, tq=tq),
        grid=(bsz, heads // nh),
        in_specs=[head_seq, head_seq, head_seq,
                  pl.BlockSpec((seqlen, LANES), lambda b, h: (b, 0)),
                  pl.BlockSpec((nh, LANES, 2 * HEAD_DIM), lambda b, h: (h, 0, 0)),
                  pl.BlockSpec((SUBLANES, HEAD_DIM), lambda b, h: (0, 0)),
                  head_seq],
        out_specs=head_seq,
        out_shape=jax.ShapeDtypeStruct((heads, m, HEAD_DIM), BF16),
        scratch_shapes=[pltpu.VMEM((nh, seqlen, 2 * HEAD_DIM), BF16),
                        pltpu.VMEM((nh, seqlen, 2 * HEAD_DIM), BF16),
                        pltpu.VMEM((nh, HEAD_DIM, seqlen), BF16),
                        pltpu.VMEM((nh, 2, tq, tq), F32),
                        pltpu.VMEM((nh, 2, tq, tq), BF16),
                        pltpu.VMEM((nh, 1, tq), F32),
                        pltpu.VMEM((nh, 1, tq), F32),
                        pltpu.VMEM((nh, HEAD_DIM, tq), F32)],
        compiler_params=_params(("arbitrary", "arbitrary")),
        name="fox_attn",
    )(q, k, v, fparts, sel, cst, z)


def kernel(x, mem, pre_norm_g, post_norm_g, w_in_a, lam_re, lam_im, log_step, b_re, b_im,
           c_re, c_im, d_skip, w_glu, b_glu, kv_norm_g, w_kv, w_fgate, b_fgate, w_in_b,
           mem_norm_g, w_mem_kv, w_out):
    bsz, seqlen, d = x.shape
    n_mem = mem.shape[1]
    main_w = w_glu.shape[1]
    mem_w = w_out.shape[1] - main_w
    scale = HEAD_DIM ** -0.5
    x2d = x.reshape(bsz * seqlen, d)
    mem2d = mem.reshape(bsz * n_mem, d)

    ones = jnp.ones((main_w,), F32)
    mem_scale = jnp.full((mem_w,), scale * LOG2E, F32)
    cs_a = jnp.concatenate([ones, ones, mem_scale, ones[:mem_w]])
    cs_b = jnp.concatenate([ones * (scale * LOG2E), ones, mem_scale, ones[:mem_w]])
    proj = functools.partial(_matmul, out_dtype=BF16, tm=1024)
    u_tm, xa = _matmul(x2d, w_in_a[0], cs_a, col0=0, n=main_w, out_dtype=F32, tm=512,
                       tn=main_w, time_major_batches=bsz, norm_gain=pre_norm_g[0],
                       name="in_proj_a_u")
    qz_a = proj(xa, w_in_a[0], cs_a, col0=2 * main_w, n=2 * mem_w, tn=2 * mem_w,
                slab_width=HEAD_DIM, name="in_proj_a_mem")
    kvm = _norm_matmul(mem2d, mem_norm_g, w_mem_kv, out_dtype=BF16, tm=512, tn=512,
                       name="mem_kv")
    s5_ops = _s5_discretise(lam_re[0], lam_im[0], log_step[0], b_re[0], b_im[0],
                            c_re[0], c_im[0])
    u3 = u_tm.reshape(seqlen, bsz, main_w)
    u3, qz_a, kvm = lax.optimization_barrier((u3, qz_a, kvm))
    yg3 = _s5(u3, *s5_ops, d_skip[0], n_pairs=128, blocks_per_step=3)
    yg_tm = yg3.reshape(seqlen, bsz * main_w)
    z_a = proj(xa, w_in_a[0], cs_a, col0=main_w, n=main_w, tn=main_w, name="in_proj_a_z")
    memo0 = _mem_attn(qz_a, kvm, 0, bsz=bsz, tq=1024, name="mem_attn0")
    w_out_bf = w_out.astype(BF16)
    h1, xkv, xb, fparts = _glu_out(
        yg_tm, z_a, memo0, x2d, w_glu[0], b_glu[0], w_out_bf, 0, post_norm_g[0],
        kv_norm_g, pre_norm_g[1], w_fgate, b_fgate, bsz=bsz, tl=512, sub=256)

    cs_kv = jnp.ones((w_kv.shape[1],), F32)
    k_sh = proj(xkv, w_kv, cs_kv, col0=0, n=main_w, tn=main_w, slab_width=HEAD_DIM,
                name="k_proj")
    v_sh = proj(xkv, w_kv, cs_kv, col0=main_w, n=main_w, tn=main_w, slab_width=HEAD_DIM,
                name="v_proj")

    q_b = proj(xb, w_in_b[0], cs_b, col0=0, n=main_w, tn=main_w, slab_width=HEAD_DIM,
               name="in_proj_b_q")
    z_b = proj(xb, w_in_b[0], cs_b, col0=main_w, n=main_w, tn=main_w, slab_width=HEAD_DIM,
               name="in_proj_b_z")
    qz_b = proj(xb, w_in_b[0], cs_b, col0=2 * main_w, n=2 * mem_w, tn=2 * mem_w,
                slab_width=HEAD_DIM, name="in_proj_b_mem")
    memo1 = _mem_attn(qz_b, kvm, 1, bsz=bsz, tq=1024, name="mem_attn1")
    att = _fox_attn(q_b, k_sh, v_sh, z_b, fparts, bsz=bsz, tq=512, heads_per_step=3)
    out = _out_proj(att, memo1, h1, w_out_bf, 1, post_norm_g[1], tl=512, sub=256)
    return out.reshape(bsz, seqlen, d)
```

```python
import functools

import jax
import jax.numpy as jnp
from jax import lax
from jax.experimental import pallas as pl
from jax.experimental.pallas import tpu as pltpu

F32 = jnp.float32
BF16 = jnp.bfloat16

EPS = 1e-6
LOG2E = 1.4426950408889634
HEAD_DIM = 128
SSM_GROUP = 16
SSM_STATE = 64
MEM_HEADS = 4
LANES = 128
SUBLANES = 8
GROUPS_PER_BLOCK = LANES // SSM_GROUP
STATE_COLS = GROUPS_PER_BLOCK * SSM_STATE
VMEM_LIMIT = 56 * 1024 * 1024


def _params(sem, vmem=VMEM_LIMIT):
    return pltpu.CompilerParams(dimension_semantics=sem, vmem_limit_bytes=vmem)


def _norm_matmul_kernel(x_ref, g_ref, w_ref, o_ref, r_ref, xn_ref):
    layer, j = pl.program_id(1), pl.program_id(2)

    @pl.when((layer == 0) & (j == 0))
    def _():
        x = x_ref[...]
        r_ref[...] = x * lax.rsqrt(jnp.mean(x * x, axis=-1, keepdims=True) + EPS)

    @pl.when(j == 0)
    def _():
        xn_ref[...] = (r_ref[...] * g_ref[...]).astype(BF16)

    o_ref[...] = jnp.dot(xn_ref[...], w_ref[...].astype(BF16),
                         preferred_element_type=F32).astype(o_ref.dtype)


def _norm_matmul(x2d, g, w, *, out_dtype, tm, tn, name):
    m, d = x2d.shape
    layers, _, n = w.shape
    assert m % tm == 0 and n % tn == 0
    return pl.pallas_call(
        _norm_matmul_kernel,
        grid=(m // tm, layers, n // tn),
        in_specs=[pl.BlockSpec((tm, d), lambda i, l, j: (i, 0)),
                  pl.BlockSpec((None, 1, d), lambda i, l, j: (l, 0, 0)),
                  pl.BlockSpec((None, d, tn), lambda i, l, j: (l, 0, j))],
        out_specs=pl.BlockSpec((None, tm, tn), lambda i, l, j: (l, i, j)),
        out_shape=jax.ShapeDtypeStruct((layers, m, n), out_dtype),
        scratch_shapes=[pltpu.VMEM((tm, d), F32), pltpu.VMEM((tm, d), BF16)],
        compiler_params=_params(("arbitrary", "arbitrary", "arbitrary")),
        name=name,
    )(x2d, g.reshape(layers, 1, d).astype(F32), w)


def _store_product(res, o_ref):
    if len(o_ref.shape) == 3:
        sw = o_ref.shape[2]
        for s in range(o_ref.shape[0]):
            o_ref[s] = res[:, s * sw:(s + 1) * sw]
    else:
        o_ref[...] = res


def _matmul_kernel(x_ref, w_ref, cs_ref, o_ref, wb_ref):
    @pl.when(pl.program_id(1) == 0)
    def _():
        wb_ref[...] = (w_ref[...] * cs_ref[...]).astype(BF16)

    res = jnp.dot(x_ref[...], wb_ref[...], preferred_element_type=F32).astype(o_ref.dtype)
    _store_product(res, o_ref)


def _norm_then_matmul_kernel(x_ref, g_ref, w_ref, cs_ref, o_ref, xn_ref, wb_ref):
    @pl.when(pl.program_id(1) == 0)
    def _():
        wb_ref[...] = (w_ref[...] * cs_ref[...]).astype(BF16)

    half = x_ref.shape[0] // 2
    xn = []
    for rows in (slice(0, half), slice(half, 2 * half)):
        x = x_ref[rows, :]
        ms = jnp.mean(x * x, axis=-1, keepdims=True)
        xn.append((x * lax.rsqrt(ms + EPS) * g_ref[...]).astype(BF16))
        xn_ref[rows, :] = xn[-1]
    res = jnp.concatenate(
        [jnp.dot(v, wb_ref[...], preferred_element_type=F32).astype(o_ref.dtype) for v in xn],
        axis=0)
    _store_product(res, o_ref)


def _matmul(xn, w, colscale, *, col0, n, out_dtype, tm, tn, time_major_batches=None,
            slab_width=None, norm_gain=None, name):
    m, d = xn.shape
    assert m % tm == 0 and n % tn == 0 and col0 % tn == 0
    nj = n // tn
    j0 = col0 // tn
    if slab_width is not None:
        assert time_major_batches is None and tn % slab_width == 0
        per_tile = tn // slab_width
        out_shape = jax.ShapeDtypeStruct((n // slab_width, m, slab_width), out_dtype)
        out_spec = pl.BlockSpec((per_tile, tm, slab_width), lambda j, i: (j, i, 0))
    elif time_major_batches is None:
        out_shape = jax.ShapeDtypeStruct((m, n), out_dtype)
        out_spec = pl.BlockSpec((tm, tn), lambda j, i: (i, j))
    else:
        bsz = time_major_batches
        seqlen = m // bsz
        assert seqlen % tm == 0
        per_b = seqlen // tm
        out_shape = jax.ShapeDtypeStruct((seqlen, bsz * n), out_dtype)
        out_spec = pl.BlockSpec((tm, tn), lambda j, i: (i % per_b, (i // per_b) * nj + j))
    x_spec = pl.BlockSpec((tm, d), lambda j, i: (i, 0))
    w_specs = [pl.BlockSpec((d, tn), lambda j, i: (0, j0 + j), pipeline_mode=pl.Buffered(1)),
               pl.BlockSpec((1, tn), lambda j, i: (0, j0 + j))]
    common = dict(grid=(nj, m // tm), scratch_shapes=[pltpu.VMEM((d, tn), BF16)],
                  compiler_params=_params(("arbitrary", "arbitrary")), name=name)
    cs = colscale.reshape(1, -1).astype(F32)
    if norm_gain is None:
        return pl.pallas_call(
            _matmul_kernel, in_specs=[x_spec] + w_specs, out_specs=out_spec,
            out_shape=out_shape, **common)(xn, w, cs)
    assert nj == 1
    return pl.pallas_call(
        _norm_then_matmul_kernel,
        in_specs=[x_spec, pl.BlockSpec((1, d), lambda j, i: (0, 0))] + w_specs,
        out_specs=[out_spec, x_spec],
        out_shape=[out_shape, jax.ShapeDtypeStruct((m, d), BF16)],
        **common)(xn, norm_gain.reshape(1, d).astype(F32), w, cs)


def _split2(x):
    hi = x.astype(BF16)
    return hi, (x - hi.astype(F32)).astype(BF16)


def _split3(x):
    hi = x.astype(BF16)
    r = x - hi.astype(F32)
    mid = r.astype(BF16)
    return hi, mid, (r - mid.astype(F32)).astype(BF16)


def _fgate_logits(xh, xl, wcat_ref, b_ref):
    both = jnp.dot(xh, wcat_ref[...], preferred_element_type=F32)
    return (both[:, :LANES] + both[:, LANES:]
            + jnp.dot(xl, wcat_ref[:, :LANES], preferred_element_type=F32)) + b_ref[...]


def _fgate_cumsum(logit, carry, heads):
    tl = logit.shape[0]
    logf = jnp.minimum(logit, 0.0) - jnp.log(1.0 + jnp.exp(-jnp.abs(logit)))
    t_idx = lax.broadcasted_iota(jnp.int32, (tl, tl), 0)
    s_idx = lax.broadcasted_iota(jnp.int32, (tl, tl), 1)
    tril = (s_idx <= t_idx).astype(BF16)
    csum = carry
    for part in _split3(logf):
        csum = csum + jnp.dot(tril, part, preferred_element_type=F32)
    lane = lax.broadcasted_iota(jnp.int32, (tl, LANES), 1)
    packed = jnp.zeros((tl, LANES), F32)
    for n, part in reversed(list(enumerate(_split3(csum * LOG2E)))):
        shifted = part.astype(F32) if n == 0 else pltpu.roll(part.astype(F32), n * heads, 1)
        packed = jnp.where(lane < (n + 1) * heads, shifted, packed)
    return packed.astype(BF16), csum[tl - 1:tl, :]


def _mem_attn_kernel(q_ref, zm_ref, kv_ref, o_ref):
    width = MEM_HEADS * HEAD_DIM
    heads = range(MEM_HEADS)
    cols = [slice(h * HEAD_DIM, (h + 1) * HEAD_DIM) for h in heads]
    st = [lax.dot_general(kv_ref[:, cols[h]], q_ref[h], (((1,), (1,)), ((), ())),
                          preferred_element_type=F32) for h in heads]
    vt = [kv_ref[:, width + h * HEAD_DIM:width + (h + 1) * HEAD_DIM].astype(F32).T.astype(BF16)
          for h in heads]
    p = [jnp.exp2(st[h] - jnp.max(st[h], axis=0, keepdims=True)) for h in heads]
    inv = [1.0 / jnp.sum(p[h], axis=0, keepdims=True) for h in heads]
    ot = [jnp.dot(vt[h], p[h].astype(BF16), preferred_element_type=F32) * inv[h] for h in heads]
    for h in heads:
        zm = zm_ref[h].astype(F32)
        o_ref[:, cols[h]] = (ot[h].T * (zm * jax.nn.sigmoid(zm))).astype(o_ref.dtype)


def _mem_attn(qz, kvm, layer, *, bsz, tq, name):
    width = MEM_HEADS * HEAD_DIM
    m = qz.shape[1]
    seqlen = m // bsz
    n_mem = kvm.shape[1] // bsz
    per_b = seqlen // tq
    spec = lambda blk: pl.BlockSpec((MEM_HEADS, tq, HEAD_DIM),
                                    lambda b, i: (blk, b * per_b + i, 0))
    return pl.pallas_call(
        _mem_attn_kernel,
        grid=(bsz, per_b),
        in_specs=[spec(0), spec(1),
                  pl.BlockSpec((None, n_mem, 2 * width), lambda b, i: (layer, b, 0))],
        out_specs=pl.BlockSpec((tq, width), lambda b, i: (b * per_b + i, 0)),
        out_shape=jax.ShapeDtypeStruct((m, width), BF16),
        compiler_params=_params(("parallel", "arbitrary")),
        name=name,
    )(qz, qz, kvm)


def _s5_kernel(u_ref, winc_ref, woutc_ref, wtc_ref, ar_ref, ai_ref, d_ref, y_ref,
               win_ref, wout_ref, wt_ref, bu_ref, xs_ref, st_ref, *, n_pairs):
    nblk = win_ref.shape[0]

    @pl.when(pl.program_id(1) == 0)
    def _():
        st_ref[...] = jnp.zeros_like(st_ref)
        for k in range(nblk):
            win_ref[k] = _s5_expand(winc_ref[k], SSM_STATE)
            wout_ref[k] = _s5_expand(woutc_ref[k], SSM_STATE)
            wt_ref[k] = _s5_expand(wtc_ref[k], SSM_GROUP)

    rows = n_pairs * SUBLANES
    u0, u1, ucat, a2, state = [], [], [], [], []
    for k in range(nblk):
        upair = u_ref[:, :, k * LANES:(k + 1) * LANES].reshape(n_pairs, 2, SUBLANES, LANES)
        u0.append(upair[:, 0].reshape(rows, LANES))
        u1.append(upair[:, 1].reshape(rows, LANES))
        ucat.append(jnp.concatenate([u0[k], u1[k]], axis=1).astype(BF16))
        bu_ref[k] = jnp.dot(ucat[k], win_ref[k], preferred_element_type=F32)
        a2.append((jnp.broadcast_to(ar_ref[k], (SUBLANES, STATE_COLS)),
                   jnp.broadcast_to(ai_ref[k], (SUBLANES, STATE_COLS))))
        state.append((st_ref[k, 0], st_ref[k, 1]))

    held = [None] * nblk
    for c in range(n_pairs):
        for k in range(nblk):
            xr, xi = state[k]
            cur = jnp.concatenate([xr, xi], axis=1)
            if c % 2 == 1:
                xs_ref[k, (c - 1) * SUBLANES:(c + 1) * SUBLANES, :] = jnp.concatenate(
                    [held[k], cur], axis=0).astype(BF16)
            held[k] = cur
            bur = bu_ref[k, c * SUBLANES:(c + 1) * SUBLANES, 0:STATE_COLS]
            bui = bu_ref[k, c * SUBLANES:(c + 1) * SUBLANES, STATE_COLS:2 * STATE_COLS]
            ar, ai = a2[k]
            state[k] = (ar * xr - ai * xi + bur, ar * xi + ai * xr + bui)

    for k in range(nblk):
        st_ref[k, 0] = state[k][0]
        st_ref[k, 1] = state[k][1]
    ys = [lax.dot_general(xs_ref[k], wout_ref[k], (((1,), (1,)), ((), ())),
                          preferred_element_type=F32)
          + jnp.dot(ucat[k], wt_ref[k], preferred_element_type=F32) for k in range(nblk)]
    for k in range(nblk):
        y = ys[k]
        d = d_ref[:, k * LANES:(k + 1) * LANES]
        y0 = jax.nn.gelu(y[:, :LANES] + d * u0[k]).reshape(n_pairs, SUBLANES, LANES)
        y1 = jax.nn.gelu(y[:, LANES:] + d * u1[k]).reshape(n_pairs, SUBLANES, LANES)
        y_ref[:, :, k * LANES:(k + 1) * LANES] = jnp.stack([y0, y1], axis=1).reshape(
            2 * n_pairs, SUBLANES, LANES)


def _s5_discretise(lam_re, lam_im, log_step, b_re, b_im, c_re, c_im):
    groups = lam_re.shape[0]
    nblk = groups // GROUPS_PER_BLOCK
    lr = lam_re.astype(F32)
    li = lam_im.astype(F32)
    dt = jnp.exp(log_step.astype(F32))[:, None]
    mag = jnp.exp(lr * dt)
    ar = mag * jnp.cos(li * dt)
    ai = mag * jnp.sin(li * dt)
    den = lr * lr + li * li
    cr = ((ar - 1.0) * lr + ai * li) / den
    ci = (ai * lr - (ar - 1.0) * li) / den
    br = b_re.astype(F32)
    bi = b_im.astype(F32)
    bbar_re = cr[..., None] * br - ci[..., None] * bi
    bbar_im = cr[..., None] * bi + ci[..., None] * br
    a2r = ar * ar - ai * ai
    a2i = 2.0 * ar * ai
    ab_re = ar[..., None] * bbar_re - ai[..., None] * bbar_im
    ab_im = ar[..., None] * bbar_im + ai[..., None] * bbar_re
    cre = c_re.astype(F32)
    cim = c_im.astype(F32)
    ca_re = cre * ar[:, None, :] - cim * ai[:, None, :]
    ca_im = cre * ai[:, None, :] + cim * ar[:, None, :]
    ca2_re = cre * a2r[:, None, :] - cim * a2i[:, None, :]
    ca2_im = cre * a2i[:, None, :] + cim * a2r[:, None, :]
    k0 = jnp.einsum('ghp,gpk->ghk', cre, bbar_re) - jnp.einsum('ghp,gpk->ghk', cim, bbar_im)
    k1 = jnp.einsum('ghp,gpk->ghk', cre, ab_re) - jnp.einsum('ghp,gpk->ghk', cim, ab_im)

    def per_block(x):
        return x.reshape((nblk, GROUPS_PER_BLOCK) + x.shape[1:])

    b_in = jnp.stack([jnp.stack([per_block(ab_re), per_block(ab_im)]),
                      jnp.stack([per_block(bbar_re), per_block(bbar_im)])])
    win_c = b_in.transpose(2, 0, 5, 1, 3, 4).reshape(nblk, 2 * SSM_GROUP, 2 * STATE_COLS)
    c_out = jnp.stack([jnp.stack([per_block(ca_re), per_block(ca2_re)]),
                       jnp.stack([-per_block(ca_im), -per_block(ca2_im)])])
    wout_c = c_out.transpose(2, 1, 4, 0, 3, 5).reshape(nblk, 2 * SSM_GROUP, 2 * STATE_COLS)
    k0b, k1b = per_block(k0), per_block(k1)
    k_mix = jnp.stack([jnp.stack([k0b, k1b]),
                       jnp.stack([jnp.zeros_like(k0b), k0b])])
    wt_c = k_mix.transpose(2, 0, 5, 1, 3, 4).reshape(nblk, 2 * SSM_GROUP, 2 * LANES)
    return (win_c, wout_c, wt_c,
            a2r.reshape(nblk, 1, STATE_COLS), a2i.reshape(nblk, 1, STATE_COLS))


def _s5_expand(compact, cols_per_group):
    width = compact.shape[1]
    col_group = (lax.broadcasted_iota(jnp.int32, (SSM_GROUP, width), 1)
                 // cols_per_group) % GROUPS_PER_BLOCK
    pieces = []
    for i in range(2):
        rows = compact[i * SSM_GROUP:(i + 1) * SSM_GROUP, :]
        for g in range(GROUPS_PER_BLOCK):
            pieces.append(jnp.where(col_group == g, rows, 0.0).astype(BF16))
    return jnp.concatenate(pieces, axis=0)


def _s5(u3, win, wout, wt, a2r, a2i, d_skip, *, n_pairs, blocks_per_step):
    seqlen, bsz, width = u3.shape
    assert bsz == SUBLANES and seqlen % (2 * n_pairs) == 0 and width % LANES == 0
    nblk = width // LANES
    assert nblk % blocks_per_step == 0 and n_pairs % 2 == 0
    nb = blocks_per_step
    rows = n_pairs * SUBLANES
    blk = pl.BlockSpec((2 * n_pairs, bsz, nb * LANES), lambda j, c: (c, 0, j))
    per_blk = lambda shape: pl.BlockSpec((nb,) + shape, lambda j, c: (j, 0, 0))
    return pl.pallas_call(
        functools.partial(_s5_kernel, n_pairs=n_pairs),
        grid=(nblk // nb, seqlen // (2 * n_pairs)),
        in_specs=[blk,
                  per_blk((2 * SSM_GROUP, 2 * STATE_COLS)),
                  per_blk((2 * SSM_GROUP, 2 * STATE_COLS)),
                  per_blk((2 * SSM_GROUP, 2 * LANES)),
                  per_blk((1, STATE_COLS)),
                  per_blk((1, STATE_COLS)),
                  pl.BlockSpec((1, nb * LANES), lambda j, c: (0, j))],
        out_specs=blk,
        out_shape=jax.ShapeDtypeStruct((seqlen, bsz, width), F32),
        scratch_shapes=[pltpu.VMEM((nb, 2 * LANES, 2 * STATE_COLS), BF16),
                        pltpu.VMEM((nb, 2 * LANES, 2 * STATE_COLS), BF16),
                        pltpu.VMEM((nb, 2 * LANES, 2 * LANES), BF16),
                        pltpu.VMEM((nb, rows, 2 * STATE_COLS), F32),
                        pltpu.VMEM((nb, rows, 2 * STATE_COLS), BF16),
                        pltpu.VMEM((nb, 2, SUBLANES, STATE_COLS), F32)],
        compiler_params=_params(("arbitrary", "arbitrary")),
        name="s5_scan",
    )(u3, win, wout, wt, a2r, a2i, d_skip.reshape(1, width).astype(F32))


def _glu_out_kernel(yg_ref, z_ref, memo_ref, h_ref, wglu_ref, bglu_ref,
                    wmain_ref, wmem_ref, g_ref, gkv_ref, gnext_ref, wf_ref, bf_ref,
                    out_ref, xkv_ref, xnext_ref, fp_ref, carry_ref, *, heads, sub):
    @pl.when(pl.program_id(1) == 0)
    def _():
        carry_ref[...] = jnp.zeros_like(carry_ref)

    subs = [slice(s * sub, (s + 1) * sub) for s in range(out_ref.shape[0] // sub)]
    t = [jnp.dot(yg_ref[rows, :].astype(BF16), wglu_ref[...], preferred_element_type=F32)
         + bglu_ref[...] for rows in subs]
    for rows, tt in zip(subs, t):
        z = z_ref[rows, :].astype(F32)
        main = yg_ref[rows, :] * jax.nn.sigmoid(tt) * (z * jax.nn.sigmoid(z))
        o = jnp.dot(main.astype(BF16), wmain_ref[...], preferred_element_type=F32)
        o = o + jnp.dot(memo_ref[rows, :], wmem_ref[...], preferred_element_type=F32)
        ms = jnp.mean(o * o, axis=-1, keepdims=True)
        out_ref[rows, :] = h_ref[rows, :] + o * lax.rsqrt(ms + EPS) * g_ref[...]
    split = []
    for rows in subs:
        h1 = out_ref[rows, :]
        r = h1 * lax.rsqrt(jnp.mean(h1 * h1, axis=-1, keepdims=True) + EPS)
        xnext_ref[rows, :] = (r * gnext_ref[...]).astype(BF16)
        split.append(_split2(r * gkv_ref[...]))
        xkv_ref[rows, :] = split[-1][0]
    logit = [_fgate_logits(xh, xl, wf_ref, bf_ref) for xh, xl in split]
    for rows, lg in zip(subs, logit):
        fp_ref[rows, :], carry_ref[...] = _fgate_cumsum(lg, carry_ref[...], heads)


def _out_kernel(main_ref, memo_ref, h_ref, wmain_ref, wmem_ref, g_ref, out_ref, *, sub):
    subs = [slice(s * sub, (s + 1) * sub) for s in range(out_ref.shape[0] // sub)]
    o = []
    for rows in subs:
        main = jnp.concatenate([main_ref[hd, rows, :] for hd in range(main_ref.shape[0])],
                               axis=1)
        o.append(jnp.dot(main, wmain_ref[...], preferred_element_type=F32)
                 + jnp.dot(memo_ref[rows, :], wmem_ref[...], preferred_element_type=F32))
    for rows, oo in zip(subs, o):
        ms = jnp.mean(oo * oo, axis=-1, keepdims=True)
        out_ref[rows, :] = h_ref[rows, :] + oo * lax.rsqrt(ms + EPS) * g_ref[...]


def _resident(shape):
    return pl.BlockSpec(shape, lambda *_: (0,) * len(shape), pipeline_mode=pl.Buffered(1))


def _w_out_specs(layer, main_w, mem_w, d):
    assert main_w % mem_w == 0
    return [pl.BlockSpec((None, main_w, d), lambda *_: (layer, 0, 0),
                         pipeline_mode=pl.Buffered(1)),
            pl.BlockSpec((None, mem_w, d), lambda *_: (layer, main_w // mem_w, 0),
                         pipeline_mode=pl.Buffered(1))]


def _glu_out(yg_tm, rest, memo, h2d, w_glu, b_glu, w_out_bf, layer, g, g_kv, g_next,
             w_fgate, b_fgate, *, bsz, tl, sub):
    assert tl % sub == 0
    m, d = h2d.shape
    seqlen = m // bsz
    main_w = w_glu.shape[0]
    mem_w = w_out_bf.shape[1] - main_w
    heads = w_fgate.shape[1]
    assert 3 * heads <= LANES and seqlen % tl == 0
    wpad = jnp.zeros((d, LANES), F32).at[:, :heads].set(w_fgate.astype(F32))
    wcat = jnp.concatenate(_split2(wpad), axis=1)
    bpad = jnp.zeros((1, LANES), F32).at[0, :heads].set(b_fgate.astype(F32))
    per_b = seqlen // tl
    row = lambda b, i: (b * per_b + i, 0)
    vec = lambda v: v.reshape(1, -1).astype(F32)
    return pl.pallas_call(
        functools.partial(_glu_out_kernel, heads=heads, sub=sub),
        grid=(bsz, per_b),
        in_specs=[pl.BlockSpec((tl, main_w), lambda b, i: (i, b)),
                  pl.BlockSpec((tl, main_w), row),
                  pl.BlockSpec((tl, mem_w), row),
                  pl.BlockSpec((tl, d), row),
                  _resident((main_w, main_w)),
                  _resident((1, main_w))]
                 + _w_out_specs(layer, main_w, mem_w, d)
                 + [_resident((1, d)),
                    _resident((1, d)),
                    _resident((1, d)),
                    _resident((d, 2 * LANES)),
                    _resident((1, LANES))],
        out_specs=[pl.BlockSpec((tl, d), row),
                   pl.BlockSpec((tl, d), row),
                   pl.BlockSpec((tl, d), row),
                   pl.BlockSpec((tl, LANES), row)],
        out_shape=[jax.ShapeDtypeStruct((m, d), F32),
                   jax.ShapeDtypeStruct((m, d), BF16),
                   jax.ShapeDtypeStruct((m, d), BF16),
                   jax.ShapeDtypeStruct((m, LANES), BF16)],
        scratch_shapes=[pltpu.VMEM((1, LANES), F32)],
        compiler_params=_params(("arbitrary", "arbitrary")),
        name="glu_out_proj",
    )(yg_tm, rest, memo, h2d, w_glu.astype(BF16), vec(b_glu), w_out_bf, w_out_bf,
      vec(g), vec(g_kv), vec(g_next), wcat, bpad)


def _out_proj(main, memo, h2d, w_out_bf, layer, g, *, tl, sub):
    m, d = h2d.shape
    heads, _, dh = main.shape
    main_w = heads * dh
    mem_w = memo.shape[1]
    row = lambda i: (i, 0)
    return pl.pallas_call(
        functools.partial(_out_kernel, sub=sub),
        grid=(m // tl,),
        in_specs=[pl.BlockSpec((heads, tl, dh), lambda i: (0, i, 0)),
                  pl.BlockSpec((tl, mem_w), row),
                  pl.BlockSpec((tl, d), row)]
                 + _w_out_specs(layer, main_w, mem_w, d)
                 + [_resident((1, d))],
        out_specs=pl.BlockSpec((tl, d), row),
        out_shape=jax.ShapeDtypeStruct((m, d), F32),
        compiler_params=_params(("parallel",)),
        name="out_proj",
    )(main, memo, h2d, w_out_bf, w_out_bf, g.reshape(1, d).astype(F32))


FOX_CHUNK = 64


def _fox_kernel(q_ref, k_ref, v_ref, fp_ref, sel_ref, cst_ref, z_ref, o_ref,
                ka_ref, qa_ref, vt_ref, st_ref, pt_ref, m_ref, l_ref, acc_ref, *, tq):
    nh, seqlen, _ = q_ref.shape
    ncg = tq // LANES
    nch = tq // FOX_CHUNK
    for hh in range(nh):
        ext = jnp.dot(fp_ref[...], sel_ref[hh], preferred_element_type=F32)
        ka_ref[hh, :, :HEAD_DIM] = k_ref[hh]
        ka_ref[hh, :, HEAD_DIM:] = (ext[:, :HEAD_DIM] + cst_ref[0:1, :]).astype(BF16)
        qa_ref[hh, :, :HEAD_DIM] = q_ref[hh]
        qa_ref[hh, :, HEAD_DIM:] = (ext[:, HEAD_DIM:] + cst_ref[1:2, :]).astype(BF16)
        vt_ref[hh] = v_ref[hh].astype(F32).T.astype(BF16)
        for ci in range(nch):
            for g in range(ncg):
                if g * LANES + LANES - 1 < ci * FOX_CHUNK:
                    pt_ref[hh, 1, ci * FOX_CHUNK:(ci + 1) * FOX_CHUNK,
                           g * LANES:(g + 1) * LANES] = jnp.zeros((FOX_CHUNK, LANES), BF16)

    half = tq // 2

    def scores(hh, qi, kj, slot):
        nt = (((1,), (1,)), ((), ()))
        k0, q0 = kj * tq, qi * tq
        if kj < qi:
            st_ref[hh, slot] = lax.dot_general(ka_ref[hh, k0:k0 + tq, :], qa_ref[hh, q0:q0 + tq, :],
                                               nt, preferred_element_type=F32)
        else:
            st_ref[hh, slot, :half, :] = lax.dot_general(
                ka_ref[hh, k0:k0 + half, :], qa_ref[hh, q0:q0 + tq, :], nt,
                preferred_element_type=F32)
            st_ref[hh, slot, half:, half:] = lax.dot_general(
                ka_ref[hh, k0 + half:k0 + tq, :], qa_ref[hh, q0 + half:q0 + tq, :], nt,
                preferred_element_type=F32)

    def piece(hh, slot, ci, g, masked):
        r0, c0 = ci * FOX_CHUNK, g * LANES
        if masked and c0 + LANES - 1 < r0:
            return None
        x = st_ref[hh, slot, r0:r0 + FOX_CHUNK, c0:c0 + LANES]
        if masked and c0 < r0 + FOX_CHUNK - 1:
            key = r0 + lax.broadcasted_iota(jnp.int32, (FOX_CHUNK, LANES), 0)
            qry = c0 + lax.broadcasted_iota(jnp.int32, (FOX_CHUNK, LANES), 1)
            x = jnp.where(key <= qry, x, -jnp.inf)
        return x

    def fold(x, op):
        return op(x.reshape(FOX_CHUNK // SUBLANES, SUBLANES, LANES), axis=0)

    def tile_stats(hh, slot, masked, first):
        mx = [None] * ncg
        for ci in range(nch):
            for g in range(ncg):
                x = piece(hh, slot, ci, g, masked)
                if x is not None:
                    f = fold(x, jnp.max)
                    mx[g] = f if mx[g] is None else jnp.maximum(mx[g], f)
        m_new = jnp.concatenate([jnp.max(v, axis=0, keepdims=True) for v in mx], axis=1)
        alpha = None
        if not first:
            m_old = m_ref[hh]
            m_new = jnp.maximum(m_old, m_new)
            alpha = jnp.exp2(m_old - m_new)
        m_ref[hh] = m_new
        return m_new, alpha

    def tile_probs(hh, slot, masked, m_new):
        ls = [None] * ncg
        for ci in range(nch):
            for g in range(ncg):
                r0, c0 = ci * FOX_CHUNK, g * LANES
                x = piece(hh, slot, ci, g, masked)
                if x is None:
                    continue
                p = jnp.exp2(x - m_new[:, c0:c0 + LANES])
                f = fold(p, jnp.sum)
                ls[g] = f if ls[g] is None else ls[g] + f
                pt_ref[hh, int(masked), r0:r0 + FOX_CHUNK, c0:c0 + LANES] = p.astype(BF16)
        return jnp.concatenate([jnp.sum(v, axis=0, keepdims=True) for v in ls], axis=1)

    def tile_update(hh, kj, masked, alpha, l_new):
        k0 = kj * tq
        if masked:
            pv = jnp.dot(vt_ref[hh, :, k0:k0 + half], pt_ref[hh, 1, :half, :],
                         preferred_element_type=F32)
            late = jnp.dot(vt_ref[hh, :, k0 + half:k0 + tq], pt_ref[hh, 1, half:, half:],
                           preferred_element_type=F32)
            pv = jnp.concatenate([pv[:, :half], pv[:, half:] + late], axis=1)
        else:
            pv = jnp.dot(vt_ref[hh, :, k0:k0 + tq], pt_ref[hh, 0], preferred_element_type=F32)
        if alpha is None:
            l_ref[hh] = l_new
            acc_ref[hh] = pv
        else:
            l_ref[hh] = alpha * l_ref[hh] + l_new
            acc_ref[hh] = alpha * acc_ref[hh] + pv

    heads = range(nh)
    for qi in range(seqlen // tq):
        rows = slice(qi * tq, (qi + 1) * tq)
        for hh in heads:
            scores(hh, qi, 0, 0)
        for kj in range(qi + 1):
            slot, masked = kj % 2, kj == qi
            if kj < qi:
                for hh in heads:
                    scores(hh, qi, kj + 1, (kj + 1) % 2)
            stats = [tile_stats(hh, slot, masked, kj == 0) for hh in heads]
            sums = [tile_probs(hh, slot, masked, stats[hh][0]) for hh in heads]
            for hh in heads:
                tile_update(hh, kj, masked, stats[hh][1], sums[hh])
        for hh in range(nh):
            z = z_ref[hh, rows, :].astype(F32)
            o = (acc_ref[hh] / l_ref[hh]).T
            o_ref[hh, rows, :] = (o * (z * jax.nn.sigmoid(z))).astype(o_ref.dtype)


def _fox_attn(q, k, v, z, fparts, *, bsz, tq, heads_per_step):
    heads, m, _ = q.shape
    seqlen = m // bsz
    nterms = 3
    assert seqlen % tq == 0 and tq % FOX_CHUNK == 0 and nterms * heads <= LANES
    h_idx = jnp.arange(heads)[:, None, None]
    r_idx = jnp.arange(LANES)[None, :, None]
    c_idx = jnp.arange(2 * HEAD_DIM)[None, None, :]
    key_side = (c_idx < nterms) & (r_idx == c_idx * heads + h_idx)
    qc = c_idx - HEAD_DIM - nterms
    qry_side = (qc >= 0) & (qc < nterms) & (r_idx == qc * heads + h_idx)
    sel = (qry_side.astype(F32) - key_side.astype(F32)).astype(BF16)
    col = jnp.arange(HEAD_DIM)
    cst = jnp.zeros((SUBLANES, HEAD_DIM), F32)
    cst = cst.at[0].set(((col >= nterms) & (col < 2 * nterms)).astype(F32))
    cst = cst.at[1].set((col < nterms).astype(F32))
    nh = heads_per_step
    assert heads % nh == 0
    head_seq = pl.BlockSpec((nh, seqlen, HEAD_DIM), lambda b, h: (h, b, 0))
    return pl.pallas_call(
        functools.partial(_fox_kernel, tq=tq),
        grid=(bsz, heads // nh),
        in_specs=[head_seq, head_seq, head_seq,
                  pl.BlockSpec((seqlen, LANES), lambda b, h: (b, 0)),
                  pl.BlockSpec((nh, LANES, 2 * HEAD_DIM), lambda b, h: (h, 0, 0)),
                  pl.BlockSpec((SUBLANES, HEAD_DIM), lambda b, h: (0, 0)),
                  head_seq],
        out_specs=head_seq,
        out_shape=jax.ShapeDtypeStruct((heads, m, HEAD_DIM), BF16),
        scratch_shapes=[pltpu.VMEM((nh, seqlen, 2 * HEAD_DIM), BF16),
                        pltpu.VMEM((nh, seqlen, 2 * HEAD_DIM), BF16),
                        pltpu.VMEM((nh, HEAD_DIM, seqlen), BF16),
                        pltpu.VMEM((nh, 2, tq, tq), F32),
                        pltpu.VMEM((nh, 2, tq, tq), BF16),
                        pltpu.VMEM((nh, 1, tq), F32),
                        pltpu.VMEM((nh, 1, tq), F32),
                        pltpu.VMEM((nh, HEAD_DIM, tq), F32)],
        compiler_params=_params(("arbitrary", "arbitrary")),
        name="fox_attn",
    )(q, k, v, fparts, sel, cst, z)


def kernel(x, mem, pre_norm_g, post_norm_g, w_in_a, lam_re, lam_im, log_step, b_re, b_im,
           c_re, c_im, d_skip, w_glu, b_glu, kv_norm_g, w_kv, w_fgate, b_fgate, w_in_b,
           mem_norm_g, w_mem_kv, w_out):
    bsz, seqlen, d = x.shape
    n_mem = mem.shape[1]
    main_w = w_glu.shape[1]
    mem_w = w_out.shape[1] - main_w
    scale = HEAD_DIM ** -0.5
    x2d = x.reshape(bsz * seqlen, d)
    mem2d = mem.reshape(bsz * n_mem, d)

    ones = jnp.ones((main_w,), F32)
    mem_scale = jnp.full((mem_w,), scale * LOG2E, F32)
    cs_a = jnp.concatenate([ones, ones, mem_scale, ones[:mem_w]])
    cs_b = jnp.concatenate([ones * (scale * LOG2E), ones, mem_scale, ones[:mem_w]])
    proj = functools.partial(_matmul, out_dtype=BF16, tm=1024)
    u_tm, xa = _matmul(x2d, w_in_a[0], cs_a, col0=0, n=main_w, out_dtype=F32, tm=512,
                       tn=main_w, time_major_batches=bsz, norm_gain=pre_norm_g[0],
                       name="in_proj_a_u")
    qz_a = proj(xa, w_in_a[0], cs_a, col0=2 * main_w, n=2 * mem_w, tn=2 * mem_w,
                slab_width=HEAD_DIM, name="in_proj_a_mem")
    kvm = _norm_matmul(mem2d, mem_norm_g, w_mem_kv, out_dtype=BF16, tm=512, tn=512,
                       name="mem_kv")
    s5_ops = _s5_discretise(lam_re[0], lam_im[0], log_step[0], b_re[0], b_im[0],
                            c_re[0], c_im[0])
    u3 = u_tm.reshape(seqlen, bsz, main_w)
    u3, qz_a, kvm = lax.optimization_barrier((u3, qz_a, kvm))
    yg3 = _s5(u3, *s5_ops, d_skip[0], n_pairs=128, blocks_per_step=3)
    yg_tm = yg3.reshape(seqlen, bsz * main_w)
    z_a = proj(xa, w_in_a[0], cs_a, col0=main_w, n=main_w, tn=main_w, name="in_proj_a_z")
    memo0 = _mem_attn(qz_a, kvm, 0, bsz=bsz, tq=1024, name="mem_attn0")
    w_out_bf = w_out.astype(BF16)
    h1, xkv, xb, fparts = _glu_out(
        yg_tm, z_a, memo0, x2d, w_glu[0], b_glu[0], w_out_bf, 0, post_norm_g[0],
        kv_norm_g, pre_norm_g[1], w_fgate, b_fgate, bsz=bsz, tl=512, sub=256)

    cs_kv = jnp.ones((w_kv.shape[1],), F32)
    k_sh = proj(xkv, w_kv, cs_kv, col0=0, n=main_w, tn=main_w, slab_width=HEAD_DIM,
                name="k_proj")
    v_sh = proj(xkv, w_kv, cs_kv, col0=main_w, n=main_w, tn=main_w, slab_width=HEAD_DIM,
                name="v_proj")

    q_b = proj(xb, w_in_b[0], cs_b, col0=0, n=main_w, tn=main_w, slab_width=HEAD_DIM,
               name="in_proj_b_q")
    z_b = proj(xb, w_in_b[0], cs_b, col0=main_w, n=main_w, tn=main_w, slab_width=HEAD_DIM,
               name="in_proj_b_z")
    qz_b = proj(xb, w_in_b[0], cs_b, col0=2 * main_w, n=2 * mem_w, tn=2 * mem_w,
                slab_width=HEAD_DIM, name="in_proj_b_mem")
    memo1 = _mem_attn(qz_b, kvm, 1, bsz=bsz, tq=1024, name="mem_attn1")
    att = _fox_attn(q_b, k_sh, v_sh, z_b, fparts, bsz=bsz, tq=256, heads_per_step=3)
    out = _out_proj(att, memo1, h1, w_out_bf, 1, post_norm_g[1], tl=512, sub=256)
    return out.reshape(bsz, seqlen, d)
```

```python
import functools

import jax
import jax.numpy as jnp
from jax import lax
from jax.experimental import pallas as pl
from jax.experimental.pallas import tpu as pltpu

F32 = jnp.float32
BF16 = jnp.bfloat16

EPS = 1e-6
LOG2E = 1.4426950408889634
HEAD_DIM = 128
SSM_GROUP = 16
SSM_STATE = 64
MEM_HEADS = 4
LANES = 128
SUBLANES = 8
GROUPS_PER_BLOCK = LANES // SSM_GROUP
STATE_COLS = GROUPS_PER_BLOCK * SSM_STATE
VMEM_LIMIT = 56 * 1024 * 1024


def _params(sem, vmem=VMEM_LIMIT):
    return pltpu.CompilerParams(dimension_semantics=sem, vmem_limit_bytes=vmem)


def _norm_matmul_kernel(x_ref, g_ref, w_ref, o_ref, r_ref, xn_ref):
    layer, j = pl.program_id(1), pl.program_id(2)

    @pl.when((layer == 0) & (j == 0))
    def _():
        x = x_ref[...]
        r_ref[...] = x * lax.rsqrt(jnp.mean(x * x, axis=-1, keepdims=True) + EPS)

    @pl.when(j == 0)
    def _():
        xn_ref[...] = (r_ref[...] * g_ref[...]).astype(BF16)

    o_ref[...] = jnp.dot(xn_ref[...], w_ref[...].astype(BF16),
                         preferred_element_type=F32).astype(o_ref.dtype)


def _norm_matmul(x2d, g, w, *, out_dtype, tm, tn, name):
    m, d = x2d.shape
    layers, _, n = w.shape
    assert m % tm == 0 and n % tn == 0
    return pl.pallas_call(
        _norm_matmul_kernel,
        grid=(m // tm, layers, n // tn),
        in_specs=[pl.BlockSpec((tm, d), lambda i, l, j: (i, 0)),
                  pl.BlockSpec((None, 1, d), lambda i, l, j: (l, 0, 0)),
                  pl.BlockSpec((None, d, tn), lambda i, l, j: (l, 0, j))],
        out_specs=pl.BlockSpec((None, tm, tn), lambda i, l, j: (l, i, j)),
        out_shape=jax.ShapeDtypeStruct((layers, m, n), out_dtype),
        scratch_shapes=[pltpu.VMEM((tm, d), F32), pltpu.VMEM((tm, d), BF16)],
        compiler_params=_params(("arbitrary", "arbitrary", "arbitrary")),
        name=name,
    )(x2d, g.reshape(layers, 1, d).astype(F32), w)


def _store_product(res, o_ref):
    if len(o_ref.shape) == 3:
        sw = o_ref.shape[2]
        for s in range(o_ref.shape[0]):
            o_ref[s] = res[:, s * sw:(s + 1) * sw]
    else:
        o_ref[...] = res


def _matmul_kernel(x_ref, w_ref, cs_ref, o_ref, wb_ref):
    @pl.when(pl.program_id(1) == 0)
    def _():
        wb_ref[...] = (w_ref[...] * cs_ref[...]).astype(BF16)

    res = jnp.dot(x_ref[...], wb_ref[...], preferred_element_type=F32).astype(o_ref.dtype)
    _store_product(res, o_ref)


def _norm_then_matmul_kernel(x_ref, g_ref, w_ref, cs_ref, o_ref, xn_ref, wb_ref):
    @pl.when(pl.program_id(1) == 0)
    def _():
        wb_ref[...] = (w_ref[...] * cs_ref[...]).astype(BF16)

    half = x_ref.shape[0] // 2
    xn = []
    for rows in (slice(0, half), slice(half, 2 * half)):
        x = x_ref[rows, :]
        ms = jnp.mean(x * x, axis=-1, keepdims=True)
        xn.append((x * lax.rsqrt(ms + EPS) * g_ref[...]).astype(BF16))
        xn_ref[rows, :] = xn[-1]
    res = jnp.concatenate(
        [jnp.dot(v, wb_ref[...], preferred_element_type=F32).astype(o_ref.dtype) for v in xn],
        axis=0)
    _store_product(res, o_ref)


def _matmul(xn, w, colscale, *, col0, n, out_dtype, tm, tn, time_major_batches=None,
            slab_width=None, norm_gain=None, name):
    m, d = xn.shape
    assert m % tm == 0 and n % tn == 0 and col0 % tn == 0
    nj = n // tn
    j0 = col0 // tn
    if slab_width is not None:
        assert time_major_batches is None and tn % slab_width == 0
        per_tile = tn // slab_width
        out_shape = jax.ShapeDtypeStruct((n // slab_width, m, slab_width), out_dtype)
        out_spec = pl.BlockSpec((per_tile, tm, slab_width), lambda j, i: (j, i, 0))
    elif time_major_batches is None:
        out_shape = jax.ShapeDtypeStruct((m, n), out_dtype)
        out_spec = pl.BlockSpec((tm, tn), lambda j, i: (i, j))
    else:
        bsz = time_major_batches
        seqlen = m // bsz
        assert seqlen % tm == 0
        per_b = seqlen // tm
        out_shape = jax.ShapeDtypeStruct((seqlen, bsz * n), out_dtype)
        out_spec = pl.BlockSpec((tm, tn), lambda j, i: (i % per_b, (i // per_b) * nj + j))
    x_spec = pl.BlockSpec((tm, d), lambda j, i: (i, 0))
    w_specs = [pl.BlockSpec((d, tn), lambda j, i: (0, j0 + j), pipeline_mode=pl.Buffered(1)),
               pl.BlockSpec((1, tn), lambda j, i: (0, j0 + j))]
    common = dict(grid=(nj, m // tm), scratch_shapes=[pltpu.VMEM((d, tn), BF16)],
                  compiler_params=_params(("arbitrary", "arbitrary")), name=name)
    cs = colscale.reshape(1, -1).astype(F32)
    if norm_gain is None:
        return pl.pallas_call(
            _matmul_kernel, in_specs=[x_spec] + w_specs, out_specs=out_spec,
            out_shape=out_shape, **common)(xn, w, cs)
    assert nj == 1
    return pl.pallas_call(
        _norm_then_matmul_kernel,
        in_specs=[x_spec, pl.BlockSpec((1, d), lambda j, i: (0, 0))] + w_specs,
        out_specs=[out_spec, x_spec],
        out_shape=[out_shape, jax.ShapeDtypeStruct((m, d), BF16)],
        **common)(xn, norm_gain.reshape(1, d).astype(F32), w, cs)


def _split2(x):
    hi = x.astype(BF16)
    return hi, (x - hi.astype(F32)).astype(BF16)


def _split3(x):
    hi = x.astype(BF16)
    r = x - hi.astype(F32)
    mid = r.astype(BF16)
    return hi, mid, (r - mid.astype(F32)).astype(BF16)


def _fgate_logits(xh, xl, wcat_ref, b_ref):
    both = jnp.dot(xh, wcat_ref[...], preferred_element_type=F32)
    return (both[:, :LANES] + both[:, LANES:]
            + jnp.dot(xl, wcat_ref[:, :LANES], preferred_element_type=F32)) + b_ref[...]


def _fgate_cumsum(logit, carry, heads):
    tl = logit.shape[0]
    logf = jnp.minimum(logit, 0.0) - jnp.log(1.0 + jnp.exp(-jnp.abs(logit)))
    t_idx = lax.broadcasted_iota(jnp.int32, (tl, tl), 0)
    s_idx = lax.broadcasted_iota(jnp.int32, (tl, tl), 1)
    tril = (s_idx <= t_idx).astype(BF16)
    csum = carry
    for part in _split3(logf):
        csum = csum + jnp.dot(tril, part, preferred_element_type=F32)
    lane = lax.broadcasted_iota(jnp.int32, (tl, LANES), 1)
    packed = jnp.zeros((tl, LANES), F32)
    for n, part in reversed(list(enumerate(_split3(csum * LOG2E)))):
        shifted = part.astype(F32) if n == 0 else pltpu.roll(part.astype(F32), n * heads, 1)
        packed = jnp.where(lane < (n + 1) * heads, shifted, packed)
    return packed.astype(BF16), csum[tl - 1:tl, :]


def _mem_attn_kernel(q_ref, zm_ref, kv_ref, o_ref):
    width = MEM_HEADS * HEAD_DIM
    heads = range(MEM_HEADS)
    cols = [slice(h * HEAD_DIM, (h + 1) * HEAD_DIM) for h in heads]
    st = [lax.dot_general(kv_ref[:, cols[h]], q_ref[h], (((1,), (1,)), ((), ())),
                          preferred_element_type=F32) for h in heads]
    vt = [kv_ref[:, width + h * HEAD_DIM:width + (h + 1) * HEAD_DIM].astype(F32).T.astype(BF16)
          for h in heads]
    p = [jnp.exp2(st[h] - jnp.max(st[h], axis=0, keepdims=True)) for h in heads]
    inv = [1.0 / jnp.sum(p[h], axis=0, keepdims=True) for h in heads]
    ot = [jnp.dot(vt[h], p[h].astype(BF16), preferred_element_type=F32) * inv[h] for h in heads]
    for h in heads:
        zm = zm_ref[h].astype(F32)
        o_ref[:, cols[h]] = (ot[h].T * (zm * jax.nn.sigmoid(zm))).astype(o_ref.dtype)


def _mem_attn(qz, kvm, layer, *, bsz, tq, name):
    width = MEM_HEADS * HEAD_DIM
    m = qz.shape[1]
    seqlen = m // bsz
    n_mem = kvm.shape[1] // bsz
    per_b = seqlen // tq
    spec = lambda blk: pl.BlockSpec((MEM_HEADS, tq, HEAD_DIM),
                                    lambda b, i: (blk, b * per_b + i, 0))
    return pl.pallas_call(
        _mem_attn_kernel,
        grid=(bsz, per_b),
        in_specs=[spec(0), spec(1),
                  pl.BlockSpec((None, n_mem, 2 * width), lambda b, i: (layer, b, 0))],
        out_specs=pl.BlockSpec((tq, width), lambda b, i: (b * per_b + i, 0)),
        out_shape=jax.ShapeDtypeStruct((m, width), BF16),
        compiler_params=_params(("parallel", "arbitrary")),
        name=name,
    )(qz, qz, kvm)


def _s5_kernel(u_ref, winc_ref, woutc_ref, wtc_ref, ar_ref, ai_ref, d_ref, y_ref,
               win_ref, wout_ref, wt_ref, bu_ref, xs_ref, st_ref, *, n_pairs):
    nblk = win_ref.shape[0]

    @pl.when(pl.program_id(1) == 0)
    def _():
        st_ref[...] = jnp.zeros_like(st_ref)
        for k in range(nblk):
            win_ref[k] = _s5_expand(winc_ref[k], SSM_STATE)
            wout_ref[k] = _s5_expand(woutc_ref[k], SSM_STATE)
            wt_ref[k] = _s5_expand(wtc_ref[k], SSM_GROUP)

    rows = n_pairs * SUBLANES
    u0, u1, ucat, a2, state = [], [], [], [], []
    for k in range(nblk):
        upair = u_ref[:, :, k * LANES:(k + 1) * LANES].reshape(n_pairs, 2, SUBLANES, LANES)
        u0.append(upair[:, 0].reshape(rows, LANES))
        u1.append(upair[:, 1].reshape(rows, LANES))
        ucat.append(jnp.concatenate([u0[k], u1[k]], axis=1).astype(BF16))
        bu_ref[k] = jnp.dot(ucat[k], win_ref[k], preferred_element_type=F32)
        a2.append((jnp.broadcast_to(ar_ref[k], (SUBLANES, STATE_COLS)),
                   jnp.broadcast_to(ai_ref[k], (SUBLANES, STATE_COLS))))
        state.append((st_ref[k, 0], st_ref[k, 1]))

    held = [None] * nblk
    for c in range(n_pairs):
        for k in range(nblk):
            xr, xi = state[k]
            cur = jnp.concatenate([xr, xi], axis=1)
            if c % 2 == 1:
                xs_ref[k, (c - 1) * SUBLANES:(c + 1) * SUBLANES, :] = jnp.concatenate(
                    [held[k], cur], axis=0).astype(BF16)
            held[k] = cur
            bur = bu_ref[k, c * SUBLANES:(c + 1) * SUBLANES, 0:STATE_COLS]
            bui = bu_ref[k, c * SUBLANES:(c + 1) * SUBLANES, STATE_COLS:2 * STATE_COLS]
            ar, ai = a2[k]
            state[k] = (ar * xr - ai * xi + bur, ar * xi + ai * xr + bui)

    for k in range(nblk):
        st_ref[k, 0] = state[k][0]
        st_ref[k, 1] = state[k][1]
    ys = [lax.dot_general(xs_ref[k], wout_ref[k], (((1,), (1,)), ((), ())),
                          preferred_element_type=F32)
          + jnp.dot(ucat[k], wt_ref[k], preferred_element_type=F32) for k in range(nblk)]
    for k in range(nblk):
        y = ys[k]
        d = d_ref[:, k * LANES:(k + 1) * LANES]
        y0 = jax.nn.gelu(y[:, :LANES] + d * u0[k]).reshape(n_pairs, SUBLANES, LANES)
        y1 = jax.nn.gelu(y[:, LANES:] + d * u1[k]).reshape(n_pairs, SUBLANES, LANES)
        y_ref[:, :, k * LANES:(k + 1) * LANES] = jnp.stack([y0, y1], axis=1).reshape(
            2 * n_pairs, SUBLANES, LANES)


def _s5_discretise(lam_re, lam_im, log_step, b_re, b_im, c_re, c_im):
    groups = lam_re.shape[0]
    nblk = groups // GROUPS_PER_BLOCK
    lr = lam_re.astype(F32)
    li = lam_im.astype(F32)
    dt = jnp.exp(log_step.astype(F32))[:, None]
    mag = jnp.exp(lr * dt)
    ar = mag * jnp.cos(li * dt)
    ai = mag * jnp.sin(li * dt)
    den = lr * lr + li * li
    cr = ((ar - 1.0) * lr + ai * li) / den
    ci = (ai * lr - (ar - 1.0) * li) / den
    br = b_re.astype(F32)
    bi = b_im.astype(F32)
    bbar_re = cr[..., None] * br - ci[..., None] * bi
    bbar_im = cr[..., None] * bi + ci[..., None] * br
    a2r = ar * ar - ai * ai
    a2i = 2.0 * ar * ai
    ab_re = ar[..., None] * bbar_re - ai[..., None] * bbar_im
    ab_im = ar[..., None] * bbar_im + ai[..., None] * bbar_re
    cre = c_re.astype(F32)
    cim = c_im.astype(F32)
    ca_re = cre * ar[:, None, :] - cim * ai[:, None, :]
    ca_im = cre * ai[:, None, :] + cim * ar[:, None, :]
    ca2_re = cre * a2r[:, None, :] - cim * a2i[:, None, :]
    ca2_im = cre * a2i[:, None, :] + cim * a2r[:, None, :]
    k0 = jnp.einsum('ghp,gpk->ghk', cre, bbar_re) - jnp.einsum('ghp,gpk->ghk', cim, bbar_im)
    k1 = jnp.einsum('ghp,gpk->ghk', cre, ab_re) - jnp.einsum('ghp,gpk->ghk', cim, ab_im)

    def per_block(x):
        return x.reshape((nblk, GROUPS_PER_BLOCK) + x.shape[1:])

    b_in = jnp.stack([jnp.stack([per_block(ab_re), per_block(ab_im)]),
                      jnp.stack([per_block(bbar_re), per_block(bbar_im)])])
    win_c = b_in.transpose(2, 0, 5, 1, 3, 4).reshape(nblk, 2 * SSM_GROUP, 2 * STATE_COLS)
    c_out = jnp.stack([jnp.stack([per_block(ca_re), per_block(ca2_re)]),
                       jnp.stack([-per_block(ca_im), -per_block(ca2_im)])])
    wout_c = c_out.transpose(2, 1, 4, 0, 3, 5).reshape(nblk, 2 * SSM_GROUP, 2 * STATE_COLS)
    k0b, k1b = per_block(k0), per_block(k1)
    k_mix = jnp.stack([jnp.stack([k0b, k1b]),
                       jnp.stack([jnp.zeros_like(k0b), k0b])])
    wt_c = k_mix.transpose(2, 0, 5, 1, 3, 4).reshape(nblk, 2 * SSM_GROUP, 2 * LANES)
    return (win_c, wout_c, wt_c,
            a2r.reshape(nblk, 1, STATE_COLS), a2i.reshape(nblk, 1, STATE_COLS))


def _s5_expand(compact, cols_per_group):
    width = compact.shape[1]
    col_group = (lax.broadcasted_iota(jnp.int32, (SSM_GROUP, width), 1)
                 // cols_per_group) % GROUPS_PER_BLOCK
    pieces = []
    for i in range(2):
        rows = compact[i * SSM_GROUP:(i + 1) * SSM_GROUP, :]
        for g in range(GROUPS_PER_BLOCK):
            pieces.append(jnp.where(col_group == g, rows, 0.0).astype(BF16))
    return jnp.concatenate(pieces, axis=0)


def _s5(u3, win, wout, wt, a2r, a2i, d_skip, *, n_pairs, blocks_per_step):
    seqlen, bsz, width = u3.shape
    assert bsz == SUBLANES and seqlen % (2 * n_pairs) == 0 and width % LANES == 0
    nblk = width // LANES
    assert nblk % blocks_per_step == 0 and n_pairs % 2 == 0
    nb = blocks_per_step
    rows = n_pairs * SUBLANES
    blk = pl.BlockSpec((2 * n_pairs, bsz, nb * LANES), lambda j, c: (c, 0, j))
    per_blk = lambda shape: pl.BlockSpec((nb,) + shape, lambda j, c: (j, 0, 0))
    return pl.pallas_call(
        functools.partial(_s5_kernel, n_pairs=n_pairs),
        grid=(nblk // nb, seqlen // (2 * n_pairs)),
        in_specs=[blk,
                  per_blk((2 * SSM_GROUP, 2 * STATE_COLS)),
                  per_blk((2 * SSM_GROUP, 2 * STATE_COLS)),
                  per_blk((2 * SSM_GROUP, 2 * LANES)),
                  per_blk((1, STATE_COLS)),
                  per_blk((1, STATE_COLS)),
                  pl.BlockSpec((1, nb * LANES), lambda j, c: (0, j))],
        out_specs=blk,
        out_shape=jax.ShapeDtypeStruct((seqlen, bsz, width), F32),
        scratch_shapes=[pltpu.VMEM((nb, 2 * LANES, 2 * STATE_COLS), BF16),
                        pltpu.VMEM((nb, 2 * LANES, 2 * STATE_COLS), BF16),
                        pltpu.VMEM((nb, 2 * LANES, 2 * LANES), BF16),
                        pltpu.VMEM((nb, rows, 2 * STATE_COLS), F32),
                        pltpu.VMEM((nb, rows, 2 * STATE_COLS), BF16),
                        pltpu.VMEM((nb, 2, SUBLANES, STATE_COLS), F32)],
        compiler_params=_params(("arbitrary", "arbitrary")),
        name="s5_scan",
    )(u3, win, wout, wt, a2r, a2i, d_skip.reshape(1, width).astype(F32))


def _glu_out_kernel(yg_ref, z_ref, memo_ref, h_ref, wglu_ref, bglu_ref,
                    wmain_ref, wmem_ref, g_ref, gkv_ref, gnext_ref, wf_ref, bf_ref,
                    out_ref, xkv_ref, xnext_ref, fp_ref, carry_ref, *, heads, sub):
    @pl.when(pl.program_id(1) == 0)
    def _():
        carry_ref[...] = jnp.zeros_like(carry_ref)

    subs = [slice(s * sub, (s + 1) * sub) for s in range(out_ref.shape[0] // sub)]
    t = [jnp.dot(yg_ref[rows, :].astype(BF16), wglu_ref[...], preferred_element_type=F32)
         + bglu_ref[...] for rows in subs]
    for rows, tt in zip(subs, t):
        z = z_ref[rows, :].astype(F32)
        main = yg_ref[rows, :] * jax.nn.sigmoid(tt) * (z * jax.nn.sigmoid(z))
        o = jnp.dot(main.astype(BF16), wmain_ref[...], preferred_element_type=F32)
        o = o + jnp.dot(memo_ref[rows, :], wmem_ref[...], preferred_element_type=F32)
        ms = jnp.mean(o * o, axis=-1, keepdims=True)
        out_ref[rows, :] = h_ref[rows, :] + o * lax.rsqrt(ms + EPS) * g_ref[...]
    split = []
    for rows in subs:
        h1 = out_ref[rows, :]
        r = h1 * lax.rsqrt(jnp.mean(h1 * h1, axis=-1, keepdims=True) + EPS)
        xnext_ref[rows, :] = (r * gnext_ref[...]).astype(BF16)
        split.append(_split2(r * gkv_ref[...]))
        xkv_ref[rows, :] = split[-1][0]
    logit = [_fgate_logits(xh, xl, wf_ref, bf_ref) for xh, xl in split]
    for rows, lg in zip(subs, logit):
        fp_ref[rows, :], carry_ref[...] = _fgate_cumsum(lg, carry_ref[...], heads)


def _out_kernel(main_ref, memo_ref, h_ref, wmain_ref, wmem_ref, g_ref, out_ref, *, sub):
    subs = [slice(s * sub, (s + 1) * sub) for s in range(out_ref.shape[0] // sub)]
    o = []
    for rows in subs:
        main = jnp.concatenate([main_ref[hd, rows, :] for hd in range(main_ref.shape[0])],
                               axis=1)
        o.append(jnp.dot(main, wmain_ref[...], preferred_element_type=F32)
                 + jnp.dot(memo_ref[rows, :], wmem_ref[...], preferred_element_type=F32))
    for rows, oo in zip(subs, o):
        ms = jnp.mean(oo * oo, axis=-1, keepdims=True)
        out_ref[rows, :] = h_ref[rows, :] + oo * lax.rsqrt(ms + EPS) * g_ref[...]


def _resident(shape):
    return pl.BlockSpec(shape, lambda *_: (0,) * len(shape), pipeline_mode=pl.Buffered(1))


def _w_out_specs(layer, main_w, mem_w, d):
    assert main_w % mem_w == 0
    return [pl.BlockSpec((None, main_w, d), lambda *_: (layer, 0, 0),
                         pipeline_mode=pl.Buffered(1)),
            pl.BlockSpec((None, mem_w, d), lambda *_: (layer, main_w // mem_w, 0),
                         pipeline_mode=pl.Buffered(1))]


def _glu_out(yg_tm, rest, memo, h2d, w_glu, b_glu, w_out_bf, layer, g, g_kv, g_next,
             w_fgate, b_fgate, *, bsz, tl, sub):
    assert tl % sub == 0
    m, d = h2d.shape
    seqlen = m // bsz
    main_w = w_glu.shape[0]
    mem_w = w_out_bf.shape[1] - main_w
    heads = w_fgate.shape[1]
    assert 3 * heads <= LANES and seqlen % tl == 0
    wpad = jnp.zeros((d, LANES), F32).at[:, :heads].set(w_fgate.astype(F32))
    wcat = jnp.concatenate(_split2(wpad), axis=1)
    bpad = jnp.zeros((1, LANES), F32).at[0, :heads].set(b_fgate.astype(F32))
    per_b = seqlen // tl
    row = lambda b, i: (b * per_b + i, 0)
    vec = lambda v: v.reshape(1, -1).astype(F32)
    return pl.pallas_call(
        functools.partial(_glu_out_kernel, heads=heads, sub=sub),
        grid=(bsz, per_b),
        in_specs=[pl.BlockSpec((tl, main_w), lambda b, i: (i, b)),
                  pl.BlockSpec((tl, main_w), row),
                  pl.BlockSpec((tl, mem_w), row),
                  pl.BlockSpec((tl, d), row),
                  _resident((main_w, main_w)),
                  _resident((1, main_w))]
                 + _w_out_specs(layer, main_w, mem_w, d)
                 + [_resident((1, d)),
                    _resident((1, d)),
                    _resident((1, d)),
                    _resident((d, 2 * LANES)),
                    _resident((1, LANES))],
        out_specs=[pl.BlockSpec((tl, d), row),
                   pl.BlockSpec((tl, d), row),
                   pl.BlockSpec((tl, d), row),
                   pl.BlockSpec((tl, LANES), row)],
        out_shape=[jax.ShapeDtypeStruct((m, d), F32),
                   jax.ShapeDtypeStruct((m, d), BF16),
                   jax.ShapeDtypeStruct((m, d), BF16),
                   jax.ShapeDtypeStruct((m, LANES), BF16)],
        scratch_shapes=[pltpu.VMEM((1, LANES), F32)],
        compiler_params=_params(("arbitrary", "arbitrary")),
        name="glu_out_proj",
    )(yg_tm, rest, memo, h2d, w_glu.astype(BF16), vec(b_glu), w_out_bf, w_out_bf,
      vec(g), vec(g_kv), vec(g_next), wcat, bpad)


def _out_proj(main, memo, h2d, w_out_bf, layer, g, *, tl, sub):
    m, d = h2d.shape
    heads, _, dh = main.shape
    main_w = heads * dh
    mem_w = memo.shape[1]
    row = lambda i: (i, 0)
    return pl.pallas_call(
        functools.partial(_out_kernel, sub=sub),
        grid=(m // tl,),
        in_specs=[pl.BlockSpec((heads, tl, dh), lambda i: (0, i, 0)),
                  pl.BlockSpec((tl, mem_w), row),
                  pl.BlockSpec((tl, d), row)]
                 + _w_out_specs(layer, main_w, mem_w, d)
                 + [_resident((1, d))],
        out_specs=pl.BlockSpec((tl, d), row),
        out_shape=jax.ShapeDtypeStruct((m, d), F32),
        compiler_params=_params(("parallel",)),
        name="out_proj",
    )(main, memo, h2d, w_out_bf, w_out_bf, g.reshape(1, d).astype(F32))


FOX_CHUNK = 64


def _fox_kernel(q_ref, k_ref, v_ref, fp_ref, sel_ref, cst_ref, z_ref, o_ref,
                ka_ref, qa_ref, vt_ref, st_ref, pt_ref, m_ref, l_ref, acc_ref, *, tq):
    nh, seqlen, _ = q_ref.shape
    ncg = tq // LANES
    nch = tq // FOX_CHUNK
    for hh in range(nh):
        ext = jnp.dot(fp_ref[...], sel_ref[hh], preferred_element_type=F32)
        ka_ref[hh, :, :HEAD_DIM] = k_ref[hh]
        ka_ref[hh, :, HEAD_DIM:] = (ext[:, :HEAD_DIM] + cst_ref[0:1, :]).astype(BF16)
        qa_ref[hh, :, :HEAD_DIM] = q_ref[hh]
        qa_ref[hh, :, HEAD_DIM:] = (ext[:, HEAD_DIM:] + cst_ref[1:2, :]).astype(BF16)
        vt_ref[hh] = v_ref[hh].astype(F32).T.astype(BF16)
        for ci in range(nch):
            for g in range(ncg):
                if g * LANES + LANES - 1 < ci * FOX_CHUNK:
                    pt_ref[hh, 1, ci * FOX_CHUNK:(ci + 1) * FOX_CHUNK,
                           g * LANES:(g + 1) * LANES] = jnp.zeros((FOX_CHUNK, LANES), BF16)

    half = tq // 2

    def scores(hh, qi, kj, slot):
        nt = (((1,), (1,)), ((), ()))
        k0, q0 = kj * tq, qi * tq
        if kj < qi:
            st_ref[hh, slot] = lax.dot_general(ka_ref[hh, k0:k0 + tq, :], qa_ref[hh, q0:q0 + tq, :],
                                               nt, preferred_element_type=F32)
        else:
            st_ref[hh, slot, :half, :] = lax.dot_general(
                ka_ref[hh, k0:k0 + half, :], qa_ref[hh, q0:q0 + tq, :], nt,
                preferred_element_type=F32)
            st_ref[hh, slot, half:, half:] = lax.dot_general(
                ka_ref[hh, k0 + half:k0 + tq, :], qa_ref[hh, q0 + half:q0 + tq, :], nt,
                preferred_element_type=F32)

    def piece(hh, slot, ci, g, masked):
        r0, c0 = ci * FOX_CHUNK, g * LANES
        if masked and c0 + LANES - 1 < r0:
            return None
        x = st_ref[hh, slot, r0:r0 + FOX_CHUNK, c0:c0 + LANES]
        if masked and c0 < r0 + FOX_CHUNK - 1:
            key = r0 + lax.broadcasted_iota(jnp.int32, (FOX_CHUNK, LANES), 0)
            qry = c0 + lax.broadcasted_iota(jnp.int32, (FOX_CHUNK, LANES), 1)
            x = jnp.where(key <= qry, x, -jnp.inf)
        return x

    def fold(x, op):
        return op(x.reshape(FOX_CHUNK // SUBLANES, SUBLANES, LANES), axis=0)

    def tile_stats(hh, slot, masked, first):
        mx = [None] * ncg
        for ci in range(nch):
            for g in range(ncg):
                x = piece(hh, slot, ci, g, masked)
                if x is not None:
                    f = fold(x, jnp.max)
                    mx[g] = f if mx[g] is None else jnp.maximum(mx[g], f)
        m_new = jnp.concatenate([jnp.max(v, axis=0, keepdims=True) for v in mx], axis=1)
        alpha = None
        if not first:
            m_old = m_ref[hh]
            m_new = jnp.maximum(m_old, m_new)
            alpha = jnp.exp2(m_old - m_new)
        m_ref[hh] = m_new
        return m_new, alpha

    def tile_probs(hh, slot, masked, m_new):
        ls = [None] * ncg
        for ci in range(nch):
            for g in range(ncg):
                r0, c0 = ci * FOX_CHUNK, g * LANES
                x = piece(hh, slot, ci, g, masked)
                if x is None:
                    continue
                p = jnp.exp2(x - m_new[:, c0:c0 + LANES])
                f = fold(p, jnp.sum)
                ls[g] = f if ls[g] is None else ls[g] + f
                pt_ref[hh, int(masked), r0:r0 + FOX_CHUNK, c0:c0 + LANES] = p.astype(BF16)
        return jnp.concatenate([jnp.sum(v, axis=0, keepdims=True) for v in ls], axis=1)

    def tile_update(hh, kj, masked, alpha, l_new):
        k0 = kj * tq
        if masked:
            pv = jnp.dot(vt_ref[hh, :, k0:k0 + half], pt_ref[hh, 1, :half, :],
                         preferred_element_type=F32)
            late = jnp.dot(vt_ref[hh, :, k0 + half:k0 + tq], pt_ref[hh, 1, half:, half:],
                           preferred_element_type=F32)
            pv = jnp.concatenate([pv[:, :half], pv[:, half:] + late], axis=1)
        else:
            pv = jnp.dot(vt_ref[hh, :, k0:k0 + tq], pt_ref[hh, 0], preferred_element_type=F32)
        if alpha is None:
            l_ref[hh] = l_new
            acc_ref[hh] = pv
        else:
            l_ref[hh] = alpha * l_ref[hh] + l_new
            acc_ref[hh] = alpha * acc_ref[hh] + pv

    heads = range(nh)
    for qi in range(seqlen // tq):
        rows = slice(qi * tq, (qi + 1) * tq)
        for hh in heads:
            scores(hh, qi, 0, 0)
        for kj in range(qi + 1):
            slot, masked = kj % 2, kj == qi
            if kj < qi:
                for hh in heads:
                    scores(hh, qi, kj + 1, (kj + 1) % 2)
            stats = [tile_stats(hh, slot, masked, kj == 0) for hh in heads]
            sums = [tile_probs(hh, slot, masked, stats[hh][0]) for hh in heads]
            for hh in heads:
                tile_update(hh, kj, masked, stats[hh][1], sums[hh])
        for hh in range(nh):
            z = z_ref[hh, rows, :].astype(F32)
            o = (acc_ref[hh] / l_ref[hh]).T
            o_ref[hh, rows, :] = (o * (z * jax.nn.sigmoid(z))).astype(o_ref.dtype)


def _fox_attn(q, k, v, z, fparts, *, bsz, tq, heads_per_step):
    heads, m, _ = q.shape
    seqlen = m // bsz
    nterms = 3
    assert seqlen % tq == 0 and tq % FOX_CHUNK == 0 and nterms * heads <= LANES
    h_idx = jnp.arange(heads)[:, None, None]
    r_idx = jnp.arange(LANES)[None, :, None]
    c_idx = jnp.arange(2 * HEAD_DIM)[None, None, :]
    key_side = (c_idx < nterms) & (r_idx == c_idx * heads + h_idx)
    qc = c_idx - HEAD_DIM - nterms
    qry_side = (qc >= 0) & (qc < nterms) & (r_idx == qc * heads + h_idx)
    sel = (qry_side.astype(F32) - key_side.astype(F32)).astype(BF16)
    col = jnp.arange(HEAD_DIM)
    cst = jnp.zeros((SUBLANES, HEAD_DIM), F32)
    cst = cst.at[0].set(((col >= nterms) & (col < 2 * nterms)).astype(F32))
    cst = cst.at[1].set((col < nterms).astype(F32))
    nh = heads_per_step
    assert heads % nh == 0
    head_seq = pl.BlockSpec((nh, seqlen, HEAD_DIM), lambda b, h: (h, b, 0))
    return pl.pallas_call(
        functools.partial(_fox_kernel, tq=tq),
        grid=(bsz, heads // nh),
        in_specs=[head_seq, head_seq, head_seq,
                  pl.BlockSpec((seqlen, LANES), lambda b, h: (b, 0)),
                  pl.BlockSpec((nh, LANES, 2 * HEAD_DIM), lambda b, h: (h, 0, 0)),
                  pl.BlockSpec((SUBLANES, HEAD_DIM), lambda b, h: (0, 0)),
                  head_seq],
        out_specs=head_seq,
        out_shape=jax.ShapeDtypeStruct((heads, m, HEAD_DIM), BF16),
        scratch_shapes=[pltpu.VMEM((nh, seqlen, 2 * HEAD_DIM), BF16),
                        pltpu.VMEM((nh, seqlen, 2 * HEAD_DIM), BF16),
                        pltpu.VMEM((nh, HEAD_DIM, seqlen), BF16),
                        pltpu.VMEM((nh, 2, tq, tq), F32),
                        pltpu.VMEM((nh, 2, tq, tq), BF16),
                        pltpu.VMEM((nh, 1, tq), F32),
                        pltpu.VMEM((nh, 1, tq), F32),
                        pltpu.VMEM((nh, HEAD_DIM, tq), F32)],
        compiler_params=_params(("arbitrary", "arbitrary")),
        name="fox_attn",
    )(q, k, v, fparts, sel, cst, z)


def kernel(x, mem, pre_norm_g, post_norm_g, w_in_a, lam_re, lam_im, log_step, b_re, b_im,
           c_re, c_im, d_skip, w_glu, b_glu, kv_norm_g, w_kv, w_fgate, b_fgate, w_in_b,
           mem_norm_g, w_mem_kv, w_out):
    bsz, seqlen, d = x.shape
    n_mem = mem.shape[1]
    main_w = w_glu.shape[1]
    mem_w = w_out.shape[1] - main_w
    scale = HEAD_DIM ** -0.5
    x2d = x.reshape(bsz * seqlen, d)
    mem2d = mem.reshape(bsz * n_mem, d)

    ones = jnp.ones((main_w,), F32)
    mem_scale = jnp.full((mem_w,), scale * LOG2E, F32)
    cs_a = jnp.concatenate([ones, ones, mem_scale, ones[:mem_w]])
    cs_b = jnp.concatenate([ones * (scale * LOG2E), ones, mem_scale, ones[:mem_w]])
    proj = functools.partial(_matmul, out_dtype=BF16, tm=1024)
    u_tm, xa = _matmul(x2d, w_in_a[0], cs_a, col0=0, n=main_w, out_dtype=F32, tm=512,
                       tn=main_w, time_major_batches=bsz, norm_gain=pre_norm_g[0],
                       name="in_proj_a_u")
    qz_a = proj(xa, w_in_a[0], cs_a, col0=2 * main_w, n=2 * mem_w, tn=2 * mem_w,
                slab_width=HEAD_DIM, name="in_proj_a_mem")
    kvm = _norm_matmul(mem2d, mem_norm_g, w_mem_kv, out_dtype=BF16, tm=512, tn=512,
                       name="mem_kv")
    s5_ops = _s5_discretise(lam_re[0], lam_im[0], log_step[0], b_re[0], b_im[0],
                            c_re[0], c_im[0])
    u3 = u_tm.reshape(seqlen, bsz, main_w)
    u3, qz_a, kvm = lax.optimization_barrier((u3, qz_a, kvm))
    yg3 = _s5(u3, *s5_ops, d_skip[0], n_pairs=128, blocks_per_step=3)
    yg_tm = yg3.reshape(seqlen, bsz * main_w)
    z_a = proj(xa, w_in_a[0], cs_a, col0=main_w, n=main_w, tn=main_w, name="in_proj_a_z")
    memo0 = _mem_attn(qz_a, kvm, 0, bsz=bsz, tq=1024, name="mem_attn0")
    w_out_bf = w_out.astype(BF16)
    h1, xkv, xb, fparts = _glu_out(
        yg_tm, z_a, memo0, x2d, w_glu[0], b_glu[0], w_out_bf, 0, post_norm_g[0],
        kv_norm_g, pre_norm_g[1], w_fgate, b_fgate, bsz=bsz, tl=512, sub=256)

    cs_kv = jnp.ones((w_kv.shape[1],), F32)
    k_sh = proj(xkv, w_kv, cs_kv, col0=0, n=main_w, tn=main_w, slab_width=HEAD_DIM,
                name="k_proj")
    v_sh = proj(xkv, w_kv, cs_kv, col0=main_w, n=main_w, tn=main_w, slab_width=HEAD_DIM,
                name="v_proj")

    q_b = proj(xb, w_in_b[0], cs_b, col0=0, n=main_w, tn=main_w, slab_width=HEAD_DIM,
               name="in_proj_b_q")
    z_b = proj(xb, w_in_b[0], cs_b, col0=main_w, n=main_w, tn=main_w, slab_width=HEAD_DIM,
               name="in_proj_b_z")
    qz_b = proj(xb, w_in_b[0], cs_b, col0=2 * main_w, n=2 * mem_w, tn=2 * mem_w,
                slab_width=HEAD_DIM, name="in_proj_b_mem")
    memo1 = _mem_attn(qz_b, kvm, 1, bsz=bsz, tq=1024, name="mem_attn1")
    att = _fox_attn(q_b, k_sh, v_sh, z_b, fparts, bsz=bsz, tq=512, heads_per_step=4)
    out = _out_proj(att, memo1, h1, w_out_bf, 1, post_norm_g[1], tl=512, sub=256)
    return out.reshape(bsz, seqlen, d)
```
